```python
import math
import jax
import jax.numpy as jnp
from jax import lax
import numpy as np

D_MODEL = 2048
BATCH = 2
SEQ = 4096
DEPTH = 4

GRID_W = 64
CTX_LEN = 256
N_MIXERS = 4
MIXER_HYENA = 0
MIXER_RWKV = 1
MIXER_MLA = 2
MIXER_GQA = 3
N_MOD = 9
D_FF = 5632
NORM_EPS = 1e-6
Q_BLOCK = 128
ROPE_THETA = 10000.0

HY_ORDER = 2
HY_EMB_DIM = 33
HY_FILTER_HIDDEN = 64
HY_DECAY_TARGET = 1e-2
HY_DECAY_PCT_SHORT = 0.3
HY_DECAY_PCT_LONG = 1.5

RW_HEAD = 64
RW_HEADS = D_MODEL // RW_HEAD
RW_DECAY_LORA = 96
RW_A_LORA = 96
RW_GATE_LORA = 256
RW_GN_EPS = 1e-5 * RW_HEAD
RW_N_SHIFT = 6

MLA_HEADS = 16
MLA_NOPE = 128
MLA_ROPE = 64
MLA_V = 128
MLA_Q_RANK = 512
MLA_KV_RANK = 512

GQA_HEADS = 16
GQA_KV_HEADS = 8
GQA_HEAD = 128
GQA_Q_DIM = GQA_HEADS * GQA_HEAD

kernel_name = "hybrid_interleaved_diffusion_trunk"


def _count(m):
    return len(range(m, DEPTH, N_MIXERS))


def _ctx_needed_after(i):
    return any(j % N_MIXERS != MIXER_HYENA for j in range(i + 1, DEPTH))


def rms_norm(u, g):
    uf = u.astype(jnp.float32)
    y = uf * lax.rsqrt(jnp.mean(uf * uf, -1, keepdims=True) + NORM_EPS)
    return (y * g.astype(jnp.float32)).astype(u.dtype)


def adaln(cvec, w, b):
    m = jax.nn.silu(cvec) @ w + b
    return m.reshape(cvec.shape[0], N_MOD, 1, D_MODEL)


def modulated_norm(h, mods, s, g):
    return rms_norm(h, g) * (1.0 + mods[:, 3 * s + 1]) + mods[:, 3 * s]


def macaron_half_ffn(h, mods, s, g, w_gu, w_down):
    n = modulated_norm(h, mods, s, g)
    a, b = jnp.split(n @ w_gu, 2, axis=-1)
    return h + 0.5 * mods[:, 3 * s + 2] * ((jax.nn.silu(a) * b) @ w_down)


def shift_prev(u):
    return jnp.pad(u, ((0, 0), (1, 0), (0, 0)))[:, :-1]


def shift_next(u):
    return jnp.pad(u, ((0, 0), (0, 1), (0, 0)))[:, 1:]


def grid_positions(T):
    rows = T // GRID_W
    row = jnp.repeat(jnp.arange(rows, dtype=jnp.int32), GRID_W)
    col = jnp.tile(jnp.arange(GRID_W, dtype=jnp.int32), rows)
    return row, col


def rope_1d(u, pos):
    half = u.shape[-1] // 2
    inv_freq = ROPE_THETA ** (-jnp.arange(half, dtype=jnp.float32) / half)
    ang = pos.astype(jnp.float32)[:, None] * inv_freq[None]
    cos = jnp.cos(ang)[None, :, None]
    sin = jnp.sin(ang)[None, :, None]
    uf = u.astype(jnp.float32)
    u1, u2 = uf[..., :half], uf[..., half:]
    return jnp.concatenate([u1 * cos - u2 * sin, u1 * sin + u2 * cos], -1).astype(u.dtype)


def rope_2d(u, row, col):
    d = u.shape[-1] // 2
    return jnp.concatenate([rope_1d(u[..., :d], row), rope_1d(u[..., d:], col)], -1)


def block_attention(q, k, v, scale):
    B, T, H, dq = q.shape
    Hkv, dv = k.shape[2], v.shape[-1]
    G = H // Hkv
    nb = T // Q_BLOCK
    qb = jnp.moveaxis(q.reshape(B, nb, Q_BLOCK, Hkv, G, dq), 1, 0)

    def one(q_blk):
        s = jnp.einsum('bqhgd,bkhd->bhgqk', q_blk, k).astype(jnp.float32) * scale
        p = jax.nn.softmax(s, axis=-1).astype(v.dtype)
        return jnp.einsum('bhgqk,bkhd->bqhgd', p, v)

    o = lax.map(one, qb)
    return jnp.moveaxis(o, 0, 1).reshape(B, T, H * dv)


def short_conv3(u, w, b):
    return shift_prev(u) * w[0] + u * w[1] + shift_next(u) * w[2] + b


def hyena_filter_spectra(L, f_w1, f_b1, f_w2, f_b2, f_w3, sin_freq):
    f32 = jnp.float32
    bands = (HY_EMB_DIM - 1) // 2
    t = jnp.linspace(0.0, 1.0, L, dtype=f32)[:, None]
    w = (2.0 * math.pi / L) * jnp.arange(L, dtype=f32)[:, None]
    f = jnp.linspace(1e-4, bands - 1, bands, dtype=f32)[None]
    z = jnp.concatenate([t, jnp.cos(f * w), -jnp.sin(f * w)], -1)
    h = jnp.sin(sin_freq[0].astype(f32) * (z @ f_w1.astype(f32) + f_b1.astype(f32)))
    h = jnp.sin(sin_freq[1].astype(f32) * (h @ f_w2.astype(f32) + f_b2.astype(f32)))
    h = (h @ f_w3.astype(f32)).reshape(L, HY_ORDER, 2, D_MODEL)
    deltas = jnp.abs(jnp.linspace(math.log(HY_DECAY_TARGET) / HY_DECAY_PCT_SHORT,
                                  math.log(HY_DECAY_TARGET) / HY_DECAY_PCT_LONG, D_MODEL, dtype=f32))
    h = h * jnp.exp(-t * deltas)[:, None, None, :]
    fwd = h[:, :, 0]
    bwd = h[1:, :, 1][::-1]
    g = jnp.concatenate([fwd, jnp.zeros((1, HY_ORDER, D_MODEL), f32), bwd], 0)
    g = g * lax.rsqrt(jnp.sum(g * g, 0, keepdims=True) + 1e-12)
    return jnp.fft.rfft(g, axis=0)


def fft_long_conv(u, spec, skip):
    L = u.shape[1]
    uf = u.astype(jnp.float32)
    y = jnp.fft.irfft(jnp.fft.rfft(uf, n=2 * L, axis=1) * spec[None], n=2 * L, axis=1)[:, :L]
    return (y + uf * skip.astype(jnp.float32)).astype(u.dtype)


def hyena_sequence(h, w_in, b_in, conv_w, conv_b, f_w1, f_b1, f_w2, f_b2, f_w3, sin_freq, skip, w_out, b_out):
    L = h.shape[1]
    u = short_conv3(h @ w_in + b_in, conv_w, conv_b)
    v, x1, x2 = jnp.split(u, 3, axis=-1)
    spec = hyena_filter_spectra(L, f_w1, f_b1, f_w2, f_b2, f_w3, sin_freq)
    z = v
    for n, gate in enumerate((x1, x2)):
        z = gate * fft_long_conv(z, spec[:, n], skip[n])
    return z @ w_out + b_out


def rwkv7_token_tensors(h, mix, w_rkv, w0, w1, w2, a0, a1, a2, g1, g2, k_k, k_a):
    B, L, _ = h.shape
    f32 = jnp.float32

    def heads(t):
        return t.astype(f32).reshape(B, L, RW_HEADS, RW_HEAD)

    d_prev = shift_prev(h) - h
    d_next = shift_next(h) - h
    xs = h[:, :, None] + d_prev[:, :, None] * mix[0] + d_next[:, :, None] * mix[1]
    xr, xw, xk, xv, xa, xg = (xs[:, :, n] for n in range(RW_N_SHIFT))
    r = heads(xr @ w_rkv[0])
    k = xk @ w_rkv[1]
    v = heads(xv @ w_rkv[2])
    g = jax.nn.sigmoid(xg @ g1) @ g2
    kk = heads(k * k_k)
    kk = kk * lax.rsqrt(jnp.sum(kk * kk, -1, keepdims=True) + 1e-12)
    dirs = []
    for d in range(2):
        log_w = -jax.nn.softplus(-(w0[d] + jnp.tanh(xw @ w1[d]) @ w2[d])) - 0.5
        decay = heads(jnp.exp(-jnp.exp(log_w.astype(f32))))
        a = jax.nn.sigmoid(a0[d] + (xa @ a1[d]) @ a2[d])
        k_d = heads(k * (1.0 + (a - 1.0) * k_a))
        dirs.append((decay, k_d, -kk, kk * heads(a)))
    return r, v, g, dirs


def wkv7_scan(s0, r, decay, k, v, aa, bb, reverse):
    def step(s, inp):
        r_t, w_t, k_t, v_t, a_t, b_t = inp
        sa = jnp.einsum('bhvk,bhk->bhv', s, a_t)
        s = s * w_t[:, :, None, :] + sa[..., None] * b_t[:, :, None, :] + v_t[..., None] * k_t[:, :, None, :]
        return s, jnp.einsum('bhvk,bhk->bhv', s, r_t)

    xs = tuple(jnp.moveaxis(t, 1, 0) for t in (r, decay, k, v, aa, bb))
    s_fin, ys = lax.scan(step, s0, xs, reverse=reverse)
    return jnp.moveaxis(ys, 0, 1), s_fin


def rwkv7_readout(r, v, g, dirs, ys, r_k, ln_w, ln_b, w_o, dtype):
    B, L = r.shape[:2]
    f32 = jnp.float32
    y = ys[0] + ys[1]
    mu = jnp.mean(y, -1, keepdims=True)
    var = jnp.mean(jnp.square(y - mu), -1, keepdims=True)
    y = ((y - mu) * lax.rsqrt(var + RW_GN_EPS)).reshape(B, L, D_MODEL) * ln_w.astype(f32) + ln_b.astype(f32)
    rk = r * r_k.astype(f32)
    bonus = (jnp.sum(rk * dirs[0][1], -1, keepdims=True) + jnp.sum(rk * dirs[1][1], -1, keepdims=True)) * v
    y = y + bonus.reshape(B, L, D_MODEL)
    return (y.astype(dtype) * g) @ w_o


def rwkv7_mixer(hl, hc, ctx_out, mix, w_rkv, w0, w1, w2, a0, a1, a2, g1, g2, k_k, k_a, r_k, ln_w, ln_b, w_o):
    shared = (mix, w_rkv, w0, w1, w2, a0, a1, a2, g1, g2, k_k, k_a)
    r_l, v_l, g_l, dirs_l = rwkv7_token_tensors(hl, *shared)
    r_c, v_c, g_c, dirs_c = rwkv7_token_tensors(hc, *shared)
    s_zero = jnp.zeros((hl.shape[0], RW_HEADS, RW_HEAD, RW_HEAD), jnp.float32)
    ys_l, ys_c = [], []
    for d, rev in enumerate((False, True)):
        decay_c, k_c, aa_c, bb_c = dirs_c[d]
        y_c, s_ctx = wkv7_scan(s_zero, r_c, decay_c, k_c, v_c, aa_c, bb_c, rev)
        decay_l, k_l, aa_l, bb_l = dirs_l[d]
        y_l, _ = wkv7_scan(s_ctx, r_l, decay_l, k_l, v_l, aa_l, bb_l, rev)
        ys_c.append(y_c)
        ys_l.append(y_l)
    yl = rwkv7_readout(r_l, v_l, g_l, dirs_l, ys_l, r_k, ln_w, ln_b, w_o, hl.dtype)
    yc = rwkv7_readout(r_c, v_c, g_c, dirs_c, ys_c, r_k, ln_w, ln_b, w_o, hc.dtype) if ctx_out else None
    return yl, yc


def mla_queries(h, w_down, q_norm_g, w_uq, pos):
    B, L, _ = h.shape
    cq = rms_norm(h @ w_down[:, :MLA_Q_RANK], q_norm_g)
    q = (cq @ w_uq).reshape(B, L, MLA_HEADS, MLA_NOPE + MLA_ROPE)
    q_pe = q[..., MLA_NOPE:]
    if pos is not None:
        q_pe = rope_2d(q_pe, *pos)
    return jnp.concatenate([q[..., :MLA_NOPE], q_pe], -1)


def mla_keys_values(h, w_down, kv_norm_g, w_ukv, pos):
    B, L, _ = h.shape
    ckv = h @ w_down[:, MLA_Q_RANK:]
    c_kv = rms_norm(ckv[..., :MLA_KV_RANK], kv_norm_g)
    k_pe = ckv[..., MLA_KV_RANK:][:, :, None, :]
    if pos is not None:
        k_pe = rope_2d(k_pe, *pos)
    kv = (c_kv @ w_ukv).reshape(B, L, MLA_HEADS, MLA_NOPE + MLA_V)
    k = jnp.concatenate([kv[..., :MLA_NOPE], jnp.broadcast_to(k_pe, (B, L, MLA_HEADS, MLA_ROPE))], -1)
    return k, kv[..., MLA_NOPE:]


def mla_mixer(hl, hc, lat_pos, ctx_out, w_down, q_norm_g, w_uq, kv_norm_g, w_ukv, w_o):
    scale = (MLA_NOPE + MLA_ROPE) ** -0.5
    k_c, v_c = mla_keys_values(hc, w_down, kv_norm_g, w_ukv, None)
    k_l, v_l = mla_keys_values(hl, w_down, kv_norm_g, w_ukv, lat_pos)
    q_l = mla_queries(hl, w_down, q_norm_g, w_uq, lat_pos)
    yl = block_attention(q_l, jnp.concatenate([k_c, k_l], 1), jnp.concatenate([v_c, v_l], 1), scale) @ w_o
    yc = None
    if ctx_out:
        yc = block_attention(mla_queries(hc, w_down, q_norm_g, w_uq, None), k_c, v_c, scale) @ w_o
    return yl, yc


def gqa_queries(h, w_qkv, q_norm_g, pos):
    B, L, _ = h.shape
    q = rms_norm((h @ w_qkv[:, :GQA_Q_DIM]).reshape(B, L, GQA_HEADS, GQA_HEAD), q_norm_g)
    if pos is not None:
        q = rope_2d(q, *pos)
    return q


def gqa_keys_values(h, w_qkv, k_norm_g, pos):
    B, L, _ = h.shape
    kv = (h @ w_qkv[:, GQA_Q_DIM:]).reshape(B, L, 2, GQA_KV_HEADS, GQA_HEAD)
    k = rms_norm(kv[:, :, 0], k_norm_g)
    if pos is not None:
        k = rope_2d(k, *pos)
    return k, kv[:, :, 1]


def gqa_mixer(hl, hc, lat_pos, ctx_out, w_qkv, q_norm_g, k_norm_g, w_o):
    scale = GQA_HEAD ** -0.5
    k_c, v_c = gqa_keys_values(hc, w_qkv, k_norm_g, None)
    k_l, v_l = gqa_keys_values(hl, w_qkv, k_norm_g, lat_pos)
    q_l = gqa_queries(hl, w_qkv, q_norm_g, lat_pos)
    yl = block_attention(q_l, jnp.concatenate([k_c, k_l], 1), jnp.concatenate([v_c, v_l], 1), scale) @ w_o
    yc = None
    if ctx_out:
        yc = block_attention(gqa_queries(hc, w_qkv, q_norm_g, None), k_c, v_c, scale) @ w_o
    return yl, yc


def setup_inputs(seed: int = 0) -> dict:
    key = jax.random.key(seed)
    keys = iter(jax.random.split(key, 96))
    f32 = jnp.float32
    D = D_MODEL

    def nrm(shape, scale):
        return jax.random.normal(next(keys), shape, f32) * scale

    def near_one(shape):
        return 1.0 + nrm(shape, 0.02)

    def unif(shape, lo, hi):
        return jax.random.uniform(next(keys), shape, f32, lo, hi)

    nA, nB, nC, nD = (_count(m) for m in range(N_MIXERS))
    return {
        "x": nrm((BATCH, SEQ, D), 1.0),
        "c": nrm((BATCH, D), 1.0),
        "ctx": nrm((BATCH, CTX_LEN, D), 1.0),
        "c_ctx": nrm((D,), 1.0),
        "ada_w": nrm((DEPTH, D, N_MOD * D), 0.5 * D ** -0.5),
        "ada_b": nrm((DEPTH, N_MOD * D), 0.02),
        "norm_g": near_one((DEPTH, 3, D)),
        "ffn_w_gu": nrm((DEPTH, 2, D, 2 * D_FF), D ** -0.5),
        "ffn_w_down": nrm((DEPTH, 2, D_FF, D), D_FF ** -0.5),
        "hy_w_in": nrm((nA, D, 3 * D), D ** -0.5),
        "hy_b_in": nrm((nA, 3 * D), 0.02),
        "hy_conv_w": nrm((nA, 3, 3 * D), 3 ** -0.5),
        "hy_conv_b": nrm((nA, 3 * D), 0.02),
        "hy_f_w1": nrm((nA, HY_EMB_DIM, HY_FILTER_HIDDEN), HY_EMB_DIM ** -0.5),
        "hy_f_b1": nrm((nA, HY_FILTER_HIDDEN), 0.1),
        "hy_f_w2": nrm((nA, HY_FILTER_HIDDEN, HY_FILTER_HIDDEN), HY_FILTER_HIDDEN ** -0.5),
        "hy_f_b2": nrm((nA, HY_FILTER_HIDDEN), 0.1),
        "hy_f_w3": nrm((nA, HY_FILTER_HIDDEN, HY_ORDER * 2 * D), HY_FILTER_HIDDEN ** -0.5),
        "hy_sin_freq": near_one((nA, 2, HY_FILTER_HIDDEN)),
        "hy_skip": nrm((nA, HY_ORDER, D), 0.5),
        "hy_w_out": nrm((nA, D, D), D ** -0.5),
        "hy_b_out": nrm((nA, D), 0.02),
        "rw_mix": unif((nB, 2, RW_N_SHIFT, D), 0.0, 0.5),
        "rw_w_rkv": nrm((nB, 3, D, D), D ** -0.5),
        "rw_w0": unif((nB, 2, D), -6.5, -1.5),
        "rw_w1": nrm((nB, 2, D, RW_DECAY_LORA), D ** -0.5),
        "rw_w2": nrm((nB, 2, RW_DECAY_LORA, D), 0.3 * RW_DECAY_LORA ** -0.5),
        "rw_a0": nrm((nB, 2, D), 0.1),
        "rw_a1": nrm((nB, 2, D, RW_A_LORA), D ** -0.5),
        "rw_a2": nrm((nB, 2, RW_A_LORA, D), 0.5 * RW_A_LORA ** -0.5),
        "rw_g1": nrm((nB, D, RW_GATE_LORA), D ** -0.5),
        "rw_g2": nrm((nB, RW_GATE_LORA, D), RW_GATE_LORA ** -0.5),
        "rw_k_k": unif((nB, D), 0.7, 1.0),
        "rw_k_a": near_one((nB, D)),
        "rw_r_k": nrm((nB, RW_HEADS, RW_HEAD), 0.1),
        "rw_ln_w": near_one((nB, D)),
        "rw_ln_b": nrm((nB, D), 0.02),
        "rw_w_o": nrm((nB, D, D), D ** -0.5),
        "mla_w_down": nrm((nC, D, MLA_Q_RANK + MLA_KV_RANK + MLA_ROPE), D ** -0.5),
        "mla_q_norm_g": near_one((nC, MLA_Q_RANK)),
        "mla_w_uq": nrm((nC, MLA_Q_RANK, MLA_HEADS * (MLA_NOPE + MLA_ROPE)), MLA_Q_RANK ** -0.5),
        "mla_kv_norm_g": near_one((nC, MLA_KV_RANK)),
        "mla_w_ukv": nrm((nC, MLA_KV_RANK, MLA_HEADS * (MLA_NOPE + MLA_V)), MLA_KV_RANK ** -0.5),
        "mla_w_o": nrm((nC, MLA_HEADS * MLA_V, D), (MLA_HEADS * MLA_V) ** -0.5),
        "gqa_w_qkv": nrm((nD, D, (GQA_HEADS + 2 * GQA_KV_HEADS) * GQA_HEAD), D ** -0.5),
        "gqa_q_norm_g": near_one((nD, GQA_HEAD)),
        "gqa_k_norm_g": near_one((nD, GQA_HEAD)),
        "gqa_w_o": nrm((nD, GQA_HEADS * GQA_HEAD, D), (GQA_HEADS * GQA_HEAD) ** -0.5),
        "final_g": near_one((D,)),
    }


def reference(x, c, ctx, c_ctx, ada_w, ada_b, norm_g, ffn_w_gu, ffn_w_down,
              hy_w_in, hy_b_in, hy_conv_w, hy_conv_b, hy_f_w1, hy_f_b1, hy_f_w2, hy_f_b2, hy_f_w3,
              hy_sin_freq, hy_skip, hy_w_out, hy_b_out,
              rw_mix, rw_w_rkv, rw_w0, rw_w1, rw_w2, rw_a0, rw_a1, rw_a2, rw_g1, rw_g2,
              rw_k_k, rw_k_a, rw_r_k, rw_ln_w, rw_ln_b, rw_w_o,
              mla_w_down, mla_q_norm_g, mla_w_uq, mla_kv_norm_g, mla_w_ukv, mla_w_o,
              gqa_w_qkv, gqa_q_norm_g, gqa_k_norm_g, gqa_w_o,
              final_g):
    lat_pos = grid_positions(x.shape[1])
    for i in range(DEPTH):
        m, j = i % N_MIXERS, i // N_MIXERS
        ctx_out = _ctx_needed_after(i)
        ctx_in = ctx_out or m != MIXER_HYENA
        mod_l = adaln(c, ada_w[i], ada_b[i])
        x = macaron_half_ffn(x, mod_l, 0, norm_g[i, 0], ffn_w_gu[i, 0], ffn_w_down[i, 0])
        hl = modulated_norm(x, mod_l, 1, norm_g[i, 1])
        if ctx_in:
            mod_c = adaln(c_ctx[None], ada_w[i], ada_b[i])
            ctx = macaron_half_ffn(ctx, mod_c, 0, norm_g[i, 0], ffn_w_gu[i, 0], ffn_w_down[i, 0])
            hc = modulated_norm(ctx, mod_c, 1, norm_g[i, 1])
        if m == MIXER_HYENA:
            hy = (hy_w_in[j], hy_b_in[j], hy_conv_w[j], hy_conv_b[j], hy_f_w1[j], hy_f_b1[j], hy_f_w2[j],
                  hy_f_b2[j], hy_f_w3[j], hy_sin_freq[j], hy_skip[j], hy_w_out[j], hy_b_out[j])
            yl = hyena_sequence(hl, *hy)
            yc = hyena_sequence(hc, *hy) if ctx_out else None
        elif m == MIXER_RWKV:
            yl, yc = rwkv7_mixer(hl, hc, ctx_out, rw_mix[j], rw_w_rkv[j], rw_w0[j], rw_w1[j], rw_w2[j],
                                 rw_a0[j], rw_a1[j], rw_a2[j], rw_g1[j], rw_g2[j], rw_k_k[j], rw_k_a[j],
                                 rw_r_k[j], rw_ln_w[j], rw_ln_b[j], rw_w_o[j])
        elif m == MIXER_MLA:
            yl, yc = mla_mixer(hl, hc, lat_pos, ctx_out, mla_w_down[j], mla_q_norm_g[j], mla_w_uq[j],
                               mla_kv_norm_g[j], mla_w_ukv[j], mla_w_o[j])
        else:
            yl, yc = gqa_mixer(hl, hc, lat_pos, ctx_out, gqa_w_qkv[j], gqa_q_norm_g[j], gqa_k_norm_g[j], gqa_w_o[j])
        x = x + mod_l[:, 5] * yl
        x = macaron_half_ffn(x, mod_l, 2, norm_g[i, 2], ffn_w_gu[i, 1], ffn_w_down[i, 1])
        if ctx_out:
            ctx = ctx + mod_c[:, 5] * yc
            ctx = macaron_half_ffn(ctx, mod_c, 2, norm_g[i, 2], ffn_w_gu[i, 1], ffn_w_down[i, 1])
    return rms_norm(x, final_g)
```

```python
import functools
import math

import numpy as np
import jax
import jax.numpy as jnp
from jax import lax
from jax.experimental import pallas as pl
from jax.experimental.pallas import tpu as pltpu

F32 = jnp.float32
BF16 = jnp.bfloat16

VMEM_CAP_BYTES = 56 * 1024 * 1024
LANES = 128
SUBLANES = 8

NORM_EPS = 1e-6
N_MOD = 9
GRID_W = 64
ROPE_THETA = 10000.0
HY_DECAY_TARGET = 1e-2
HY_DECAY_PCT_SHORT = 0.3
HY_DECAY_PCT_LONG = 1.5
RW_HEAD = 64
RW_LORA_PAD = 128
GQA_HEAD = 128
GQA_HEADS = 16
GQA_KV_HEADS = 8
MLA_HEADS = 16
MLA_NOPE = 128
MLA_ROPE = 64
MLA_V = 128
MLA_Q_RANK = 512
MLA_KV_RANK = 512


def _cparams(sem):
    return pltpu.CompilerParams(dimension_semantics=sem, vmem_limit_bytes=VMEM_CAP_BYTES)


def _dot(a, b):
    return jnp.dot(a, b, preferred_element_type=F32)


def _dot_nt(a, b):
    return lax.dot_general(a, b, (((1,), (1,)), ((), ())), preferred_element_type=F32)


def _dot_tn(a, b):
    return lax.dot_general(a, b, (((0,), (0,)), ((), ())), preferred_element_type=F32)


def _split2(x):
    hi = x.astype(BF16)
    lo = (x - hi.astype(F32)).astype(BF16)
    return hi, lo


def _split3(x):
    hi = x.astype(BF16)
    r1 = x - hi.astype(F32)
    mid = r1.astype(BF16)
    lo = (r1 - mid.astype(F32)).astype(BF16)
    return hi, mid, lo


def _dot3(a, b, f=_dot):
    ah, al = _split2(a)
    bh, bl = _split2(b)
    return f(ah, bh) + f(ah, bl) + f(al, bh)


def _dotc3(mh, ml, d):
    dh, dl = _split2(d)
    return _dot(mh, dh) + _dot(ml, dh) + _dot(mh, dl)


def _rms(x, g):
    return x * lax.rsqrt(jnp.mean(x * x, -1, keepdims=True) + NORM_EPS) * g


def _pick(n, cands):
    for c in cands:
        if n % c == 0:
            return c
    raise ValueError(f"no tile for {n}")


def _adaln_body(c_ref, w_ref, b_ref, o_ref):
    c = c_ref[...]
    s = c * jax.nn.sigmoid(c)
    o_ref[0] = _dot(s.astype(BF16), w_ref[0].astype(BF16)) + b_ref[0]


def adaln_all(cvec8, ada_w, ada_b):
    depth, d, n = ada_w.shape
    tn = _pick(n, (2048, 1024, 512, 256, 128))
    return pl.pallas_call(
        _adaln_body,
        grid=(depth, n // tn),
        in_specs=[
            pl.BlockSpec((8, d), lambda l, j: (0, 0)),
            pl.BlockSpec((1, d, tn), lambda l, j: (l, 0, j)),
            pl.BlockSpec((1, 1, tn), lambda l, j: (l, 0, j)),
        ],
        out_specs=pl.BlockSpec((1, 8, tn), lambda l, j: (l, 0, j)),
        out_shape=jax.ShapeDtypeStruct((depth, 8, n), F32),
        compiler_params=_cparams(("arbitrary", "arbitrary")),
        name="adaln",
    )(cvec8, ada_w, ada_b.reshape(depth, 1, n))


def _mm_body(*refs, has_bias, act, has_norm, has_res):
    it = iter(refs)
    x_ref = next(it)
    w_ref = next(it)
    b_ref = next(it) if has_bias else None
    g_ref = next(it) if has_norm else None
    r_ref = next(it) if has_res else None
    gate_ref = next(it) if has_res else None
    o_ref = next(it)
    x = x_ref[...]
    if has_norm:
        x = _rms(x.astype(F32), g_ref[...])
    acc = _dot(x.astype(BF16), w_ref[...])
    if has_bias:
        acc = acc + b_ref[...]
    if act == "sigmoid":
        acc = jax.nn.sigmoid(acc)
    elif act == "tanh":
        acc = jnp.tanh(acc)
    if has_res:
        acc = r_ref[...] + gate_ref[0] * acc
    o_ref[...] = acc.astype(o_ref.dtype)


def mm(x, w, *, bias=None, act=None, norm_g=None, res=None, gate=None, out_dtype=F32,
       tm=512, rows=None, x_col_block=0, x_row_block0=0, seq_rows=None, name="mm"):
    k, n = w.shape
    m = rows if rows is not None else x.shape[0]
    assert m % tm == 0, (m, tm)
    tn = _pick(n, (512, 384, 256, 128))
    has_res = res is not None
    ins = [x, w]
    specs = [pl.BlockSpec((tm, k), lambda i, j: (i + x_row_block0, x_col_block)),
             pl.BlockSpec((k, tn), lambda i, j: (0, j))]
    if bias is not None:
        ins.append(bias.reshape(1, n).astype(F32))
        specs.append(pl.BlockSpec((1, tn), lambda i, j: (0, j)))
    if norm_g is not None:
        ins.append(norm_g.reshape(1, k).astype(F32))
        specs.append(pl.BlockSpec((1, k), lambda i, j: (0, 0)))
    if has_res:
        ngroups = gate.shape[0]
        per = seq_rows // tm
        ins += [res, gate]
        specs += [pl.BlockSpec((tm, tn), lambda i, j: (i, j)),
                  pl.BlockSpec((1, 1, tn), lambda i, j: (jnp.minimum(i // per, ngroups - 1), 0, j))]
    body = functools.partial(_mm_body, has_bias=bias is not None, act=act,
                             has_norm=norm_g is not None, has_res=has_res)
    return pl.pallas_call(
        body,
        grid=(m // tm, n // tn),
        in_specs=specs,
        out_specs=pl.BlockSpec((tm, tn), lambda i, j: (i, j)),
        out_shape=jax.ShapeDtypeStruct((m, n), out_dtype),
        compiler_params=_cparams(("parallel", "arbitrary")),
        name=name,
    )(*ins)


def _ffn_body(*refs, s, n_f, has_next, next_s):
    it = iter(refs)
    x_ref, mod_ref, g_ref, wa_ref, wb_ref, wd_ref = (next(it) for _ in range(6))
    gn_ref = next(it) if has_next else None
    o_ref = next(it)
    hn_ref = next(it) if has_next else None
    xn_sc = next(it)
    acc_sc = next(it)
    j = pl.program_id(1)

    @pl.when(j == 0)
    def _():
        n = _rms(x_ref[...], g_ref[s:s + 1, :])
        n = n * (1.0 + mod_ref[0, 3 * s + 1:3 * s + 2, :]) + mod_ref[0, 3 * s:3 * s + 1, :]
        xn_sc[...] = n.astype(BF16)
        acc_sc[...] = jnp.zeros_like(acc_sc)

    xn = xn_sc[...]
    a = _dot(xn, wa_ref[...])
    b = _dot(xn, wb_ref[...])
    h = (a * jax.nn.sigmoid(a)) * b
    acc_sc[...] += _dot(h.astype(BF16), wd_ref[...])

    @pl.when(j == n_f - 1)
    def _():
        xnew = x_ref[...] + 0.5 * mod_ref[0, 3 * s + 2:3 * s + 3, :] * acc_sc[...]
        o_ref[...] = xnew
        if has_next:
            hn = _rms(xnew, gn_ref[...])
            if next_s is not None:
                hn = hn * (1.0 + mod_ref[0, 3 * next_s + 1:3 * next_s + 2, :]) + mod_ref[0, 3 * next_s:3 * next_s + 1, :]
            hn_ref[...] = hn.astype(hn_ref.dtype)


def ffn_half(x, mods, norm_g3, w_gu, w_down, *, s, rows, seq_rows, next_g=None, next_s=None,
             next_dtype=F32, tm=512, fc=512):
    d = x.shape[1]
    d_ff = w_down.shape[0]
    n_f = d_ff // fc
    assert d_ff % fc == 0 and rows % tm == 0
    per = seq_rows // tm
    ngroups = mods.shape[0]
    has_next = next_g is not None
    ins = [x, mods, norm_g3, w_gu, w_gu, w_down]
    specs = [
        pl.BlockSpec((tm, d), lambda i, j: (i, 0)),
        pl.BlockSpec((1, N_MOD, d), lambda i, j: (jnp.minimum(i // per, ngroups - 1), 0, 0)),
        pl.BlockSpec((3, d), lambda i, j: (0, 0)),
        pl.BlockSpec((d, fc), lambda i, j: (0, j)),
        pl.BlockSpec((d, fc), lambda i, j: (0, n_f + j)),
        pl.BlockSpec((fc, d), lambda i, j: (j, 0)),
    ]
    out_shape = [jax.ShapeDtypeStruct((rows, d), F32)]
    out_specs = [pl.BlockSpec((tm, d), lambda i, j: (i, 0))]
    if has_next:
        ins.append(next_g.reshape(1, d))
        specs.append(pl.BlockSpec((1, d), lambda i, j: (0, 0)))
        out_shape.append(jax.ShapeDtypeStruct((rows, d), next_dtype))
        out_specs.append(pl.BlockSpec((tm, d), lambda i, j: (i, 0)))
    body = functools.partial(_ffn_body, s=s, n_f=n_f, has_next=has_next, next_s=next_s)
    outs = pl.pallas_call(
        body,
        grid=(rows // tm, n_f),
        in_specs=specs,
        out_specs=out_specs,
        out_shape=out_shape,
        scratch_shapes=[pltpu.VMEM((tm, d), BF16), pltpu.VMEM((tm, d), F32)],
        compiler_params=_cparams(("parallel", "arbitrary")),
        name="ffn_half",
    )(*ins)
    return outs if has_next else outs[0]


def _attn_body(*refs, n_parts, use_lat):
    it = iter(refs)
    qs = [next(it) for _ in range(n_parts)]
    kls = [next(it) for _ in range(n_parts)] if use_lat else []
    kcs = [next(it) for _ in range(n_parts)]
    vl_ref = next(it) if use_lat else None
    vc_ref = next(it)
    o_ref = next(it)
    s_c = _dot_nt(qs[0][...], kcs[0][...])
    for p in range(1, n_parts):
        s_c = s_c + _dot_nt(qs[p][...], kcs[p][...])
    m = jnp.max(s_c, -1, keepdims=True)
    if use_lat:
        s_l = _dot_nt(qs[0][...], kls[0][...])
        for p in range(1, n_parts):
            s_l = s_l + _dot_nt(qs[p][...], kls[p][...])
        m = jnp.maximum(m, jnp.max(s_l, -1, keepdims=True))
    p_c = jnp.exp(s_c - m)
    den = jnp.sum(p_c, -1, keepdims=True)
    o = _dot(p_c.astype(BF16), vc_ref[...])
    if use_lat:
        p_l = jnp.exp(s_l - m)
        den = den + jnp.sum(p_l, -1, keepdims=True)
        o = o + _dot(p_l.astype(BF16), vl_ref[...])
    o_ref[...] = (o / den).astype(o_ref.dtype)


def attention(q_parts, k_parts, v_arr, v_cb, *, n_heads, batch, seq, ctx_len, lat_queries, tq=512):
    n_parts = len(q_parts)
    ctx0 = (batch * seq) // ctx_len
    if lat_queries:
        tq = min(tq, seq)
        nq = seq // tq
        q_row = lambda b, t: b * nq + t
        rows_q = batch * seq
    else:
        tq = ctx_len
        nq = 1
        q_row = lambda b, t: ctx0 + b
        rows_q = batch * ctx_len
    ins, specs = [], []
    for arr, cb in q_parts:
        ins.append(arr)
        specs.append(pl.BlockSpec((tq, LANES), lambda b, h, t, cb=cb: (q_row(b, t), cb(h))))
    if lat_queries:
        for arr, cb in k_parts:
            ins.append(arr)
            specs.append(pl.BlockSpec((seq, LANES), lambda b, h, t, cb=cb: (b, cb(h))))
    for arr, cb in k_parts:
        ins.append(arr)
        specs.append(pl.BlockSpec((ctx_len, LANES), lambda b, h, t, cb=cb: (ctx0 + b, cb(h))))
    if lat_queries:
        ins.append(v_arr)
        specs.append(pl.BlockSpec((seq, LANES), lambda b, h, t: (b, v_cb(h))))
    ins.append(v_arr)
    specs.append(pl.BlockSpec((ctx_len, LANES), lambda b, h, t: (ctx0 + b, v_cb(h))))
    if lat_queries:
        o_row = lambda b, t: b * nq + t
    else:
        o_row = lambda b, t: b
    return pl.pallas_call(
        functools.partial(_attn_body, n_parts=n_parts, use_lat=lat_queries),
        grid=(batch, n_heads, nq),
        in_specs=specs,
        out_specs=pl.BlockSpec((tq, LANES), lambda b, h, t: (o_row(b, t), h)),
        out_shape=jax.ShapeDtypeStruct((rows_q, n_heads * LANES), BF16),
        compiler_params=_cparams(("parallel", "parallel", "arbitrary")),
        name="attention_lat" if lat_queries else "attention_ctx",
    )(*ins)


def _rope_tables(seq, ctx_len, batch, rot_dim):
    half = rot_dim // 4
    inv_freq = ROPE_THETA ** (-jnp.arange(half, dtype=F32) / half)
    t = jnp.arange(seq, dtype=jnp.int32)
    row = (t // GRID_W).astype(F32)
    col = (t % GRID_W).astype(F32)
    ang_r = row[:, None] * inv_freq[None]
    ang_c = col[:, None] * inv_freq[None]
    cos = jnp.concatenate([jnp.cos(ang_r), jnp.cos(ang_r), jnp.cos(ang_c), jnp.cos(ang_c)], -1)
    sin = jnp.concatenate([jnp.sin(ang_r), jnp.sin(ang_r), jnp.sin(ang_c), jnp.sin(ang_c)], -1)
    cos = jnp.concatenate([jnp.tile(cos, (batch, 1)), jnp.ones((batch * ctx_len, rot_dim), F32)], 0)
    sin = jnp.concatenate([jnp.tile(sin, (batch, 1)), jnp.zeros((batch * ctx_len, rot_dim), F32)], 0)
    return cos, sin


def _rot_cols(w, rot_dim):
    lead = w.shape[:-1]
    q = rot_dim // 4
    w5 = w.reshape(lead + (-1, 2, 2, q))
    rot = jnp.concatenate([-w5[..., 1:2, :], w5[..., 0:1, :]], axis=-2)
    return rot.reshape(w.shape)


def _gqa_prep_body(a_ref, ar_ref, cos_ref, sin_ref, g_ref, gr_ref, o_ref, *, scale):
    a = a_ref[...]
    r = lax.rsqrt(jnp.mean(a * a, -1, keepdims=True) + NORM_EPS) * scale
    o = (a * g_ref[...]) * cos_ref[...] + (ar_ref[...] * gr_ref[...]) * sin_ref[...]
    o_ref[...] = (o * r).astype(o_ref.dtype)


def gqa_prep(raw, cb0, cbr0, n_heads, cos, sin, g, g_rot, scale, tm=1088):
    m = raw.shape[0]
    return pl.pallas_call(
        functools.partial(_gqa_prep_body, scale=scale),
        grid=(m // tm, n_heads),
        in_specs=[
            pl.BlockSpec((tm, LANES), lambda i, h: (i, cb0 + h)),
            pl.BlockSpec((tm, LANES), lambda i, h: (i, cbr0 + h)),
            pl.BlockSpec((tm, LANES), lambda i, h: (i, 0)),
            pl.BlockSpec((tm, LANES), lambda i, h: (i, 0)),
            pl.BlockSpec((1, LANES), lambda i, h: (0, 0)),
            pl.BlockSpec((1, LANES), lambda i, h: (0, 0)),
        ],
        out_specs=pl.BlockSpec((tm, LANES), lambda i, h: (i, h)),
        out_shape=jax.ShapeDtypeStruct((m, n_heads * LANES), BF16),
        compiler_params=_cparams(("parallel", "arbitrary")),
        name="gqa_prep",
    )(raw, raw, cos, sin, g.reshape(1, LANES), g_rot.reshape(1, LANES))


def _mla_rope_body(x_ref, c_ref, s_ref, o_ref, *, scale, with_nope):
    if with_nope:
        a = x_ref[:, :LANES]
        b = x_ref[:, LANES:]
    else:
        b = x_ref[...]
    roped = (b * c_ref[...] + pltpu.roll(b, LANES // 2, 1) * s_ref[...]) * scale
    if with_nope:
        o_ref[:, :LANES] = (a * scale).astype(o_ref.dtype)
        o_ref[:, LANES:] = roped.astype(o_ref.dtype)
    else:
        o_ref[...] = roped.astype(o_ref.dtype)


def mla_rope(raw, cb0, n_heads, cos, sin, scale, with_nope, tm=1088):
    m = raw.shape[0]
    w = 2 * LANES if with_nope else LANES
    return pl.pallas_call(
        functools.partial(_mla_rope_body, scale=scale, with_nope=with_nope),
        grid=(m // tm, n_heads),
        in_specs=[
            pl.BlockSpec((tm, w), lambda i, h: (i, cb0 + h)),
            pl.BlockSpec((tm, LANES), lambda i, h: (i, 0)),
            pl.BlockSpec((tm, LANES), lambda i, h: (i, 0)),
        ],
        out_specs=pl.BlockSpec((tm, w), lambda i, h: (i, h)),
        out_shape=jax.ShapeDtypeStruct((m, n_heads * w), BF16),
        compiler_params=_cparams(("parallel", "arbitrary")),
        name="mla_rope",
    )(raw, cos, sin)


class _FftPlan:
    def __init__(self, L):
        self.L = L
        self.N = 2 * L
        self.N2 = 128 if L >= 1024 else 64
        self.N1 = self.N // self.N2
        self.n1_in = self.N1 // 2
        self.K1 = self.N1 // 2 + 1
        self.nq = self.N2 // SUBLANES
        self.spec_rows = self.K1 * 2 * self.N2
        self.kf_cols = max(self.n1_in * SUBLANES, LANES)
        self.kf_rows = 2 * self.K1 * SUBLANES
        self.kb_cols = -(-self.kf_rows // LANES) * LANES
        self.kb_rows = self.n1_in * SUBLANES

    def constants(self):
        N, N1, N2, K1, n1_in = self.N, self.N1, self.N2, self.K1, self.n1_in
        eye = np.eye(SUBLANES)
        k1 = np.arange(K1)[:, None]
        n1 = np.arange(n1_in)[None, :]
        th = 2 * np.pi * k1 * n1 / N1
        kf = np.zeros((self.kf_rows, self.kf_cols))
        kf[:K1 * SUBLANES, :n1_in * SUBLANES] = np.kron(np.cos(th), eye)
        kf[K1 * SUBLANES:, :n1_in * SUBLANES] = np.kron(-np.sin(th), eye)
        c = np.full((K1,), 2.0)
        c[0] = 1.0
        c[-1] = 1.0
        thb = 2 * np.pi * np.arange(n1_in)[:, None] * np.arange(K1)[None, :] / N1
        kb = np.zeros((self.kb_rows, self.kb_cols))
        kb[:, :K1 * SUBLANES] = np.kron(np.cos(thb) * c[None, :], eye)
        kb[:, K1 * SUBLANES:2 * K1 * SUBLANES] = np.kron(-np.sin(thb) * c[None, :], eye)
        a = 2 * np.pi * np.outer(np.arange(N2), np.arange(N2)) / N2
        C, S = np.cos(a), np.sin(a)
        w2f = np.block([[C, S], [-S, C]])
        w2i = np.block([[C, -S], [S, C]])
        tw_ang = 2 * np.pi * np.outer(np.arange(K1), np.arange(N2)) / N
        twr = np.repeat(np.cos(tw_ang).reshape(-1, 1), LANES, axis=1)
        twi = np.repeat(np.sin(tw_ang).reshape(-1, 1), LANES, axis=1)

        def hl(m):
            m32 = jnp.asarray(m, F32)
            hi = m32.astype(BF16)
            lo = (m32 - hi.astype(F32)).astype(BF16)
            return hi, lo

        out = {}
        for name, mat in (("kf", kf), ("kb", kb), ("w2f", w2f), ("w2i", w2i)):
            out[name + "_h"], out[name + "_l"] = hl(mat)
        out["twr"] = jnp.asarray(twr, F32)
        out["twi"] = jnp.asarray(twi, F32)
        return out


def _fft_major_fwd(plan, x_ref, kf_h, kf_l, spec_ref):
    N2, K1, n1_in = plan.N2, plan.K1, plan.n1_in
    pad_rows = plan.kf_cols - n1_in * SUBLANES

    def body(q, carry):
        tiles = [x_ref[pl.ds(pl.multiple_of(N2 * n + SUBLANES * q, SUBLANES), SUBLANES), :] for n in range(n1_in)]
        if pad_rows:
            tiles.append(jnp.zeros((pad_rows, tiles[0].shape[1]), F32))
        g = jnp.concatenate(tiles, axis=0)
        o = _dotc3(kf_h[...], kf_l[...], g)
        for part in range(2):
            for k in range(K1):
                src = (part * K1 + k) * SUBLANES
                dst = (2 * k + part) * N2
                spec_ref[pl.ds(pl.multiple_of(dst + SUBLANES * q, SUBLANES), SUBLANES), :] = o[src:src + SUBLANES, :]
        return carry

    lax.fori_loop(0, plan.nq, body, 0)


def _cmul(ar, ai, br, bi):
    return ar * br - ai * bi, ar * bi + ai * br


def _fft_minor_fwd(plan, spec_ref, w2f_h, w2f_l, twr_ref, twi_ref):
    N2 = plan.N2

    def body(k, carry):
        base = pl.multiple_of(2 * N2 * k, 2 * N2)
        tb = pl.multiple_of(N2 * k, N2)
        ar = spec_ref[pl.ds(base, N2), :]
        ai = spec_ref[pl.ds(base + N2, N2), :]
        tr = twr_ref[pl.ds(tb, N2), :]
        ti = twi_ref[pl.ds(tb, N2), :]
        xr, xi = _cmul(ar, ai, tr, -ti)
        y = _dotc3(w2f_h[...], w2f_l[...], jnp.concatenate([xr, xi], axis=0))
        spec_ref[pl.ds(base, 2 * N2), :] = y
        return carry

    lax.fori_loop(0, plan.K1, body, 0)


def _fft_fwd(plan, x_ref, consts, spec_ref):
    _fft_major_fwd(plan, x_ref, consts["kf_h"], consts["kf_l"], spec_ref)
    _fft_minor_fwd(plan, spec_ref, consts["w2f_h"], consts["w2f_l"], consts["twr"], consts["twi"])


def _fft_conv_apply(plan, x_ref, g_ref, consts, spec_ref, y_ref):
    N2, K1, n1_in = plan.N2, plan.K1, plan.n1_in
    _fft_major_fwd(plan, x_ref, consts["kf_h"], consts["kf_l"], spec_ref)
    w2f_h, w2f_l, w2i_h, w2i_l = consts["w2f_h"], consts["w2f_l"], consts["w2i_h"], consts["w2i_l"]
    twr_ref, twi_ref = consts["twr"], consts["twi"]

    def body(k, carry):
        base = pl.multiple_of(2 * N2 * k, 2 * N2)
        tb = pl.multiple_of(N2 * k, N2)
        tr = twr_ref[pl.ds(tb, N2), :]
        ti = twi_ref[pl.ds(tb, N2), :]
        xr, xi = _cmul(spec_ref[pl.ds(base, N2), :], spec_ref[pl.ds(base + N2, N2), :], tr, -ti)
        y = _dotc3(w2f_h[...], w2f_l[...], jnp.concatenate([xr, xi], axis=0))
        pr, pi = _cmul(y[:N2], y[N2:], g_ref[pl.ds(base, N2), :], g_ref[pl.ds(base + N2, N2), :])
        z = _dotc3(w2i_h[...], w2i_l[...], jnp.concatenate([pr, pi], axis=0))
        zr, zi = _cmul(z[:N2], z[N2:], tr, ti)
        spec_ref[pl.ds(base, N2), :] = zr
        spec_ref[pl.ds(base + N2, N2), :] = zi
        return carry

    lax.fori_loop(0, K1, body, 0)

    pad_rows = plan.kb_cols - plan.kf_rows
    kb_h, kb_l = consts["kb_h"], consts["kb_l"]

    def body2(q, carry):
        tiles = []
        for part in range(2):
            for k in range(K1):
                src = (2 * k + part) * N2
                tiles.append(spec_ref[pl.ds(pl.multiple_of(src + SUBLANES * q, SUBLANES), SUBLANES), :])
        if pad_rows:
            tiles.append(jnp.zeros((pad_rows, tiles[0].shape[1]), F32))
        t = jnp.concatenate(tiles, axis=0)
        o = _dotc3(kb_h[...], kb_l[...], t)
        for n in range(n1_in):
            y_ref[pl.ds(pl.multiple_of(N2 * n + SUBLANES * q, SUBLANES), SUBLANES), :] = o[n * SUBLANES:(n + 1) * SUBLANES, :]
        return carry

    lax.fori_loop(0, plan.nq, body2, 0)


_CONST_NAMES = ("kf_h", "kf_l", "kb_h", "kb_l", "w2f_h", "w2f_l", "w2i_h", "w2i_l", "twr", "twi")


def _const_specs(consts, nargs):
    ins, specs = [], []
    for nme in _CONST_NAMES:
        a = consts[nme]
        ins.append(a)
        if nargs == 1:
            specs.append(pl.BlockSpec(a.shape, lambda c: (0, 0)))
        else:
            specs.append(pl.BlockSpec(a.shape, lambda c, b: (0, 0)))
    return ins, specs


def _hy_hidden_body(z_ref, w1_ref, b1_ref, w2_ref, b2_ref, sf_ref, o_ref):
    h = jnp.sin(sf_ref[0:1, :] * (_dot3(z_ref[...], w1_ref[...]) + b1_ref[...]))
    h = jnp.sin(sf_ref[1:2, :] * (_dot3(h, w2_ref[...]) + b2_ref[...]))
    o_ref[...] = h


def hyena_hidden(L, f_w1, f_b1, f_w2, f_b2, sin_freq):
    emb, hid = f_w1.shape
    bands = (emb - 1) // 2
    t = jnp.linspace(0.0, 1.0, L, dtype=F32)[:, None]
    w = (2.0 * math.pi / L) * jnp.arange(L, dtype=F32)[:, None]
    f = jnp.linspace(1e-4, bands - 1, bands, dtype=F32)[None]
    z = jnp.concatenate([t, jnp.cos(f * w), -jnp.sin(f * w)], -1)
    embp = -(-emb // LANES) * LANES
    z = jnp.pad(z, ((0, 0), (0, embp - emb)))
    w1 = jnp.pad(f_w1, ((0, embp - emb), (0, 0)))
    ins = [z, w1, f_b1.reshape(1, hid), f_w2, f_b2.reshape(1, hid), sin_freq]
    return pl.pallas_call(
        _hy_hidden_body,
        out_shape=jax.ShapeDtypeStruct((L, hid), F32),
        name="hyena_hidden",
    )(*ins)


def _hy_filter_body(*refs, plan, n_order):
    it = iter(refs)
    h_ref = next(it)
    w3_refs = [next(it) for _ in range(2 * n_order)]
    dl_ref = next(it)
    consts = {nme: next(it) for nme in _CONST_NAMES}
    g_ref = next(it)
    x_sc = next(it)
    spec_a = next(it)
    spec_b = next(it)
    L, N2, K1 = plan.L, plan.N2, plan.K1
    t = lax.broadcasted_iota(jnp.int32, (L, LANES), 0).astype(F32) * (1.0 / (L - 1))
    win = jnp.exp(-t * dl_ref[...])
    row0 = lax.broadcasted_iota(jnp.int32, (L, LANES), 0) == 0
    h = h_ref[...]
    for n in range(n_order):
        fwd = _dot3(h, w3_refs[2 * n][...]) * win
        bwd = jnp.where(row0, 0.0, _dot3(h, w3_refs[2 * n + 1][...]) * win)
        ss = jnp.sum(fwd * fwd, 0, keepdims=True) + jnp.sum(bwd * bwd, 0, keepdims=True)
        nu = lax.rsqrt(ss + 1e-12) * (1.0 / plan.N)
        x_sc[...] = fwd * nu
        _fft_fwd(plan, x_sc, consts, spec_a)
        x_sc[...] = bwd * nu
        _fft_fwd(plan, x_sc, consts, spec_b)

        def body(k, carry):
            base = pl.multiple_of(2 * N2 * k, 2 * N2)
            g_ref[n, pl.ds(base, N2), :] = spec_a[pl.ds(base, N2), :] + spec_b[pl.ds(base, N2), :]
            g_ref[n, pl.ds(base + N2, N2), :] = spec_a[pl.ds(base + N2, N2), :] - spec_b[pl.ds(base + N2, N2), :]
            return carry

        lax.fori_loop(0, K1, body, 0)


def hyena_filter_spectra(plan, consts, hidden, f_w3, d_model, n_order):
    L = plan.L
    hid = hidden.shape[1]
    ncb = d_model // LANES
    deltas = jnp.abs(jnp.linspace(math.log(HY_DECAY_TARGET) / HY_DECAY_PCT_SHORT,
                                  math.log(HY_DECAY_TARGET) / HY_DECAY_PCT_LONG, d_model, dtype=F32)).reshape(1, d_model)
    ins = [hidden]
    specs = [pl.BlockSpec((L, hid), lambda c: (0, 0))]
    for n in range(n_order):
        for dr in range(2):
            ins.append(f_w3)
            specs.append(pl.BlockSpec((hid, LANES), lambda c, n=n, dr=dr: (0, (2 * n + dr) * ncb + c)))
    ins.append(deltas)
    specs.append(pl.BlockSpec((1, LANES), lambda c: (0, c)))
    ci, cs = _const_specs(consts, 1)
    ins += ci
    specs += cs
    return pl.pallas_call(
        functools.partial(_hy_filter_body, plan=plan, n_order=n_order),
        grid=(ncb,),
        in_specs=specs,
        out_specs=pl.BlockSpec((n_order, plan.spec_rows, LANES), lambda c: (0, 0, c)),
        out_shape=jax.ShapeDtypeStruct((n_order, plan.spec_rows, d_model), F32),
        scratch_shapes=[pltpu.VMEM((L, LANES), F32), pltpu.VMEM((plan.spec_rows, LANES), F32),
                        pltpu.VMEM((plan.spec_rows, LANES), F32)],
        compiler_params=_cparams(("arbitrary",)),
        name="hyena_filter",
    )(*ins)


def _conv3(p, w_ref, b_ref, L):
    rows = lax.broadcasted_iota(jnp.int32, p.shape, 0)
    prev = jnp.where(rows == 0, 0.0, pltpu.roll(p, 1, 0))
    nxt = jnp.where(rows == L - 1, 0.0, pltpu.roll(p, L - 1, 0))
    return prev * w_ref[0:1, :] + p * w_ref[1:2, :] + nxt * w_ref[2:3, :] + b_ref[...]


def _hy_conv_body(*refs, plan, order, conv_u):
    it = iter(refs)
    u_ref, gate_ref = next(it), next(it)
    if conv_u:
        cwu_ref, cbu_ref = next(it), next(it)
    cwg_ref, cbg_ref = next(it), next(it)
    skip_ref = next(it)
    g_ref = next(it)
    consts = {nme: next(it) for nme in _CONST_NAMES}
    o_ref = next(it)
    x_sc, y_sc, spec_sc = next(it), next(it), next(it)
    L = plan.L
    u = u_ref[...]
    if conv_u:
        u = _conv3(u, cwu_ref, cbu_ref, L)
    x_sc[...] = u
    _fft_conv_apply(plan, x_sc, g_ref.at[0], consts, spec_sc, y_sc)
    gate = _conv3(gate_ref[...], cwg_ref, cbg_ref, L)
    o_ref[...] = (gate * (y_sc[...] + x_sc[...] * skip_ref[order:order + 1, :])).astype(o_ref.dtype)


def hyena_conv(plan, consts, u_arr, u_cb0, gate_arr, gate_cb0, conv_w, conv_b, skip, spectra, *,
               order, conv_u, u_row_block0, gate_row_block0, batch, d_model, out_dtype):
    L = plan.L
    ncb = d_model // LANES
    ins = [u_arr, gate_arr]
    specs = [pl.BlockSpec((L, LANES), lambda c, b: (u_row_block0 + b, u_cb0 + c)),
             pl.BlockSpec((L, LANES), lambda c, b: (gate_row_block0 + b, gate_cb0 + c))]
    if conv_u:
        ins += [conv_w, conv_b]
        specs += [pl.BlockSpec((3, LANES), lambda c, b: (0, u_cb0 + c)),
                  pl.BlockSpec((1, LANES), lambda c, b: (0, u_cb0 + c))]
    ins += [conv_w, conv_b, skip, spectra]
    specs += [pl.BlockSpec((3, LANES), lambda c, b: (0, gate_cb0 + c)),
              pl.BlockSpec((1, LANES), lambda c, b: (0, gate_cb0 + c)),
              pl.BlockSpec((skip.shape[0], LANES), lambda c, b: (0, c)),
              pl.BlockSpec((1, plan.spec_rows, LANES), lambda c, b: (order, 0, c))]
    ci, cs = _const_specs(consts, 2)
    ins += ci
    specs += cs
    return pl.pallas_call(
        functools.partial(_hy_conv_body, plan=plan, order=order, conv_u=conv_u),
        grid=(ncb, batch),
        in_specs=specs,
        out_specs=pl.BlockSpec((L, LANES), lambda c, b: (b, c)),
        out_shape=jax.ShapeDtypeStruct((batch * L, d_model), out_dtype),
        scratch_shapes=[pltpu.VMEM((L, LANES), F32), pltpu.VMEM((L, LANES), F32),
                        pltpu.VMEM((plan.spec_rows, LANES), F32)],
        compiler_params=_cparams(("parallel", "arbitrary")),
        name=f"hyena_conv{order}_L{L}",
    )(*ins)


def hyena_mixer(hmix, p, *, batch, seq, ctx_len, d_model, want_ctx):
    n_order = p["skip"].shape[0]
    pr = mm(hmix, p["w_in"], bias=p["b_in"], name="hy_in", rows=(batch * seq + (batch * ctx_len if want_ctx else 0)))
    conv_b = p["conv_b"].reshape(1, -1)
    ncb = d_model // LANES
    outs = []
    streams = [(seq, 0)]
    if want_ctx:
        streams.append((ctx_len, (batch * seq) // ctx_len))
    for L, rb0 in streams:
        plan = _FftPlan(L)
        consts = plan.constants()
        hidden = hyena_hidden(L, p["f_w1"], p["f_b1"], p["f_w2"], p["f_b2"], p["sin_freq"])
        spectra = hyena_filter_spectra(plan, consts, hidden, p["f_w3"], d_model, n_order)
        z1 = hyena_conv(plan, consts, pr, 0, pr, ncb, p["conv_w"], conv_b, p["skip"], spectra,
                        order=0, conv_u=True, u_row_block0=rb0, gate_row_block0=rb0, batch=batch, d_model=d_model,
                        out_dtype=F32)
        z2 = hyena_conv(plan, consts, z1, 0, pr, 2 * ncb, p["conv_w"], conv_b, p["skip"], spectra,
                        order=1, conv_u=False, u_row_block0=0, gate_row_block0=rb0, batch=batch, d_model=d_model,
                        out_dtype=BF16)
        outs.append(z2)
    return outs[0] if len(outs) == 1 else jnp.concatenate(outs, 0)


def _headsum(x, e_ref):
    hi, lo = _split2(x)
    return _dot(hi, e_ref[...]) + _dot(lo, e_ref[...])


def _rw_mix_body(h_ref, hp_ref, hn_ref, mix_ref, o_ref, *, tm, lat_tiles, tiles_per_seq):
    i = pl.program_id(0)
    is_ctx = i >= lat_tiles
    first = jnp.logical_or(is_ctx, i % tiles_per_seq == 0)
    last = jnp.logical_or(is_ctx, i % tiles_per_seq == tiles_per_seq - 1)
    h = h_ref[...]
    rows = lax.broadcasted_iota(jnp.int32, h.shape, 0)
    pedge = jnp.where(first, 0.0, hp_ref[SUBLANES - 1:SUBLANES, :])
    nedge = jnp.where(last, 0.0, hn_ref[0:1, :])
    prev = jnp.where(rows == 0, pedge, pltpu.roll(h, 1, 0))
    nxt = jnp.where(rows == tm - 1, nedge, pltpu.roll(h, tm - 1, 0))
    dp = prev - h
    dn = nxt - h
    for n in range(o_ref.shape[0]):
        o_ref[n] = (h + dp * mix_ref[0, n:n + 1, :] + dn * mix_ref[1, n:n + 1, :]).astype(o_ref.dtype)


def rwkv_mix(h, mix, *, batch, seq, ctx_len):
    m, d = h.shape
    tm = ctx_len
    n_shift = mix.shape[1]
    nblk = m // SUBLANES
    per = tm // SUBLANES
    body = functools.partial(_rw_mix_body, tm=tm, lat_tiles=(batch * seq) // tm, tiles_per_seq=seq // tm)
    return pl.pallas_call(
        body,
        grid=(m // tm,),
        in_specs=[
            pl.BlockSpec((tm, d), lambda i: (i, 0)),
            pl.BlockSpec((SUBLANES, d), lambda i: (jnp.maximum(i * per - 1, 0), 0)),
            pl.BlockSpec((SUBLANES, d), lambda i: (jnp.minimum((i + 1) * per, nblk - 1), 0)),
            pl.BlockSpec((2, n_shift, d), lambda i: (0, 0, 0)),
        ],
        out_specs=pl.BlockSpec((n_shift, tm, d), lambda i: (0, i, 0)),
        out_shape=jax.ShapeDtypeStruct((n_shift, m, d), BF16),
        compiler_params=_cparams(("parallel",)),
        name="rwkv_mix",
    )(h, h, h, mix)


def _rw_prep_body(k_ref, lw0_ref, lw1_ref, al0_ref, al1_ref, w0_ref, a0_ref, kk_ref, ka_ref, e_ref,
                  olw0, olw1, okd0, okd1, obb0, obb1, oaa):
    k = k_ref[...]
    kk0 = k * kk_ref[...]
    kk = kk0 * lax.rsqrt(_headsum(kk0 * kk0, e_ref) + 1e-12)
    oaa[...] = -kk
    for d, (lw_ref, al_ref, olw, okd, obb) in enumerate(((lw0_ref, al0_ref, olw0, okd0, obb0),
                                                         (lw1_ref, al1_ref, olw1, okd1, obb1))):
        z = -(w0_ref[d:d + 1, :] + lw_ref[...])
        softplus = jnp.maximum(z, 0.0) + jnp.log(1.0 + jnp.exp(-jnp.abs(z)))
        log_w = -softplus - 0.5
        olw[...] = -jnp.exp(log_w)
        a = jax.nn.sigmoid(a0_ref[d:d + 1, :] + al_ref[...])
        okd[...] = k * (1.0 + (a - 1.0) * ka_ref[...])
        obb[...] = kk * a


def rwkv_prep(k, lw0, lw1, al0, al1, w0, a0, k_k, k_a, e128, tm=1088):
    m, d = k.shape
    blk = pl.BlockSpec((tm, LANES), lambda i, c: (i, c))
    vec2 = pl.BlockSpec((2, LANES), lambda i, c: (0, c))
    vec1 = pl.BlockSpec((1, LANES), lambda i, c: (0, c))
    sh = jax.ShapeDtypeStruct((m, d), F32)
    return pl.pallas_call(
        _rw_prep_body,
        grid=(m // tm, d // LANES),
        in_specs=[blk] * 5 + [vec2, vec2, vec1, vec1, pl.BlockSpec((LANES, LANES), lambda i, c: (0, 0))],
        out_specs=[blk] * 7,
        out_shape=[sh] * 7,
        compiler_params=_cparams(("parallel", "arbitrary")),
        name="rwkv_prep",
    )(k, lw0, lw1, al0, al1, w0, a0, k_k.reshape(1, d), k_a.reshape(1, d), e128)


def _unit_tri_inverse(n, row, col, tri_strict, size):
    same = lambda s: (row >> int(math.log2(s))) == (col >> int(math.log2(s)))
    eye = (row == col).astype(F32)
    n4 = jnp.where(same(4), n, 0.0)
    t = (eye + n4) + _dot3(eye + n4, _dot3(n4, n4))
    s = 4
    while s < size:
        off = jnp.logical_and(jnp.logical_and(same(2 * s), jnp.logical_not(same(s))), tri_strict)
        t = t + _dot3(_dot3(t, jnp.where(off, n, 0.0)), t)
        s *= 2
    return t


def _wkv_body(r_ref, lw_ref, k_ref, v_ref, a_ref, b_ref, o_ref, s_sc, *, chunk, n_heads, reverse):
    c = pl.program_id(2)

    @pl.when(c == 0)
    def _():
        s_sc[...] = jnp.zeros_like(s_sc)

    C = chunk
    row = lax.broadcasted_iota(jnp.int32, (C, C), 0)
    col = lax.broadcasted_iota(jnp.int32, (C, C), 1)
    if reverse:
        tri_incl = col >= row
        tri_strict = col > row
        last = 0
    else:
        tri_incl = col <= row
        tri_strict = col < row
        last = C - 1
    cum_m = tri_incl.astype(BF16)
    outs = []
    for h in range(n_heads):
        sl = slice(h * RW_HEAD, (h + 1) * RW_HEAD)
        lw = lw_ref[:, sl]
        l1, l2, l3 = _split3(lw)
        cum = _dot(cum_m, l1) + _dot(cum_m, l2) + _dot(cum_m, l3)
        p_inc = jnp.exp(cum)
        p_exc = jnp.exp(cum - lw)
        p_inv = jnp.exp(-cum)
        x1 = jnp.concatenate([a_ref[:, sl] * p_exc, r_ref[:, sl] * p_inc], axis=0)
        x2 = jnp.concatenate([b_ref[:, sl] * p_inv, k_ref[:, sl] * p_inv], axis=0)
        v = v_ref[:, sl]
        s0 = s_sc[h]
        mfull = _dot3(x1, x2, _dot_nt)
        n_ab = jnp.where(tri_strict, mfull[:C, :C], 0.0)
        a_ak = jnp.where(tri_strict, mfull[:C, C:], 0.0)
        a_rb = jnp.where(tri_incl, mfull[C:, :C], 0.0)
        a_rk = jnp.where(tri_incl, mfull[C:, C:], 0.0)
        w0 = _dot3(x1, s0, _dot_nt)
        u = _dot3(_unit_tri_inverse(n_ab, row, col, tri_strict, C), w0[:C] + _dot3(a_ak, v))
        uv = jnp.concatenate([u, v], axis=0)
        y = w0[C:] + _dot3(jnp.concatenate([a_rb, a_rk], axis=1), uv)
        s_new = (s0 + _dot3(uv, x2, _dot_tn)) * p_inc[last:last + 1, :]
        s_sc[h] = s_new
        outs.append(y)
    o_ref[...] = jnp.concatenate(outs, axis=1)


def wkv7(r, lw, kd, v, aa, bb, *, reverse, batch, seq, ctx_len, chunk=128, heads_per_step=8):
    m, d = r.shape
    heads_per_step = min(heads_per_step, d // RW_HEAD)
    wl = heads_per_step * RW_HEAD
    n_ctx = ctx_len // chunk
    n_lat = seq // chunk
    n_chunks = n_ctx + n_lat
    ctx0 = (batch * seq) // chunk

    def row_block(b, c):
        if reverse:
            return jnp.where(c < n_ctx, ctx0 + b * n_ctx + (n_ctx - 1 - c), b * n_lat + (n_lat - 1 - (c - n_ctx)))
        return jnp.where(c < n_ctx, ctx0 + b * n_ctx + c, b * n_lat + (c - n_ctx))

    blk = pl.BlockSpec((chunk, wl), lambda b, g, c: (row_block(b, c), g))
    body = functools.partial(_wkv_body, chunk=chunk, n_heads=heads_per_step, reverse=reverse)
    return pl.pallas_call(
        body,
        grid=(batch, d // wl, n_chunks),
        in_specs=[blk] * 6,
        out_specs=blk,
        out_shape=jax.ShapeDtypeStruct((m, d), F32),
        scratch_shapes=[pltpu.VMEM((heads_per_step, RW_HEAD, RW_HEAD), F32)],
        compiler_params=_cparams(("parallel", "parallel", "arbitrary")),
        name="wkv7_rev" if reverse else "wkv7_fwd",
    )(r, lw, kd, v, aa, bb)


def _rw_readout_body(y0_ref, y1_ref, r_ref, v_ref, kd0_ref, kd1_ref, g_ref, rk_ref, lnw_ref, lnb_ref, e_ref, o_ref,
                     *, gn_eps):
    y = y0_ref[...] + y1_ref[...]
    inv = 1.0 / RW_HEAD
    mu = _headsum(y, e_ref) * inv
    dlt = y - mu
    var = _headsum(dlt * dlt, e_ref) * inv
    yn = dlt * lax.rsqrt(var + gn_eps) * lnw_ref[...] + lnb_ref[...]
    rk = r_ref[...] * rk_ref[...]
    bonus = (_headsum(rk * kd0_ref[...], e_ref) + _headsum(rk * kd1_ref[...], e_ref)) * v_ref[...]
    o_ref[...] = ((yn + bonus) * g_ref[...]).astype(o_ref.dtype)


def rwkv_readout(y0, y1, r, v, kd0, kd1, g, r_k, ln_w, ln_b, e128, gn_eps, rows, tm=512):
    d = r.shape[1]
    blk = pl.BlockSpec((tm, LANES), lambda i, c: (i, c))
    vec1 = pl.BlockSpec((1, LANES), lambda i, c: (0, c))
    return pl.pallas_call(
        functools.partial(_rw_readout_body, gn_eps=gn_eps),
        grid=(rows // tm, d // LANES),
        in_specs=[blk] * 7 + [vec1] * 3 + [pl.BlockSpec((LANES, LANES), lambda i, c: (0, 0))],
        out_specs=blk,
        out_shape=jax.ShapeDtypeStruct((rows, d), BF16),
        compiler_params=_cparams(("parallel", "arbitrary")),
        name="rwkv_readout",
    )(y0, y1, r, v, kd0, kd1, g, r_k.reshape(1, d), ln_w.reshape(1, d), ln_b.reshape(1, d), e128)


def rwkv_mixer(hmix, p, *, batch, seq, ctx_len, d_model):
    xs = rwkv_mix(hmix, p["mix"], batch=batch, seq=seq, ctx_len=ctx_len)
    m = hmix.shape[0]
    xs2 = xs.reshape(xs.shape[0] * m, d_model)
    tm = 1088 if m % 1088 == 0 else 512

    def proj(n, w, **kw):
        return mm(xs2, w, tm=tm, rows=m, x_row_block0=n * (m // tm), **kw)

    r = proj(0, p["w_r"], name="rw_r")
    k = proj(2, p["w_k"], name="rw_k")
    v = proj(3, p["w_v"], name="rw_v")
    g = mm(proj(5, p["g1"], act="sigmoid", out_dtype=BF16, name="rw_g1"), p["g2"], tm=tm, name="rw_g2")
    lws, als = [], []
    for d in range(2):
        tw = proj(1, p["w1"][d], act="tanh", out_dtype=BF16, name="rw_w1")
        lws.append(mm(tw, p["w2"][d], tm=tm, name="rw_w2"))
        ah = proj(4, p["a1"][d], out_dtype=BF16, name="rw_a1")
        als.append(mm(ah, p["a2"][d], tm=tm, name="rw_a2"))
    e128 = jnp.asarray(np.kron(np.eye(LANES // RW_HEAD), np.ones((RW_HEAD, RW_HEAD))), BF16)
    lw0, lw1, kd0, kd1, bb0, bb1, aa = rwkv_prep(k, lws[0], lws[1], als[0], als[1], p["w0"], p["a0"],
                                                 p["k_k"], p["k_a"], e128, tm=tm)
    kw = dict(batch=batch, seq=seq, ctx_len=ctx_len)
    y0 = wkv7(r, lw0, kd0, v, aa, bb0, reverse=False, **kw)
    y1 = wkv7(r, lw1, kd1, v, aa, bb1, reverse=True, **kw)
    return rwkv_readout(y0, y1, r, v, kd0, kd1, g, p["r_k"], p["ln_w"], p["ln_b"], e128,
                        1e-5 * RW_HEAD, rows=m)


def mla_mixer(hmix, p, *, batch, seq, ctx_len, want_ctx):
    m = hmix.shape[0]
    tm = 1088 if m % 1088 == 0 else 512
    scale = (MLA_NOPE + MLA_ROPE) ** -0.5
    down = mm(hmix, p["w_down_ext"], tm=tm, name="mla_down")
    kq = MLA_Q_RANK
    q_raw = mm(down, p["w_uq_ext"], norm_g=p["q_norm_g"], x_col_block=0, tm=tm, name="mla_uq")
    kv = mm(down, p["w_ukv"], norm_g=p["kv_norm_g"], x_col_block=1, tm=tm, out_dtype=BF16, name="mla_ukv")
    cos, sin = _rope_tables(seq, ctx_len, batch, MLA_ROPE)
    zeros = jnp.zeros_like(cos)
    cos_p = jnp.concatenate([cos, zeros], -1)
    sin_p = jnp.concatenate([sin, zeros], -1)
    q = mla_rope(q_raw, 0, MLA_HEADS, cos_p, sin_p, scale, True, tm=tm)
    kpe = mla_rope(down, (kq + MLA_KV_RANK) // LANES, 1, cos_p, sin_p, 1.0, False, tm=tm)
    q_parts = [(q, lambda h: 2 * h), (q, lambda h: 2 * h + 1)]
    k_parts = [(kv, lambda h: 2 * h), (kpe, lambda h: 0)]
    kw = dict(n_heads=MLA_HEADS, batch=batch, seq=seq, ctx_len=ctx_len)
    o_lat = attention(q_parts, k_parts, kv, lambda h: 2 * h + 1, lat_queries=True, **kw)
    if not want_ctx:
        return o_lat
    o_ctx = attention(q_parts, k_parts, kv, lambda h: 2 * h + 1, lat_queries=False, **kw)
    return jnp.concatenate([o_lat, o_ctx], 0)


def gqa_mixer(hmix, p, *, batch, seq, ctx_len, want_ctx):
    m = hmix.shape[0]
    tm = 1088 if m % 1088 == 0 else 512
    scale = GQA_HEAD ** -0.5
    raw = mm(hmix, p["w_qkv_ext"], tm=tm, name="gqa_qkv")
    cos, sin = _rope_tables(seq, ctx_len, batch, GQA_HEAD)
    nq, nk = GQA_HEADS, GQA_KV_HEADS
    q = gqa_prep(raw, 0, nq, nq, cos, sin, p["q_norm_g"], p["q_norm_g_rot"], scale, tm=tm)
    k = gqa_prep(raw, 2 * nq, 2 * nq + nk, nk, cos, sin, p["k_norm_g"], p["k_norm_g_rot"], 1.0, tm=tm)
    vcb0 = 2 * nq + 2 * nk
    vb = mm_cast(raw, vcb0, nk, tm=tm)
    grp = nq // nk
    kw = dict(n_heads=nq, batch=batch, seq=seq, ctx_len=ctx_len)
    o_lat = attention([(q, lambda h: h)], [(k, lambda h: h // grp)], vb, lambda h: h // grp, lat_queries=True, **kw)
    if not want_ctx:
        return o_lat
    o_ctx = attention([(q, lambda h: h)], [(k, lambda h: h // grp)], vb, lambda h: h // grp, lat_queries=False, **kw)
    return jnp.concatenate([o_lat, o_ctx], 0)


def _cast_body(x_ref, o_ref):
    o_ref[...] = x_ref[...].astype(o_ref.dtype)


def mm_cast(raw, cb0, nblocks, tm=512):
    m = raw.shape[0]
    return pl.pallas_call(
        _cast_body,
        grid=(m // tm, nblocks),
        in_specs=[pl.BlockSpec((tm, LANES), lambda i, c: (i, cb0 + c))],
        out_specs=pl.BlockSpec((tm, LANES), lambda i, c: (i, c)),
        out_shape=jax.ShapeDtypeStruct((m, nblocks * LANES), BF16),
        compiler_params=_cparams(("parallel", "arbitrary")),
        name="cast_cols",
    )(raw)


def _rot_gain(g):
    q = g.shape[-1] // 4
    g4 = g.reshape(2, 2, q)
    return jnp.concatenate([g4[:, 1:2], g4[:, 0:1]], axis=1).reshape(g.shape)


def kernel(x, c, ctx, c_ctx, ada_w, ada_b, norm_g, ffn_w_gu, ffn_w_down, hy_w_in, hy_b_in, hy_conv_w, hy_conv_b, hy_f_w1, hy_f_b1, hy_f_w2, hy_f_b2, hy_f_w3, hy_sin_freq, hy_skip, hy_w_out, hy_b_out, rw_mix, rw_w_rkv, rw_w0, rw_w1, rw_w2, rw_a0, rw_a1, rw_a2, rw_g1, rw_g2, rw_k_k, rw_k_a, rw_r_k, rw_ln_w, rw_ln_b, rw_w_o, mla_w_down, mla_q_norm_g, mla_w_uq, mla_kv_norm_g, mla_w_ukv, mla_w_o, gqa_w_qkv, gqa_q_norm_g, gqa_k_norm_g, gqa_w_o, final_g):
    batch, seq, d = x.shape
    ctx_len = ctx.shape[1]
    depth = ada_w.shape[0]
    n_mix = 4
    rows_lat = batch * seq
    rows_all = rows_lat + batch * ctx_len
    bf = lambda a: a.astype(BF16)

    cvec = jnp.concatenate([c, c_ctx[None], jnp.zeros((8 - batch - 1, d), F32)], 0)
    mods_all = adaln_all(cvec, ada_w, ada_b)[:, :batch + 1].reshape(depth, batch + 1, N_MOD, d)

    h = jnp.concatenate([x.reshape(rows_lat, d), ctx.reshape(batch * ctx_len, d)], 0)

    for i in range(depth):
        mi, j = i % n_mix, i // n_mix
        ctx_out = any(jj % n_mix != 0 for jj in range(i + 1, depth))
        mods = mods_all[i]
        gate5 = mods[:, 5:6, :]
        mix_dtype = F32 if mi == 1 else BF16
        h, hmix = ffn_half(h, mods, norm_g[i], bf(ffn_w_gu[i, 0]), bf(ffn_w_down[i, 0]), s=0, rows=rows_all,
                           seq_rows=seq, next_g=norm_g[i, 1], next_s=1, next_dtype=mix_dtype)
        rows_out = rows_all if ctx_out else rows_lat
        kw = dict(batch=batch, seq=seq, ctx_len=ctx_len)
        if mi == 0:
            p = dict(w_in=bf(hy_w_in[j]), b_in=hy_b_in[j], conv_w=hy_conv_w[j], conv_b=hy_conv_b[j],
                     f_w1=hy_f_w1[j], f_b1=hy_f_b1[j], f_w2=hy_f_w2[j], f_b2=hy_f_b2[j], f_w3=hy_f_w3[j],
                     sin_freq=hy_sin_freq[j], skip=hy_skip[j])
            z = hyena_mixer(hmix, p, d_model=d, want_ctx=ctx_out, **kw)
            w_o, b_o = bf(hy_w_out[j]), hy_b_out[j]
        elif mi == 1:
            pad = lambda w, ax: jnp.pad(w, [(0, RW_LORA_PAD - w.shape[a]) if a == ax else (0, 0) for a in range(w.ndim)])
            p = dict(mix=rw_mix[j], w_r=bf(rw_w_rkv[j, 0]), w_k=bf(rw_w_rkv[j, 1]), w_v=bf(rw_w_rkv[j, 2]),
                     g1=bf(rw_g1[j]), g2=bf(rw_g2[j]),
                     w1=bf(pad(rw_w1[j], 2)), w2=bf(pad(rw_w2[j], 1)), a1=bf(pad(rw_a1[j], 2)), a2=bf(pad(rw_a2[j], 1)),
                     w0=rw_w0[j], a0=rw_a0[j], k_k=rw_k_k[j], k_a=rw_k_a[j], r_k=rw_r_k[j].reshape(-1),
                     ln_w=rw_ln_w[j], ln_b=rw_ln_b[j])
            z = rwkv_mixer(hmix, p, d_model=d, **kw)
            w_o, b_o = bf(rw_w_o[j]), None
        elif mi == 2:
            wd = mla_w_down[j]
            pe = wd[:, MLA_Q_RANK + MLA_KV_RANK:]
            wuq = mla_w_uq[j].reshape(MLA_Q_RANK, MLA_HEADS, MLA_NOPE + MLA_ROPE)
            wq_pe = wuq[..., MLA_NOPE:]
            wuq_ext = jnp.concatenate([wuq[..., :MLA_NOPE], wq_pe,
                                       _rot_cols(wq_pe.reshape(MLA_Q_RANK, -1), MLA_ROPE).reshape(wq_pe.shape)], -1)
            p = dict(w_down_ext=bf(jnp.concatenate([wd, _rot_cols(pe, MLA_ROPE)], 1)),
                     w_uq_ext=bf(wuq_ext.reshape(MLA_Q_RANK, -1)), w_ukv=bf(mla_w_ukv[j]),
                     q_norm_g=mla_q_norm_g[j], kv_norm_g=mla_kv_norm_g[j])
            z = mla_mixer(hmix, p, want_ctx=ctx_out, **kw)
            w_o, b_o = bf(mla_w_o[j]), None
        else:
            w = gqa_w_qkv[j]
            nqc = GQA_HEADS * GQA_HEAD
            nkc = GQA_KV_HEADS * GQA_HEAD
            wq, wk, wv = w[:, :nqc], w[:, nqc:nqc + nkc], w[:, nqc + nkc:]
            p = dict(w_qkv_ext=bf(jnp.concatenate([wq, _rot_cols(wq, GQA_HEAD), wk, _rot_cols(wk, GQA_HEAD), wv], 1)),
                     q_norm_g=gqa_q_norm_g[j], q_norm_g_rot=_rot_gain(gqa_q_norm_g[j]),
                     k_norm_g=gqa_k_norm_g[j], k_norm_g_rot=_rot_gain(gqa_k_norm_g[j]))
            z = gqa_mixer(hmix, p, want_ctx=ctx_out, **kw)
            w_o, b_o = bf(gqa_w_o[j]), None
        h = mm(z, w_o, bias=b_o, res=h, gate=gate5, seq_rows=seq, rows=rows_out, name="mix_out")
        last = i == depth - 1
        if last:
            h, out = ffn_half(h, mods, norm_g[i], bf(ffn_w_gu[i, 1]), bf(ffn_w_down[i, 1]), s=2, rows=rows_out,
                              seq_rows=seq, next_g=final_g, next_s=None, next_dtype=F32)
        else:
            h = ffn_half(h, mods, norm_g[i], bf(ffn_w_gu[i, 1]), bf(ffn_w_down[i, 1]), s=2, rows=rows_out,
                         seq_rows=seq)
    return out[:rows_lat].reshape(batch, seq, d)
```

```python
import functools
import math

import numpy as np
import jax
import jax.numpy as jnp
from jax import lax
from jax.experimental import pallas as pl
from jax.experimental.pallas import tpu as pltpu

F32 = jnp.float32
BF16 = jnp.bfloat16

VMEM_CAP_BYTES = 56 * 1024 * 1024
LANES = 128
SUBLANES = 8

NORM_EPS = 1e-6
N_MOD = 9
GRID_W = 64
ROPE_THETA = 10000.0
HY_DECAY_TARGET = 1e-2
HY_DECAY_PCT_SHORT = 0.3
HY_DECAY_PCT_LONG = 1.5
RW_HEAD = 64
RW_LORA_PAD = 128
GQA_HEAD = 128
GQA_HEADS = 16
GQA_KV_HEADS = 8
MLA_HEADS = 16
MLA_NOPE = 128
MLA_ROPE = 64
MLA_V = 128
MLA_Q_RANK = 512
MLA_KV_RANK = 512


def _cparams(sem):
    return pltpu.CompilerParams(dimension_semantics=sem, vmem_limit_bytes=VMEM_CAP_BYTES)


def _dot(a, b):
    return jnp.dot(a, b, preferred_element_type=F32)


def _dot_nt(a, b):
    return lax.dot_general(a, b, (((1,), (1,)), ((), ())), preferred_element_type=F32)


def _dot_tn(a, b):
    return lax.dot_general(a, b, (((0,), (0,)), ((), ())), preferred_element_type=F32)


def _split2(x):
    hi = x.astype(BF16)
    lo = (x - hi.astype(F32)).astype(BF16)
    return hi, lo


def _split3(x):
    hi = x.astype(BF16)
    r1 = x - hi.astype(F32)
    mid = r1.astype(BF16)
    lo = (r1 - mid.astype(F32)).astype(BF16)
    return hi, mid, lo


def _dot3(a, b, f=_dot):
    ah, al = _split2(a)
    bh, bl = _split2(b)
    return f(ah, bh) + f(ah, bl) + f(al, bh)


def _dotc3(mh, ml, d):
    dh, dl = _split2(d)
    return _dot(mh, dh) + _dot(ml, dh) + _dot(mh, dl)


def _rms(x, g):
    return x * lax.rsqrt(jnp.mean(x * x, -1, keepdims=True) + NORM_EPS) * g


def _pick(n, cands):
    for c in cands:
        if n % c == 0:
            return c
    raise ValueError(f"no tile for {n}")


def _adaln_body(c_ref, w_ref, b_ref, o_ref):
    c = c_ref[...]
    s = c * jax.nn.sigmoid(c)
    o_ref[0] = _dot(s.astype(BF16), w_ref[0].astype(BF16)) + b_ref[0]


def adaln_all(cvec8, ada_w, ada_b):
    depth, d, n = ada_w.shape
    tn = _pick(n, (2048, 1024, 512, 256, 128))
    return pl.pallas_call(
        _adaln_body,
        grid=(depth, n // tn),
        in_specs=[
            pl.BlockSpec((8, d), lambda l, j: (0, 0)),
            pl.BlockSpec((1, d, tn), lambda l, j: (l, 0, j)),
            pl.BlockSpec((1, 1, tn), lambda l, j: (l, 0, j)),
        ],
        out_specs=pl.BlockSpec((1, 8, tn), lambda l, j: (l, 0, j)),
        out_shape=jax.ShapeDtypeStruct((depth, 8, n), F32),
        compiler_params=_cparams(("arbitrary", "arbitrary")),
        name="adaln",
    )(cvec8, ada_w, ada_b.reshape(depth, 1, n))


def _mm_body(*refs, has_bias, act, has_norm, has_res):
    it = iter(refs)
    x_ref = next(it)
    w_ref = next(it)
    b_ref = next(it) if has_bias else None
    g_ref = next(it) if has_norm else None
    r_ref = next(it) if has_res else None
    gate_ref = next(it) if has_res else None
    o_ref = next(it)
    x = x_ref[...]
    if has_norm:
        x = _rms(x.astype(F32), g_ref[...])
    acc = _dot(x.astype(BF16), w_ref[...])
    if has_bias:
        acc = acc + b_ref[...]
    if act == "sigmoid":
        acc = jax.nn.sigmoid(acc)
    elif act == "tanh":
        acc = jnp.tanh(acc)
    if has_res:
        acc = r_ref[...] + gate_ref[0] * acc
    o_ref[...] = acc.astype(o_ref.dtype)


def mm(x, w, *, bias=None, act=None, norm_g=None, res=None, gate=None, out_dtype=F32,
       tm=512, rows=None, x_col_block=0, x_row_block0=0, seq_rows=None, name="mm"):
    k, n = w.shape
    m = rows if rows is not None else x.shape[0]
    assert m % tm == 0, (m, tm)
    tn = _pick(n, (512, 384, 256, 128))
    has_res = res is not None
    ins = [x, w]
    specs = [pl.BlockSpec((tm, k), lambda i, j: (i + x_row_block0, x_col_block)),
             pl.BlockSpec((k, tn), lambda i, j: (0, j))]
    if bias is not None:
        ins.append(bias.reshape(1, n).astype(F32))
        specs.append(pl.BlockSpec((1, tn), lambda i, j: (0, j)))
    if norm_g is not None:
        ins.append(norm_g.reshape(1, k).astype(F32))
        specs.append(pl.BlockSpec((1, k), lambda i, j: (0, 0)))
    if has_res:
        ngroups = gate.shape[0]
        per = seq_rows // tm
        ins += [res, gate]
        specs += [pl.BlockSpec((tm, tn), lambda i, j: (i, j)),
                  pl.BlockSpec((1, 1, tn), lambda i, j: (jnp.minimum(i // per, ngroups - 1), 0, j))]
    body = functools.partial(_mm_body, has_bias=bias is not None, act=act,
                             has_norm=norm_g is not None, has_res=has_res)
    return pl.pallas_call(
        body,
        grid=(m // tm, n // tn),
        in_specs=specs,
        out_specs=pl.BlockSpec((tm, tn), lambda i, j: (i, j)),
        out_shape=jax.ShapeDtypeStruct((m, n), out_dtype),
        compiler_params=_cparams(("parallel", "arbitrary")),
        name=name,
    )(*ins)


def _ffn_body(*refs, s, n_f, has_next, next_s):
    it = iter(refs)
    x_ref, mod_ref, g_ref, wa_ref, wb_ref, wd_ref = (next(it) for _ in range(6))
    gn_ref = next(it) if has_next else None
    o_ref = next(it)
    hn_ref = next(it) if has_next else None
    xn_sc = next(it)
    acc_sc = next(it)
    j = pl.program_id(1)

    @pl.when(j == 0)
    def _():
        n = _rms(x_ref[...], g_ref[s:s + 1, :])
        n = n * (1.0 + mod_ref[0, 3 * s + 1:3 * s + 2, :]) + mod_ref[0, 3 * s:3 * s + 1, :]
        xn_sc[...] = n.astype(BF16)
        acc_sc[...] = jnp.zeros_like(acc_sc)

    xn = xn_sc[...]
    a = _dot(xn, wa_ref[...])
    b = _dot(xn, wb_ref[...])
    h = (a * jax.nn.sigmoid(a)) * b
    acc_sc[...] += _dot(h.astype(BF16), wd_ref[...])

    @pl.when(j == n_f - 1)
    def _():
        xnew = x_ref[...] + 0.5 * mod_ref[0, 3 * s + 2:3 * s + 3, :] * acc_sc[...]
        o_ref[...] = xnew
        if has_next:
            hn = _rms(xnew, gn_ref[...])
            if next_s is not None:
                hn = hn * (1.0 + mod_ref[0, 3 * next_s + 1:3 * next_s + 2, :]) + mod_ref[0, 3 * next_s:3 * next_s + 1, :]
            hn_ref[...] = hn.astype(hn_ref.dtype)


def ffn_half(x, mods, norm_g3, w_gu, w_down, *, s, rows, seq_rows, next_g=None, next_s=None,
             next_dtype=F32, tm=512, fc=512):
    d = x.shape[1]
    d_ff = w_down.shape[0]
    n_f = d_ff // fc
    assert d_ff % fc == 0 and rows % tm == 0
    per = seq_rows // tm
    ngroups = mods.shape[0]
    has_next = next_g is not None
    ins = [x, mods, norm_g3, w_gu, w_gu, w_down]
    specs = [
        pl.BlockSpec((tm, d), lambda i, j: (i, 0)),
        pl.BlockSpec((1, N_MOD, d), lambda i, j: (jnp.minimum(i // per, ngroups - 1), 0, 0)),
        pl.BlockSpec((3, d), lambda i, j: (0, 0)),
        pl.BlockSpec((d, fc), lambda i, j: (0, j)),
        pl.BlockSpec((d, fc), lambda i, j: (0, n_f + j)),
        pl.BlockSpec((fc, d), lambda i, j: (j, 0)),
    ]
    out_shape = [jax.ShapeDtypeStruct((rows, d), F32)]
    out_specs = [pl.BlockSpec((tm, d), lambda i, j: (i, 0))]
    if has_next:
        ins.append(next_g.reshape(1, d))
        specs.append(pl.BlockSpec((1, d), lambda i, j: (0, 0)))
        out_shape.append(jax.ShapeDtypeStruct((rows, d), next_dtype))
        out_specs.append(pl.BlockSpec((tm, d), lambda i, j: (i, 0)))
    body = functools.partial(_ffn_body, s=s, n_f=n_f, has_next=has_next, next_s=next_s)
    outs = pl.pallas_call(
        body,
        grid=(rows // tm, n_f),
        in_specs=specs,
        out_specs=out_specs,
        out_shape=out_shape,
        scratch_shapes=[pltpu.VMEM((tm, d), BF16), pltpu.VMEM((tm, d), F32)],
        compiler_params=_cparams(("parallel", "arbitrary")),
        name="ffn_half",
    )(*ins)
    return outs if has_next else outs[0]


def _attn_body(*refs, n_parts, use_lat):
    it = iter(refs)
    qs = [next(it) for _ in range(n_parts)]
    kls = [next(it) for _ in range(n_parts)] if use_lat else []
    kcs = [next(it) for _ in range(n_parts)]
    vl_ref = next(it) if use_lat else None
    vc_ref = next(it)
    o_ref = next(it)
    s_c = _dot_nt(qs[0][...], kcs[0][...])
    for p in range(1, n_parts):
        s_c = s_c + _dot_nt(qs[p][...], kcs[p][...])
    m = jnp.max(s_c, -1, keepdims=True)
    if use_lat:
        s_l = _dot_nt(qs[0][...], kls[0][...])
        for p in range(1, n_parts):
            s_l = s_l + _dot_nt(qs[p][...], kls[p][...])
        m = jnp.maximum(m, jnp.max(s_l, -1, keepdims=True))
    p_c = jnp.exp(s_c - m)
    den = jnp.sum(p_c, -1, keepdims=True)
    o = _dot(p_c.astype(BF16), vc_ref[...])
    if use_lat:
        p_l = jnp.exp(s_l - m)
        den = den + jnp.sum(p_l, -1, keepdims=True)
        o = o + _dot(p_l.astype(BF16), vl_ref[...])
    o_ref[...] = (o / den).astype(o_ref.dtype)


def attention(q_parts, k_parts, v_arr, v_cb, *, n_heads, batch, seq, ctx_len, lat_queries, tq=512):
    n_parts = len(q_parts)
    ctx0 = (batch * seq) // ctx_len
    if lat_queries:
        tq = min(tq, seq)
        nq = seq // tq
        q_row = lambda b, t: b * nq + t
        rows_q = batch * seq
    else:
        tq = ctx_len
        nq = 1
        q_row = lambda b, t: ctx0 + b
        rows_q = batch * ctx_len
    ins, specs = [], []
    for arr, cb in q_parts:
        ins.append(arr)
        specs.append(pl.BlockSpec((tq, LANES), lambda b, h, t, cb=cb: (q_row(b, t), cb(h))))
    if lat_queries:
        for arr, cb in k_parts:
            ins.append(arr)
            specs.append(pl.BlockSpec((seq, LANES), lambda b, h, t, cb=cb: (b, cb(h))))
    for arr, cb in k_parts:
        ins.append(arr)
        specs.append(pl.BlockSpec((ctx_len, LANES), lambda b, h, t, cb=cb: (ctx0 + b, cb(h))))
    if lat_queries:
        ins.append(v_arr)
        specs.append(pl.BlockSpec((seq, LANES), lambda b, h, t: (b, v_cb(h))))
    ins.append(v_arr)
    specs.append(pl.BlockSpec((ctx_len, LANES), lambda b, h, t: (ctx0 + b, v_cb(h))))
    if lat_queries:
        o_row = lambda b, t: b * nq + t
    else:
        o_row = lambda b, t: b
    return pl.pallas_call(
        functools.partial(_attn_body, n_parts=n_parts, use_lat=lat_queries),
        grid=(batch, n_heads, nq),
        in_specs=specs,
        out_specs=pl.BlockSpec((tq, LANES), lambda b, h, t: (o_row(b, t), h)),
        out_shape=jax.ShapeDtypeStruct((rows_q, n_heads * LANES), BF16),
        compiler_params=_cparams(("parallel", "parallel", "arbitrary")),
        name="attention_lat" if lat_queries else "attention_ctx",
    )(*ins)


def _rope_tables(seq, ctx_len, batch, rot_dim):
    half = rot_dim // 4
    inv_freq = ROPE_THETA ** (-jnp.arange(half, dtype=F32) / half)
    t = jnp.arange(seq, dtype=jnp.int32)
    row = (t // GRID_W).astype(F32)
    col = (t % GRID_W).astype(F32)
    ang_r = row[:, None] * inv_freq[None]
    ang_c = col[:, None] * inv_freq[None]
    cos = jnp.concatenate([jnp.cos(ang_r), jnp.cos(ang_r), jnp.cos(ang_c), jnp.cos(ang_c)], -1)
    sin = jnp.concatenate([jnp.sin(ang_r), jnp.sin(ang_r), jnp.sin(ang_c), jnp.sin(ang_c)], -1)
    cos = jnp.concatenate([jnp.tile(cos, (batch, 1)), jnp.ones((batch * ctx_len, rot_dim), F32)], 0)
    sin = jnp.concatenate([jnp.tile(sin, (batch, 1)), jnp.zeros((batch * ctx_len, rot_dim), F32)], 0)
    return cos, sin


def _rot_cols(w, rot_dim):
    lead = w.shape[:-1]
    q = rot_dim // 4
    w5 = w.reshape(lead + (-1, 2, 2, q))
    rot = jnp.concatenate([-w5[..., 1:2, :], w5[..., 0:1, :]], axis=-2)
    return rot.reshape(w.shape)


def _gqa_prep_body(a_ref, ar_ref, cos_ref, sin_ref, g_ref, gr_ref, o_ref, *, scale):
    a = a_ref[...]
    r = lax.rsqrt(jnp.mean(a * a, -1, keepdims=True) + NORM_EPS) * scale
    o = (a * g_ref[...]) * cos_ref[...] + (ar_ref[...] * gr_ref[...]) * sin_ref[...]
    o_ref[...] = (o * r).astype(o_ref.dtype)


def gqa_prep(raw, cb0, cbr0, n_heads, cos, sin, g, g_rot, scale, tm=1088):
    m = raw.shape[0]
    return pl.pallas_call(
        functools.partial(_gqa_prep_body, scale=scale),
        grid=(m // tm, n_heads),
        in_specs=[
            pl.BlockSpec((tm, LANES), lambda i, h: (i, cb0 + h)),
            pl.BlockSpec((tm, LANES), lambda i, h: (i, cbr0 + h)),
            pl.BlockSpec((tm, LANES), lambda i, h: (i, 0)),
            pl.BlockSpec((tm, LANES), lambda i, h: (i, 0)),
            pl.BlockSpec((1, LANES), lambda i, h: (0, 0)),
            pl.BlockSpec((1, LANES), lambda i, h: (0, 0)),
        ],
        out_specs=pl.BlockSpec((tm, LANES), lambda i, h: (i, h)),
        out_shape=jax.ShapeDtypeStruct((m, n_heads * LANES), BF16),
        compiler_params=_cparams(("parallel", "arbitrary")),
        name="gqa_prep",
    )(raw, raw, cos, sin, g.reshape(1, LANES), g_rot.reshape(1, LANES))


def _mla_rope_body(x_ref, c_ref, s_ref, o_ref, *, scale, with_nope):
    if with_nope:
        a = x_ref[:, :LANES]
        b = x_ref[:, LANES:]
    else:
        b = x_ref[...]
    roped = (b * c_ref[...] + pltpu.roll(b, LANES // 2, 1) * s_ref[...]) * scale
    if with_nope:
        o_ref[:, :LANES] = (a * scale).astype(o_ref.dtype)
        o_ref[:, LANES:] = roped.astype(o_ref.dtype)
    else:
        o_ref[...] = roped.astype(o_ref.dtype)


def mla_rope(raw, cb0, n_heads, cos, sin, scale, with_nope, tm=1088):
    m = raw.shape[0]
    w = 2 * LANES if with_nope else LANES
    return pl.pallas_call(
        functools.partial(_mla_rope_body, scale=scale, with_nope=with_nope),
        grid=(m // tm, n_heads),
        in_specs=[
            pl.BlockSpec((tm, w), lambda i, h: (i, cb0 + h)),
            pl.BlockSpec((tm, LANES), lambda i, h: (i, 0)),
            pl.BlockSpec((tm, LANES), lambda i, h: (i, 0)),
        ],
        out_specs=pl.BlockSpec((tm, w), lambda i, h: (i, h)),
        out_shape=jax.ShapeDtypeStruct((m, n_heads * w), BF16),
        compiler_params=_cparams(("parallel", "arbitrary")),
        name="mla_rope",
    )(raw, cos, sin)


class _FftPlan:
    def __init__(self, L):
        self.L = L
        self.N = 2 * L
        self.N2 = 128 if L >= 1024 else 64
        self.N1 = self.N // self.N2
        self.n1_in = self.N1 // 2
        self.K1 = self.N1 // 2 + 1
        self.nq = self.N2 // SUBLANES
        self.spec_rows = self.K1 * 2 * self.N2
        self.kf_cols = max(self.n1_in * SUBLANES, LANES)
        self.kf_rows = 2 * self.K1 * SUBLANES
        self.kb_cols = -(-self.kf_rows // LANES) * LANES
        self.kb_rows = self.n1_in * SUBLANES

    def constants(self):
        N, N1, N2, K1, n1_in = self.N, self.N1, self.N2, self.K1, self.n1_in
        eye = np.eye(SUBLANES)
        k1 = np.arange(K1)[:, None]
        n1 = np.arange(n1_in)[None, :]
        th = 2 * np.pi * k1 * n1 / N1
        kf = np.zeros((self.kf_rows, self.kf_cols))
        kf[:K1 * SUBLANES, :n1_in * SUBLANES] = np.kron(np.cos(th), eye)
        kf[K1 * SUBLANES:, :n1_in * SUBLANES] = np.kron(-np.sin(th), eye)
        c = np.full((K1,), 2.0)
        c[0] = 1.0
        c[-1] = 1.0
        thb = 2 * np.pi * np.arange(n1_in)[:, None] * np.arange(K1)[None, :] / N1
        kb = np.zeros((self.kb_rows, self.kb_cols))
        kb[:, :K1 * SUBLANES] = np.kron(np.cos(thb) * c[None, :], eye)
        kb[:, K1 * SUBLANES:2 * K1 * SUBLANES] = np.kron(-np.sin(thb) * c[None, :], eye)
        a = 2 * np.pi * np.outer(np.arange(N2), np.arange(N2)) / N2
        C, S = np.cos(a), np.sin(a)
        w2f = np.block([[C, S], [-S, C]])
        w2i = np.block([[C, -S], [S, C]])
        tw_ang = 2 * np.pi * np.outer(np.arange(K1), np.arange(N2)) / N
        twr = np.repeat(np.cos(tw_ang).reshape(-1, 1), LANES, axis=1)
        twi = np.repeat(np.sin(tw_ang).reshape(-1, 1), LANES, axis=1)

        def hl(m):
            m32 = jnp.asarray(m, F32)
            hi = m32.astype(BF16)
            lo = (m32 - hi.astype(F32)).astype(BF16)
            return hi, lo

        out = {}
        for name, mat in (("kf", kf), ("kb", kb), ("w2f", w2f), ("w2i", w2i)):
            out[name + "_h"], out[name + "_l"] = hl(mat)
        out["twr"] = jnp.asarray(twr, F32)
        out["twi"] = jnp.asarray(twi, F32)
        return out


def _fft_major_fwd(plan, x_ref, kf_h, kf_l, spec_ref):
    N2, K1, n1_in = plan.N2, plan.K1, plan.n1_in
    pad_rows = plan.kf_cols - n1_in * SUBLANES

    def body(q, carry):
        tiles = [x_ref[pl.ds(pl.multiple_of(N2 * n + SUBLANES * q, SUBLANES), SUBLANES), :] for n in range(n1_in)]
        if pad_rows:
            tiles.append(jnp.zeros((pad_rows, tiles[0].shape[1]), F32))
        g = jnp.concatenate(tiles, axis=0)
        o = _dotc3(kf_h[...], kf_l[...], g)
        for part in range(2):
            for k in range(K1):
                src = (part * K1 + k) * SUBLANES
                dst = (2 * k + part) * N2
                spec_ref[pl.ds(pl.multiple_of(dst + SUBLANES * q, SUBLANES), SUBLANES), :] = o[src:src + SUBLANES, :]
        return carry

    lax.fori_loop(0, plan.nq, body, 0)


def _cmul(ar, ai, br, bi):
    return ar * br - ai * bi, ar * bi + ai * br


def _fft_minor_fwd(plan, spec_ref, w2f_h, w2f_l, twr_ref, twi_ref):
    N2 = plan.N2

    def body(k, carry):
        base = pl.multiple_of(2 * N2 * k, 2 * N2)
        tb = pl.multiple_of(N2 * k, N2)
        ar = spec_ref[pl.ds(base, N2), :]
        ai = spec_ref[pl.ds(base + N2, N2), :]
        tr = twr_ref[pl.ds(tb, N2), :]
        ti = twi_ref[pl.ds(tb, N2), :]
        xr, xi = _cmul(ar, ai, tr, -ti)
        y = _dotc3(w2f_h[...], w2f_l[...], jnp.concatenate([xr, xi], axis=0))
        spec_ref[pl.ds(base, 2 * N2), :] = y
        return carry

    lax.fori_loop(0, plan.K1, body, 0)


def _fft_fwd(plan, x_ref, consts, spec_ref):
    _fft_major_fwd(plan, x_ref, consts["kf_h"], consts["kf_l"], spec_ref)
    _fft_minor_fwd(plan, spec_ref, consts["w2f_h"], consts["w2f_l"], consts["twr"], consts["twi"])


def _fft_conv_apply(plan, x_ref, g_ref, consts, spec_ref, y_ref):
    N2, K1, n1_in = plan.N2, plan.K1, plan.n1_in
    _fft_major_fwd(plan, x_ref, consts["kf_h"], consts["kf_l"], spec_ref)
    w2f_h, w2f_l, w2i_h, w2i_l = consts["w2f_h"], consts["w2f_l"], consts["w2i_h"], consts["w2i_l"]
    twr_ref, twi_ref = consts["twr"], consts["twi"]

    def body(k, carry):
        base = pl.multiple_of(2 * N2 * k, 2 * N2)
        tb = pl.multiple_of(N2 * k, N2)
        tr = twr_ref[pl.ds(tb, N2), :]
        ti = twi_ref[pl.ds(tb, N2), :]
        xr, xi = _cmul(spec_ref[pl.ds(base, N2), :], spec_ref[pl.ds(base + N2, N2), :], tr, -ti)
        y = _dotc3(w2f_h[...], w2f_l[...], jnp.concatenate([xr, xi], axis=0))
        pr, pi = _cmul(y[:N2], y[N2:], g_ref[pl.ds(base, N2), :], g_ref[pl.ds(base + N2, N2), :])
        z = _dotc3(w2i_h[...], w2i_l[...], jnp.concatenate([pr, pi], axis=0))
        zr, zi = _cmul(z[:N2], z[N2:], tr, ti)
        spec_ref[pl.ds(base, N2), :] = zr
        spec_ref[pl.ds(base + N2, N2), :] = zi
        return carry

    lax.fori_loop(0, K1, body, 0)

    pad_rows = plan.kb_cols - plan.kf_rows
    kb_h, kb_l = consts["kb_h"], consts["kb_l"]

    def body2(q, carry):
        tiles = []
        for part in range(2):
            for k in range(K1):
                src = (2 * k + part) * N2
                tiles.append(spec_ref[pl.ds(pl.multiple_of(src + SUBLANES * q, SUBLANES), SUBLANES), :])
        if pad_rows:
            tiles.append(jnp.zeros((pad_rows, tiles[0].shape[1]), F32))
        t = jnp.concatenate(tiles, axis=0)
        o = _dotc3(kb_h[...], kb_l[...], t)
        for n in range(n1_in):
            y_ref[pl.ds(pl.multiple_of(N2 * n + SUBLANES * q, SUBLANES), SUBLANES), :] = o[n * SUBLANES:(n + 1) * SUBLANES, :]
        return carry

    lax.fori_loop(0, plan.nq, body2, 0)


_CONST_NAMES = ("kf_h", "kf_l", "kb_h", "kb_l", "w2f_h", "w2f_l", "w2i_h", "w2i_l", "twr", "twi")


def _const_specs(consts, nargs):
    ins, specs = [], []
    for nme in _CONST_NAMES:
        a = consts[nme]
        ins.append(a)
        if nargs == 1:
            specs.append(pl.BlockSpec(a.shape, lambda c: (0, 0)))
        else:
            specs.append(pl.BlockSpec(a.shape, lambda c, b: (0, 0)))
    return ins, specs


def _hy_hidden_body(z_ref, w1_ref, b1_ref, w2_ref, b2_ref, sf_ref, o_ref):
    h = jnp.sin(sf_ref[0:1, :] * (_dot3(z_ref[...], w1_ref[...]) + b1_ref[...]))
    h = jnp.sin(sf_ref[1:2, :] * (_dot3(h, w2_ref[...]) + b2_ref[...]))
    o_ref[...] = h


def hyena_hidden(L, f_w1, f_b1, f_w2, f_b2, sin_freq):
    emb, hid = f_w1.shape
    bands = (emb - 1) // 2
    t = jnp.linspace(0.0, 1.0, L, dtype=F32)[:, None]
    w = (2.0 * math.pi / L) * jnp.arange(L, dtype=F32)[:, None]
    f = jnp.linspace(1e-4, bands - 1, bands, dtype=F32)[None]
    z = jnp.concatenate([t, jnp.cos(f * w), -jnp.sin(f * w)], -1)
    embp = -(-emb // LANES) * LANES
    z = jnp.pad(z, ((0, 0), (0, embp - emb)))
    w1 = jnp.pad(f_w1, ((0, embp - emb), (0, 0)))
    ins = [z, w1, f_b1.reshape(1, hid), f_w2, f_b2.reshape(1, hid), sin_freq]
    return pl.pallas_call(
        _hy_hidden_body,
        out_shape=jax.ShapeDtypeStruct((L, hid), F32),
        name="hyena_hidden",
    )(*ins)


def _hy_filter_body(*refs, plan, n_order):
    it = iter(refs)
    h_ref = next(it)
    w3_refs = [next(it) for _ in range(2 * n_order)]
    dl_ref = next(it)
    consts = {nme: next(it) for nme in _CONST_NAMES}
    g_ref = next(it)
    x_sc = next(it)
    spec_a = next(it)
    spec_b = next(it)
    L, N2, K1 = plan.L, plan.N2, plan.K1
    t = lax.broadcasted_iota(jnp.int32, (L, LANES), 0).astype(F32) * (1.0 / (L - 1))
    win = jnp.exp(-t * dl_ref[...])
    row0 = lax.broadcasted_iota(jnp.int32, (L, LANES), 0) == 0
    h = h_ref[...]
    for n in range(n_order):
        fwd = _dot3(h, w3_refs[2 * n][...]) * win
        bwd = jnp.where(row0, 0.0, _dot3(h, w3_refs[2 * n + 1][...]) * win)
        ss = jnp.sum(fwd * fwd, 0, keepdims=True) + jnp.sum(bwd * bwd, 0, keepdims=True)
        nu = lax.rsqrt(ss + 1e-12) * (1.0 / plan.N)
        x_sc[...] = fwd * nu
        _fft_fwd(plan, x_sc, consts, spec_a)
        x_sc[...] = bwd * nu
        _fft_fwd(plan, x_sc, consts, spec_b)

        def body(k, carry):
            base = pl.multiple_of(2 * N2 * k, 2 * N2)
            g_ref[n, pl.ds(base, N2), :] = spec_a[pl.ds(base, N2), :] + spec_b[pl.ds(base, N2), :]
            g_ref[n, pl.ds(base + N2, N2), :] = spec_a[pl.ds(base + N2, N2), :] - spec_b[pl.ds(base + N2, N2), :]
            return carry

        lax.fori_loop(0, K1, body, 0)


def hyena_filter_spectra(plan, consts, hidden, f_w3, d_model, n_order):
    L = plan.L
    hid = hidden.shape[1]
    ncb = d_model // LANES
    deltas = jnp.abs(jnp.linspace(math.log(HY_DECAY_TARGET) / HY_DECAY_PCT_SHORT,
                                  math.log(HY_DECAY_TARGET) / HY_DECAY_PCT_LONG, d_model, dtype=F32)).reshape(1, d_model)
    ins = [hidden]
    specs = [pl.BlockSpec((L, hid), lambda c: (0, 0))]
    for n in range(n_order):
        for dr in range(2):
            ins.append(f_w3)
            specs.append(pl.BlockSpec((hid, LANES), lambda c, n=n, dr=dr: (0, (2 * n + dr) * ncb + c)))
    ins.append(deltas)
    specs.append(pl.BlockSpec((1, LANES), lambda c: (0, c)))
    ci, cs = _const_specs(consts, 1)
    ins += ci
    specs += cs
    return pl.pallas_call(
        functools.partial(_hy_filter_body, plan=plan, n_order=n_order),
        grid=(ncb,),
        in_specs=specs,
        out_specs=pl.BlockSpec((n_order, plan.spec_rows, LANES), lambda c: (0, 0, c)),
        out_shape=jax.ShapeDtypeStruct((n_order, plan.spec_rows, d_model), F32),
        scratch_shapes=[pltpu.VMEM((L, LANES), F32), pltpu.VMEM((plan.spec_rows, LANES), F32),
                        pltpu.VMEM((plan.spec_rows, LANES), F32)],
        compiler_params=_cparams(("arbitrary",)),
        name="hyena_filter",
    )(*ins)


def _conv3(p, w_ref, b_ref, L):
    rows = lax.broadcasted_iota(jnp.int32, p.shape, 0)
    prev = jnp.where(rows == 0, 0.0, pltpu.roll(p, 1, 0))
    nxt = jnp.where(rows == L - 1, 0.0, pltpu.roll(p, L - 1, 0))
    return prev * w_ref[0:1, :] + p * w_ref[1:2, :] + nxt * w_ref[2:3, :] + b_ref[...]


def _hy_conv_body(*refs, plan, order, conv_u):
    it = iter(refs)
    u_ref, gate_ref = next(it), next(it)
    if conv_u:
        cwu_ref, cbu_ref = next(it), next(it)
    cwg_ref, cbg_ref = next(it), next(it)
    skip_ref = next(it)
    g_ref = next(it)
    consts = {nme: next(it) for nme in _CONST_NAMES}
    o_ref = next(it)
    x_sc, y_sc, spec_sc = next(it), next(it), next(it)
    L = plan.L
    u = u_ref[...]
    if conv_u:
        u = _conv3(u, cwu_ref, cbu_ref, L)
    x_sc[...] = u
    _fft_conv_apply(plan, x_sc, g_ref.at[0], consts, spec_sc, y_sc)
    gate = _conv3(gate_ref[...], cwg_ref, cbg_ref, L)
    o_ref[...] = (gate * (y_sc[...] + x_sc[...] * skip_ref[order:order + 1, :])).astype(o_ref.dtype)


def hyena_conv(plan, consts, u_arr, u_cb0, gate_arr, gate_cb0, conv_w, conv_b, skip, spectra, *,
               order, conv_u, u_row_block0, gate_row_block0, batch, d_model, out_dtype):
    L = plan.L
    ncb = d_model // LANES
    ins = [u_arr, gate_arr]
    specs = [pl.BlockSpec((L, LANES), lambda c, b: (u_row_block0 + b, u_cb0 + c)),
             pl.BlockSpec((L, LANES), lambda c, b: (gate_row_block0 + b, gate_cb0 + c))]
    if conv_u:
        ins += [conv_w, conv_b]
        specs += [pl.BlockSpec((3, LANES), lambda c, b: (0, u_cb0 + c)),
                  pl.BlockSpec((1, LANES), lambda c, b: (0, u_cb0 + c))]
    ins += [conv_w, conv_b, skip, spectra]
    specs += [pl.BlockSpec((3, LANES), lambda c, b: (0, gate_cb0 + c)),
              pl.BlockSpec((1, LANES), lambda c, b: (0, gate_cb0 + c)),
              pl.BlockSpec((skip.shape[0], LANES), lambda c, b: (0, c)),
              pl.BlockSpec((1, plan.spec_rows, LANES), lambda c, b: (order, 0, c))]
    ci, cs = _const_specs(consts, 2)
    ins += ci
    specs += cs
    return pl.pallas_call(
        functools.partial(_hy_conv_body, plan=plan, order=order, conv_u=conv_u),
        grid=(ncb, batch),
        in_specs=specs,
        out_specs=pl.BlockSpec((L, LANES), lambda c, b: (b, c)),
        out_shape=jax.ShapeDtypeStruct((batch * L, d_model), out_dtype),
        scratch_shapes=[pltpu.VMEM((L, LANES), F32), pltpu.VMEM((L, LANES), F32),
                        pltpu.VMEM((plan.spec_rows, LANES), F32)],
        compiler_params=_cparams(("parallel", "arbitrary")),
        name=f"hyena_conv{order}_L{L}",
    )(*ins)


def hyena_mixer(hmix, p, *, batch, seq, ctx_len, d_model, want_ctx):
    n_order = p["skip"].shape[0]
    pr = mm(hmix, p["w_in"], bias=p["b_in"], name="hy_in", rows=(batch * seq + (batch * ctx_len if want_ctx else 0)))
    conv_b = p["conv_b"].reshape(1, -1)
    ncb = d_model // LANES
    outs = []
    streams = [(seq, 0)]
    if want_ctx:
        streams.append((ctx_len, (batch * seq) // ctx_len))
    for L, rb0 in streams:
        plan = _FftPlan(L)
        consts = plan.constants()
        hidden = hyena_hidden(L, p["f_w1"], p["f_b1"], p["f_w2"], p["f_b2"], p["sin_freq"])
        spectra = hyena_filter_spectra(plan, consts, hidden, p["f_w3"], d_model, n_order)
        z1 = hyena_conv(plan, consts, pr, 0, pr, ncb, p["conv_w"], conv_b, p["skip"], spectra,
                        order=0, conv_u=True, u_row_block0=rb0, gate_row_block0=rb0, batch=batch, d_model=d_model,
                        out_dtype=F32)
        z2 = hyena_conv(plan, consts, z1, 0, pr, 2 * ncb, p["conv_w"], conv_b, p["skip"], spectra,
                        order=1, conv_u=False, u_row_block0=0, gate_row_block0=rb0, batch=batch, d_model=d_model,
                        out_dtype=BF16)
        outs.append(z2)
    return outs[0] if len(outs) == 1 else jnp.concatenate(outs, 0)


def _headsum(x, e_ref):
    hi, lo = _split2(x)
    return _dot(hi, e_ref[...]) + _dot(lo, e_ref[...])


def _rw_mix_body(h_ref, hp_ref, hn_ref, mix_ref, o_ref, *, tm, lat_tiles, tiles_per_seq):
    i = pl.program_id(0)
    is_ctx = i >= lat_tiles
    first = jnp.logical_or(is_ctx, i % tiles_per_seq == 0)
    last = jnp.logical_or(is_ctx, i % tiles_per_seq == tiles_per_seq - 1)
    h = h_ref[...]
    rows = lax.broadcasted_iota(jnp.int32, h.shape, 0)
    pedge = jnp.where(first, 0.0, hp_ref[SUBLANES - 1:SUBLANES, :])
    nedge = jnp.where(last, 0.0, hn_ref[0:1, :])
    prev = jnp.where(rows == 0, pedge, pltpu.roll(h, 1, 0))
    nxt = jnp.where(rows == tm - 1, nedge, pltpu.roll(h, tm - 1, 0))
    dp = prev - h
    dn = nxt - h
    for n in range(o_ref.shape[0]):
        o_ref[n] = (h + dp * mix_ref[0, n:n + 1, :] + dn * mix_ref[1, n:n + 1, :]).astype(o_ref.dtype)


def rwkv_mix(h, mix, *, batch, seq, ctx_len):
    m, d = h.shape
    tm = ctx_len
    n_shift = mix.shape[1]
    nblk = m // SUBLANES
    per = tm // SUBLANES
    body = functools.partial(_rw_mix_body, tm=tm, lat_tiles=(batch * seq) // tm, tiles_per_seq=seq // tm)
    return pl.pallas_call(
        body,
        grid=(m // tm,),
        in_specs=[
            pl.BlockSpec((tm, d), lambda i: (i, 0)),
            pl.BlockSpec((SUBLANES, d), lambda i: (jnp.maximum(i * per - 1, 0), 0)),
            pl.BlockSpec((SUBLANES, d), lambda i: (jnp.minimum((i + 1) * per, nblk - 1), 0)),
            pl.BlockSpec((2, n_shift, d), lambda i: (0, 0, 0)),
        ],
        out_specs=pl.BlockSpec((n_shift, tm, d), lambda i: (0, i, 0)),
        out_shape=jax.ShapeDtypeStruct((n_shift, m, d), BF16),
        compiler_params=_cparams(("parallel",)),
        name="rwkv_mix",
    )(h, h, h, mix)


def _rw_prep_body(k_ref, lw0_ref, lw1_ref, al0_ref, al1_ref, w0_ref, a0_ref, kk_ref, ka_ref, e_ref,
                  olw0, olw1, okd0, okd1, obb0, obb1, oaa):
    k = k_ref[...]
    kk0 = k * kk_ref[...]
    kk = kk0 * lax.rsqrt(_headsum(kk0 * kk0, e_ref) + 1e-12)
    oaa[...] = -kk
    for d, (lw_ref, al_ref, olw, okd, obb) in enumerate(((lw0_ref, al0_ref, olw0, okd0, obb0),
                                                         (lw1_ref, al1_ref, olw1, okd1, obb1))):
        z = -(w0_ref[d:d + 1, :] + lw_ref[...])
        softplus = jnp.maximum(z, 0.0) + jnp.log(1.0 + jnp.exp(-jnp.abs(z)))
        log_w = -softplus - 0.5
        olw[...] = -jnp.exp(log_w)
        a = jax.nn.sigmoid(a0_ref[d:d + 1, :] + al_ref[...])
        okd[...] = k * (1.0 + (a - 1.0) * ka_ref[...])
        obb[...] = kk * a


def rwkv_prep(k, lw0, lw1, al0, al1, w0, a0, k_k, k_a, e128, tm=1088):
    m, d = k.shape
    blk = pl.BlockSpec((tm, LANES), lambda i, c: (i, c))
    vec2 = pl.BlockSpec((2, LANES), lambda i, c: (0, c))
    vec1 = pl.BlockSpec((1, LANES), lambda i, c: (0, c))
    sh = jax.ShapeDtypeStruct((m, d), F32)
    return pl.pallas_call(
        _rw_prep_body,
        grid=(m // tm, d // LANES),
        in_specs=[blk] * 5 + [vec2, vec2, vec1, vec1, pl.BlockSpec((LANES, LANES), lambda i, c: (0, 0))],
        out_specs=[blk] * 7,
        out_shape=[sh] * 7,
        compiler_params=_cparams(("parallel", "arbitrary")),
        name="rwkv_prep",
    )(k, lw0, lw1, al0, al1, w0, a0, k_k.reshape(1, d), k_a.reshape(1, d), e128)


def _b16(x):
    return x.astype(BF16)


def _wkv_body(r_ref, lw_ref, k_ref, v_ref, a_ref, b_ref, o_ref, s_sc, *, chunk, n_heads, reverse):
    c = pl.program_id(2)

    @pl.when(c == 0)
    def _():
        s_sc[...] = jnp.zeros_like(s_sc)

    C = chunk
    hs = range(n_heads)
    row = lax.broadcasted_iota(jnp.int32, (C, C), 0)
    col = lax.broadcasted_iota(jnp.int32, (C, C), 1)
    if reverse:
        tri_incl = col >= row
        tri_strict = col > row
        last = 0
    else:
        tri_incl = col <= row
        tri_strict = col < row
        last = C - 1
    same = lambda s: (row >> int(math.log2(s))) == (col >> int(math.log2(s)))
    eye = (row == col).astype(F32)
    cum_m = tri_incl.astype(BF16)
    sls = [slice(h * RW_HEAD, (h + 1) * RW_HEAD) for h in hs]

    lw = [lw_ref[:, sl] for sl in sls]
    cum = []
    for h in hs:
        l1, l2, l3 = _split3(lw[h])
        cum.append(_dot(cum_m, l1) + _dot(cum_m, l2) + _dot(cum_m, l3))
    p_inc = [jnp.exp(cum[h]) for h in hs]
    x1 = [jnp.concatenate([a_ref[:, sls[h]] * jnp.exp(cum[h] - lw[h]), r_ref[:, sls[h]] * p_inc[h]], axis=0) for h in hs]
    x2 = []
    for h in hs:
        p_inv = jnp.exp(-cum[h])
        x2.append(jnp.concatenate([b_ref[:, sls[h]] * p_inv, k_ref[:, sls[h]] * p_inv], axis=0))
    v = [v_ref[:, sl] for sl in sls]
    s0 = [s_sc[h] for h in hs]
    mfull = [_dot3(x1[h], x2[h], _dot_nt) for h in hs]
    w0 = [_dot3(x1[h], s0[h], _dot_nt) for h in hs]
    n_ab = [jnp.where(tri_strict, mfull[h][:C, :C], 0.0) for h in hs]
    a_ak = [_b16(jnp.where(tri_strict, mfull[h][:C, C:], 0.0)) for h in hs]
    a_r = [_b16(jnp.concatenate([jnp.where(tri_incl, mfull[h][C:, :C], 0.0),
                                 jnp.where(tri_incl, mfull[h][C:, C:], 0.0)], axis=1)) for h in hs]
    rhs = [w0[h][:C] + _dot(a_ak[h], _b16(v[h])) for h in hs]

    n4 = [jnp.where(same(4), n_ab[h], 0.0) for h in hs]
    n4b = [_b16(n4[h]) for h in hs]
    sq = [_dot(n4b[h], n4b[h]) for h in hs]
    t = [(eye + n4[h]) + _dot(_b16(eye + n4[h]), _b16(sq[h])) for h in hs]
    s = 4
    while s < C:
        off = jnp.logical_and(jnp.logical_and(same(2 * s), jnp.logical_not(same(s))), tri_strict)
        tb = [_b16(t[h]) for h in hs]
        tmp = [_dot(tb[h], _b16(jnp.where(off, n_ab[h], 0.0))) for h in hs]
        t = [t[h] + _dot(_b16(tmp[h]), tb[h]) for h in hs]
        s *= 2
    tb = [_b16(t[h]) for h in hs]
    u = [_dot(tb[h], _b16(rhs[h])) for h in hs]
    res = [rhs[h] - u[h] + _dot3(n_ab[h], u[h]) for h in hs]
    u = [u[h] + _dot(tb[h], _b16(res[h])) for h in hs]
    uv = [jnp.concatenate([u[h], v[h]], axis=0) for h in hs]
    y = [w0[h][C:] + _dot(a_r[h], _b16(uv[h])) for h in hs]
    for h in hs:
        s_sc[h] = (s0[h] + _dot3(uv[h], x2[h], _dot_tn)) * p_inc[h][last:last + 1, :]
    o_ref[...] = jnp.concatenate(y, axis=1)


def wkv7(r, lw, kd, v, aa, bb, *, reverse, batch, seq, ctx_len, chunk=128, heads_per_step=8):
    m, d = r.shape
    heads_per_step = min(heads_per_step, d // RW_HEAD)
    wl = heads_per_step * RW_HEAD
    n_ctx = ctx_len // chunk
    n_lat = seq // chunk
    n_chunks = n_ctx + n_lat
    ctx0 = (batch * seq) // chunk

    def row_block(b, c):
        if reverse:
            return jnp.where(c < n_ctx, ctx0 + b * n_ctx + (n_ctx - 1 - c), b * n_lat + (n_lat - 1 - (c - n_ctx)))
        return jnp.where(c < n_ctx, ctx0 + b * n_ctx + c, b * n_lat + (c - n_ctx))

    blk = pl.BlockSpec((chunk, wl), lambda b, g, c: (row_block(b, c), g))
    body = functools.partial(_wkv_body, chunk=chunk, n_heads=heads_per_step, reverse=reverse)
    return pl.pallas_call(
        body,
        grid=(batch, d // wl, n_chunks),
        in_specs=[blk] * 6,
        out_specs=blk,
        out_shape=jax.ShapeDtypeStruct((m, d), F32),
        scratch_shapes=[pltpu.VMEM((heads_per_step, RW_HEAD, RW_HEAD), F32)],
        compiler_params=_cparams(("parallel", "parallel", "arbitrary")),
        name="wkv7_rev" if reverse else "wkv7_fwd",
    )(r, lw, kd, v, aa, bb)


def _rw_readout_body(y0_ref, y1_ref, r_ref, v_ref, kd0_ref, kd1_ref, g_ref, rk_ref, lnw_ref, lnb_ref, e_ref, o_ref,
                     *, gn_eps):
    y = y0_ref[...] + y1_ref[...]
    inv = 1.0 / RW_HEAD
    mu = _headsum(y, e_ref) * inv
    dlt = y - mu
    var = _headsum(dlt * dlt, e_ref) * inv
    yn = dlt * lax.rsqrt(var + gn_eps) * lnw_ref[...] + lnb_ref[...]
    rk = r_ref[...] * rk_ref[...]
    bonus = (_headsum(rk * kd0_ref[...], e_ref) + _headsum(rk * kd1_ref[...], e_ref)) * v_ref[...]
    o_ref[...] = ((yn + bonus) * g_ref[...]).astype(o_ref.dtype)


def rwkv_readout(y0, y1, r, v, kd0, kd1, g, r_k, ln_w, ln_b, e128, gn_eps, rows, tm=512):
    d = r.shape[1]
    blk = pl.BlockSpec((tm, LANES), lambda i, c: (i, c))
    vec1 = pl.BlockSpec((1, LANES), lambda i, c: (0, c))
    return pl.pallas_call(
        functools.partial(_rw_readout_body, gn_eps=gn_eps),
        grid=(rows // tm, d // LANES),
        in_specs=[blk] * 7 + [vec1] * 3 + [pl.BlockSpec((LANES, LANES), lambda i, c: (0, 0))],
        out_specs=blk,
        out_shape=jax.ShapeDtypeStruct((rows, d), BF16),
        compiler_params=_cparams(("parallel", "arbitrary")),
        name="rwkv_readout",
    )(y0, y1, r, v, kd0, kd1, g, r_k.reshape(1, d), ln_w.reshape(1, d), ln_b.reshape(1, d), e128)


def rwkv_mixer(hmix, p, *, batch, seq, ctx_len, d_model):
    xs = rwkv_mix(hmix, p["mix"], batch=batch, seq=seq, ctx_len=ctx_len)
    m = hmix.shape[0]
    xs2 = xs.reshape(xs.shape[0] * m, d_model)
    tm = 1088 if m % 1088 == 0 else 512

    def proj(n, w, **kw):
        return mm(xs2, w, tm=tm, rows=m, x_row_block0=n * (m // tm), **kw)

    r = proj(0, p["w_r"], name="rw_r")
    k = proj(2, p["w_k"], name="rw_k")
    v = proj(3, p["w_v"], name="rw_v")
    g = mm(proj(5, p["g1"], act="sigmoid", out_dtype=BF16, name="rw_g1"), p["g2"], tm=tm, name="rw_g2")
    lws, als = [], []
    for d in range(2):
        tw = proj(1, p["w1"][d], act="tanh", out_dtype=BF16, name="rw_w1")
        lws.append(mm(tw, p["w2"][d], tm=tm, name="rw_w2"))
        ah = proj(4, p["a1"][d], out_dtype=BF16, name="rw_a1")
        als.append(mm(ah, p["a2"][d], tm=tm, name="rw_a2"))
    e128 = jnp.asarray(np.kron(np.eye(LANES // RW_HEAD), np.ones((RW_HEAD, RW_HEAD))), BF16)
    lw0, lw1, kd0, kd1, bb0, bb1, aa = rwkv_prep(k, lws[0], lws[1], als[0], als[1], p["w0"], p["a0"],
                                                 p["k_k"], p["k_a"], e128, tm=tm)
    kw = dict(batch=batch, seq=seq, ctx_len=ctx_len)
    y0 = wkv7(r, lw0, kd0, v, aa, bb0, reverse=False, **kw)
    y1 = wkv7(r, lw1, kd1, v, aa, bb1, reverse=True, **kw)
    return rwkv_readout(y0, y1, r, v, kd0, kd1, g, p["r_k"], p["ln_w"], p["ln_b"], e128,
                        1e-5 * RW_HEAD, rows=m)


def mla_mixer(hmix, p, *, batch, seq, ctx_len, want_ctx):
    m = hmix.shape[0]
    tm = 1088 if m % 1088 == 0 else 512
    scale = (MLA_NOPE + MLA_ROPE) ** -0.5
    down = mm(hmix, p["w_down_ext"], tm=tm, name="mla_down")
    kq = MLA_Q_RANK
    q_raw = mm(down, p["w_uq_ext"], norm_g=p["q_norm_g"], x_col_block=0, tm=tm, name="mla_uq")
    kv = mm(down, p["w_ukv"], norm_g=p["kv_norm_g"], x_col_block=1, tm=tm, out_dtype=BF16, name="mla_ukv")
    cos, sin = _rope_tables(seq, ctx_len, batch, MLA_ROPE)
    zeros = jnp.zeros_like(cos)
    cos_p = jnp.concatenate([cos, zeros], -1)
    sin_p = jnp.concatenate([sin, zeros], -1)
    q = mla_rope(q_raw, 0, MLA_HEADS, cos_p, sin_p, scale, True, tm=tm)
    kpe = mla_rope(down, (kq + MLA_KV_RANK) // LANES, 1, cos_p, sin_p, 1.0, False, tm=tm)
    q_parts = [(q, lambda h: 2 * h), (q, lambda h: 2 * h + 1)]
    k_parts = [(kv, lambda h: 2 * h), (kpe, lambda h: 0)]
    kw = dict(n_heads=MLA_HEADS, batch=batch, seq=seq, ctx_len=ctx_len)
    o_lat = attention(q_parts, k_parts, kv, lambda h: 2 * h + 1, lat_queries=True, **kw)
    if not want_ctx:
        return o_lat
    o_ctx = attention(q_parts, k_parts, kv, lambda h: 2 * h + 1, lat_queries=False, **kw)
    return jnp.concatenate([o_lat, o_ctx], 0)


def gqa_mixer(hmix, p, *, batch, seq, ctx_len, want_ctx):
    m = hmix.shape[0]
    tm = 1088 if m % 1088 == 0 else 512
    scale = GQA_HEAD ** -0.5
    raw = mm(hmix, p["w_qkv_ext"], tm=tm, name="gqa_qkv")
    cos, sin = _rope_tables(seq, ctx_len, batch, GQA_HEAD)
    nq, nk = GQA_HEADS, GQA_KV_HEADS
    q = gqa_prep(raw, 0, nq, nq, cos, sin, p["q_norm_g"], p["q_norm_g_rot"], scale, tm=tm)
    k = gqa_prep(raw, 2 * nq, 2 * nq + nk, nk, cos, sin, p["k_norm_g"], p["k_norm_g_rot"], 1.0, tm=tm)
    vcb0 = 2 * nq + 2 * nk
    vb = mm_cast(raw, vcb0, nk, tm=tm)
    grp = nq // nk
    kw = dict(n_heads=nq, batch=batch, seq=seq, ctx_len=ctx_len)
    o_lat = attention([(q, lambda h: h)], [(k, lambda h: h // grp)], vb, lambda h: h // grp, lat_queries=True, **kw)
    if not want_ctx:
        return o_lat
    o_ctx = attention([(q, lambda h: h)], [(k, lambda h: h // grp)], vb, lambda h: h // grp, lat_queries=False, **kw)
    return jnp.concatenate([o_lat, o_ctx], 0)


def _cast_body(x_ref, o_ref):
    o_ref[...] = x_ref[...].astype(o_ref.dtype)


def mm_cast(raw, cb0, nblocks, tm=512):
    m = raw.shape[0]
    return pl.pallas_call(
        _cast_body,
        grid=(m // tm, nblocks),
        in_specs=[pl.BlockSpec((tm, LANES), lambda i, c: (i, cb0 + c))],
        out_specs=pl.BlockSpec((tm, LANES), lambda i, c: (i, c)),
        out_shape=jax.ShapeDtypeStruct((m, nblocks * LANES), BF16),
        compiler_params=_cparams(("parallel", "arbitrary")),
        name="cast_cols",
    )(raw)


def _rot_gain(g):
    q = g.shape[-1] // 4
    g4 = g.reshape(2, 2, q)
    return jnp.concatenate([g4[:, 1:2], g4[:, 0:1]], axis=1).reshape(g.shape)


def kernel(x, c, ctx, c_ctx, ada_w, ada_b, norm_g, ffn_w_gu, ffn_w_down, hy_w_in, hy_b_in, hy_conv_w, hy_conv_b, hy_f_w1, hy_f_b1, hy_f_w2, hy_f_b2, hy_f_w3, hy_sin_freq, hy_skip, hy_w_out, hy_b_out, rw_mix, rw_w_rkv, rw_w0, rw_w1, rw_w2, rw_a0, rw_a1, rw_a2, rw_g1, rw_g2, rw_k_k, rw_k_a, rw_r_k, rw_ln_w, rw_ln_b, rw_w_o, mla_w_down, mla_q_norm_g, mla_w_uq, mla_kv_norm_g, mla_w_ukv, mla_w_o, gqa_w_qkv, gqa_q_norm_g, gqa_k_norm_g, gqa_w_o, final_g):
    batch, seq, d = x.shape
    ctx_len = ctx.shape[1]
    depth = ada_w.shape[0]
    n_mix = 4
    rows_lat = batch * seq
    rows_all = rows_lat + batch * ctx_len
    bf = lambda a: a.astype(BF16)

    cvec = jnp.concatenate([c, c_ctx[None], jnp.zeros((8 - batch - 1, d), F32)], 0)
    mods_all = adaln_all(cvec, ada_w, ada_b)[:, :batch + 1].reshape(depth, batch + 1, N_MOD, d)

    h = jnp.concatenate([x.reshape(rows_lat, d), ctx.reshape(batch * ctx_len, d)], 0)

    for i in range(depth):
        mi, j = i % n_mix, i // n_mix
        ctx_out = any(jj % n_mix != 0 for jj in range(i + 1, depth))
        mods = mods_all[i]
        gate5 = mods[:, 5:6, :]
        mix_dtype = F32 if mi == 1 else BF16
        h, hmix = ffn_half(h, mods, norm_g[i], bf(ffn_w_gu[i, 0]), bf(ffn_w_down[i, 0]), s=0, rows=rows_all,
                           seq_rows=seq, next_g=norm_g[i, 1], next_s=1, next_dtype=mix_dtype)
        rows_out = rows_all if ctx_out else rows_lat
        kw = dict(batch=batch, seq=seq, ctx_len=ctx_len)
        if mi == 0:
            p = dict(w_in=bf(hy_w_in[j]), b_in=hy_b_in[j], conv_w=hy_conv_w[j], conv_b=hy_conv_b[j],
                     f_w1=hy_f_w1[j], f_b1=hy_f_b1[j], f_w2=hy_f_w2[j], f_b2=hy_f_b2[j], f_w3=hy_f_w3[j],
                     sin_freq=hy_sin_freq[j], skip=hy_skip[j])
            z = hyena_mixer(hmix, p, d_model=d, want_ctx=ctx_out, **kw)
            w_o, b_o = bf(hy_w_out[j]), hy_b_out[j]
        elif mi == 1:
            pad = lambda w, ax: jnp.pad(w, [(0, RW_LORA_PAD - w.shape[a]) if a == ax else (0, 0) for a in range(w.ndim)])
            p = dict(mix=rw_mix[j], w_r=bf(rw_w_rkv[j, 0]), w_k=bf(rw_w_rkv[j, 1]), w_v=bf(rw_w_rkv[j, 2]),
                     g1=bf(rw_g1[j]), g2=bf(rw_g2[j]),
                     w1=bf(pad(rw_w1[j], 2)), w2=bf(pad(rw_w2[j], 1)), a1=bf(pad(rw_a1[j], 2)), a2=bf(pad(rw_a2[j], 1)),
                     w0=rw_w0[j], a0=rw_a0[j], k_k=rw_k_k[j], k_a=rw_k_a[j], r_k=rw_r_k[j].reshape(-1),
                     ln_w=rw_ln_w[j], ln_b=rw_ln_b[j])
            z = rwkv_mixer(hmix, p, d_model=d, **kw)
            w_o, b_o = bf(rw_w_o[j]), None
        elif mi == 2:
            wd = mla_w_down[j]
            pe = wd[:, MLA_Q_RANK + MLA_KV_RANK:]
            wuq = mla_w_uq[j].reshape(MLA_Q_RANK, MLA_HEADS, MLA_NOPE + MLA_ROPE)
            wq_pe = wuq[..., MLA_NOPE:]
            wuq_ext = jnp.concatenate([wuq[..., :MLA_NOPE], wq_pe,
                                       _rot_cols(wq_pe.reshape(MLA_Q_RANK, -1), MLA_ROPE).reshape(wq_pe.shape)], -1)
            p = dict(w_down_ext=bf(jnp.concatenate([wd, _rot_cols(pe, MLA_ROPE)], 1)),
                     w_uq_ext=bf(wuq_ext.reshape(MLA_Q_RANK, -1)), w_ukv=bf(mla_w_ukv[j]),
                     q_norm_g=mla_q_norm_g[j], kv_norm_g=mla_kv_norm_g[j])
            z = mla_mixer(hmix, p, want_ctx=ctx_out, **kw)
            w_o, b_o = bf(mla_w_o[j]), None
        else:
            w = gqa_w_qkv[j]
            nqc = GQA_HEADS * GQA_HEAD
            nkc = GQA_KV_HEADS * GQA_HEAD
            wq, wk, wv = w[:, :nqc], w[:, nqc:nqc + nkc], w[:, nqc + nkc:]
            p = dict(w_qkv_ext=bf(jnp.concatenate([wq, _rot_cols(wq, GQA_HEAD), wk, _rot_cols(wk, GQA_HEAD), wv], 1)),
                     q_norm_g=gqa_q_norm_g[j], q_norm_g_rot=_rot_gain(gqa_q_norm_g[j]),
                     k_norm_g=gqa_k_norm_g[j], k_norm_g_rot=_rot_gain(gqa_k_norm_g[j]))
            z = gqa_mixer(hmix, p, want_ctx=ctx_out, **kw)
            w_o, b_o = bf(gqa_w_o[j]), None
        h = mm(z, w_o, bias=b_o, res=h, gate=gate5, seq_rows=seq, rows=rows_out, name="mix_out")
        last = i == depth - 1
        if last:
            h, out = ffn_half(h, mods, norm_g[i], bf(ffn_w_gu[i, 1]), bf(ffn_w_down[i, 1]), s=2, rows=rows_out,
                              seq_rows=seq, next_g=final_g, next_s=None, next_dtype=F32)
        else:
            h = ffn_half(h, mods, norm_g[i], bf(ffn_w_gu[i, 1]), bf(ffn_w_down[i, 1]), s=2, rows=rows_out,
                         seq_rows=seq)
    return out[:rows_lat].reshape(batch, seq, d)
```

```python
import functools
import math

import numpy as np
import jax
import jax.numpy as jnp
from jax import lax
from jax.experimental import pallas as pl
from jax.experimental.pallas import tpu as pltpu

F32 = jnp.float32
BF16 = jnp.bfloat16

VMEM_CAP_BYTES = 56 * 1024 * 1024
LANES = 128
SUBLANES = 8

NORM_EPS = 1e-6
N_MOD = 9
GRID_W = 64
ROPE_THETA = 10000.0
HY_DECAY_TARGET = 1e-2
HY_DECAY_PCT_SHORT = 0.3
HY_DECAY_PCT_LONG = 1.5
RW_HEAD = 64
RW_LORA_PAD = 128
GQA_HEAD = 128
GQA_HEADS = 16
GQA_KV_HEADS = 8
MLA_HEADS = 16
MLA_NOPE = 128
MLA_ROPE = 64
MLA_V = 128
MLA_Q_RANK = 512
MLA_KV_RANK = 512


def _cparams(sem):
    return pltpu.CompilerParams(dimension_semantics=sem, vmem_limit_bytes=VMEM_CAP_BYTES)


def _dot(a, b):
    return jnp.dot(a, b, preferred_element_type=F32)


def _dot_nt(a, b):
    return lax.dot_general(a, b, (((1,), (1,)), ((), ())), preferred_element_type=F32)


def _dot_tn(a, b):
    return lax.dot_general(a, b, (((0,), (0,)), ((), ())), preferred_element_type=F32)


def _split2(x):
    hi = x.astype(BF16)
    lo = (x - hi.astype(F32)).astype(BF16)
    return hi, lo


def _split3(x):
    hi = x.astype(BF16)
    r1 = x - hi.astype(F32)
    mid = r1.astype(BF16)
    lo = (r1 - mid.astype(F32)).astype(BF16)
    return hi, mid, lo


def _dot3(a, b, f=_dot):
    ah, al = _split2(a)
    bh, bl = _split2(b)
    return f(ah, bh) + f(ah, bl) + f(al, bh)


def _dotc3(mh, ml, d):
    dh, dl = _split2(d)
    return _dot(mh, dh) + _dot(ml, dh) + _dot(mh, dl)


def _dotc2(mh, ml, d):
    dh = d.astype(BF16)
    return _dot(mh, dh) + _dot(ml, dh)


def _rms(x, g):
    return x * lax.rsqrt(jnp.mean(x * x, -1, keepdims=True) + NORM_EPS) * g


def _pick(n, cands):
    for c in cands:
        if n % c == 0:
            return c
    raise ValueError(f"no tile for {n}")


def _adaln_body(c_ref, w_ref, b_ref, o_ref):
    c = c_ref[...]
    s = c * jax.nn.sigmoid(c)
    o_ref[0] = _dot(s.astype(BF16), w_ref[0].astype(BF16)) + b_ref[0]


def adaln_all(cvec8, ada_w, ada_b):
    depth, d, n = ada_w.shape
    tn = _pick(n, (2048, 1024, 512, 256, 128))
    return pl.pallas_call(
        _adaln_body,
        grid=(depth, n // tn),
        in_specs=[
            pl.BlockSpec((8, d), lambda l, j: (0, 0)),
            pl.BlockSpec((1, d, tn), lambda l, j: (l, 0, j)),
            pl.BlockSpec((1, 1, tn), lambda l, j: (l, 0, j)),
        ],
        out_specs=pl.BlockSpec((1, 8, tn), lambda l, j: (l, 0, j)),
        out_shape=jax.ShapeDtypeStruct((depth, 8, n), F32),
        compiler_params=_cparams(("arbitrary", "arbitrary")),
        name="adaln",
    )(cvec8, ada_w, ada_b.reshape(depth, 1, n))


def _mm_body(*refs, has_bias, act, has_norm, has_res):
    it = iter(refs)
    x_ref = next(it)
    w_ref = next(it)
    b_ref = next(it) if has_bias else None
    g_ref = next(it) if has_norm else None
    r_ref = next(it) if has_res else None
    gate_ref = next(it) if has_res else None
    o_ref = next(it)
    x = x_ref[...]
    if has_norm:
        x = _rms(x.astype(F32), g_ref[...])
    acc = _dot(x.astype(BF16), w_ref[...])
    if has_bias:
        acc = acc + b_ref[...]
    if act == "sigmoid":
        acc = jax.nn.sigmoid(acc)
    elif act == "tanh":
        acc = jnp.tanh(acc)
    if has_res:
        acc = r_ref[...] + gate_ref[0] * acc
    o_ref[...] = acc.astype(o_ref.dtype)


def mm(x, w, *, bias=None, act=None, norm_g=None, res=None, gate=None, out_dtype=F32,
       tm=512, rows=None, x_col_block=0, x_row_block0=0, seq_rows=None, name="mm"):
    k, n = w.shape
    m = rows if rows is not None else x.shape[0]
    assert m % tm == 0, (m, tm)
    tn = _pick(n, (512, 384, 256, 128))
    has_res = res is not None
    ins = [x, w]
    specs = [pl.BlockSpec((tm, k), lambda i, j: (i + x_row_block0, x_col_block)),
             pl.BlockSpec((k, tn), lambda i, j: (0, j))]
    if bias is not None:
        ins.append(bias.reshape(1, n).astype(F32))
        specs.append(pl.BlockSpec((1, tn), lambda i, j: (0, j)))
    if norm_g is not None:
        ins.append(norm_g.reshape(1, k).astype(F32))
        specs.append(pl.BlockSpec((1, k), lambda i, j: (0, 0)))
    if has_res:
        ngroups = gate.shape[0]
        per = seq_rows // tm
        ins += [res, gate]
        specs += [pl.BlockSpec((tm, tn), lambda i, j: (i, j)),
                  pl.BlockSpec((1, 1, tn), lambda i, j: (jnp.minimum(i // per, ngroups - 1), 0, j))]
    body = functools.partial(_mm_body, has_bias=bias is not None, act=act,
                             has_norm=norm_g is not None, has_res=has_res)
    return pl.pallas_call(
        body,
        grid=(m // tm, n // tn),
        in_specs=specs,
        out_specs=pl.BlockSpec((tm, tn), lambda i, j: (i, j)),
        out_shape=jax.ShapeDtypeStruct((m, n), out_dtype),
        compiler_params=_cparams(("parallel", "arbitrary")),
        name=name,
    )(*ins)


def _ffn_body(*refs, s, n_f, has_next, next_s):
    it = iter(refs)
    x_ref, mod_ref, g_ref, wa_ref, wb_ref, wd_ref = (next(it) for _ in range(6))
    gn_ref = next(it) if has_next else None
    o_ref = next(it)
    hn_ref = next(it) if has_next else None
    xn_sc = next(it)
    acc_sc = next(it)
    j = pl.program_id(1)

    @pl.when(j == 0)
    def _():
        n = _rms(x_ref[...], g_ref[s:s + 1, :])
        n = n * (1.0 + mod_ref[0, 3 * s + 1:3 * s + 2, :]) + mod_ref[0, 3 * s:3 * s + 1, :]
        xn_sc[...] = n.astype(BF16)
        acc_sc[...] = jnp.zeros_like(acc_sc)

    xn = xn_sc[...]
    a = _dot(xn, wa_ref[...])
    b = _dot(xn, wb_ref[...])
    h = (a * jax.nn.sigmoid(a)) * b
    acc_sc[...] += _dot(h.astype(BF16), wd_ref[...])

    @pl.when(j == n_f - 1)
    def _():
        xnew = x_ref[...] + 0.5 * mod_ref[0, 3 * s + 2:3 * s + 3, :] * acc_sc[...]
        o_ref[...] = xnew
        if has_next:
            hn = _rms(xnew, gn_ref[...])
            if next_s is not None:
                hn = hn * (1.0 + mod_ref[0, 3 * next_s + 1:3 * next_s + 2, :]) + mod_ref[0, 3 * next_s:3 * next_s + 1, :]
            hn_ref[...] = hn.astype(hn_ref.dtype)


def ffn_half(x, mods, norm_g3, w_gu, w_down, *, s, rows, seq_rows, next_g=None, next_s=None,
             next_dtype=F32, tm=512, fc=512):
    d = x.shape[1]
    d_ff = w_down.shape[0]
    n_f = d_ff // fc
    assert d_ff % fc == 0 and rows % tm == 0
    per = seq_rows // tm
    ngroups = mods.shape[0]
    has_next = next_g is not None
    ins = [x, mods, norm_g3, w_gu, w_gu, w_down]
    specs = [
        pl.BlockSpec((tm, d), lambda i, j: (i, 0)),
        pl.BlockSpec((1, N_MOD, d), lambda i, j: (jnp.minimum(i // per, ngroups - 1), 0, 0)),
        pl.BlockSpec((3, d), lambda i, j: (0, 0)),
        pl.BlockSpec((d, fc), lambda i, j: (0, j)),
        pl.BlockSpec((d, fc), lambda i, j: (0, n_f + j)),
        pl.BlockSpec((fc, d), lambda i, j: (j, 0)),
    ]
    out_shape = [jax.ShapeDtypeStruct((rows, d), F32)]
    out_specs = [pl.BlockSpec((tm, d), lambda i, j: (i, 0))]
    if has_next:
        ins.append(next_g.reshape(1, d))
        specs.append(pl.BlockSpec((1, d), lambda i, j: (0, 0)))
        out_shape.append(jax.ShapeDtypeStruct((rows, d), next_dtype))
        out_specs.append(pl.BlockSpec((tm, d), lambda i, j: (i, 0)))
    body = functools.partial(_ffn_body, s=s, n_f=n_f, has_next=has_next, next_s=next_s)
    outs = pl.pallas_call(
        body,
        grid=(rows // tm, n_f),
        in_specs=specs,
        out_specs=out_specs,
        out_shape=out_shape,
        scratch_shapes=[pltpu.VMEM((tm, d), BF16), pltpu.VMEM((tm, d), F32)],
        compiler_params=_cparams(("parallel", "arbitrary")),
        name="ffn_half",
    )(*ins)
    return outs if has_next else outs[0]


def _attn_body(*refs, use_lat):
    it = iter(refs)
    q_ref = next(it)
    kl_ref = next(it) if use_lat else None
    kc_ref = next(it)
    vl_ref = next(it) if use_lat else None
    vc_ref = next(it)
    o_ref = next(it)
    q = q_ref[...]
    s_c = _dot_nt(kc_ref[...], q)
    m = jnp.max(s_c, 0, keepdims=True)
    if use_lat:
        s_l = _dot_nt(kl_ref[...], q)
        m = jnp.maximum(m, jnp.max(s_l, 0, keepdims=True))
    p_c = jnp.exp(s_c - m)
    den = jnp.sum(p_c, 0, keepdims=True)
    o = _dot_tn(vc_ref[...], p_c.astype(BF16))
    if use_lat:
        p_l = jnp.exp(s_l - m)
        den = den + jnp.sum(p_l, 0, keepdims=True)
        o = o + _dot_tn(vl_ref[...], p_l.astype(BF16))
    o_ref[...] = (o / den).T.astype(o_ref.dtype)


def attention(q_part, k_part, v_arr, v_cb, *, dqk, n_heads, batch, seq, ctx_len, lat_queries, tq=512):
    q_parts, k_parts = [q_part], [k_part]
    ctx0 = (batch * seq) // ctx_len
    if lat_queries:
        tq = min(tq, seq)
        nq = seq // tq
        q_row = lambda b, t: b * nq + t
        rows_q = batch * seq
    else:
        tq = ctx_len
        nq = 1
        q_row = lambda b, t: ctx0 + b
        rows_q = batch * ctx_len
    ins, specs = [], []
    for arr, cb in q_parts:
        ins.append(arr)
        specs.append(pl.BlockSpec((tq, dqk), lambda b, h, t, cb=cb: (q_row(b, t), cb(h))))
    if lat_queries:
        for arr, cb in k_parts:
            ins.append(arr)
            specs.append(pl.BlockSpec((seq, dqk), lambda b, h, t, cb=cb: (b, cb(h))))
    for arr, cb in k_parts:
        ins.append(arr)
        specs.append(pl.BlockSpec((ctx_len, dqk), lambda b, h, t, cb=cb: (ctx0 + b, cb(h))))
    if lat_queries:
        ins.append(v_arr)
        specs.append(pl.BlockSpec((seq, LANES), lambda b, h, t: (b, v_cb(h))))
    ins.append(v_arr)
    specs.append(pl.BlockSpec((ctx_len, LANES), lambda b, h, t: (ctx0 + b, v_cb(h))))
    if lat_queries:
        o_row = lambda b, t: b * nq + t
    else:
        o_row = lambda b, t: b
    return pl.pallas_call(
        functools.partial(_attn_body, use_lat=lat_queries),
        grid=(batch, n_heads, nq),
        in_specs=specs,
        out_specs=pl.BlockSpec((tq, LANES), lambda b, h, t: (o_row(b, t), h)),
        out_shape=jax.ShapeDtypeStruct((rows_q, n_heads * LANES), BF16),
        compiler_params=_cparams(("parallel", "parallel", "arbitrary")),
        name="attention_lat" if lat_queries else "attention_ctx",
    )(*ins)


def _rope_tables(seq, ctx_len, batch, rot_dim):
    half = rot_dim // 4
    inv_freq = ROPE_THETA ** (-jnp.arange(half, dtype=F32) / half)
    t = jnp.arange(seq, dtype=jnp.int32)
    row = (t // GRID_W).astype(F32)
    col = (t % GRID_W).astype(F32)
    ang_r = row[:, None] * inv_freq[None]
    ang_c = col[:, None] * inv_freq[None]
    cos = jnp.concatenate([jnp.cos(ang_r), jnp.cos(ang_r), jnp.cos(ang_c), jnp.cos(ang_c)], -1)
    sin = jnp.concatenate([jnp.sin(ang_r), jnp.sin(ang_r), jnp.sin(ang_c), jnp.sin(ang_c)], -1)
    cos = jnp.concatenate([jnp.tile(cos, (batch, 1)), jnp.ones((batch * ctx_len, rot_dim), F32)], 0)
    sin = jnp.concatenate([jnp.tile(sin, (batch, 1)), jnp.zeros((batch * ctx_len, rot_dim), F32)], 0)
    return cos, sin


def _rot_cols(w, rot_dim):
    lead = w.shape[:-1]
    q = rot_dim // 4
    w5 = w.reshape(lead + (-1, 2, 2, q))
    rot = jnp.concatenate([-w5[..., 1:2, :], w5[..., 0:1, :]], axis=-2)
    return rot.reshape(w.shape)


def _gqa_prep_body(a_ref, ar_ref, cos_ref, sin_ref, g_ref, gr_ref, o_ref, *, scale):
    a = a_ref[...]
    r = lax.rsqrt(jnp.mean(a * a, -1, keepdims=True) + NORM_EPS) * scale
    o = (a * g_ref[...]) * cos_ref[...] + (ar_ref[...] * gr_ref[...]) * sin_ref[...]
    o_ref[...] = (o * r).astype(o_ref.dtype)


def gqa_prep(raw, cb0, cbr0, n_heads, cos, sin, g, g_rot, scale, tm=1088):
    m = raw.shape[0]
    return pl.pallas_call(
        functools.partial(_gqa_prep_body, scale=scale),
        grid=(m // tm, n_heads),
        in_specs=[
            pl.BlockSpec((tm, LANES), lambda i, h: (i, cb0 + h)),
            pl.BlockSpec((tm, LANES), lambda i, h: (i, cbr0 + h)),
            pl.BlockSpec((tm, LANES), lambda i, h: (i, 0)),
            pl.BlockSpec((tm, LANES), lambda i, h: (i, 0)),
            pl.BlockSpec((1, LANES), lambda i, h: (0, 0)),
            pl.BlockSpec((1, LANES), lambda i, h: (0, 0)),
        ],
        out_specs=pl.BlockSpec((tm, LANES), lambda i, h: (i, h)),
        out_shape=jax.ShapeDtypeStruct((m, n_heads * LANES), BF16),
        compiler_params=_cparams(("parallel", "arbitrary")),
        name="gqa_prep",
    )(raw, raw, cos, sin, g.reshape(1, LANES), g_rot.reshape(1, LANES))


def _mla_rope_body(x_ref, c_ref, s_ref, o_ref, *, scale, with_nope):
    if with_nope:
        a = x_ref[:, :LANES]
        b = x_ref[:, LANES:]
    else:
        b = x_ref[...]
    roped = (b * c_ref[...] + pltpu.roll(b, LANES // 2, 1) * s_ref[...]) * scale
    if with_nope:
        o_ref[:, :LANES] = (a * scale).astype(o_ref.dtype)
        o_ref[:, LANES:] = roped.astype(o_ref.dtype)
    else:
        o_ref[...] = roped.astype(o_ref.dtype)


def mla_rope(raw, cb0, n_heads, cos, sin, scale, with_nope, tm=1088):
    m = raw.shape[0]
    w = 2 * LANES if with_nope else LANES
    return pl.pallas_call(
        functools.partial(_mla_rope_body, scale=scale, with_nope=with_nope),
        grid=(m // tm, n_heads),
        in_specs=[
            pl.BlockSpec((tm, w), lambda i, h: (i, cb0 + h)),
            pl.BlockSpec((tm, LANES), lambda i, h: (i, 0)),
            pl.BlockSpec((tm, LANES), lambda i, h: (i, 0)),
        ],
        out_specs=pl.BlockSpec((tm, w), lambda i, h: (i, h)),
        out_shape=jax.ShapeDtypeStruct((m, n_heads * w), BF16),
        compiler_params=_cparams(("parallel", "arbitrary")),
        name="mla_rope",
    )(raw, cos, sin)


def _mla_kcat_body(kn_ref, pe_ref, c_ref, s_ref, o_ref):
    b = pe_ref[...]
    o_ref[:, :LANES] = kn_ref[...]
    o_ref[:, LANES:] = (b * c_ref[...] + pltpu.roll(b, LANES // 2, 1) * s_ref[...]).astype(o_ref.dtype)


def mla_kcat(kv, down, pe_cb, n_heads, cos, sin, tm=1088):
    m = kv.shape[0]
    return pl.pallas_call(
        _mla_kcat_body,
        grid=(m // tm, n_heads),
        in_specs=[
            pl.BlockSpec((tm, LANES), lambda i, h: (i, 2 * h)),
            pl.BlockSpec((tm, LANES), lambda i, h: (i, pe_cb)),
            pl.BlockSpec((tm, LANES), lambda i, h: (i, 0)),
            pl.BlockSpec((tm, LANES), lambda i, h: (i, 0)),
        ],
        out_specs=pl.BlockSpec((tm, 2 * LANES), lambda i, h: (i, h)),
        out_shape=jax.ShapeDtypeStruct((m, n_heads * 2 * LANES), BF16),
        compiler_params=_cparams(("parallel", "arbitrary")),
        name="mla_kcat",
    )(kv, down, cos, sin)


class _FftPlan:
    def __init__(self, L):
        self.L = L
        self.N = 2 * L
        self.N2 = 128 if L >= 1024 else 64
        self.N1 = self.N // self.N2
        self.n1_in = self.N1 // 2
        self.K1 = self.N1 // 2 + 1
        self.nq = self.N2 // SUBLANES
        self.spec_rows = self.K1 * 2 * self.N2
        self.kf_cols = max(self.n1_in * SUBLANES, LANES)
        self.kf_rows = 2 * self.K1 * SUBLANES
        self.kb_cols = -(-self.kf_rows // LANES) * LANES
        self.kb_rows = self.n1_in * SUBLANES
        self.k1_unroll = 3 if self.K1 % 3 == 0 else 1

    def constants(self):
        N, N1, N2, K1, n1_in = self.N, self.N1, self.N2, self.K1, self.n1_in
        eye = np.eye(SUBLANES)
        k1 = np.arange(K1)[:, None]
        n1 = np.arange(n1_in)[None, :]
        th = 2 * np.pi * k1 * n1 / N1
        kf = np.zeros((self.kf_rows, self.kf_cols))
        kf[:K1 * SUBLANES, :n1_in * SUBLANES] = np.kron(np.cos(th), eye)
        kf[K1 * SUBLANES:, :n1_in * SUBLANES] = np.kron(-np.sin(th), eye)
        c = np.full((K1,), 2.0)
        c[0] = 1.0
        c[-1] = 1.0
        thb = 2 * np.pi * np.arange(n1_in)[:, None] * np.arange(K1)[None, :] / N1
        kb = np.zeros((self.kb_rows, self.kb_cols))
        kb[:, :K1 * SUBLANES] = np.kron(np.cos(thb) * c[None, :], eye)
        kb[:, K1 * SUBLANES:2 * K1 * SUBLANES] = np.kron(-np.sin(thb) * c[None, :], eye)
        a = 2 * np.pi * np.outer(np.arange(N2), np.arange(N2)) / N2
        C, S = np.cos(a), np.sin(a)
        w2f = np.block([[C, S], [-S, C]])
        w2i = np.block([[C, -S], [S, C]])
        tw_ang = 2 * np.pi * np.outer(np.arange(K1), np.arange(N2)) / N
        twr = np.repeat(np.cos(tw_ang).reshape(-1, 1), LANES, axis=1)
        twi = np.repeat(np.sin(tw_ang).reshape(-1, 1), LANES, axis=1)

        def hl(m):
            m32 = jnp.asarray(m, F32)
            hi = m32.astype(BF16)
            lo = (m32 - hi.astype(F32)).astype(BF16)
            return hi, lo

        out = {}
        for name, mat in (("kf", kf), ("kb", kb), ("w2f", w2f), ("w2i", w2i)):
            out[name + "_h"], out[name + "_l"] = hl(mat)
        out["twr"] = jnp.asarray(twr, F32)
        out["twi"] = jnp.asarray(twi, F32)
        return out


def _fft_major_fwd(plan, x_ref, kf_h, kf_l, spec_ref, dotc=_dotc3):
    N2, K1, n1_in = plan.N2, plan.K1, plan.n1_in
    pad_rows = plan.kf_cols - n1_in * SUBLANES

    def body(q, carry):
        tiles = [x_ref[pl.ds(pl.multiple_of(N2 * n + SUBLANES * q, SUBLANES), SUBLANES), :] for n in range(n1_in)]
        if pad_rows:
            tiles.append(jnp.zeros((pad_rows, tiles[0].shape[1]), F32))
        g = jnp.concatenate(tiles, axis=0)
        o = dotc(kf_h[...], kf_l[...], g)
        for part in range(2):
            for k in range(K1):
                src = (part * K1 + k) * SUBLANES
                dst = (2 * k + part) * N2
                spec_ref[pl.ds(pl.multiple_of(dst + SUBLANES * q, SUBLANES), SUBLANES), :] = o[src:src + SUBLANES, :]
        return carry

    lax.fori_loop(0, plan.nq, body, 0, unroll=2)


def _cmul(ar, ai, br, bi):
    return ar * br - ai * bi, ar * bi + ai * br


def _fft_minor_fwd(plan, spec_ref, w2f_h, w2f_l, twr_ref, twi_ref):
    N2 = plan.N2

    def body(k, carry):
        base = pl.multiple_of(2 * N2 * k, 2 * N2)
        tb = pl.multiple_of(N2 * k, N2)
        ar = spec_ref[pl.ds(base, N2), :]
        ai = spec_ref[pl.ds(base + N2, N2), :]
        tr = twr_ref[pl.ds(tb, N2), :]
        ti = twi_ref[pl.ds(tb, N2), :]
        xr, xi = _cmul(ar, ai, tr, -ti)
        y = _dotc3(w2f_h[...], w2f_l[...], jnp.concatenate([xr, xi], axis=0))
        spec_ref[pl.ds(base, 2 * N2), :] = y
        return carry

    lax.fori_loop(0, plan.K1, body, 0, unroll=plan.k1_unroll)


def _fft_fwd(plan, x_ref, consts, spec_ref):
    _fft_major_fwd(plan, x_ref, consts["kf_h"], consts["kf_l"], spec_ref)
    _fft_minor_fwd(plan, spec_ref, consts["w2f_h"], consts["w2f_l"], consts["twr"], consts["twi"])


def _fft_conv_apply(plan, x_ref, g_ref, consts, spec_ref, y_ref):
    N2, K1, n1_in = plan.N2, plan.K1, plan.n1_in
    _fft_major_fwd(plan, x_ref, consts["kf_h"], consts["kf_l"], spec_ref, dotc=_dotc2)
    w2f_h, w2f_l, w2i_h, w2i_l = consts["w2f_h"], consts["w2f_l"], consts["w2i_h"], consts["w2i_l"]
    twr_ref, twi_ref = consts["twr"], consts["twi"]

    def body(k, carry):
        base = pl.multiple_of(2 * N2 * k, 2 * N2)
        tb = pl.multiple_of(N2 * k, N2)
        tr = twr_ref[pl.ds(tb, N2), :]
        ti = twi_ref[pl.ds(tb, N2), :]
        xr, xi = _cmul(spec_ref[pl.ds(base, N2), :], spec_ref[pl.ds(base + N2, N2), :], tr, -ti)
        y = _dotc2(w2f_h[...], w2f_l[...], jnp.concatenate([xr, xi], axis=0))
        pr, pi = _cmul(y[:N2], y[N2:], g_ref[pl.ds(base, N2), :], g_ref[pl.ds(base + N2, N2), :])
        z = _dotc2(w2i_h[...], w2i_l[...], jnp.concatenate([pr, pi], axis=0))
        zr, zi = _cmul(z[:N2], z[N2:], tr, ti)
        spec_ref[pl.ds(base, N2), :] = zr
        spec_ref[pl.ds(base + N2, N2), :] = zi
        return carry

    lax.fori_loop(0, K1, body, 0, unroll=plan.k1_unroll)

    pad_rows = plan.kb_cols - plan.kf_rows
    kb_h, kb_l = consts["kb_h"], consts["kb_l"]

    def body2(q, carry):
        tiles = []
        for part in range(2):
            for k in range(K1):
                src = (2 * k + part) * N2
                tiles.append(spec_ref[pl.ds(pl.multiple_of(src + SUBLANES * q, SUBLANES), SUBLANES), :])
        if pad_rows:
            tiles.append(jnp.zeros((pad_rows, tiles[0].shape[1]), F32))
        t = jnp.concatenate(tiles, axis=0)
        o = _dotc2(kb_h[...], kb_l[...], t)
        for n in range(n1_in):
            y_ref[pl.ds(pl.multiple_of(N2 * n + SUBLANES * q, SUBLANES), SUBLANES), :] = o[n * SUBLANES:(n + 1) * SUBLANES, :]
        return carry

    lax.fori_loop(0, plan.nq, body2, 0, unroll=2)


_CONST_NAMES = ("kf_h", "kf_l", "kb_h", "kb_l", "w2f_h", "w2f_l", "w2i_h", "w2i_l", "twr", "twi")


def _const_specs(consts, nargs):
    ins, specs = [], []
    for nme in _CONST_NAMES:
        a = consts[nme]
        ins.append(a)
        if nargs == 1:
            specs.append(pl.BlockSpec(a.shape, lambda c: (0, 0)))
        else:
            specs.append(pl.BlockSpec(a.shape, lambda c, b: (0, 0)))
    return ins, specs


def _hy_hidden_body(z_ref, w1_ref, b1_ref, w2_ref, b2_ref, sf_ref, o_ref):
    h = jnp.sin(sf_ref[0:1, :] * (_dot3(z_ref[...], w1_ref[...]) + b1_ref[...]))
    h = jnp.sin(sf_ref[1:2, :] * (_dot3(h, w2_ref[...]) + b2_ref[...]))
    o_ref[...] = h


def hyena_hidden(L, f_w1, f_b1, f_w2, f_b2, sin_freq):
    emb, hid = f_w1.shape
    bands = (emb - 1) // 2
    t = jnp.linspace(0.0, 1.0, L, dtype=F32)[:, None]
    w = (2.0 * math.pi / L) * jnp.arange(L, dtype=F32)[:, None]
    f = jnp.linspace(1e-4, bands - 1, bands, dtype=F32)[None]
    z = jnp.concatenate([t, jnp.cos(f * w), -jnp.sin(f * w)], -1)
    embp = -(-emb // LANES) * LANES
    z = jnp.pad(z, ((0, 0), (0, embp - emb)))
    w1 = jnp.pad(f_w1, ((0, embp - emb), (0, 0)))
    ins = [z, w1, f_b1.reshape(1, hid), f_w2, f_b2.reshape(1, hid), sin_freq]
    return pl.pallas_call(
        _hy_hidden_body,
        out_shape=jax.ShapeDtypeStruct((L, hid), F32),
        name="hyena_hidden",
    )(*ins)


def _hy_filter_body(*refs, plan, n_order):
    it = iter(refs)
    h_ref = next(it)
    w3_refs = [next(it) for _ in range(2 * n_order)]
    dl_ref = next(it)
    consts = {nme: next(it) for nme in _CONST_NAMES}
    g_ref = next(it)
    x_sc = next(it)
    spec_a = next(it)
    spec_b = next(it)
    L, N2, K1 = plan.L, plan.N2, plan.K1
    t = lax.broadcasted_iota(jnp.int32, (L, LANES), 0).astype(F32) * (1.0 / (L - 1))
    win = jnp.exp(-t * dl_ref[...])
    row0 = lax.broadcasted_iota(jnp.int32, (L, LANES), 0) == 0
    h = h_ref[...]
    for n in range(n_order):
        fwd = _dot3(h, w3_refs[2 * n][...]) * win
        bwd = jnp.where(row0, 0.0, _dot3(h, w3_refs[2 * n + 1][...]) * win)
        ss = jnp.sum(fwd * fwd, 0, keepdims=True) + jnp.sum(bwd * bwd, 0, keepdims=True)
        nu = lax.rsqrt(ss + 1e-12) * (1.0 / plan.N)
        x_sc[...] = fwd * nu
        _fft_fwd(plan, x_sc, consts, spec_a)
        x_sc[...] = bwd * nu
        _fft_fwd(plan, x_sc, consts, spec_b)

        def body(k, carry):
            base = pl.multiple_of(2 * N2 * k, 2 * N2)
            g_ref[n, pl.ds(base, N2), :] = spec_a[pl.ds(base, N2), :] + spec_b[pl.ds(base, N2), :]
            g_ref[n, pl.ds(base + N2, N2), :] = spec_a[pl.ds(base + N2, N2), :] - spec_b[pl.ds(base + N2, N2), :]
            return carry

        lax.fori_loop(0, K1, body, 0)


def hyena_filter_spectra(plan, consts, hidden, f_w3, d_model, n_order):
    L = plan.L
    hid = hidden.shape[1]
    ncb = d_model // LANES
    deltas = jnp.abs(jnp.linspace(math.log(HY_DECAY_TARGET) / HY_DECAY_PCT_SHORT,
                                  math.log(HY_DECAY_TARGET) / HY_DECAY_PCT_LONG, d_model, dtype=F32)).reshape(1, d_model)
    ins = [hidden]
    specs = [pl.BlockSpec((L, hid), lambda c: (0, 0))]
    for n in range(n_order):
        for dr in range(2):
            ins.append(f_w3)
            specs.append(pl.BlockSpec((hid, LANES), lambda c, n=n, dr=dr: (0, (2 * n + dr) * ncb + c)))
    ins.append(deltas)
    specs.append(pl.BlockSpec((1, LANES), lambda c: (0, c)))
    ci, cs = _const_specs(consts, 1)
    ins += ci
    specs += cs
    return pl.pallas_call(
        functools.partial(_hy_filter_body, plan=plan, n_order=n_order),
        grid=(ncb,),
        in_specs=specs,
        out_specs=pl.BlockSpec((n_order, plan.spec_rows, LANES), lambda c: (0, 0, c)),
        out_shape=jax.ShapeDtypeStruct((n_order, plan.spec_rows, d_model), F32),
        scratch_shapes=[pltpu.VMEM((L, LANES), F32), pltpu.VMEM((plan.spec_rows, LANES), F32),
                        pltpu.VMEM((plan.spec_rows, LANES), F32)],
        compiler_params=_cparams(("arbitrary",)),
        name="hyena_filter",
    )(*ins)


def _conv3(p, w_ref, b_ref, L):
    rows = lax.broadcasted_iota(jnp.int32, p.shape, 0)
    prev = jnp.where(rows == 0, 0.0, pltpu.roll(p, 1, 0))
    nxt = jnp.where(rows == L - 1, 0.0, pltpu.roll(p, L - 1, 0))
    return prev * w_ref[0:1, :] + p * w_ref[1:2, :] + nxt * w_ref[2:3, :] + b_ref[...]


def _hy_conv_body(*refs, plan, order, conv_u):
    it = iter(refs)
    u_ref, gate_ref = next(it), next(it)
    if conv_u:
        cwu_ref, cbu_ref = next(it), next(it)
    cwg_ref, cbg_ref = next(it), next(it)
    skip_ref = next(it)
    g_ref = next(it)
    consts = {nme: next(it) for nme in _CONST_NAMES}
    o_ref = next(it)
    x_sc, y_sc, spec_sc = next(it), next(it), next(it)
    L = plan.L
    u = u_ref[...]
    if conv_u:
        u = _conv3(u, cwu_ref, cbu_ref, L)
    x_sc[...] = u
    _fft_conv_apply(plan, x_sc, g_ref.at[0], consts, spec_sc, y_sc)
    gate = _conv3(gate_ref[...], cwg_ref, cbg_ref, L)
    o_ref[...] = (gate * (y_sc[...] + x_sc[...] * skip_ref[order:order + 1, :])).astype(o_ref.dtype)


def hyena_conv(plan, consts, u_arr, u_cb0, gate_arr, gate_cb0, conv_w, conv_b, skip, spectra, *,
               order, conv_u, u_row_block0, gate_row_block0, batch, d_model, out_dtype):
    L = plan.L
    ncb = d_model // LANES
    ins = [u_arr, gate_arr]
    specs = [pl.BlockSpec((L, LANES), lambda c, b: (u_row_block0 + b, u_cb0 + c)),
             pl.BlockSpec((L, LANES), lambda c, b: (gate_row_block0 + b, gate_cb0 + c))]
    if conv_u:
        ins += [conv_w, conv_b]
        specs += [pl.BlockSpec((3, LANES), lambda c, b: (0, u_cb0 + c)),
                  pl.BlockSpec((1, LANES), lambda c, b: (0, u_cb0 + c))]
    ins += [conv_w, conv_b, skip, spectra]
    specs += [pl.BlockSpec((3, LANES), lambda c, b: (0, gate_cb0 + c)),
              pl.BlockSpec((1, LANES), lambda c, b: (0, gate_cb0 + c)),
              pl.BlockSpec((skip.shape[0], LANES), lambda c, b: (0, c)),
              pl.BlockSpec((1, plan.spec_rows, LANES), lambda c, b: (order, 0, c))]
    ci, cs = _const_specs(consts, 2)
    ins += ci
    specs += cs
    return pl.pallas_call(
        functools.partial(_hy_conv_body, plan=plan, order=order, conv_u=conv_u),
        grid=(ncb, batch),
        in_specs=specs,
        out_specs=pl.BlockSpec((L, LANES), lambda c, b: (b, c)),
        out_shape=jax.ShapeDtypeStruct((batch * L, d_model), out_dtype),
        scratch_shapes=[pltpu.VMEM((L, LANES), F32), pltpu.VMEM((L, LANES), F32),
                        pltpu.VMEM((plan.spec_rows, LANES), F32)],
        compiler_params=_cparams(("parallel", "arbitrary")),
        name=f"hyena_conv{order}_L{L}",
    )(*ins)


def hyena_mixer(hmix, p, *, batch, seq, ctx_len, d_model, want_ctx):
    n_order = p["skip"].shape[0]
    pr = mm(hmix, p["w_in"], bias=p["b_in"], name="hy_in", rows=(batch * seq + (batch * ctx_len if want_ctx else 0)))
    conv_b = p["conv_b"].reshape(1, -1)
    ncb = d_model // LANES
    outs = []
    streams = [(seq, 0)]
    if want_ctx:
        streams.append((ctx_len, (batch * seq) // ctx_len))
    for L, rb0 in streams:
        plan = _FftPlan(L)
        consts = plan.constants()
        hidden = hyena_hidden(L, p["f_w1"], p["f_b1"], p["f_w2"], p["f_b2"], p["sin_freq"])
        spectra = hyena_filter_spectra(plan, consts, hidden, p["f_w3"], d_model, n_order)
        z1 = hyena_conv(plan, consts, pr, 0, pr, ncb, p["conv_w"], conv_b, p["skip"], spectra,
                        order=0, conv_u=True, u_row_block0=rb0, gate_row_block0=rb0, batch=batch, d_model=d_model,
                        out_dtype=F32)
        z2 = hyena_conv(plan, consts, z1, 0, pr, 2 * ncb, p["conv_w"], conv_b, p["skip"], spectra,
                        order=1, conv_u=False, u_row_block0=0, gate_row_block0=rb0, batch=batch, d_model=d_model,
                        out_dtype=BF16)
        outs.append(z2)
    return outs[0] if len(outs) == 1 else jnp.concatenate(outs, 0)


def _headsum(x, e_ref):
    hi, lo = _split2(x)
    return _dot(hi, e_ref[...]) + _dot(lo, e_ref[...])


def _rw_mix_body(h_ref, hp_ref, hn_ref, mix_ref, o_ref, *, tm, lat_tiles, tiles_per_seq):
    i = pl.program_id(0)
    is_ctx = i >= lat_tiles
    first = jnp.logical_or(is_ctx, i % tiles_per_seq == 0)
    last = jnp.logical_or(is_ctx, i % tiles_per_seq == tiles_per_seq - 1)
    h = h_ref[...]
    rows = lax.broadcasted_iota(jnp.int32, h.shape, 0)
    pedge = jnp.where(first, 0.0, hp_ref[SUBLANES - 1:SUBLANES, :])
    nedge = jnp.where(last, 0.0, hn_ref[0:1, :])
    prev = jnp.where(rows == 0, pedge, pltpu.roll(h, 1, 0))
    nxt = jnp.where(rows == tm - 1, nedge, pltpu.roll(h, tm - 1, 0))
    dp = prev - h
    dn = nxt - h
    for n in range(o_ref.shape[0]):
        o_ref[n] = (h + dp * mix_ref[0, n:n + 1, :] + dn * mix_ref[1, n:n + 1, :]).astype(o_ref.dtype)


def rwkv_mix(h, mix, *, batch, seq, ctx_len):
    m, d = h.shape
    tm = ctx_len
    n_shift = mix.shape[1]
    nblk = m // SUBLANES
    per = tm // SUBLANES
    body = functools.partial(_rw_mix_body, tm=tm, lat_tiles=(batch * seq) // tm, tiles_per_seq=seq // tm)
    return pl.pallas_call(
        body,
        grid=(m // tm,),
        in_specs=[
            pl.BlockSpec((tm, d), lambda i: (i, 0)),
            pl.BlockSpec((SUBLANES, d), lambda i: (jnp.maximum(i * per - 1, 0), 0)),
            pl.BlockSpec((SUBLANES, d), lambda i: (jnp.minimum((i + 1) * per, nblk - 1), 0)),
            pl.BlockSpec((2, n_shift, d), lambda i: (0, 0, 0)),
        ],
        out_specs=pl.BlockSpec((n_shift, tm, d), lambda i: (0, i, 0)),
        out_shape=jax.ShapeDtypeStruct((n_shift, m, d), BF16),
        compiler_params=_cparams(("parallel",)),
        name="rwkv_mix",
    )(h, h, h, mix)


def _rw_prep_body(k_ref, lw0_ref, lw1_ref, al0_ref, al1_ref, w0_ref, a0_ref, kk_ref, ka_ref, e_ref,
                  olw0, olw1, okd0, okd1, obb0, obb1, oaa):
    k = k_ref[...]
    kk0 = k * kk_ref[...]
    kk = kk0 * lax.rsqrt(_headsum(kk0 * kk0, e_ref) + 1e-12)
    oaa[...] = -kk
    for d, (lw_ref, al_ref, olw, okd, obb) in enumerate(((lw0_ref, al0_ref, olw0, okd0, obb0),
                                                         (lw1_ref, al1_ref, olw1, okd1, obb1))):
        z = -(w0_ref[d:d + 1, :] + lw_ref[...])
        softplus = jnp.maximum(z, 0.0) + jnp.log(1.0 + jnp.exp(-jnp.abs(z)))
        log_w = -softplus - 0.5
        olw[...] = -jnp.exp(log_w)
        a = jax.nn.sigmoid(a0_ref[d:d + 1, :] + al_ref[...])
        okd[...] = k * (1.0 + (a - 1.0) * ka_ref[...])
        obb[...] = kk * a


def rwkv_prep(k, lw0, lw1, al0, al1, w0, a0, k_k, k_a, e128, tm=1088):
    m, d = k.shape
    blk = pl.BlockSpec((tm, LANES), lambda i, c: (i, c))
    vec2 = pl.BlockSpec((2, LANES), lambda i, c: (0, c))
    vec1 = pl.BlockSpec((1, LANES), lambda i, c: (0, c))
    sh = jax.ShapeDtypeStruct((m, d), F32)
    return pl.pallas_call(
        _rw_prep_body,
        grid=(m // tm, d // LANES),
        in_specs=[blk] * 5 + [vec2, vec2, vec1, vec1, pl.BlockSpec((LANES, LANES), lambda i, c: (0, 0))],
        out_specs=[blk] * 7,
        out_shape=[sh] * 7,
        compiler_params=_cparams(("parallel", "arbitrary")),
        name="rwkv_prep",
    )(k, lw0, lw1, al0, al1, w0, a0, k_k.reshape(1, d), k_a.reshape(1, d), e128)


def _b16(x):
    return x.astype(BF16)


def _wkv_body(r_ref, lw_ref, k_ref, v_ref, a_ref, b_ref, o_ref, s_sc, *, chunk, n_heads, reverse):
    c = pl.program_id(2)

    @pl.when(c == 0)
    def _():
        s_sc[...] = jnp.zeros_like(s_sc)

    C = chunk
    hs = range(n_heads)
    row = lax.broadcasted_iota(jnp.int32, (C, C), 0)
    col = lax.broadcasted_iota(jnp.int32, (C, C), 1)
    if reverse:
        tri_incl = col >= row
        tri_strict = col > row
        last = 0
    else:
        tri_incl = col <= row
        tri_strict = col < row
        last = C - 1
    same = lambda s: (row >> int(math.log2(s))) == (col >> int(math.log2(s)))
    eye = (row == col).astype(F32)
    cum_m = tri_incl.astype(BF16)
    sls = [slice(h * RW_HEAD, (h + 1) * RW_HEAD) for h in hs]

    lw = [lw_ref[:, sl] for sl in sls]
    cum = []
    for h in hs:
        l1, l2, l3 = _split3(lw[h])
        cum.append(_dot(cum_m, l1) + _dot(cum_m, l2) + _dot(cum_m, l3))
    p_inc = [jnp.exp(cum[h]) for h in hs]
    x1 = [jnp.concatenate([a_ref[:, sls[h]] * jnp.exp(cum[h] - lw[h]), r_ref[:, sls[h]] * p_inc[h]], axis=0) for h in hs]
    x2 = []
    for h in hs:
        p_inv = jnp.exp(-cum[h])
        x2.append(jnp.concatenate([b_ref[:, sls[h]] * p_inv, k_ref[:, sls[h]] * p_inv], axis=0))
    v = [v_ref[:, sl] for sl in sls]
    s0 = [s_sc[h] for h in hs]
    mfull = [_dot3(x1[h], x2[h], _dot_nt) for h in hs]
    w0 = [_dot3(x1[h], s0[h], _dot_nt) for h in hs]
    n_ab = [jnp.where(tri_strict, mfull[h][:C, :C], 0.0) for h in hs]
    a_ak = [_b16(jnp.where(tri_strict, mfull[h][:C, C:], 0.0)) for h in hs]
    a_r = [_b16(jnp.concatenate([jnp.where(tri_incl, mfull[h][C:, :C], 0.0),
                                 jnp.where(tri_incl, mfull[h][C:, C:], 0.0)], axis=1)) for h in hs]
    rhs = [w0[h][:C] + _dot(a_ak[h], _b16(v[h])) for h in hs]

    n4 = [jnp.where(same(4), n_ab[h], 0.0) for h in hs]
    n4b = [_b16(n4[h]) for h in hs]
    sq = [_dot(n4b[h], n4b[h]) for h in hs]
    t = [(eye + n4[h]) + _dot(_b16(eye + n4[h]), _b16(sq[h])) for h in hs]
    s = 4
    while s < C:
        off = jnp.logical_and(jnp.logical_and(same(2 * s), jnp.logical_not(same(s))), tri_strict)
        tb = [_b16(t[h]) for h in hs]
        tmp = [_dot(tb[h], _b16(jnp.where(off, n_ab[h], 0.0))) for h in hs]
        t = [t[h] + _dot(_b16(tmp[h]), tb[h]) for h in hs]
        s *= 2
    tb = [_b16(t[h]) for h in hs]
    u = [_dot(tb[h], _b16(rhs[h])) for h in hs]
    res = [rhs[h] - u[h] + _dot3(n_ab[h], u[h]) for h in hs]
    u = [u[h] + _dot(tb[h], _b16(res[h])) for h in hs]
    uv = [jnp.concatenate([u[h], v[h]], axis=0) for h in hs]
    y = [w0[h][C:] + _dot(a_r[h], _b16(uv[h])) for h in hs]
    for h in hs:
        s_sc[h] = (s0[h] + _dot3(uv[h], x2[h], _dot_tn)) * p_inc[h][last:last + 1, :]
    o_ref[...] = jnp.concatenate(y, axis=1)


def wkv7(r, lw, kd, v, aa, bb, *, reverse, batch, seq, ctx_len, chunk=128, heads_per_step=8):
    m, d = r.shape
    heads_per_step = min(heads_per_step, d // RW_HEAD)
    wl = heads_per_step * RW_HEAD
    n_ctx = ctx_len // chunk
    n_lat = seq // chunk
    n_chunks = n_ctx + n_lat
    ctx0 = (batch * seq) // chunk

    def row_block(b, c):
        if reverse:
            return jnp.where(c < n_ctx, ctx0 + b * n_ctx + (n_ctx - 1 - c), b * n_lat + (n_lat - 1 - (c - n_ctx)))
        return jnp.where(c < n_ctx, ctx0 + b * n_ctx + c, b * n_lat + (c - n_ctx))

    blk = pl.BlockSpec((chunk, wl), lambda b, g, c: (row_block(b, c), g))
    body = functools.partial(_wkv_body, chunk=chunk, n_heads=heads_per_step, reverse=reverse)
    return pl.pallas_call(
        body,
        grid=(batch, d // wl, n_chunks),
        in_specs=[blk] * 6,
        out_specs=blk,
        out_shape=jax.ShapeDtypeStruct((m, d), F32),
        scratch_shapes=[pltpu.VMEM((heads_per_step, RW_HEAD, RW_HEAD), F32)],
        compiler_params=_cparams(("parallel", "parallel", "arbitrary")),
        name="wkv7_rev" if reverse else "wkv7_fwd",
    )(r, lw, kd, v, aa, bb)


def _rw_readout_body(y0_ref, y1_ref, r_ref, v_ref, kd0_ref, kd1_ref, g_ref, rk_ref, lnw_ref, lnb_ref, e_ref, o_ref,
                     *, gn_eps):
    y = y0_ref[...] + y1_ref[...]
    inv = 1.0 / RW_HEAD
    mu = _headsum(y, e_ref) * inv
    dlt = y - mu
    var = _headsum(dlt * dlt, e_ref) * inv
    yn = dlt * lax.rsqrt(var + gn_eps) * lnw_ref[...] + lnb_ref[...]
    rk = r_ref[...] * rk_ref[...]
    bonus = (_headsum(rk * kd0_ref[...], e_ref) + _headsum(rk * kd1_ref[...], e_ref)) * v_ref[...]
    o_ref[...] = ((yn + bonus) * g_ref[...]).astype(o_ref.dtype)


def rwkv_readout(y0, y1, r, v, kd0, kd1, g, r_k, ln_w, ln_b, e128, gn_eps, rows, tm=512):
    d = r.shape[1]
    blk = pl.BlockSpec((tm, LANES), lambda i, c: (i, c))
    vec1 = pl.BlockSpec((1, LANES), lambda i, c: (0, c))
    return pl.pallas_call(
        functools.partial(_rw_readout_body, gn_eps=gn_eps),
        grid=(rows // tm, d // LANES),
        in_specs=[blk] * 7 + [vec1] * 3 + [pl.BlockSpec((LANES, LANES), lambda i, c: (0, 0))],
        out_specs=blk,
        out_shape=jax.ShapeDtypeStruct((rows, d), BF16),
        compiler_params=_cparams(("parallel", "arbitrary")),
        name="rwkv_readout",
    )(y0, y1, r, v, kd0, kd1, g, r_k.reshape(1, d), ln_w.reshape(1, d), ln_b.reshape(1, d), e128)


def rwkv_mixer(hmix, p, *, batch, seq, ctx_len, d_model):
    xs = rwkv_mix(hmix, p["mix"], batch=batch, seq=seq, ctx_len=ctx_len)
    m = hmix.shape[0]
    xs2 = xs.reshape(xs.shape[0] * m, d_model)
    tm = 1088 if m % 1088 == 0 else 512

    def proj(n, w, **kw):
        return mm(xs2, w, tm=tm, rows=m, x_row_block0=n * (m // tm), **kw)

    r = proj(0, p["w_r"], name="rw_r")
    k = proj(2, p["w_k"], name="rw_k")
    v = proj(3, p["w_v"], name="rw_v")
    g = mm(proj(5, p["g1"], act="sigmoid", out_dtype=BF16, name="rw_g1"), p["g2"], tm=tm, name="rw_g2")
    lws, als = [], []
    for d in range(2):
        tw = proj(1, p["w1"][d], act="tanh", out_dtype=BF16, name="rw_w1")
        lws.append(mm(tw, p["w2"][d], tm=tm, name="rw_w2"))
        ah = proj(4, p["a1"][d], out_dtype=BF16, name="rw_a1")
        als.append(mm(ah, p["a2"][d], tm=tm, name="rw_a2"))
    e128 = jnp.asarray(np.kron(np.eye(LANES // RW_HEAD), np.ones((RW_HEAD, RW_HEAD))), BF16)
    lw0, lw1, kd0, kd1, bb0, bb1, aa = rwkv_prep(k, lws[0], lws[1], als[0], als[1], p["w0"], p["a0"],
                                                 p["k_k"], p["k_a"], e128, tm=tm)
    kw = dict(batch=batch, seq=seq, ctx_len=ctx_len)
    y0 = wkv7(r, lw0, kd0, v, aa, bb0, reverse=False, **kw)
    y1 = wkv7(r, lw1, kd1, v, aa, bb1, reverse=True, **kw)
    return rwkv_readout(y0, y1, r, v, kd0, kd1, g, p["r_k"], p["ln_w"], p["ln_b"], e128,
                        1e-5 * RW_HEAD, rows=m)


def mla_mixer(hmix, p, *, batch, seq, ctx_len, want_ctx):
    m = hmix.shape[0]
    tm = 1088 if m % 1088 == 0 else 512
    scale = (MLA_NOPE + MLA_ROPE) ** -0.5
    down = mm(hmix, p["w_down_ext"], tm=tm, name="mla_down")
    kq = MLA_Q_RANK
    q_raw = mm(down, p["w_uq_ext"], norm_g=p["q_norm_g"], x_col_block=0, tm=tm, name="mla_uq")
    kv = mm(down, p["w_ukv"], norm_g=p["kv_norm_g"], x_col_block=1, tm=tm, out_dtype=BF16, name="mla_ukv")
    cos, sin = _rope_tables(seq, ctx_len, batch, MLA_ROPE)
    zeros = jnp.zeros_like(cos)
    cos_p = jnp.concatenate([cos, zeros], -1)
    sin_p = jnp.concatenate([sin, zeros], -1)
    q = mla_rope(q_raw, 0, MLA_HEADS, cos_p, sin_p, scale, True, tm=tm)
    kcat = mla_kcat(kv, down, (kq + MLA_KV_RANK) // LANES, MLA_HEADS, cos_p, sin_p, tm=tm)
    kw = dict(dqk=2 * LANES, n_heads=MLA_HEADS, batch=batch, seq=seq, ctx_len=ctx_len)
    o_lat = attention((q, lambda h: h), (kcat, lambda h: h), kv, lambda h: 2 * h + 1, lat_queries=True, **kw)
    if not want_ctx:
        return o_lat
    o_ctx = attention((q, lambda h: h), (kcat, lambda h: h), kv, lambda h: 2 * h + 1, lat_queries=False, **kw)
    return jnp.concatenate([o_lat, o_ctx], 0)


def gqa_mixer(hmix, p, *, batch, seq, ctx_len, want_ctx):
    m = hmix.shape[0]
    tm = 1088 if m % 1088 == 0 else 512
    scale = GQA_HEAD ** -0.5
    raw = mm(hmix, p["w_qkv_ext"], tm=tm, name="gqa_qkv")
    cos, sin = _rope_tables(seq, ctx_len, batch, GQA_HEAD)
    nq, nk = GQA_HEADS, GQA_KV_HEADS
    q = gqa_prep(raw, 0, nq, nq, cos, sin, p["q_norm_g"], p["q_norm_g_rot"], scale, tm=tm)
    k = gqa_prep(raw, 2 * nq, 2 * nq + nk, nk, cos, sin, p["k_norm_g"], p["k_norm_g_rot"], 1.0, tm=tm)
    vcb0 = 2 * nq + 2 * nk
    vb = mm_cast(raw, vcb0, nk, tm=tm)
    grp = nq // nk
    kw = dict(dqk=LANES, n_heads=nq, batch=batch, seq=seq, ctx_len=ctx_len)
    o_lat = attention((q, lambda h: h), (k, lambda h: h // grp), vb, lambda h: h // grp, lat_queries=True, **kw)
    if not want_ctx:
        return o_lat
    o_ctx = attention((q, lambda h: h), (k, lambda h: h // grp), vb, lambda h: h // grp, lat_queries=False, **kw)
    return jnp.concatenate([o_lat, o_ctx], 0)


def _cast_body(x_ref, o_ref):
    o_ref[...] = x_ref[...].astype(o_ref.dtype)


def mm_cast(raw, cb0, nblocks, tm=512):
    m = raw.shape[0]
    return pl.pallas_call(
        _cast_body,
        grid=(m // tm, nblocks),
        in_specs=[pl.BlockSpec((tm, LANES), lambda i, c: (i, cb0 + c))],
        out_specs=pl.BlockSpec((tm, LANES), lambda i, c: (i, c)),
        out_shape=jax.ShapeDtypeStruct((m, nblocks * LANES), BF16),
        compiler_params=_cparams(("parallel", "arbitrary")),
        name="cast_cols",
    )(raw)


def _rot_gain(g):
    q = g.shape[-1] // 4
    g4 = g.reshape(2, 2, q)
    return jnp.concatenate([g4[:, 1:2], g4[:, 0:1]], axis=1).reshape(g.shape)


def kernel(x, c, ctx, c_ctx, ada_w, ada_b, norm_g, ffn_w_gu, ffn_w_down, hy_w_in, hy_b_in, hy_conv_w, hy_conv_b, hy_f_w1, hy_f_b1, hy_f_w2, hy_f_b2, hy_f_w3, hy_sin_freq, hy_skip, hy_w_out, hy_b_out, rw_mix, rw_w_rkv, rw_w0, rw_w1, rw_w2, rw_a0, rw_a1, rw_a2, rw_g1, rw_g2, rw_k_k, rw_k_a, rw_r_k, rw_ln_w, rw_ln_b, rw_w_o, mla_w_down, mla_q_norm_g, mla_w_uq, mla_kv_norm_g, mla_w_ukv, mla_w_o, gqa_w_qkv, gqa_q_norm_g, gqa_k_norm_g, gqa_w_o, final_g):
    batch, seq, d = x.shape
    ctx_len = ctx.shape[1]
    depth = ada_w.shape[0]
    n_mix = 4
    rows_lat = batch * seq
    rows_all = rows_lat + batch * ctx_len
    bf = lambda a: a.astype(BF16)

    cvec = jnp.concatenate([c, c_ctx[None], jnp.zeros((8 - batch - 1, d), F32)], 0)
    mods_all = adaln_all(cvec, ada_w, ada_b)[:, :batch + 1].reshape(depth, batch + 1, N_MOD, d)

    h = jnp.concatenate([x.reshape(rows_lat, d), ctx.reshape(batch * ctx_len, d)], 0)

    for i in range(depth):
        mi, j = i % n_mix, i // n_mix
        ctx_out = any(jj % n_mix != 0 for jj in range(i + 1, depth))
        mods = mods_all[i]
        gate5 = mods[:, 5:6, :]
        mix_dtype = F32 if mi == 1 else BF16
        h, hmix = ffn_half(h, mods, norm_g[i], bf(ffn_w_gu[i, 0]), bf(ffn_w_down[i, 0]), s=0, rows=rows_all,
                           seq_rows=seq, next_g=norm_g[i, 1], next_s=1, next_dtype=mix_dtype)
        rows_out = rows_all if ctx_out else rows_lat
        kw = dict(batch=batch, seq=seq, ctx_len=ctx_len)
        if mi == 0:
            p = dict(w_in=bf(hy_w_in[j]), b_in=hy_b_in[j], conv_w=hy_conv_w[j], conv_b=hy_conv_b[j],
                     f_w1=hy_f_w1[j], f_b1=hy_f_b1[j], f_w2=hy_f_w2[j], f_b2=hy_f_b2[j], f_w3=hy_f_w3[j],
                     sin_freq=hy_sin_freq[j], skip=hy_skip[j])
            z = hyena_mixer(hmix, p, d_model=d, want_ctx=ctx_out, **kw)
            w_o, b_o = bf(hy_w_out[j]), hy_b_out[j]
        elif mi == 1:
            pad = lambda w, ax: jnp.pad(w, [(0, RW_LORA_PAD - w.shape[a]) if a == ax else (0, 0) for a in range(w.ndim)])
            p = dict(mix=rw_mix[j], w_r=bf(rw_w_rkv[j, 0]), w_k=bf(rw_w_rkv[j, 1]), w_v=bf(rw_w_rkv[j, 2]),
                     g1=bf(rw_g1[j]), g2=bf(rw_g2[j]),
                     w1=bf(pad(rw_w1[j], 2)), w2=bf(pad(rw_w2[j], 1)), a1=bf(pad(rw_a1[j], 2)), a2=bf(pad(rw_a2[j], 1)),
                     w0=rw_w0[j], a0=rw_a0[j], k_k=rw_k_k[j], k_a=rw_k_a[j], r_k=rw_r_k[j].reshape(-1),
                     ln_w=rw_ln_w[j], ln_b=rw_ln_b[j])
            z = rwkv_mixer(hmix, p, d_model=d, **kw)
            w_o, b_o = bf(rw_w_o[j]), None
        elif mi == 2:
            wd = mla_w_down[j]
            pe = wd[:, MLA_Q_RANK + MLA_KV_RANK:]
            wuq = mla_w_uq[j].reshape(MLA_Q_RANK, MLA_HEADS, MLA_NOPE + MLA_ROPE)
            wq_pe = wuq[..., MLA_NOPE:]
            wuq_ext = jnp.concatenate([wuq[..., :MLA_NOPE], wq_pe,
                                       _rot_cols(wq_pe.reshape(MLA_Q_RANK, -1), MLA_ROPE).reshape(wq_pe.shape)], -1)
            p = dict(w_down_ext=bf(jnp.concatenate([wd, _rot_cols(pe, MLA_ROPE)], 1)),
                     w_uq_ext=bf(wuq_ext.reshape(MLA_Q_RANK, -1)), w_ukv=bf(mla_w_ukv[j]),
                     q_norm_g=mla_q_norm_g[j], kv_norm_g=mla_kv_norm_g[j])
            z = mla_mixer(hmix, p, want_ctx=ctx_out, **kw)
            w_o, b_o = bf(mla_w_o[j]), None
        else:
            w = gqa_w_qkv[j]
            nqc = GQA_HEADS * GQA_HEAD
            nkc = GQA_KV_HEADS * GQA_HEAD
            wq, wk, wv = w[:, :nqc], w[:, nqc:nqc + nkc], w[:, nqc + nkc:]
            p = dict(w_qkv_ext=bf(jnp.concatenate([wq, _rot_cols(wq, GQA_HEAD), wk, _rot_cols(wk, GQA_HEAD), wv], 1)),
                     q_norm_g=gqa_q_norm_g[j], q_norm_g_rot=_rot_gain(gqa_q_norm_g[j]),
                     k_norm_g=gqa_k_norm_g[j], k_norm_g_rot=_rot_gain(gqa_k_norm_g[j]))
            z = gqa_mixer(hmix, p, want_ctx=ctx_out, **kw)
            w_o, b_o = bf(gqa_w_o[j]), None
        h = mm(z, w_o, bias=b_o, res=h, gate=gate5, seq_rows=seq, rows=rows_out, name="mix_out")
        last = i == depth - 1
        if last:
            h, out = ffn_half(h, mods, norm_g[i], bf(ffn_w_gu[i, 1]), bf(ffn_w_down[i, 1]), s=2, rows=rows_out,
                              seq_rows=seq, next_g=final_g, next_s=None, next_dtype=F32)
        else:
            h = ffn_half(h, mods, norm_g[i], bf(ffn_w_gu[i, 1]), bf(ffn_w_down[i, 1]), s=2, rows=rows_out,
                         seq_rows=seq)
    return out[:rows_lat].reshape(batch, seq, d)
```

```python
import functools
import math

import numpy as np
import jax
import jax.numpy as jnp
from jax import lax
from jax.experimental import pallas as pl
from jax.experimental.pallas import tpu as pltpu

F32 = jnp.float32
BF16 = jnp.bfloat16

VMEM_CAP_BYTES = 56 * 1024 * 1024
LANES = 128
SUBLANES = 8

NORM_EPS = 1e-6
N_MOD = 9
GRID_W = 64
ROPE_THETA = 10000.0
HY_DECAY_TARGET = 1e-2
HY_DECAY_PCT_SHORT = 0.3
HY_DECAY_PCT_LONG = 1.5
RW_HEAD = 64
RW_LORA_PAD = 128
GQA_HEAD = 128
GQA_HEADS = 16
GQA_KV_HEADS = 8
MLA_HEADS = 16
MLA_NOPE = 128
MLA_ROPE = 64
MLA_V = 128
MLA_Q_RANK = 512
MLA_KV_RANK = 512


def _cparams(sem):
    return pltpu.CompilerParams(dimension_semantics=sem, vmem_limit_bytes=VMEM_CAP_BYTES)


def _dot(a, b):
    return jnp.dot(a, b, preferred_element_type=F32)


def _dot_nt(a, b):
    return lax.dot_general(a, b, (((1,), (1,)), ((), ())), preferred_element_type=F32)


def _dot_tn(a, b):
    return lax.dot_general(a, b, (((0,), (0,)), ((), ())), preferred_element_type=F32)


def _split2(x):
    hi = x.astype(BF16)
    lo = (x - hi.astype(F32)).astype(BF16)
    return hi, lo


def _split3(x):
    hi = x.astype(BF16)
    r1 = x - hi.astype(F32)
    mid = r1.astype(BF16)
    lo = (r1 - mid.astype(F32)).astype(BF16)
    return hi, mid, lo


def _dot3(a, b, f=_dot):
    ah, al = _split2(a)
    bh, bl = _split2(b)
    return f(ah, bh) + f(ah, bl) + f(al, bh)


def _dotc3(mh, ml, d):
    dh, dl = _split2(d)
    return _dot(mh, dh) + _dot(ml, dh) + _dot(mh, dl)


def _dotc2(mh, ml, d):
    dh = d.astype(BF16)
    return _dot(mh, dh) + _dot(ml, dh)


def _rms(x, g):
    return x * lax.rsqrt(jnp.mean(x * x, -1, keepdims=True) + NORM_EPS) * g


def _pick(n, cands):
    for c in cands:
        if n % c == 0:
            return c
    raise ValueError(f"no tile for {n}")


def _adaln_body(c_ref, w_ref, b_ref, o_ref):
    c = c_ref[...]
    s = c * jax.nn.sigmoid(c)
    o_ref[0] = _dot(s.astype(BF16), w_ref[0].astype(BF16)) + b_ref[0]


def adaln_all(cvec8, ada_w, ada_b):
    depth, d, n = ada_w.shape
    tn = _pick(n, (2048, 1024, 512, 256, 128))
    return pl.pallas_call(
        _adaln_body,
        grid=(depth, n // tn),
        in_specs=[
            pl.BlockSpec((8, d), lambda l, j: (0, 0)),
            pl.BlockSpec((1, d, tn), lambda l, j: (l, 0, j)),
            pl.BlockSpec((1, 1, tn), lambda l, j: (l, 0, j)),
        ],
        out_specs=pl.BlockSpec((1, 8, tn), lambda l, j: (l, 0, j)),
        out_shape=jax.ShapeDtypeStruct((depth, 8, n), F32),
        compiler_params=_cparams(("arbitrary", "arbitrary")),
        name="adaln",
    )(cvec8, ada_w, ada_b.reshape(depth, 1, n))


def _mm_body(*refs, has_bias, act, has_norm, has_res):
    it = iter(refs)
    x_ref = next(it)
    w_ref = next(it)
    b_ref = next(it) if has_bias else None
    g_ref = next(it) if has_norm else None
    r_ref = next(it) if has_res else None
    gate_ref = next(it) if has_res else None
    o_ref = next(it)
    x = x_ref[...]
    if has_norm:
        x = _rms(x.astype(F32), g_ref[...])
    acc = _dot(x.astype(BF16), w_ref[...])
    if has_bias:
        acc = acc + b_ref[...]
    if act == "sigmoid":
        acc = jax.nn.sigmoid(acc)
    elif act == "tanh":
        acc = jnp.tanh(acc)
    if has_res:
        acc = r_ref[...] + gate_ref[0] * acc
    o_ref[...] = acc.astype(o_ref.dtype)


def mm(x, w, *, bias=None, act=None, norm_g=None, res=None, gate=None, out_dtype=F32,
       tm=512, rows=None, x_col_block=0, x_row_block0=0, seq_rows=None, name="mm"):
    k, n = w.shape
    m = rows if rows is not None else x.shape[0]
    assert m % tm == 0, (m, tm)
    tn = _pick(n, (512, 384, 256, 128))
    has_res = res is not None
    ins = [x, w]
    specs = [pl.BlockSpec((tm, k), lambda i, j: (i + x_row_block0, x_col_block)),
             pl.BlockSpec((k, tn), lambda i, j: (0, j))]
    if bias is not None:
        ins.append(bias.reshape(1, n).astype(F32))
        specs.append(pl.BlockSpec((1, tn), lambda i, j: (0, j)))
    if norm_g is not None:
        ins.append(norm_g.reshape(1, k).astype(F32))
        specs.append(pl.BlockSpec((1, k), lambda i, j: (0, 0)))
    if has_res:
        ngroups = gate.shape[0]
        per = seq_rows // tm
        ins += [res, gate]
        specs += [pl.BlockSpec((tm, tn), lambda i, j: (i, j)),
                  pl.BlockSpec((1, 1, tn), lambda i, j: (jnp.minimum(i // per, ngroups - 1), 0, j))]
    body = functools.partial(_mm_body, has_bias=bias is not None, act=act,
                             has_norm=norm_g is not None, has_res=has_res)
    return pl.pallas_call(
        body,
        grid=(m // tm, n // tn),
        in_specs=specs,
        out_specs=pl.BlockSpec((tm, tn), lambda i, j: (i, j)),
        out_shape=jax.ShapeDtypeStruct((m, n), out_dtype),
        compiler_params=_cparams(("parallel", "arbitrary")),
        name=name,
    )(*ins)


def _ffn_body(*refs, s, n_f, has_next, next_s):
    it = iter(refs)
    x_ref, mod_ref, g_ref, wa_ref, wb_ref, wd_ref = (next(it) for _ in range(6))
    gn_ref = next(it) if has_next else None
    o_ref = next(it)
    hn_ref = next(it) if has_next else None
    xn_sc = next(it)
    acc_sc = next(it)
    j = pl.program_id(1)

    @pl.when(j == 0)
    def _():
        n = _rms(x_ref[...], g_ref[s:s + 1, :])
        n = n * (1.0 + mod_ref[0, 3 * s + 1:3 * s + 2, :]) + mod_ref[0, 3 * s:3 * s + 1, :]
        xn_sc[...] = n.astype(BF16)
        acc_sc[...] = jnp.zeros_like(acc_sc)

    xn = xn_sc[...]
    a = _dot(xn, wa_ref[...])
    b = _dot(xn, wb_ref[...])
    h = (a * jax.nn.sigmoid(a)) * b
    acc_sc[...] += _dot(h.astype(BF16), wd_ref[...])

    @pl.when(j == n_f - 1)
    def _():
        xnew = x_ref[...] + 0.5 * mod_ref[0, 3 * s + 2:3 * s + 3, :] * acc_sc[...]
        o_ref[...] = xnew
        if has_next:
            hn = _rms(xnew, gn_ref[...])
            if next_s is not None:
                hn = hn * (1.0 + mod_ref[0, 3 * next_s + 1:3 * next_s + 2, :]) + mod_ref[0, 3 * next_s:3 * next_s + 1, :]
            hn_ref[...] = hn.astype(hn_ref.dtype)


def ffn_half(x, mods, norm_g3, w_gu, w_down, *, layer, which, s, rows, seq_rows, next_g=None, next_s=None,
             next_dtype=F32, tm=512, fc=512):
    d = x.shape[1]
    d_ff = w_down.shape[2]
    n_f = d_ff // fc
    assert d_ff % fc == 0 and rows % tm == 0
    per = seq_rows // tm
    ngroups = mods.shape[0]
    has_next = next_g is not None
    ins = [x, mods, norm_g3, w_gu, w_gu, w_down]
    specs = [
        pl.BlockSpec((tm, d), lambda i, j: (i, 0)),
        pl.BlockSpec((1, N_MOD, d), lambda i, j: (jnp.minimum(i // per, ngroups - 1), 0, 0)),
        pl.BlockSpec((3, d), lambda i, j: (0, 0)),
        pl.BlockSpec((None, None, d, fc), lambda i, j: (layer, which, 0, j)),
        pl.BlockSpec((None, None, d, fc), lambda i, j: (layer, which, 0, n_f + j)),
        pl.BlockSpec((None, None, fc, d), lambda i, j: (layer, which, j, 0)),
    ]
    out_shape = [jax.ShapeDtypeStruct((rows, d), F32)]
    out_specs = [pl.BlockSpec((tm, d), lambda i, j: (i, 0))]
    if has_next:
        ins.append(next_g.reshape(1, d))
        specs.append(pl.BlockSpec((1, d), lambda i, j: (0, 0)))
        out_shape.append(jax.ShapeDtypeStruct((rows, d), next_dtype))
        out_specs.append(pl.BlockSpec((tm, d), lambda i, j: (i, 0)))
    body = functools.partial(_ffn_body, s=s, n_f=n_f, has_next=has_next, next_s=next_s)
    outs = pl.pallas_call(
        body,
        grid=(rows // tm, n_f),
        in_specs=specs,
        out_specs=out_specs,
        out_shape=out_shape,
        scratch_shapes=[pltpu.VMEM((tm, d), BF16), pltpu.VMEM((tm, d), F32)],
        compiler_params=_cparams(("parallel", "arbitrary")),
        name="ffn_half",
    )(*ins)
    return outs if has_next else outs[0]


def _attn_body(*refs, use_lat):
    it = iter(refs)
    q_ref = next(it)
    kl_ref = next(it) if use_lat else None
    kc_ref = next(it)
    vl_ref = next(it) if use_lat else None
    vc_ref = next(it)
    o_ref = next(it)
    q = q_ref[...]
    s_c = _dot_nt(kc_ref[...], q)
    m = jnp.max(s_c, 0, keepdims=True)
    if use_lat:
        s_l = _dot_nt(kl_ref[...], q)
        m = jnp.maximum(m, jnp.max(s_l, 0, keepdims=True))
    p_c = jnp.exp(s_c - m)
    den = jnp.sum(p_c, 0, keepdims=True)
    o = _dot_tn(vc_ref[...], p_c.astype(BF16))
    if use_lat:
        p_l = jnp.exp(s_l - m)
        den = den + jnp.sum(p_l, 0, keepdims=True)
        o = o + _dot_tn(vl_ref[...], p_l.astype(BF16))
    o_ref[...] = (o / den).T.astype(o_ref.dtype)


def attention(q_part, k_part, v_arr, v_cb, *, dqk, n_heads, batch, seq, ctx_len, lat_queries, tq=512):
    q_parts, k_parts = [q_part], [k_part]
    ctx0 = (batch * seq) // ctx_len
    if lat_queries:
        tq = min(tq, seq)
        nq = seq // tq
        q_row = lambda b, t: b * nq + t
        rows_q = batch * seq
    else:
        tq = ctx_len
        nq = 1
        q_row = lambda b, t: ctx0 + b
        rows_q = batch * ctx_len
    ins, specs = [], []
    for arr, cb in q_parts:
        ins.append(arr)
        specs.append(pl.BlockSpec((tq, dqk), lambda b, h, t, cb=cb: (q_row(b, t), cb(h))))
    if lat_queries:
        for arr, cb in k_parts:
            ins.append(arr)
            specs.append(pl.BlockSpec((seq, dqk), lambda b, h, t, cb=cb: (b, cb(h))))
    for arr, cb in k_parts:
        ins.append(arr)
        specs.append(pl.BlockSpec((ctx_len, dqk), lambda b, h, t, cb=cb: (ctx0 + b, cb(h))))
    if lat_queries:
        ins.append(v_arr)
        specs.append(pl.BlockSpec((seq, LANES), lambda b, h, t: (b, v_cb(h))))
    ins.append(v_arr)
    specs.append(pl.BlockSpec((ctx_len, LANES), lambda b, h, t: (ctx0 + b, v_cb(h))))
    if lat_queries:
        o_row = lambda b, t: b * nq + t
    else:
        o_row = lambda b, t: b
    return pl.pallas_call(
        functools.partial(_attn_body, use_lat=lat_queries),
        grid=(batch, n_heads, nq),
        in_specs=specs,
        out_specs=pl.BlockSpec((tq, LANES), lambda b, h, t: (o_row(b, t), h)),
        out_shape=jax.ShapeDtypeStruct((rows_q, n_heads * LANES), BF16),
        compiler_params=_cparams(("parallel", "parallel", "arbitrary")),
        name="attention_lat" if lat_queries else "attention_ctx",
    )(*ins)


def _rope_tables(seq, ctx_len, batch, rot_dim):
    half = rot_dim // 4
    inv_freq = ROPE_THETA ** (-jnp.arange(half, dtype=F32) / half)
    t = jnp.arange(seq, dtype=jnp.int32)
    row = (t // GRID_W).astype(F32)
    col = (t % GRID_W).astype(F32)
    ang_r = row[:, None] * inv_freq[None]
    ang_c = col[:, None] * inv_freq[None]
    cos = jnp.concatenate([jnp.cos(ang_r), jnp.cos(ang_r), jnp.cos(ang_c), jnp.cos(ang_c)], -1)
    sin = jnp.concatenate([jnp.sin(ang_r), jnp.sin(ang_r), jnp.sin(ang_c), jnp.sin(ang_c)], -1)
    cos = jnp.concatenate([jnp.tile(cos, (batch, 1)), jnp.ones((batch * ctx_len, rot_dim), F32)], 0)
    sin = jnp.concatenate([jnp.tile(sin, (batch, 1)), jnp.zeros((batch * ctx_len, rot_dim), F32)], 0)
    return cos, sin


def _rot_cols(w, rot_dim):
    lead = w.shape[:-1]
    q = rot_dim // 4
    w5 = w.reshape(lead + (-1, 2, 2, q))
    rot = jnp.concatenate([-w5[..., 1:2, :], w5[..., 0:1, :]], axis=-2)
    return rot.reshape(w.shape)


def _gqa_prep_body(a_ref, ar_ref, cos_ref, sin_ref, g_ref, gr_ref, o_ref, *, scale):
    a = a_ref[...]
    r = lax.rsqrt(jnp.mean(a * a, -1, keepdims=True) + NORM_EPS) * scale
    o = (a * g_ref[...]) * cos_ref[...] + (ar_ref[...] * gr_ref[...]) * sin_ref[...]
    o_ref[...] = (o * r).astype(o_ref.dtype)


def gqa_prep(raw, cb0, cbr0, n_heads, cos, sin, g, g_rot, scale, tm=1088):
    m = raw.shape[0]
    return pl.pallas_call(
        functools.partial(_gqa_prep_body, scale=scale),
        grid=(m // tm, n_heads),
        in_specs=[
            pl.BlockSpec((tm, LANES), lambda i, h: (i, cb0 + h)),
            pl.BlockSpec((tm, LANES), lambda i, h: (i, cbr0 + h)),
            pl.BlockSpec((tm, LANES), lambda i, h: (i, 0)),
            pl.BlockSpec((tm, LANES), lambda i, h: (i, 0)),
            pl.BlockSpec((1, LANES), lambda i, h: (0, 0)),
            pl.BlockSpec((1, LANES), lambda i, h: (0, 0)),
        ],
        out_specs=pl.BlockSpec((tm, LANES), lambda i, h: (i, h)),
        out_shape=jax.ShapeDtypeStruct((m, n_heads * LANES), BF16),
        compiler_params=_cparams(("parallel", "arbitrary")),
        name="gqa_prep",
    )(raw, raw, cos, sin, g.reshape(1, LANES), g_rot.reshape(1, LANES))


def _mla_rope_body(x_ref, c_ref, s_ref, o_ref, *, scale, with_nope):
    if with_nope:
        a = x_ref[:, :LANES]
        b = x_ref[:, LANES:]
    else:
        b = x_ref[...]
    roped = (b * c_ref[...] + pltpu.roll(b, LANES // 2, 1) * s_ref[...]) * scale
    if with_nope:
        o_ref[:, :LANES] = (a * scale).astype(o_ref.dtype)
        o_ref[:, LANES:] = roped.astype(o_ref.dtype)
    else:
        o_ref[...] = roped.astype(o_ref.dtype)


def mla_rope(raw, cb0, n_heads, cos, sin, scale, with_nope, tm=1088):
    m = raw.shape[0]
    w = 2 * LANES if with_nope else LANES
    return pl.pallas_call(
        functools.partial(_mla_rope_body, scale=scale, with_nope=with_nope),
        grid=(m // tm, n_heads),
        in_specs=[
            pl.BlockSpec((tm, w), lambda i, h: (i, cb0 + h)),
            pl.BlockSpec((tm, LANES), lambda i, h: (i, 0)),
            pl.BlockSpec((tm, LANES), lambda i, h: (i, 0)),
        ],
        out_specs=pl.BlockSpec((tm, w), lambda i, h: (i, h)),
        out_shape=jax.ShapeDtypeStruct((m, n_heads * w), BF16),
        compiler_params=_cparams(("parallel", "arbitrary")),
        name="mla_rope",
    )(raw, cos, sin)


def _mla_kcat_body(kn_ref, pe_ref, c_ref, s_ref, o_ref):
    b = pe_ref[...]
    o_ref[:, :LANES] = kn_ref[...]
    o_ref[:, LANES:] = (b * c_ref[...] + pltpu.roll(b, LANES // 2, 1) * s_ref[...]).astype(o_ref.dtype)


def mla_kcat(kv, down, pe_cb, n_heads, cos, sin, tm=1088):
    m = kv.shape[0]
    return pl.pallas_call(
        _mla_kcat_body,
        grid=(m // tm, n_heads),
        in_specs=[
            pl.BlockSpec((tm, LANES), lambda i, h: (i, 2 * h)),
            pl.BlockSpec((tm, LANES), lambda i, h: (i, pe_cb)),
            pl.BlockSpec((tm, LANES), lambda i, h: (i, 0)),
            pl.BlockSpec((tm, LANES), lambda i, h: (i, 0)),
        ],
        out_specs=pl.BlockSpec((tm, 2 * LANES), lambda i, h: (i, h)),
        out_shape=jax.ShapeDtypeStruct((m, n_heads * 2 * LANES), BF16),
        compiler_params=_cparams(("parallel", "arbitrary")),
        name="mla_kcat",
    )(kv, down, cos, sin)


class _FftPlan:
    def __init__(self, L):
        self.L = L
        self.N = 2 * L
        self.N2 = 128 if L >= 1024 else 64
        self.N1 = self.N // self.N2
        self.n1_in = self.N1 // 2
        self.K1 = self.N1 // 2 + 1
        self.nq = self.N2 // SUBLANES
        self.spec_rows = self.K1 * 2 * self.N2
        self.kf_cols = max(self.n1_in * SUBLANES, LANES)
        self.kf_rows = 2 * self.K1 * SUBLANES
        self.kb_cols = -(-self.kf_rows // LANES) * LANES
        self.kb_rows = self.n1_in * SUBLANES
        self.k1_unroll = 3 if self.K1 % 3 == 0 else 1

    def constants(self):
        N, N1, N2, K1, n1_in = self.N, self.N1, self.N2, self.K1, self.n1_in
        eye = np.eye(SUBLANES)
        k1 = np.arange(K1)[:, None]
        n1 = np.arange(n1_in)[None, :]
        th = 2 * np.pi * k1 * n1 / N1
        kf = np.zeros((self.kf_rows, self.kf_cols))
        kf[:K1 * SUBLANES, :n1_in * SUBLANES] = np.kron(np.cos(th), eye)
        kf[K1 * SUBLANES:, :n1_in * SUBLANES] = np.kron(-np.sin(th), eye)
        c = np.full((K1,), 2.0)
        c[0] = 1.0
        c[-1] = 1.0
        thb = 2 * np.pi * np.arange(n1_in)[:, None] * np.arange(K1)[None, :] / N1
        kb = np.zeros((self.kb_rows, self.kb_cols))
        kb[:, :K1 * SUBLANES] = np.kron(np.cos(thb) * c[None, :], eye)
        kb[:, K1 * SUBLANES:2 * K1 * SUBLANES] = np.kron(-np.sin(thb) * c[None, :], eye)
        a = 2 * np.pi * np.outer(np.arange(N2), np.arange(N2)) / N2
        C, S = np.cos(a), np.sin(a)
        w2f = np.block([[C, S], [-S, C]])
        w2i = np.block([[C, -S], [S, C]])
        tw_ang = 2 * np.pi * np.outer(np.arange(N2), np.arange(K1)) / N
        twr = np.repeat(np.cos(tw_ang), LANES, axis=1)
        twi = np.repeat(np.sin(tw_ang), LANES, axis=1)

        def hl(m):
            m32 = jnp.asarray(m, F32)
            hi = m32.astype(BF16)
            lo = (m32 - hi.astype(F32)).astype(BF16)
            return hi, lo

        out = {}
        for name, mat in (("kf", kf), ("kb", kb), ("w2f", w2f), ("w2i", w2i)):
            out[name + "_h"], out[name + "_l"] = hl(mat)
        out["twr"] = jnp.asarray(twr, F32)
        out["twi"] = jnp.asarray(twi, F32)
        return out


def _cmul(ar, ai, br, bi):
    return ar * br - ai * bi, ar * bi + ai * br


def _time_to_strided(plan, x_ref, xt_ref):
    N2, nq = plan.N2, plan.nq
    for n in range(plan.n1_in):
        xt_ref[n * SUBLANES:(n + 1) * SUBLANES, :] = jnp.concatenate(
            [x_ref[N2 * n + SUBLANES * q:N2 * n + SUBLANES * (q + 1), :] for q in range(nq)], axis=1)
    pad = plan.kf_cols - plan.n1_in * SUBLANES
    if pad:
        xt_ref[plan.n1_in * SUBLANES:, :] = jnp.zeros((pad, xt_ref.shape[1]), F32)


def _strided_to_spectrum(plan, a_ref, s_ref, c):
    N2, K1, nq = plan.N2, plan.K1, plan.nq
    for part in range(2):
        for k in range(K1):
            r0 = (part * K1 + k) * SUBLANES
            s_ref[part * N2:(part + 1) * N2, k * c:(k + 1) * c] = jnp.concatenate(
                [a_ref[r0:r0 + SUBLANES, q * c:(q + 1) * c] for q in range(nq)], axis=0)


def _spectrum_to_strided(plan, s_ref, b_ref, c):
    N2, K1, nq = plan.N2, plan.K1, plan.nq
    for part in range(2):
        for k in range(K1):
            r0 = (part * K1 + k) * SUBLANES
            b_ref[r0:r0 + SUBLANES, :] = jnp.concatenate(
                [s_ref[part * N2 + SUBLANES * q:part * N2 + SUBLANES * (q + 1), k * c:(k + 1) * c] for q in range(nq)],
                axis=1)
    pad = plan.kb_cols - plan.kf_rows
    if pad:
        b_ref[plan.kf_rows:, :] = jnp.zeros((pad, b_ref.shape[1]), F32)


def _strided_to_time(plan, yt, c):
    nq = plan.nq
    slabs = []
    for n in range(plan.n1_in):
        blk = yt[n * SUBLANES:(n + 1) * SUBLANES, :]
        slabs.extend(blk[:, q * c:(q + 1) * c] for q in range(nq))
    return jnp.concatenate(slabs, axis=0)


def _fft_fwd(plan, x_ref, consts, xt_ref, a_ref, s_ref, dotc):
    c = x_ref.shape[1]
    _time_to_strided(plan, x_ref, xt_ref)
    a_ref[:plan.kf_rows, :] = dotc(consts["kf_h"][...], consts["kf_l"][...], xt_ref[...])
    _strided_to_spectrum(plan, a_ref, s_ref, c)
    n2 = plan.N2
    tr, ti = consts["twr"][...], consts["twi"][...]
    xr, xi = _cmul(s_ref[:n2, :], s_ref[n2:, :], tr, -ti)
    s_ref[...] = dotc(consts["w2f_h"][...], consts["w2f_l"][...], jnp.concatenate([xr, xi], axis=0))


def _fft_conv_apply(plan, x_ref, g_ref, consts, xt_ref, a_ref, s_ref):
    c = x_ref.shape[1]
    n2 = plan.N2
    _fft_fwd(plan, x_ref, consts, xt_ref, a_ref, s_ref, _dotc2)
    pr, pi = _cmul(s_ref[:n2, :], s_ref[n2:, :], g_ref[:n2, :], g_ref[n2:, :])
    z = _dotc2(consts["w2i_h"][...], consts["w2i_l"][...], jnp.concatenate([pr, pi], axis=0))
    zr, zi = _cmul(z[:n2], z[n2:], consts["twr"][...], consts["twi"][...])
    s_ref[:n2, :] = zr
    s_ref[n2:, :] = zi
    _spectrum_to_strided(plan, s_ref, a_ref, c)
    yt = _dotc2(consts["kb_h"][...], consts["kb_l"][...], a_ref[...])
    return _strided_to_time(plan, yt, c)


def _fft_scratch(plan):
    wl = plan.nq * LANES
    return [pltpu.VMEM((plan.kf_cols, wl), F32),
            pltpu.VMEM((max(plan.kf_rows, plan.kb_cols), wl), F32),
            pltpu.VMEM((2 * plan.N2, plan.K1 * LANES), F32)]


_CONST_NAMES = ("kf_h", "kf_l", "kb_h", "kb_l", "w2f_h", "w2f_l", "w2i_h", "w2i_l", "twr", "twi")


def _const_specs(consts, nargs):
    ins, specs = [], []
    for nme in _CONST_NAMES:
        a = consts[nme]
        ins.append(a)
        if nargs == 1:
            specs.append(pl.BlockSpec(a.shape, lambda c: (0, 0), pipeline_mode=pl.Buffered(1)))
        else:
            specs.append(pl.BlockSpec(a.shape, lambda c, b: (0, 0), pipeline_mode=pl.Buffered(1)))
    return ins, specs


def _hy_hidden_body(z_ref, w1_ref, b1_ref, w2_ref, b2_ref, sf_ref, o_ref):
    h = jnp.sin(sf_ref[0:1, :] * (_dot3(z_ref[...], w1_ref[...]) + b1_ref[...]))
    h = jnp.sin(sf_ref[1:2, :] * (_dot3(h, w2_ref[...]) + b2_ref[...]))
    o_ref[...] = h


def hyena_hidden(L, f_w1, f_b1, f_w2, f_b2, sin_freq):
    emb, hid = f_w1.shape
    bands = (emb - 1) // 2
    t = jnp.linspace(0.0, 1.0, L, dtype=F32)[:, None]
    w = (2.0 * math.pi / L) * jnp.arange(L, dtype=F32)[:, None]
    f = jnp.linspace(1e-4, bands - 1, bands, dtype=F32)[None]
    z = jnp.concatenate([t, jnp.cos(f * w), -jnp.sin(f * w)], -1)
    embp = -(-emb // LANES) * LANES
    z = jnp.pad(z, ((0, 0), (0, embp - emb)))
    w1 = jnp.pad(f_w1, ((0, embp - emb), (0, 0)))
    ins = [z, w1, f_b1.reshape(1, hid), f_w2, f_b2.reshape(1, hid), sin_freq]
    return pl.pallas_call(
        _hy_hidden_body,
        out_shape=jax.ShapeDtypeStruct((L, hid), F32),
        name="hyena_hidden",
    )(*ins)


def _hy_filter_body(h_ref, wf_ref, wb_ref, dl_ref, *rest, plan):
    consts = {nme: r for nme, r in zip(_CONST_NAMES, rest)}
    g_ref, x_sc, xt_sc, a_sc, s_sc, sb_sc = rest[len(_CONST_NAMES):]
    L, n2 = plan.L, plan.N2
    t = lax.broadcasted_iota(jnp.int32, (L, LANES), 0).astype(F32) * (1.0 / (L - 1))
    win = jnp.exp(-t * dl_ref[...])
    row0 = lax.broadcasted_iota(jnp.int32, (L, LANES), 0) == 0
    h = h_ref[...]
    fwd = _dot3(h, wf_ref[...]) * win
    bwd = jnp.where(row0, 0.0, _dot3(h, wb_ref[...]) * win)
    ss = jnp.sum(fwd * fwd, 0, keepdims=True) + jnp.sum(bwd * bwd, 0, keepdims=True)
    nu = lax.rsqrt(ss + 1e-12) * (1.0 / plan.N)
    x_sc[...] = bwd * nu
    _fft_fwd(plan, x_sc, consts, xt_sc, a_sc, sb_sc, _dotc3)
    x_sc[...] = fwd * nu
    _fft_fwd(plan, x_sc, consts, xt_sc, a_sc, s_sc, _dotc3)
    g_ref[:n2, :] = s_sc[:n2, :] + sb_sc[:n2, :]
    g_ref[n2:, :] = s_sc[n2:, :] - sb_sc[n2:, :]


def hyena_filter_spectra(plan, consts, hidden, f_w3, d_model, n_order):
    L = plan.L
    hid = hidden.shape[1]
    ncb = d_model // LANES
    deltas = jnp.abs(jnp.linspace(math.log(HY_DECAY_TARGET) / HY_DECAY_PCT_SHORT,
                                  math.log(HY_DECAY_TARGET) / HY_DECAY_PCT_LONG, d_model, dtype=F32)).reshape(1, d_model)
    ins = [hidden, f_w3, f_w3, deltas]
    specs = [pl.BlockSpec((L, hid), lambda c, n: (0, 0)),
             pl.BlockSpec((hid, LANES), lambda c, n: (0, (2 * n) * ncb + c)),
             pl.BlockSpec((hid, LANES), lambda c, n: (0, (2 * n + 1) * ncb + c)),
             pl.BlockSpec((1, LANES), lambda c, n: (0, c))]
    ci, cs = _const_specs(consts, 2)
    ins += ci
    specs += cs
    srows, scols = 2 * plan.N2, plan.K1 * LANES
    return pl.pallas_call(
        functools.partial(_hy_filter_body, plan=plan),
        grid=(ncb, n_order),
        in_specs=specs,
        out_specs=pl.BlockSpec((None, None, srows, scols), lambda c, n: (n, c, 0, 0)),
        out_shape=jax.ShapeDtypeStruct((n_order, ncb, srows, scols), F32),
        scratch_shapes=[pltpu.VMEM((L, LANES), F32)] + _fft_scratch(plan) + [pltpu.VMEM((srows, scols), F32)],
        compiler_params=_cparams(("arbitrary", "arbitrary")),
        name="hyena_filter",
    )(*ins)


def _conv3(p, w_ref, b_ref, L):
    rows = lax.broadcasted_iota(jnp.int32, p.shape, 0)
    prev = jnp.where(rows == 0, 0.0, pltpu.roll(p, 1, 0))
    nxt = jnp.where(rows == L - 1, 0.0, pltpu.roll(p, L - 1, 0))
    return prev * w_ref[0:1, :] + p * w_ref[1:2, :] + nxt * w_ref[2:3, :] + b_ref[...]


def _hy_conv_body(*refs, plan, order, conv_u):
    it = iter(refs)
    u_ref, gate_ref = next(it), next(it)
    if conv_u:
        cwu_ref, cbu_ref = next(it), next(it)
    cwg_ref, cbg_ref = next(it), next(it)
    skip_ref = next(it)
    g_ref = next(it)
    consts = {nme: next(it) for nme in _CONST_NAMES}
    o_ref = next(it)
    x_sc, xt_sc, a_sc, s_sc = (next(it) for _ in range(4))
    L = plan.L
    u = u_ref[...]
    if conv_u:
        u = _conv3(u, cwu_ref, cbu_ref, L)
    x_sc[...] = u
    y = _fft_conv_apply(plan, x_sc, g_ref, consts, xt_sc, a_sc, s_sc)
    gate = _conv3(gate_ref[...], cwg_ref, cbg_ref, L)
    o_ref[...] = (gate * (y + x_sc[...] * skip_ref[order:order + 1, :])).astype(o_ref.dtype)


def hyena_conv(plan, consts, u_arr, u_cb0, gate_arr, gate_cb0, conv_w, conv_b, skip, spectra, *,
               order, conv_u, u_row_block0, gate_row_block0, batch, d_model, out_dtype):
    L = plan.L
    ncb = d_model // LANES
    ins = [u_arr, gate_arr]
    specs = [pl.BlockSpec((L, LANES), lambda c, b: (u_row_block0 + b, u_cb0 + c)),
             pl.BlockSpec((L, LANES), lambda c, b: (gate_row_block0 + b, gate_cb0 + c))]
    if conv_u:
        ins += [conv_w, conv_b]
        specs += [pl.BlockSpec((3, LANES), lambda c, b: (0, u_cb0 + c)),
                  pl.BlockSpec((1, LANES), lambda c, b: (0, u_cb0 + c))]
    srows, scols = 2 * plan.N2, plan.K1 * LANES
    ins += [conv_w, conv_b, skip, spectra]
    specs += [pl.BlockSpec((3, LANES), lambda c, b: (0, gate_cb0 + c)),
              pl.BlockSpec((1, LANES), lambda c, b: (0, gate_cb0 + c)),
              pl.BlockSpec((skip.shape[0], LANES), lambda c, b: (0, c)),
              pl.BlockSpec((None, None, srows, scols), lambda c, b: (order, c, 0, 0))]
    ci, cs = _const_specs(consts, 2)
    ins += ci
    specs += cs
    return pl.pallas_call(
        functools.partial(_hy_conv_body, plan=plan, order=order, conv_u=conv_u),
        grid=(ncb, batch),
        in_specs=specs,
        out_specs=pl.BlockSpec((L, LANES), lambda c, b: (b, c)),
        out_shape=jax.ShapeDtypeStruct((batch * L, d_model), out_dtype),
        scratch_shapes=[pltpu.VMEM((L, LANES), F32)] + _fft_scratch(plan),
        compiler_params=_cparams(("parallel", "arbitrary")),
        name=f"hyena_conv{order}_L{L}",
    )(*ins)


def hyena_mixer(hmix, p, *, batch, seq, ctx_len, d_model, want_ctx):
    n_order = p["skip"].shape[0]
    pr = mm(hmix, p["w_in"], bias=p["b_in"], name="hy_in", rows=(batch * seq + (batch * ctx_len if want_ctx else 0)))
    conv_b = p["conv_b"].reshape(1, -1)
    ncb = d_model // LANES
    outs = []
    streams = [(seq, 0)]
    if want_ctx:
        streams.append((ctx_len, (batch * seq) // ctx_len))
    for L, rb0 in streams:
        plan = _FftPlan(L)
        consts = plan.constants()
        hidden = hyena_hidden(L, p["f_w1"], p["f_b1"], p["f_w2"], p["f_b2"], p["sin_freq"])
        spectra = hyena_filter_spectra(plan, consts, hidden, p["f_w3"], d_model, n_order)
        z1 = hyena_conv(plan, consts, pr, 0, pr, ncb, p["conv_w"], conv_b, p["skip"], spectra,
                        order=0, conv_u=True, u_row_block0=rb0, gate_row_block0=rb0, batch=batch, d_model=d_model,
                        out_dtype=F32)
        z2 = hyena_conv(plan, consts, z1, 0, pr, 2 * ncb, p["conv_w"], conv_b, p["skip"], spectra,
                        order=1, conv_u=False, u_row_block0=0, gate_row_block0=rb0, batch=batch, d_model=d_model,
                        out_dtype=BF16)
        outs.append(z2)
    return outs[0] if len(outs) == 1 else jnp.concatenate(outs, 0)


def _headsum(x, e_ref):
    hi, lo = _split2(x)
    return _dot(hi, e_ref[...]) + _dot(lo, e_ref[...])


def _rw_mix_body(h_ref, hp_ref, hn_ref, mix_ref, o_ref, *, tm, lat_tiles, tiles_per_seq):
    i = pl.program_id(0)
    is_ctx = i >= lat_tiles
    first = jnp.logical_or(is_ctx, i % tiles_per_seq == 0)
    last = jnp.logical_or(is_ctx, i % tiles_per_seq == tiles_per_seq - 1)
    h = h_ref[...]
    rows = lax.broadcasted_iota(jnp.int32, h.shape, 0)
    pedge = jnp.where(first, 0.0, hp_ref[SUBLANES - 1:SUBLANES, :])
    nedge = jnp.where(last, 0.0, hn_ref[0:1, :])
    prev = jnp.where(rows == 0, pedge, pltpu.roll(h, 1, 0))
    nxt = jnp.where(rows == tm - 1, nedge, pltpu.roll(h, tm - 1, 0))
    dp = prev - h
    dn = nxt - h
    for n in range(o_ref.shape[0]):
        o_ref[n] = (h + dp * mix_ref[0, n:n + 1, :] + dn * mix_ref[1, n:n + 1, :]).astype(o_ref.dtype)


def rwkv_mix(h, mix, *, batch, seq, ctx_len):
    m, d = h.shape
    tm = ctx_len
    n_shift = mix.shape[1]
    nblk = m // SUBLANES
    per = tm // SUBLANES
    body = functools.partial(_rw_mix_body, tm=tm, lat_tiles=(batch * seq) // tm, tiles_per_seq=seq // tm)
    return pl.pallas_call(
        body,
        grid=(m // tm,),
        in_specs=[
            pl.BlockSpec((tm, d), lambda i: (i, 0)),
            pl.BlockSpec((SUBLANES, d), lambda i: (jnp.maximum(i * per - 1, 0), 0)),
            pl.BlockSpec((SUBLANES, d), lambda i: (jnp.minimum((i + 1) * per, nblk - 1), 0)),
            pl.BlockSpec((2, n_shift, d), lambda i: (0, 0, 0)),
        ],
        out_specs=pl.BlockSpec((n_shift, tm, d), lambda i: (0, i, 0)),
        out_shape=jax.ShapeDtypeStruct((n_shift, m, d), BF16),
        compiler_params=_cparams(("parallel",)),
        name="rwkv_mix",
    )(h, h, h, mix)


def _rw_prep_body(k_ref, lw0_ref, lw1_ref, al0_ref, al1_ref, w0_ref, a0_ref, kk_ref, ka_ref, e_ref,
                  olw0, olw1, okd0, okd1, obb0, obb1, oaa):
    k = k_ref[...]
    kk0 = k * kk_ref[...]
    kk = kk0 * lax.rsqrt(_headsum(kk0 * kk0, e_ref) + 1e-12)
    oaa[...] = -kk
    for d, (lw_ref, al_ref, olw, okd, obb) in enumerate(((lw0_ref, al0_ref, olw0, okd0, obb0),
                                                         (lw1_ref, al1_ref, olw1, okd1, obb1))):
        z = -(w0_ref[d:d + 1, :] + lw_ref[...])
        softplus = jnp.maximum(z, 0.0) + jnp.log(1.0 + jnp.exp(-jnp.abs(z)))
        log_w = -softplus - 0.5
        olw[...] = -jnp.exp(log_w)
        a = jax.nn.sigmoid(a0_ref[d:d + 1, :] + al_ref[...])
        okd[...] = k * (1.0 + (a - 1.0) * ka_ref[...])
        obb[...] = kk * a


def rwkv_prep(k, lw0, lw1, al0, al1, w0, a0, k_k, k_a, e128, tm=1088):
    m, d = k.shape
    blk = pl.BlockSpec((tm, LANES), lambda i, c: (i, c))
    vec2 = pl.BlockSpec((2, LANES), lambda i, c: (0, c))
    vec1 = pl.BlockSpec((1, LANES), lambda i, c: (0, c))
    sh = jax.ShapeDtypeStruct((m, d), F32)
    return pl.pallas_call(
        _rw_prep_body,
        grid=(m // tm, d // LANES),
        in_specs=[blk] * 5 + [vec2, vec2, vec1, vec1, pl.BlockSpec((LANES, LANES), lambda i, c: (0, 0))],
        out_specs=[blk] * 7,
        out_shape=[sh] * 7,
        compiler_params=_cparams(("parallel", "arbitrary")),
        name="rwkv_prep",
    )(k, lw0, lw1, al0, al1, w0, a0, k_k.reshape(1, d), k_a.reshape(1, d), e128)


def _b16(x):
    return x.astype(BF16)


def _wkv_body(r_ref, lw_ref, k_ref, v_ref, a_ref, b_ref, o_ref, s_sc, *, chunk, n_heads, reverse):
    c = pl.program_id(2)

    @pl.when(c == 0)
    def _():
        s_sc[...] = jnp.zeros_like(s_sc)

    C = chunk
    hs = range(n_heads)
    row = lax.broadcasted_iota(jnp.int32, (C, C), 0)
    col = lax.broadcasted_iota(jnp.int32, (C, C), 1)
    if reverse:
        tri_incl = col >= row
        tri_strict = col > row
        last = 0
    else:
        tri_incl = col <= row
        tri_strict = col < row
        last = C - 1
    same = lambda s: (row >> int(math.log2(s))) == (col >> int(math.log2(s)))
    eye = (row == col).astype(F32)
    cum_m = tri_incl.astype(BF16)
    sls = [slice(h * RW_HEAD, (h + 1) * RW_HEAD) for h in hs]

    lw = [lw_ref[:, sl] for sl in sls]
    cum = []
    for h in hs:
        l1, l2, l3 = _split3(lw[h])
        cum.append(_dot(cum_m, l1) + _dot(cum_m, l2) + _dot(cum_m, l3))
    p_inc = [jnp.exp(cum[h]) for h in hs]
    x1 = [jnp.concatenate([a_ref[:, sls[h]] * jnp.exp(cum[h] - lw[h]), r_ref[:, sls[h]] * p_inc[h]], axis=0) for h in hs]
    x2 = []
    for h in hs:
        p_inv = jnp.exp(-cum[h])
        x2.append(jnp.concatenate([b_ref[:, sls[h]] * p_inv, k_ref[:, sls[h]] * p_inv], axis=0))
    v = [v_ref[:, sl] for sl in sls]
    s0 = [s_sc[h] for h in hs]
    mfull = [_dot3(x1[h], x2[h], _dot_nt) for h in hs]
    w0 = [_dot3(x1[h], s0[h], _dot_nt) for h in hs]
    n_ab = [jnp.where(tri_strict, mfull[h][:C, :C], 0.0) for h in hs]
    a_ak = [_b16(jnp.where(tri_strict, mfull[h][:C, C:], 0.0)) for h in hs]
    a_r = [_b16(jnp.concatenate([jnp.where(tri_incl, mfull[h][C:, :C], 0.0),
                                 jnp.where(tri_incl, mfull[h][C:, C:], 0.0)], axis=1)) for h in hs]
    rhs = [w0[h][:C] + _dot(a_ak[h], _b16(v[h])) for h in hs]

    n4 = [jnp.where(same(4), n_ab[h], 0.0) for h in hs]
    n4b = [_b16(n4[h]) for h in hs]
    sq = [_dot(n4b[h], n4b[h]) for h in hs]
    t = [(eye + n4[h]) + _dot(_b16(eye + n4[h]), _b16(sq[h])) for h in hs]
    s = 4
    while s < C:
        off = jnp.logical_and(jnp.logical_and(same(2 * s), jnp.logical_not(same(s))), tri_strict)
        tb = [_b16(t[h]) for h in hs]
        tmp = [_dot(tb[h], _b16(jnp.where(off, n_ab[h], 0.0))) for h in hs]
        t = [t[h] + _dot(_b16(tmp[h]), tb[h]) for h in hs]
        s *= 2
    tb = [_b16(t[h]) for h in hs]
    u = [_dot(tb[h], _b16(rhs[h])) for h in hs]
    res = [rhs[h] - u[h] + _dot3(n_ab[h], u[h]) for h in hs]
    u = [u[h] + _dot(tb[h], _b16(res[h])) for h in hs]
    uv = [jnp.concatenate([u[h], v[h]], axis=0) for h in hs]
    y = [w0[h][C:] + _dot(a_r[h], _b16(uv[h])) for h in hs]
    for h in hs:
        s_sc[h] = (s0[h] + _dot3(uv[h], x2[h], _dot_tn)) * p_inc[h][last:last + 1, :]
    o_ref[...] = jnp.concatenate(y, axis=1)


def wkv7(r, lw, kd, v, aa, bb, *, reverse, batch, seq, ctx_len, chunk=128, heads_per_step=8):
    m, d = r.shape
    heads_per_step = min(heads_per_step, d // RW_HEAD)
    wl = heads_per_step * RW_HEAD
    n_ctx = ctx_len // chunk
    n_lat = seq // chunk
    n_chunks = n_ctx + n_lat
    ctx0 = (batch * seq) // chunk

    def row_block(b, c):
        if reverse:
            return jnp.where(c < n_ctx, ctx0 + b * n_ctx + (n_ctx - 1 - c), b * n_lat + (n_lat - 1 - (c - n_ctx)))
        return jnp.where(c < n_ctx, ctx0 + b * n_ctx + c, b * n_lat + (c - n_ctx))

    blk = pl.BlockSpec((chunk, wl), lambda b, g, c: (row_block(b, c), g))
    body = functools.partial(_wkv_body, chunk=chunk, n_heads=heads_per_step, reverse=reverse)
    return pl.pallas_call(
        body,
        grid=(batch, d // wl, n_chunks),
        in_specs=[blk] * 6,
        out_specs=blk,
        out_shape=jax.ShapeDtypeStruct((m, d), F32),
        scratch_shapes=[pltpu.VMEM((heads_per_step, RW_HEAD, RW_HEAD), F32)],
        compiler_params=_cparams(("parallel", "parallel", "arbitrary")),
        name="wkv7_rev" if reverse else "wkv7_fwd",
    )(r, lw, kd, v, aa, bb)


def _rw_readout_body(y0_ref, y1_ref, r_ref, v_ref, kd0_ref, kd1_ref, g_ref, rk_ref, lnw_ref, lnb_ref, e_ref, o_ref,
                     *, gn_eps):
    y = y0_ref[...] + y1_ref[...]
    inv = 1.0 / RW_HEAD
    mu = _headsum(y, e_ref) * inv
    dlt = y - mu
    var = _headsum(dlt * dlt, e_ref) * inv
    yn = dlt * lax.rsqrt(var + gn_eps) * lnw_ref[...] + lnb_ref[...]
    rk = r_ref[...] * rk_ref[...]
    bonus = (_headsum(rk * kd0_ref[...], e_ref) + _headsum(rk * kd1_ref[...], e_ref)) * v_ref[...]
    o_ref[...] = ((yn + bonus) * g_ref[...]).astype(o_ref.dtype)


def rwkv_readout(y0, y1, r, v, kd0, kd1, g, r_k, ln_w, ln_b, e128, gn_eps, rows, tm=512):
    d = r.shape[1]
    blk = pl.BlockSpec((tm, LANES), lambda i, c: (i, c))
    vec1 = pl.BlockSpec((1, LANES), lambda i, c: (0, c))
    return pl.pallas_call(
        functools.partial(_rw_readout_body, gn_eps=gn_eps),
        grid=(rows // tm, d // LANES),
        in_specs=[blk] * 7 + [vec1] * 3 + [pl.BlockSpec((LANES, LANES), lambda i, c: (0, 0))],
        out_specs=blk,
        out_shape=jax.ShapeDtypeStruct((rows, d), BF16),
        compiler_params=_cparams(("parallel", "arbitrary")),
        name="rwkv_readout",
    )(y0, y1, r, v, kd0, kd1, g, r_k.reshape(1, d), ln_w.reshape(1, d), ln_b.reshape(1, d), e128)


def rwkv_mixer(hmix, p, *, batch, seq, ctx_len, d_model):
    xs = rwkv_mix(hmix, p["mix"], batch=batch, seq=seq, ctx_len=ctx_len)
    m = hmix.shape[0]
    xs2 = xs.reshape(xs.shape[0] * m, d_model)
    tm = 1088 if m % 1088 == 0 else 512

    def proj(n, w, **kw):
        return mm(xs2, w, tm=tm, rows=m, x_row_block0=n * (m // tm), **kw)

    r = proj(0, p["w_r"], name="rw_r")
    k = proj(2, p["w_k"], name="rw_k")
    v = proj(3, p["w_v"], name="rw_v")
    g = mm(proj(5, p["g1"], act="sigmoid", out_dtype=BF16, name="rw_g1"), p["g2"], tm=tm, name="rw_g2")
    lws, als = [], []
    for d in range(2):
        tw = proj(1, p["w1"][d], act="tanh", out_dtype=BF16, name="rw_w1")
        lws.append(mm(tw, p["w2"][d], tm=tm, name="rw_w2"))
        ah = proj(4, p["a1"][d], out_dtype=BF16, name="rw_a1")
        als.append(mm(ah, p["a2"][d], tm=tm, name="rw_a2"))
    e128 = jnp.asarray(np.kron(np.eye(LANES // RW_HEAD), np.ones((RW_HEAD, RW_HEAD))), BF16)
    lw0, lw1, kd0, kd1, bb0, bb1, aa = rwkv_prep(k, lws[0], lws[1], als[0], als[1], p["w0"], p["a0"],
                                                 p["k_k"], p["k_a"], e128, tm=tm)
    kw = dict(batch=batch, seq=seq, ctx_len=ctx_len)
    y0 = wkv7(r, lw0, kd0, v, aa, bb0, reverse=False, **kw)
    y1 = wkv7(r, lw1, kd1, v, aa, bb1, reverse=True, **kw)
    return rwkv_readout(y0, y1, r, v, kd0, kd1, g, p["r_k"], p["ln_w"], p["ln_b"], e128,
                        1e-5 * RW_HEAD, rows=m)


def mla_mixer(hmix, p, *, batch, seq, ctx_len, want_ctx):
    m = hmix.shape[0]
    tm = 1088 if m % 1088 == 0 else 512
    scale = (MLA_NOPE + MLA_ROPE) ** -0.5
    down = mm(hmix, p["w_down_ext"], tm=tm, name="mla_down")
    kq = MLA_Q_RANK
    q_raw = mm(down, p["w_uq_ext"], norm_g=p["q_norm_g"], x_col_block=0, tm=tm, name="mla_uq")
    kv = mm(down, p["w_ukv"], norm_g=p["kv_norm_g"], x_col_block=1, tm=tm, out_dtype=BF16, name="mla_ukv")
    cos, sin = _rope_tables(seq, ctx_len, batch, MLA_ROPE)
    zeros = jnp.zeros_like(cos)
    cos_p = jnp.concatenate([cos, zeros], -1)
    sin_p = jnp.concatenate([sin, zeros], -1)
    q = mla_rope(q_raw, 0, MLA_HEADS, cos_p, sin_p, scale, True, tm=tm)
    kcat = mla_kcat(kv, down, (kq + MLA_KV_RANK) // LANES, MLA_HEADS, cos_p, sin_p, tm=tm)
    kw = dict(dqk=2 * LANES, n_heads=MLA_HEADS, batch=batch, seq=seq, ctx_len=ctx_len)
    o_lat = attention((q, lambda h: h), (kcat, lambda h: h), kv, lambda h: 2 * h + 1, lat_queries=True, **kw)
    if not want_ctx:
        return o_lat
    o_ctx = attention((q, lambda h: h), (kcat, lambda h: h), kv, lambda h: 2 * h + 1, lat_queries=False, **kw)
    return jnp.concatenate([o_lat, o_ctx], 0)


def gqa_mixer(hmix, p, *, batch, seq, ctx_len, want_ctx):
    m = hmix.shape[0]
    tm = 1088 if m % 1088 == 0 else 512
    scale = GQA_HEAD ** -0.5
    raw = mm(hmix, p["w_qkv_ext"], tm=tm, name="gqa_qkv")
    cos, sin = _rope_tables(seq, ctx_len, batch, GQA_HEAD)
    nq, nk = GQA_HEADS, GQA_KV_HEADS
    q = gqa_prep(raw, 0, nq, nq, cos, sin, p["q_norm_g"], p["q_norm_g_rot"], scale, tm=tm)
    k = gqa_prep(raw, 2 * nq, 2 * nq + nk, nk, cos, sin, p["k_norm_g"], p["k_norm_g_rot"], 1.0, tm=tm)
    vcb0 = 2 * nq + 2 * nk
    vb = mm_cast(raw, vcb0, nk, tm=tm)
    grp = nq // nk
    kw = dict(dqk=LANES, n_heads=nq, batch=batch, seq=seq, ctx_len=ctx_len)
    o_lat = attention((q, lambda h: h), (k, lambda h: h // grp), vb, lambda h: h // grp, lat_queries=True, **kw)
    if not want_ctx:
        return o_lat
    o_ctx = attention((q, lambda h: h), (k, lambda h: h // grp), vb, lambda h: h // grp, lat_queries=False, **kw)
    return jnp.concatenate([o_lat, o_ctx], 0)


def _cast_body(x_ref, o_ref):
    o_ref[...] = x_ref[...].astype(o_ref.dtype)


def mm_cast(raw, cb0, nblocks, tm=512):
    m = raw.shape[0]
    return pl.pallas_call(
        _cast_body,
        grid=(m // tm, nblocks),
        in_specs=[pl.BlockSpec((tm, LANES), lambda i, c: (i, cb0 + c))],
        out_specs=pl.BlockSpec((tm, LANES), lambda i, c: (i, c)),
        out_shape=jax.ShapeDtypeStruct((m, nblocks * LANES), BF16),
        compiler_params=_cparams(("parallel", "arbitrary")),
        name="cast_cols",
    )(raw)


def _rot_gain(g):
    q = g.shape[-1] // 4
    g4 = g.reshape(2, 2, q)
    return jnp.concatenate([g4[:, 1:2], g4[:, 0:1]], axis=1).reshape(g.shape)


def kernel(x, c, ctx, c_ctx, ada_w, ada_b, norm_g, ffn_w_gu, ffn_w_down, hy_w_in, hy_b_in, hy_conv_w, hy_conv_b, hy_f_w1, hy_f_b1, hy_f_w2, hy_f_b2, hy_f_w3, hy_sin_freq, hy_skip, hy_w_out, hy_b_out, rw_mix, rw_w_rkv, rw_w0, rw_w1, rw_w2, rw_a0, rw_a1, rw_a2, rw_g1, rw_g2, rw_k_k, rw_k_a, rw_r_k, rw_ln_w, rw_ln_b, rw_w_o, mla_w_down, mla_q_norm_g, mla_w_uq, mla_kv_norm_g, mla_w_ukv, mla_w_o, gqa_w_qkv, gqa_q_norm_g, gqa_k_norm_g, gqa_w_o, final_g):
    batch, seq, d = x.shape
    ctx_len = ctx.shape[1]
    depth = ada_w.shape[0]
    n_mix = 4
    rows_lat = batch * seq
    rows_all = rows_lat + batch * ctx_len
    bf = lambda a: a.astype(BF16)
    w_gu_all, w_down_all = bf(ffn_w_gu), bf(ffn_w_down)

    cvec = jnp.concatenate([c, c_ctx[None], jnp.zeros((8 - batch - 1, d), F32)], 0)
    mods_all = adaln_all(cvec, ada_w, ada_b)[:, :batch + 1].reshape(depth, batch + 1, N_MOD, d)

    h = jnp.concatenate([x.reshape(rows_lat, d), ctx.reshape(batch * ctx_len, d)], 0)

    for i in range(depth):
        mi, j = i % n_mix, i // n_mix
        ctx_out = any(jj % n_mix != 0 for jj in range(i + 1, depth))
        mods = mods_all[i]
        gate5 = mods[:, 5:6, :]
        mix_dtype = F32 if mi == 1 else BF16
        h, hmix = ffn_half(h, mods, norm_g[i], w_gu_all, w_down_all, layer=i, which=0, s=0, rows=rows_all,
                           seq_rows=seq, next_g=norm_g[i, 1], next_s=1, next_dtype=mix_dtype)
        rows_out = rows_all if ctx_out else rows_lat
        kw = dict(batch=batch, seq=seq, ctx_len=ctx_len)
        if mi == 0:
            p = dict(w_in=bf(hy_w_in[j]), b_in=hy_b_in[j], conv_w=hy_conv_w[j], conv_b=hy_conv_b[j],
                     f_w1=hy_f_w1[j], f_b1=hy_f_b1[j], f_w2=hy_f_w2[j], f_b2=hy_f_b2[j], f_w3=hy_f_w3[j],
                     sin_freq=hy_sin_freq[j], skip=hy_skip[j])
            z = hyena_mixer(hmix, p, d_model=d, want_ctx=ctx_out, **kw)
            w_o, b_o = bf(hy_w_out[j]), hy_b_out[j]
        elif mi == 1:
            pad = lambda w, ax: jnp.pad(w, [(0, RW_LORA_PAD - w.shape[a]) if a == ax else (0, 0) for a in range(w.ndim)])
            p = dict(mix=rw_mix[j], w_r=bf(rw_w_rkv[j, 0]), w_k=bf(rw_w_rkv[j, 1]), w_v=bf(rw_w_rkv[j, 2]),
                     g1=bf(rw_g1[j]), g2=bf(rw_g2[j]),
                     w1=bf(pad(rw_w1[j], 2)), w2=bf(pad(rw_w2[j], 1)), a1=bf(pad(rw_a1[j], 2)), a2=bf(pad(rw_a2[j], 1)),
                     w0=rw_w0[j], a0=rw_a0[j], k_k=rw_k_k[j], k_a=rw_k_a[j], r_k=rw_r_k[j].reshape(-1),
                     ln_w=rw_ln_w[j], ln_b=rw_ln_b[j])
            z = rwkv_mixer(hmix, p, d_model=d, **kw)
            w_o, b_o = bf(rw_w_o[j]), None
        elif mi == 2:
            wd = mla_w_down[j]
            pe = wd[:, MLA_Q_RANK + MLA_KV_RANK:]
            wuq = mla_w_uq[j].reshape(MLA_Q_RANK, MLA_HEADS, MLA_NOPE + MLA_ROPE)
            wq_pe = wuq[..., MLA_NOPE:]
            wuq_ext = jnp.concatenate([wuq[..., :MLA_NOPE], wq_pe,
                                       _rot_cols(wq_pe.reshape(MLA_Q_RANK, -1), MLA_ROPE).reshape(wq_pe.shape)], -1)
            p = dict(w_down_ext=bf(jnp.concatenate([wd, _rot_cols(pe, MLA_ROPE)], 1)),
                     w_uq_ext=bf(wuq_ext.reshape(MLA_Q_RANK, -1)), w_ukv=bf(mla_w_ukv[j]),
                     q_norm_g=mla_q_norm_g[j], kv_norm_g=mla_kv_norm_g[j])
            z = mla_mixer(hmix, p, want_ctx=ctx_out, **kw)
            w_o, b_o = bf(mla_w_o[j]), None
        else:
            w = gqa_w_qkv[j]
            nqc = GQA_HEADS * GQA_HEAD
            nkc = GQA_KV_HEADS * GQA_HEAD
            wq, wk, wv = w[:, :nqc], w[:, nqc:nqc + nkc], w[:, nqc + nkc:]
            p = dict(w_qkv_ext=bf(jnp.concatenate([wq, _rot_cols(wq, GQA_HEAD), wk, _rot_cols(wk, GQA_HEAD), wv], 1)),
                     q_norm_g=gqa_q_norm_g[j], q_norm_g_rot=_rot_gain(gqa_q_norm_g[j]),
                     k_norm_g=gqa_k_norm_g[j], k_norm_g_rot=_rot_gain(gqa_k_norm_g[j]))
            z = gqa_mixer(hmix, p, want_ctx=ctx_out, **kw)
            w_o, b_o = bf(gqa_w_o[j]), None
        h = mm(z, w_o, bias=b_o, res=h, gate=gate5, seq_rows=seq, rows=rows_out, name="mix_out")
        last = i == depth - 1
        if last:
            h, out = ffn_half(h, mods, norm_g[i], w_gu_all, w_down_all, layer=i, which=1, s=2, rows=rows_out,
                              seq_rows=seq, next_g=final_g, next_s=None, next_dtype=F32)
        else:
            h = ffn_half(h, mods, norm_g[i], w_gu_all, w_down_all, layer=i, which=1, s=2, rows=rows_out,
                         seq_rows=seq)
    return out[:rows_lat].reshape(batch, seq, d)
```

```python
import functools
import math

import numpy as np
import jax
import jax.numpy as jnp
from jax import lax
from jax.experimental import pallas as pl
from jax.experimental.pallas import tpu as pltpu

F32 = jnp.float32
BF16 = jnp.bfloat16

VMEM_CAP_BYTES = 56 * 1024 * 1024
LANES = 128
SUBLANES = 8

NORM_EPS = 1e-6
N_MOD = 9
GRID_W = 64
ROPE_THETA = 10000.0
HY_DECAY_TARGET = 1e-2
HY_DECAY_PCT_SHORT = 0.3
HY_DECAY_PCT_LONG = 1.5
RW_HEAD = 64
RW_LORA_PAD = 128
GQA_HEAD = 128
GQA_HEADS = 16
GQA_KV_HEADS = 8
MLA_HEADS = 16
MLA_NOPE = 128
MLA_ROPE = 64
MLA_V = 128
MLA_Q_RANK = 512
MLA_KV_RANK = 512


def _cparams(sem):
    return pltpu.CompilerParams(dimension_semantics=sem, vmem_limit_bytes=VMEM_CAP_BYTES)


def _dot(a, b):
    return jnp.dot(a, b, preferred_element_type=F32)


def _dot_nt(a, b):
    return lax.dot_general(a, b, (((1,), (1,)), ((), ())), preferred_element_type=F32)


def _dot_tn(a, b):
    return lax.dot_general(a, b, (((0,), (0,)), ((), ())), preferred_element_type=F32)


def _split2(x):
    hi = x.astype(BF16)
    lo = (x - hi.astype(F32)).astype(BF16)
    return hi, lo


def _split3(x):
    hi = x.astype(BF16)
    r1 = x - hi.astype(F32)
    mid = r1.astype(BF16)
    lo = (r1 - mid.astype(F32)).astype(BF16)
    return hi, mid, lo


def _dot3(a, b, f=_dot):
    ah, al = _split2(a)
    bh, bl = _split2(b)
    return f(ah, bh) + f(ah, bl) + f(al, bh)


def _dotc3(mh, ml, d):
    dh, dl = _split2(d)
    return _dot(mh, dh) + _dot(ml, dh) + _dot(mh, dl)


def _dotc2(mh, ml, d):
    dh = d.astype(BF16)
    return _dot(mh, dh) + _dot(ml, dh)


def _rms(x, g):
    return x * lax.rsqrt(jnp.mean(x * x, -1, keepdims=True) + NORM_EPS) * g


def _pick(n, cands):
    for c in cands:
        if n % c == 0:
            return c
    raise ValueError(f"no tile for {n}")


def _adaln_body(c_ref, w_ref, b_ref, o_ref):
    c = c_ref[...]
    s = c * jax.nn.sigmoid(c)
    o_ref[0] = _dot(s.astype(BF16), w_ref[0].astype(BF16)) + b_ref[0]


def adaln_all(cvec8, ada_w, ada_b):
    depth, d, n = ada_w.shape
    tn = _pick(n, (2048, 1024, 512, 256, 128))
    return pl.pallas_call(
        _adaln_body,
        grid=(depth, n // tn),
        in_specs=[
            pl.BlockSpec((8, d), lambda l, j: (0, 0)),
            pl.BlockSpec((1, d, tn), lambda l, j: (l, 0, j)),
            pl.BlockSpec((1, 1, tn), lambda l, j: (l, 0, j)),
        ],
        out_specs=pl.BlockSpec((1, 8, tn), lambda l, j: (l, 0, j)),
        out_shape=jax.ShapeDtypeStruct((depth, 8, n), F32),
        compiler_params=_cparams(("arbitrary", "arbitrary")),
        name="adaln",
    )(cvec8, ada_w, ada_b.reshape(depth, 1, n))


def _mm_body(*refs, has_bias, act, has_norm, has_res):
    it = iter(refs)
    x_ref = next(it)
    w_ref = next(it)
    b_ref = next(it) if has_bias else None
    g_ref = next(it) if has_norm else None
    r_ref = next(it) if has_res else None
    gate_ref = next(it) if has_res else None
    o_ref = next(it)
    x = x_ref[...]
    if has_norm:
        x = _rms(x.astype(F32), g_ref[...])
    acc = _dot(x.astype(BF16), w_ref[...])
    if has_bias:
        acc = acc + b_ref[...]
    if act == "sigmoid":
        acc = jax.nn.sigmoid(acc)
    elif act == "tanh":
        acc = jnp.tanh(acc)
    if has_res:
        acc = r_ref[...] + gate_ref[0] * acc
    o_ref[...] = acc.astype(o_ref.dtype)


def mm(x, w, *, bias=None, act=None, norm_g=None, res=None, gate=None, out_dtype=F32,
       tm=512, tn=None, rows=None, x_col_block=0, x_row_block0=0, seq_rows=None, name="mm"):
    k, n = w.shape
    m = rows if rows is not None else x.shape[0]
    assert m % tm == 0, (m, tm)
    tn = tn or _pick(n, (512, 384, 256, 128))
    has_res = res is not None
    ins = [x, w]
    specs = [pl.BlockSpec((tm, k), lambda i, j: (i + x_row_block0, x_col_block)),
             pl.BlockSpec((k, tn), lambda i, j: (0, j))]
    if bias is not None:
        ins.append(bias.reshape(1, n).astype(F32))
        specs.append(pl.BlockSpec((1, tn), lambda i, j: (0, j)))
    if norm_g is not None:
        ins.append(norm_g.reshape(1, k).astype(F32))
        specs.append(pl.BlockSpec((1, k), lambda i, j: (0, 0)))
    if has_res:
        ngroups = gate.shape[0]
        per = seq_rows // tm
        ins += [res, gate]
        specs += [pl.BlockSpec((tm, tn), lambda i, j: (i, j)),
                  pl.BlockSpec((1, 1, tn), lambda i, j: (jnp.minimum(i // per, ngroups - 1), 0, j))]
    body = functools.partial(_mm_body, has_bias=bias is not None, act=act,
                             has_norm=norm_g is not None, has_res=has_res)
    return pl.pallas_call(
        body,
        grid=(m // tm, n // tn),
        in_specs=specs,
        out_specs=pl.BlockSpec((tm, tn), lambda i, j: (i, j)),
        out_shape=jax.ShapeDtypeStruct((m, n), out_dtype),
        compiler_params=_cparams(("parallel", "arbitrary")),
        name=name,
    )(*ins)


def _ffn_body(*refs, s, n_f, has_next, next_s):
    it = iter(refs)
    x_ref, mod_ref, g_ref, wa_ref, wb_ref, wd_ref = (next(it) for _ in range(6))
    gn_ref = next(it) if has_next else None
    o_ref = next(it)
    hn_ref = next(it) if has_next else None
    xn_sc = next(it)
    acc_sc = next(it)
    j = pl.program_id(1)

    @pl.when(j == 0)
    def _():
        n = _rms(x_ref[...], g_ref[s:s + 1, :])
        n = n * (1.0 + mod_ref[0, 3 * s + 1:3 * s + 2, :]) + mod_ref[0, 3 * s:3 * s + 1, :]
        xn_sc[...] = n.astype(BF16)
        acc_sc[...] = jnp.zeros_like(acc_sc)

    xn = xn_sc[...]
    a = _dot(xn, wa_ref[...])
    b = _dot(xn, wb_ref[...])
    h = (a * jax.nn.sigmoid(a)) * b
    acc_sc[...] += _dot(h.astype(BF16), wd_ref[...])

    @pl.when(j == n_f - 1)
    def _():
        xnew = x_ref[...] + 0.5 * mod_ref[0, 3 * s + 2:3 * s + 3, :] * acc_sc[...]
        o_ref[...] = xnew
        if has_next:
            hn = _rms(xnew, gn_ref[...])
            if next_s is not None:
                hn = hn * (1.0 + mod_ref[0, 3 * next_s + 1:3 * next_s + 2, :]) + mod_ref[0, 3 * next_s:3 * next_s + 1, :]
            hn_ref[...] = hn.astype(hn_ref.dtype)


def ffn_half(x, mods, norm_g3, w_gu, w_down, *, layer, which, s, rows, seq_rows, next_g=None, next_s=None,
             next_dtype=F32, tm=512, fc=512):
    d = x.shape[1]
    d_ff = w_down.shape[2]
    n_f = d_ff // fc
    assert d_ff % fc == 0 and rows % tm == 0
    per = seq_rows // tm
    ngroups = mods.shape[0]
    has_next = next_g is not None
    ins = [x, mods, norm_g3, w_gu, w_gu, w_down]
    specs = [
        pl.BlockSpec((tm, d), lambda i, j: (i, 0)),
        pl.BlockSpec((1, N_MOD, d), lambda i, j: (jnp.minimum(i // per, ngroups - 1), 0, 0)),
        pl.BlockSpec((3, d), lambda i, j: (0, 0)),
        pl.BlockSpec((None, None, d, fc), lambda i, j: (layer, which, 0, j)),
        pl.BlockSpec((None, None, d, fc), lambda i, j: (layer, which, 0, n_f + j)),
        pl.BlockSpec((None, None, fc, d), lambda i, j: (layer, which, j, 0)),
    ]
    out_shape = [jax.ShapeDtypeStruct((rows, d), F32)]
    out_specs = [pl.BlockSpec((tm, d), lambda i, j: (i, 0))]
    if has_next:
        ins.append(next_g.reshape(1, d))
        specs.append(pl.BlockSpec((1, d), lambda i, j: (0, 0)))
        out_shape.append(jax.ShapeDtypeStruct((rows, d), next_dtype))
        out_specs.append(pl.BlockSpec((tm, d), lambda i, j: (i, 0)))
    body = functools.partial(_ffn_body, s=s, n_f=n_f, has_next=has_next, next_s=next_s)
    outs = pl.pallas_call(
        body,
        grid=(rows // tm, n_f),
        in_specs=specs,
        out_specs=out_specs,
        out_shape=out_shape,
        scratch_shapes=[pltpu.VMEM((tm, d), BF16), pltpu.VMEM((tm, d), F32)],
        compiler_params=_cparams(("parallel", "arbitrary")),
        name="ffn_half",
    )(*ins)
    return outs if has_next else outs[0]


def _attn_body(*refs, use_lat, hp, n_kv, dqk):
    it = iter(refs)
    q_ref = next(it)
    kl_ref = next(it) if use_lat else None
    kc_ref = next(it)
    vl_refs = [next(it) for _ in range(n_kv)] if use_lat else None
    vc_refs = [next(it) for _ in range(n_kv)]
    o_ref = next(it)
    hs = range(hp)
    slot = [h * n_kv // hp for h in hs]
    q = [q_ref[:, h * dqk:(h + 1) * dqk] for h in hs]
    s_c = [_dot_nt(kc_ref[:, slot[h] * dqk:(slot[h] + 1) * dqk], q[h]) for h in hs]
    m = [jnp.max(s_c[h], 0, keepdims=True) for h in hs]
    if use_lat:
        s_l = [_dot_nt(kl_ref[:, slot[h] * dqk:(slot[h] + 1) * dqk], q[h]) for h in hs]
        m = [jnp.maximum(m[h], jnp.max(s_l[h], 0, keepdims=True)) for h in hs]
    p_c = [jnp.exp(s_c[h] - m[h]) for h in hs]
    den = [jnp.sum(p_c[h], 0, keepdims=True) for h in hs]
    o = [_dot_tn(vc_refs[slot[h]][...], p_c[h].astype(BF16)) for h in hs]
    if use_lat:
        p_l = [jnp.exp(s_l[h] - m[h]) for h in hs]
        den = [den[h] + jnp.sum(p_l[h], 0, keepdims=True) for h in hs]
        o = [o[h] + _dot_tn(vl_refs[slot[h]][...], p_l[h].astype(BF16)) for h in hs]
    for h in hs:
        o_ref[:, h * LANES:(h + 1) * LANES] = (o[h] / den[h]).T.astype(o_ref.dtype)


def attention(q_arr, k_arr, v_arr, v_cb, *, dqk, kv_group, n_heads, batch, seq, ctx_len, lat_queries, tq=512, hp=1):
    assert n_heads % hp == 0 and (kv_group % hp == 0 or hp % kv_group == 0)
    n_kv = max(hp // kv_group, 1)
    kv0 = lambda p: (p * hp) // kv_group
    ctx0 = (batch * seq) // ctx_len
    if lat_queries:
        tq = min(tq, seq)
        nq = seq // tq
        q_row = lambda b, t: b * nq + t
        o_row = lambda b, t: b * nq + t
        rows_q = batch * seq
    else:
        tq = ctx_len
        nq = 1
        q_row = lambda b, t: ctx0 + b
        o_row = lambda b, t: b
        rows_q = batch * ctx_len
    ins = [q_arr]
    specs = [pl.BlockSpec((tq, hp * dqk), lambda b, p, t: (q_row(b, t), p))]
    if lat_queries:
        ins.append(k_arr)
        specs.append(pl.BlockSpec((seq, n_kv * dqk), lambda b, p, t: (b, kv0(p) // n_kv)))
    ins.append(k_arr)
    specs.append(pl.BlockSpec((ctx_len, n_kv * dqk), lambda b, p, t: (ctx0 + b, kv0(p) // n_kv)))
    if lat_queries:
        for s in range(n_kv):
            ins.append(v_arr)
            specs.append(pl.BlockSpec((seq, LANES), lambda b, p, t, s=s: (b, v_cb(kv0(p) + s))))
    for s in range(n_kv):
        ins.append(v_arr)
        specs.append(pl.BlockSpec((ctx_len, LANES), lambda b, p, t, s=s: (ctx0 + b, v_cb(kv0(p) + s))))
    return pl.pallas_call(
        functools.partial(_attn_body, use_lat=lat_queries, hp=hp, n_kv=n_kv, dqk=dqk),
        grid=(batch, n_heads // hp, nq),
        in_specs=specs,
        out_specs=pl.BlockSpec((tq, hp * LANES), lambda b, p, t: (o_row(b, t), p)),
        out_shape=jax.ShapeDtypeStruct((rows_q, n_heads * LANES), BF16),
        compiler_params=_cparams(("parallel", "parallel", "arbitrary")),
        name="attention_lat" if lat_queries else "attention_ctx",
    )(*ins)


def _rope_tables(seq, ctx_len, batch, rot_dim):
    half = rot_dim // 4
    inv_freq = ROPE_THETA ** (-jnp.arange(half, dtype=F32) / half)
    t = jnp.arange(seq, dtype=jnp.int32)
    row = (t // GRID_W).astype(F32)
    col = (t % GRID_W).astype(F32)
    ang_r = row[:, None] * inv_freq[None]
    ang_c = col[:, None] * inv_freq[None]
    cos = jnp.concatenate([jnp.cos(ang_r), jnp.cos(ang_r), jnp.cos(ang_c), jnp.cos(ang_c)], -1)
    sin = jnp.concatenate([jnp.sin(ang_r), jnp.sin(ang_r), jnp.sin(ang_c), jnp.sin(ang_c)], -1)
    cos = jnp.concatenate([jnp.tile(cos, (batch, 1)), jnp.ones((batch * ctx_len, rot_dim), F32)], 0)
    sin = jnp.concatenate([jnp.tile(sin, (batch, 1)), jnp.zeros((batch * ctx_len, rot_dim), F32)], 0)
    return cos, sin


def _rot_cols(w, rot_dim):
    lead = w.shape[:-1]
    q = rot_dim // 4
    w5 = w.reshape(lead + (-1, 2, 2, q))
    rot = jnp.concatenate([-w5[..., 1:2, :], w5[..., 0:1, :]], axis=-2)
    return rot.reshape(w.shape)


def _gqa_prep_body(a_ref, ar_ref, cos_ref, sin_ref, g_ref, gr_ref, o_ref, *, scale):
    a = a_ref[...]
    r = lax.rsqrt(jnp.mean(a * a, -1, keepdims=True) + NORM_EPS) * scale
    o = (a * g_ref[...]) * cos_ref[...] + (ar_ref[...] * gr_ref[...]) * sin_ref[...]
    o_ref[...] = (o * r).astype(o_ref.dtype)


def gqa_prep(raw, cb0, cbr0, n_heads, cos, sin, g, g_rot, scale, tm=1088):
    m = raw.shape[0]
    return pl.pallas_call(
        functools.partial(_gqa_prep_body, scale=scale),
        grid=(m // tm, n_heads),
        in_specs=[
            pl.BlockSpec((tm, LANES), lambda i, h: (i, cb0 + h)),
            pl.BlockSpec((tm, LANES), lambda i, h: (i, cbr0 + h)),
            pl.BlockSpec((tm, LANES), lambda i, h: (i, 0)),
            pl.BlockSpec((tm, LANES), lambda i, h: (i, 0)),
            pl.BlockSpec((1, LANES), lambda i, h: (0, 0)),
            pl.BlockSpec((1, LANES), lambda i, h: (0, 0)),
        ],
        out_specs=pl.BlockSpec((tm, LANES), lambda i, h: (i, h)),
        out_shape=jax.ShapeDtypeStruct((m, n_heads * LANES), BF16),
        compiler_params=_cparams(("parallel", "arbitrary")),
        name="gqa_prep",
    )(raw, raw, cos, sin, g.reshape(1, LANES), g_rot.reshape(1, LANES))


def _mla_rope_body(x_ref, c_ref, s_ref, o_ref, *, scale, with_nope):
    if with_nope:
        a = x_ref[:, :LANES]
        b = x_ref[:, LANES:]
    else:
        b = x_ref[...]
    roped = (b * c_ref[...] + pltpu.roll(b, LANES // 2, 1) * s_ref[...]) * scale
    if with_nope:
        o_ref[:, :LANES] = (a * scale).astype(o_ref.dtype)
        o_ref[:, LANES:] = roped.astype(o_ref.dtype)
    else:
        o_ref[...] = roped.astype(o_ref.dtype)


def mla_rope(raw, cb0, n_heads, cos, sin, scale, with_nope, tm=1088):
    m = raw.shape[0]
    w = 2 * LANES if with_nope else LANES
    return pl.pallas_call(
        functools.partial(_mla_rope_body, scale=scale, with_nope=with_nope),
        grid=(m // tm, n_heads),
        in_specs=[
            pl.BlockSpec((tm, w), lambda i, h: (i, cb0 + h)),
            pl.BlockSpec((tm, LANES), lambda i, h: (i, 0)),
            pl.BlockSpec((tm, LANES), lambda i, h: (i, 0)),
        ],
        out_specs=pl.BlockSpec((tm, w), lambda i, h: (i, h)),
        out_shape=jax.ShapeDtypeStruct((m, n_heads * w), BF16),
        compiler_params=_cparams(("parallel", "arbitrary")),
        name="mla_rope",
    )(raw, cos, sin)


def _mla_kcat_body(kn_ref, pe_ref, c_ref, s_ref, o_ref):
    b = pe_ref[...]
    o_ref[:, :LANES] = kn_ref[...]
    o_ref[:, LANES:] = (b * c_ref[...] + pltpu.roll(b, LANES // 2, 1) * s_ref[...]).astype(o_ref.dtype)


def mla_kcat(kv, down, pe_cb, n_heads, cos, sin, tm=1088):
    m = kv.shape[0]
    return pl.pallas_call(
        _mla_kcat_body,
        grid=(m // tm, n_heads),
        in_specs=[
            pl.BlockSpec((tm, LANES), lambda i, h: (i, 2 * h)),
            pl.BlockSpec((tm, LANES), lambda i, h: (i, pe_cb)),
            pl.BlockSpec((tm, LANES), lambda i, h: (i, 0)),
            pl.BlockSpec((tm, LANES), lambda i, h: (i, 0)),
        ],
        out_specs=pl.BlockSpec((tm, 2 * LANES), lambda i, h: (i, h)),
        out_shape=jax.ShapeDtypeStruct((m, n_heads * 2 * LANES), BF16),
        compiler_params=_cparams(("parallel", "arbitrary")),
        name="mla_kcat",
    )(kv, down, cos, sin)


class _FftPlan:
    def __init__(self, L):
        self.L = L
        self.N = 2 * L
        self.N2 = 128 if L >= 1024 else 64
        self.N1 = self.N // self.N2
        self.n1_in = self.N1 // 2
        self.K1 = self.N1 // 2 + 1
        self.nq = self.N2 // SUBLANES
        self.spec_rows = self.K1 * 2 * self.N2
        self.kf_cols = max(self.n1_in * SUBLANES, LANES)
        self.kf_rows = 2 * self.K1 * SUBLANES
        self.kb_cols = -(-self.kf_rows // LANES) * LANES
        self.kb_rows = self.n1_in * SUBLANES
        self.k1_unroll = 3 if self.K1 % 3 == 0 else 1

    def constants(self):
        N, N1, N2, K1, n1_in = self.N, self.N1, self.N2, self.K1, self.n1_in
        eye = np.eye(SUBLANES)
        k1 = np.arange(K1)[:, None]
        n1 = np.arange(n1_in)[None, :]
        th = 2 * np.pi * k1 * n1 / N1
        kf = np.zeros((self.kf_rows, self.kf_cols))
        kf[:K1 * SUBLANES, :n1_in * SUBLANES] = np.kron(np.cos(th), eye)
        kf[K1 * SUBLANES:, :n1_in * SUBLANES] = np.kron(-np.sin(th), eye)
        c = np.full((K1,), 2.0)
        c[0] = 1.0
        c[-1] = 1.0
        thb = 2 * np.pi * np.arange(n1_in)[:, None] * np.arange(K1)[None, :] / N1
        kb = np.zeros((self.kb_rows, self.kb_cols))
        kb[:, :K1 * SUBLANES] = np.kron(np.cos(thb) * c[None, :], eye)
        kb[:, K1 * SUBLANES:2 * K1 * SUBLANES] = np.kron(-np.sin(thb) * c[None, :], eye)
        a = 2 * np.pi * np.outer(np.arange(N2), np.arange(N2)) / N2
        C, S = np.cos(a), np.sin(a)
        w2f = np.block([[C, S], [-S, C]])
        w2i = np.block([[C, -S], [S, C]])
        tw_ang = 2 * np.pi * np.outer(np.arange(N2), np.arange(K1)) / N
        twr = np.repeat(np.cos(tw_ang), LANES, axis=1)
        twi = np.repeat(np.sin(tw_ang), LANES, axis=1)

        def hl(m):
            m32 = jnp.asarray(m, F32)
            hi = m32.astype(BF16)
            lo = (m32 - hi.astype(F32)).astype(BF16)
            return hi, lo

        out = {}
        for name, mat in (("kf", kf), ("kb", kb), ("w2f", w2f), ("w2i", w2i)):
            out[name + "_h"], out[name + "_l"] = hl(mat)
        out["twr"] = jnp.asarray(twr, F32)
        out["twi"] = jnp.asarray(twi, F32)
        return out


def _cmul(ar, ai, br, bi):
    return ar * br - ai * bi, ar * bi + ai * br


def _time_to_strided(plan, x_ref, xt_ref):
    N2, nq = plan.N2, plan.nq
    for n in range(plan.n1_in):
        xt_ref[n * SUBLANES:(n + 1) * SUBLANES, :] = jnp.concatenate(
            [x_ref[N2 * n + SUBLANES * q:N2 * n + SUBLANES * (q + 1), :] for q in range(nq)], axis=1)
    pad = plan.kf_cols - plan.n1_in * SUBLANES
    if pad:
        xt_ref[plan.n1_in * SUBLANES:, :] = jnp.zeros((pad, xt_ref.shape[1]), F32)


def _strided_to_spectrum(plan, a_ref, s_ref, c):
    N2, K1, nq = plan.N2, plan.K1, plan.nq
    for part in range(2):
        for k in range(K1):
            r0 = (part * K1 + k) * SUBLANES
            s_ref[part * N2:(part + 1) * N2, k * c:(k + 1) * c] = jnp.concatenate(
                [a_ref[r0:r0 + SUBLANES, q * c:(q + 1) * c] for q in range(nq)], axis=0)


def _spectrum_to_strided(plan, s_ref, b_ref, c):
    N2, K1, nq = plan.N2, plan.K1, plan.nq
    for part in range(2):
        for k in range(K1):
            r0 = (part * K1 + k) * SUBLANES
            b_ref[r0:r0 + SUBLANES, :] = jnp.concatenate(
                [s_ref[part * N2 + SUBLANES * q:part * N2 + SUBLANES * (q + 1), k * c:(k + 1) * c] for q in range(nq)],
                axis=1)
    pad = plan.kb_cols - plan.kf_rows
    if pad:
        b_ref[plan.kf_rows:, :] = jnp.zeros((pad, b_ref.shape[1]), F32)


def _strided_to_time(plan, yt, c):
    nq = plan.nq
    slabs = []
    for n in range(plan.n1_in):
        blk = yt[n * SUBLANES:(n + 1) * SUBLANES, :]
        slabs.extend(blk[:, q * c:(q + 1) * c] for q in range(nq))
    return jnp.concatenate(slabs, axis=0)


def _fft_fwd(plan, x_ref, consts, xt_ref, a_ref, s_ref, dotc):
    c = x_ref.shape[1]
    _time_to_strided(plan, x_ref, xt_ref)
    a_ref[:plan.kf_rows, :] = dotc(consts["kf_h"][...], consts["kf_l"][...], xt_ref[...])
    _strided_to_spectrum(plan, a_ref, s_ref, c)
    n2 = plan.N2
    tr, ti = consts["twr"][...], consts["twi"][...]
    xr, xi = _cmul(s_ref[:n2, :], s_ref[n2:, :], tr, -ti)
    s_ref[...] = dotc(consts["w2f_h"][...], consts["w2f_l"][...], jnp.concatenate([xr, xi], axis=0))


def _fft_conv_apply(plan, x_ref, g_ref, consts, xt_ref, a_ref, s_ref):
    c = x_ref.shape[1]
    n2 = plan.N2
    _fft_fwd(plan, x_ref, consts, xt_ref, a_ref, s_ref, _dotc2)
    pr, pi = _cmul(s_ref[:n2, :], s_ref[n2:, :], g_ref[:n2, :], g_ref[n2:, :])
    z = _dotc2(consts["w2i_h"][...], consts["w2i_l"][...], jnp.concatenate([pr, pi], axis=0))
    zr, zi = _cmul(z[:n2], z[n2:], consts["twr"][...], consts["twi"][...])
    s_ref[:n2, :] = zr
    s_ref[n2:, :] = zi
    _spectrum_to_strided(plan, s_ref, a_ref, c)
    yt = _dotc2(consts["kb_h"][...], consts["kb_l"][...], a_ref[...])
    return _strided_to_time(plan, yt, c)


def _fft_scratch(plan):
    wl = plan.nq * LANES
    return [pltpu.VMEM((plan.kf_cols, wl), F32),
            pltpu.VMEM((max(plan.kf_rows, plan.kb_cols), wl), F32),
            pltpu.VMEM((2 * plan.N2, plan.K1 * LANES), F32)]


_CONST_NAMES = ("kf_h", "kf_l", "kb_h", "kb_l", "w2f_h", "w2f_l", "w2i_h", "w2i_l", "twr", "twi")


def _const_specs(consts, nargs):
    ins, specs = [], []
    for nme in _CONST_NAMES:
        a = consts[nme]
        ins.append(a)
        if nargs == 1:
            specs.append(pl.BlockSpec(a.shape, lambda c: (0, 0), pipeline_mode=pl.Buffered(1)))
        else:
            specs.append(pl.BlockSpec(a.shape, lambda c, b: (0, 0), pipeline_mode=pl.Buffered(1)))
    return ins, specs


def _hy_hidden_body(z_ref, w1_ref, b1_ref, w2_ref, b2_ref, sf_ref, o_ref):
    h = jnp.sin(sf_ref[0:1, :] * (_dot3(z_ref[...], w1_ref[...]) + b1_ref[...]))
    h = jnp.sin(sf_ref[1:2, :] * (_dot3(h, w2_ref[...]) + b2_ref[...]))
    o_ref[...] = h


def hyena_hidden(L, f_w1, f_b1, f_w2, f_b2, sin_freq):
    emb, hid = f_w1.shape
    bands = (emb - 1) // 2
    t = jnp.linspace(0.0, 1.0, L, dtype=F32)[:, None]
    w = (2.0 * math.pi / L) * jnp.arange(L, dtype=F32)[:, None]
    f = jnp.linspace(1e-4, bands - 1, bands, dtype=F32)[None]
    z = jnp.concatenate([t, jnp.cos(f * w), -jnp.sin(f * w)], -1)
    embp = -(-emb // LANES) * LANES
    z = jnp.pad(z, ((0, 0), (0, embp - emb)))
    w1 = jnp.pad(f_w1, ((0, embp - emb), (0, 0)))
    ins = [z, w1, f_b1.reshape(1, hid), f_w2, f_b2.reshape(1, hid), sin_freq]
    return pl.pallas_call(
        _hy_hidden_body,
        out_shape=jax.ShapeDtypeStruct((L, hid), F32),
        name="hyena_hidden",
    )(*ins)


def _hy_filter_body(h_ref, wf_ref, wb_ref, dl_ref, *rest, plan):
    consts = {nme: r for nme, r in zip(_CONST_NAMES, rest)}
    g_ref, x_sc, xt_sc, a_sc, s_sc, sb_sc = rest[len(_CONST_NAMES):]
    L, n2 = plan.L, plan.N2
    t = lax.broadcasted_iota(jnp.int32, (L, LANES), 0).astype(F32) * (1.0 / (L - 1))
    win = jnp.exp(-t * dl_ref[...])
    row0 = lax.broadcasted_iota(jnp.int32, (L, LANES), 0) == 0
    h = h_ref[...]
    fwd = _dot3(h, wf_ref[...]) * win
    bwd = jnp.where(row0, 0.0, _dot3(h, wb_ref[...]) * win)
    ss = jnp.sum(fwd * fwd, 0, keepdims=True) + jnp.sum(bwd * bwd, 0, keepdims=True)
    nu = lax.rsqrt(ss + 1e-12) * (1.0 / plan.N)
    x_sc[...] = bwd * nu
    _fft_fwd(plan, x_sc, consts, xt_sc, a_sc, sb_sc, _dotc3)
    x_sc[...] = fwd * nu
    _fft_fwd(plan, x_sc, consts, xt_sc, a_sc, s_sc, _dotc3)
    g_ref[:n2, :] = s_sc[:n2, :] + sb_sc[:n2, :]
    g_ref[n2:, :] = s_sc[n2:, :] - sb_sc[n2:, :]


def hyena_filter_spectra(plan, consts, hidden, f_w3, d_model, n_order):
    L = plan.L
    hid = hidden.shape[1]
    ncb = d_model // LANES
    deltas = jnp.abs(jnp.linspace(math.log(HY_DECAY_TARGET) / HY_DECAY_PCT_SHORT,
                                  math.log(HY_DECAY_TARGET) / HY_DECAY_PCT_LONG, d_model, dtype=F32)).reshape(1, d_model)
    ins = [hidden, f_w3, f_w3, deltas]
    specs = [pl.BlockSpec((L, hid), lambda c, n: (0, 0)),
             pl.BlockSpec((hid, LANES), lambda c, n: (0, (2 * n) * ncb + c)),
             pl.BlockSpec((hid, LANES), lambda c, n: (0, (2 * n + 1) * ncb + c)),
             pl.BlockSpec((1, LANES), lambda c, n: (0, c))]
    ci, cs = _const_specs(consts, 2)
    ins += ci
    specs += cs
    srows, scols = 2 * plan.N2, plan.K1 * LANES
    return pl.pallas_call(
        functools.partial(_hy_filter_body, plan=plan),
        grid=(ncb, n_order),
        in_specs=specs,
        out_specs=pl.BlockSpec((None, None, srows, scols), lambda c, n: (n, c, 0, 0)),
        out_shape=jax.ShapeDtypeStruct((n_order, ncb, srows, scols), F32),
        scratch_shapes=[pltpu.VMEM((L, LANES), F32)] + _fft_scratch(plan) + [pltpu.VMEM((srows, scols), F32)],
        compiler_params=_cparams(("arbitrary", "arbitrary")),
        name="hyena_filter",
    )(*ins)


def _conv3(p, w_ref, b_ref, L):
    rows = lax.broadcasted_iota(jnp.int32, p.shape, 0)
    prev = jnp.where(rows == 0, 0.0, pltpu.roll(p, 1, 0))
    nxt = jnp.where(rows == L - 1, 0.0, pltpu.roll(p, L - 1, 0))
    return prev * w_ref[0:1, :] + p * w_ref[1:2, :] + nxt * w_ref[2:3, :] + b_ref[...]


def _hy_conv_body(*refs, plan, order, conv_u):
    it = iter(refs)
    u_ref, gate_ref = next(it), next(it)
    if conv_u:
        cwu_ref, cbu_ref = next(it), next(it)
    cwg_ref, cbg_ref = next(it), next(it)
    skip_ref = next(it)
    g_ref = next(it)
    consts = {nme: next(it) for nme in _CONST_NAMES}
    o_ref = next(it)
    x_sc, xt_sc, a_sc, s_sc = (next(it) for _ in range(4))
    L = plan.L
    u = u_ref[...]
    if conv_u:
        u = _conv3(u, cwu_ref, cbu_ref, L)
    x_sc[...] = u
    y = _fft_conv_apply(plan, x_sc, g_ref, consts, xt_sc, a_sc, s_sc)
    gate = _conv3(gate_ref[...], cwg_ref, cbg_ref, L)
    o_ref[...] = (gate * (y + x_sc[...] * skip_ref[order:order + 1, :])).astype(o_ref.dtype)


def hyena_conv(plan, consts, u_arr, u_cb0, gate_arr, gate_cb0, conv_w, conv_b, skip, spectra, *,
               order, conv_u, u_row_block0, gate_row_block0, batch, d_model, out_dtype):
    L = plan.L
    ncb = d_model // LANES
    ins = [u_arr, gate_arr]
    specs = [pl.BlockSpec((L, LANES), lambda c, b: (u_row_block0 + b, u_cb0 + c)),
             pl.BlockSpec((L, LANES), lambda c, b: (gate_row_block0 + b, gate_cb0 + c))]
    if conv_u:
        ins += [conv_w, conv_b]
        specs += [pl.BlockSpec((3, LANES), lambda c, b: (0, u_cb0 + c)),
                  pl.BlockSpec((1, LANES), lambda c, b: (0, u_cb0 + c))]
    srows, scols = 2 * plan.N2, plan.K1 * LANES
    ins += [conv_w, conv_b, skip, spectra]
    specs += [pl.BlockSpec((3, LANES), lambda c, b: (0, gate_cb0 + c)),
              pl.BlockSpec((1, LANES), lambda c, b: (0, gate_cb0 + c)),
              pl.BlockSpec((skip.shape[0], LANES), lambda c, b: (0, c)),
              pl.BlockSpec((None, None, srows, scols), lambda c, b: (order, c, 0, 0))]
    ci, cs = _const_specs(consts, 2)
    ins += ci
    specs += cs
    return pl.pallas_call(
        functools.partial(_hy_conv_body, plan=plan, order=order, conv_u=conv_u),
        grid=(ncb, batch),
        in_specs=specs,
        out_specs=pl.BlockSpec((L, LANES), lambda c, b: (b, c)),
        out_shape=jax.ShapeDtypeStruct((batch * L, d_model), out_dtype),
        scratch_shapes=[pltpu.VMEM((L, LANES), F32)] + _fft_scratch(plan),
        compiler_params=_cparams(("parallel", "arbitrary")),
        name=f"hyena_conv{order}_L{L}",
    )(*ins)


def hyena_mixer(hmix, p, *, batch, seq, ctx_len, d_model, want_ctx):
    n_order = p["skip"].shape[0]
    pr = mm(hmix, p["w_in"], bias=p["b_in"], name="hy_in", rows=(batch * seq + (batch * ctx_len if want_ctx else 0)))
    conv_b = p["conv_b"].reshape(1, -1)
    ncb = d_model // LANES
    outs = []
    streams = [(seq, 0)]
    if want_ctx:
        streams.append((ctx_len, (batch * seq) // ctx_len))
    for L, rb0 in streams:
        plan = _FftPlan(L)
        consts = plan.constants()
        hidden = hyena_hidden(L, p["f_w1"], p["f_b1"], p["f_w2"], p["f_b2"], p["sin_freq"])
        spectra = hyena_filter_spectra(plan, consts, hidden, p["f_w3"], d_model, n_order)
        z1 = hyena_conv(plan, consts, pr, 0, pr, ncb, p["conv_w"], conv_b, p["skip"], spectra,
                        order=0, conv_u=True, u_row_block0=rb0, gate_row_block0=rb0, batch=batch, d_model=d_model,
                        out_dtype=F32)
        z2 = hyena_conv(plan, consts, z1, 0, pr, 2 * ncb, p["conv_w"], conv_b, p["skip"], spectra,
                        order=1, conv_u=False, u_row_block0=0, gate_row_block0=rb0, batch=batch, d_model=d_model,
                        out_dtype=BF16)
        outs.append(z2)
    return outs[0] if len(outs) == 1 else jnp.concatenate(outs, 0)


def _headsum(x, e_ref):
    hi, lo = _split2(x)
    return _dot(hi, e_ref[...]) + _dot(lo, e_ref[...])


def _rw_mix_body(h_ref, hp_ref, hn_ref, mix_ref, o_ref, *, tm, lat_tiles, tiles_per_seq):
    i = pl.program_id(0)
    is_ctx = i >= lat_tiles
    first = jnp.logical_or(is_ctx, i % tiles_per_seq == 0)
    last = jnp.logical_or(is_ctx, i % tiles_per_seq == tiles_per_seq - 1)
    h = h_ref[...]
    rows = lax.broadcasted_iota(jnp.int32, h.shape, 0)
    pedge = jnp.where(first, 0.0, hp_ref[SUBLANES - 1:SUBLANES, :])
    nedge = jnp.where(last, 0.0, hn_ref[0:1, :])
    prev = jnp.where(rows == 0, pedge, pltpu.roll(h, 1, 0))
    nxt = jnp.where(rows == tm - 1, nedge, pltpu.roll(h, tm - 1, 0))
    dp = prev - h
    dn = nxt - h
    for n in range(o_ref.shape[0]):
        o_ref[n] = (h + dp * mix_ref[0, n:n + 1, :] + dn * mix_ref[1, n:n + 1, :]).astype(o_ref.dtype)


def rwkv_mix(h, mix, *, batch, seq, ctx_len):
    m, d = h.shape
    tm = ctx_len
    n_shift = mix.shape[1]
    nblk = m // SUBLANES
    per = tm // SUBLANES
    body = functools.partial(_rw_mix_body, tm=tm, lat_tiles=(batch * seq) // tm, tiles_per_seq=seq // tm)
    return pl.pallas_call(
        body,
        grid=(m // tm,),
        in_specs=[
            pl.BlockSpec((tm, d), lambda i: (i, 0)),
            pl.BlockSpec((SUBLANES, d), lambda i: (jnp.maximum(i * per - 1, 0), 0)),
            pl.BlockSpec((SUBLANES, d), lambda i: (jnp.minimum((i + 1) * per, nblk - 1), 0)),
            pl.BlockSpec((2, n_shift, d), lambda i: (0, 0, 0)),
        ],
        out_specs=pl.BlockSpec((n_shift, tm, d), lambda i: (0, i, 0)),
        out_shape=jax.ShapeDtypeStruct((n_shift, m, d), BF16),
        compiler_params=_cparams(("parallel",)),
        name="rwkv_mix",
    )(h, h, h, mix)


def _rw_prep_body(k_ref, lw0_ref, lw1_ref, al0_ref, al1_ref, w0_ref, a0_ref, kk_ref, ka_ref, e_ref,
                  olw0, olw1, okd0, okd1, obb0, obb1, oaa):
    k = k_ref[...]
    kk0 = k * kk_ref[...]
    kk = kk0 * lax.rsqrt(_headsum(kk0 * kk0, e_ref) + 1e-12)
    oaa[...] = -kk
    for d, (lw_ref, al_ref, olw, okd, obb) in enumerate(((lw0_ref, al0_ref, olw0, okd0, obb0),
                                                         (lw1_ref, al1_ref, olw1, okd1, obb1))):
        z = -(w0_ref[d:d + 1, :] + lw_ref[...])
        softplus = jnp.maximum(z, 0.0) + jnp.log(1.0 + jnp.exp(-jnp.abs(z)))
        log_w = -softplus - 0.5
        olw[...] = -jnp.exp(log_w)
        a = jax.nn.sigmoid(a0_ref[d:d + 1, :] + al_ref[...])
        okd[...] = k * (1.0 + (a - 1.0) * ka_ref[...])
        obb[...] = kk * a


def rwkv_prep(k, lw0, lw1, al0, al1, w0, a0, k_k, k_a, e128, tm=1088):
    m, d = k.shape
    blk = pl.BlockSpec((tm, LANES), lambda i, c: (i, c))
    vec2 = pl.BlockSpec((2, LANES), lambda i, c: (0, c))
    vec1 = pl.BlockSpec((1, LANES), lambda i, c: (0, c))
    sh = jax.ShapeDtypeStruct((m, d), F32)
    return pl.pallas_call(
        _rw_prep_body,
        grid=(m // tm, d // LANES),
        in_specs=[blk] * 5 + [vec2, vec2, vec1, vec1, pl.BlockSpec((LANES, LANES), lambda i, c: (0, 0))],
        out_specs=[blk] * 7,
        out_shape=[sh] * 7,
        compiler_params=_cparams(("parallel", "arbitrary")),
        name="rwkv_prep",
    )(k, lw0, lw1, al0, al1, w0, a0, k_k.reshape(1, d), k_a.reshape(1, d), e128)


def _b16(x):
    return x.astype(BF16)


def _wkv_body(r_ref, lw_ref, k_ref, v_ref, a_ref, b_ref, o_ref, s_sc, *, chunk, n_heads, reverse):
    c = pl.program_id(2)

    @pl.when(c == 0)
    def _():
        s_sc[...] = jnp.zeros_like(s_sc)

    C = chunk
    hs = range(n_heads)
    row = lax.broadcasted_iota(jnp.int32, (C, C), 0)
    col = lax.broadcasted_iota(jnp.int32, (C, C), 1)
    if reverse:
        tri_incl = col >= row
        tri_strict = col > row
        last = 0
    else:
        tri_incl = col <= row
        tri_strict = col < row
        last = C - 1
    same = lambda s: (row >> int(math.log2(s))) == (col >> int(math.log2(s)))
    eye = (row == col).astype(F32)
    cum_m = tri_incl.astype(BF16)
    sls = [slice(h * RW_HEAD, (h + 1) * RW_HEAD) for h in hs]

    lw = [lw_ref[:, sl] for sl in sls]
    cum = []
    for h in hs:
        l1, l2, l3 = _split3(lw[h])
        cum.append(_dot(cum_m, l1) + _dot(cum_m, l2) + _dot(cum_m, l3))
    p_inc = [jnp.exp(cum[h]) for h in hs]
    x1 = [jnp.concatenate([a_ref[:, sls[h]] * jnp.exp(cum[h] - lw[h]), r_ref[:, sls[h]] * p_inc[h]], axis=0) for h in hs]
    x2 = []
    for h in hs:
        p_inv = jnp.exp(-cum[h])
        x2.append(jnp.concatenate([b_ref[:, sls[h]] * p_inv, k_ref[:, sls[h]] * p_inv], axis=0))
    v = [v_ref[:, sl] for sl in sls]
    s0 = [s_sc[h] for h in hs]
    mfull = [_dot3(x1[h], x2[h], _dot_nt) for h in hs]
    w0 = [_dot3(x1[h], s0[h], _dot_nt) for h in hs]
    n_ab = [jnp.where(tri_strict, mfull[h][:C, :C], 0.0) for h in hs]
    a_ak = [_b16(jnp.where(tri_strict, mfull[h][:C, C:], 0.0)) for h in hs]
    a_r = [_b16(jnp.concatenate([jnp.where(tri_incl, mfull[h][C:, :C], 0.0),
                                 jnp.where(tri_incl, mfull[h][C:, C:], 0.0)], axis=1)) for h in hs]
    rhs = [w0[h][:C] + _dot(a_ak[h], _b16(v[h])) for h in hs]

    n4 = [jnp.where(same(4), n_ab[h], 0.0) for h in hs]
    n4b = [_b16(n4[h]) for h in hs]
    sq = [_dot(n4b[h], n4b[h]) for h in hs]
    t = [(eye + n4[h]) + _dot(_b16(eye + n4[h]), _b16(sq[h])) for h in hs]
    s = 4
    while s < C:
        off = jnp.logical_and(jnp.logical_and(same(2 * s), jnp.logical_not(same(s))), tri_strict)
        tb = [_b16(t[h]) for h in hs]
        tmp = [_dot(tb[h], _b16(jnp.where(off, n_ab[h], 0.0))) for h in hs]
        t = [t[h] + _dot(_b16(tmp[h]), tb[h]) for h in hs]
        s *= 2
    tb = [_b16(t[h]) for h in hs]
    u = [_dot(tb[h], _b16(rhs[h])) for h in hs]
    res = [rhs[h] - u[h] + _dot3(n_ab[h], u[h]) for h in hs]
    u = [u[h] + _dot(tb[h], _b16(res[h])) for h in hs]
    uv = [jnp.concatenate([u[h], v[h]], axis=0) for h in hs]
    y = [w0[h][C:] + _dot(a_r[h], _b16(uv[h])) for h in hs]
    for h in hs:
        s_sc[h] = (s0[h] + _dot3(uv[h], x2[h], _dot_tn)) * p_inc[h][last:last + 1, :]
    o_ref[...] = jnp.concatenate(y, axis=1)


def wkv7(r, lw, kd, v, aa, bb, *, reverse, batch, seq, ctx_len, chunk=128, heads_per_step=16):
    m, d = r.shape
    heads_per_step = min(heads_per_step, d // RW_HEAD)
    wl = heads_per_step * RW_HEAD
    n_ctx = ctx_len // chunk
    n_lat = seq // chunk
    n_chunks = n_ctx + n_lat
    ctx0 = (batch * seq) // chunk

    def row_block(b, c):
        if reverse:
            return jnp.where(c < n_ctx, ctx0 + b * n_ctx + (n_ctx - 1 - c), b * n_lat + (n_lat - 1 - (c - n_ctx)))
        return jnp.where(c < n_ctx, ctx0 + b * n_ctx + c, b * n_lat + (c - n_ctx))

    blk = pl.BlockSpec((chunk, wl), lambda b, g, c: (row_block(b, c), g))
    body = functools.partial(_wkv_body, chunk=chunk, n_heads=heads_per_step, reverse=reverse)
    return pl.pallas_call(
        body,
        grid=(batch, d // wl, n_chunks),
        in_specs=[blk] * 6,
        out_specs=blk,
        out_shape=jax.ShapeDtypeStruct((m, d), F32),
        scratch_shapes=[pltpu.VMEM((heads_per_step, RW_HEAD, RW_HEAD), F32)],
        compiler_params=_cparams(("parallel", "parallel", "arbitrary")),
        name="wkv7_rev" if reverse else "wkv7_fwd",
    )(r, lw, kd, v, aa, bb)


def _rw_readout_body(y0_ref, y1_ref, r_ref, v_ref, kd0_ref, kd1_ref, g_ref, rk_ref, lnw_ref, lnb_ref, e_ref, o_ref,
                     *, gn_eps):
    y = y0_ref[...] + y1_ref[...]
    inv = 1.0 / RW_HEAD
    mu = _headsum(y, e_ref) * inv
    dlt = y - mu
    var = _headsum(dlt * dlt, e_ref) * inv
    yn = dlt * lax.rsqrt(var + gn_eps) * lnw_ref[...] + lnb_ref[...]
    rk = r_ref[...] * rk_ref[...]
    bonus = (_headsum(rk * kd0_ref[...], e_ref) + _headsum(rk * kd1_ref[...], e_ref)) * v_ref[...]
    o_ref[...] = ((yn + bonus) * g_ref[...]).astype(o_ref.dtype)


def rwkv_readout(y0, y1, r, v, kd0, kd1, g, r_k, ln_w, ln_b, e128, gn_eps, rows, tm=512):
    d = r.shape[1]
    blk = pl.BlockSpec((tm, LANES), lambda i, c: (i, c))
    vec1 = pl.BlockSpec((1, LANES), lambda i, c: (0, c))
    return pl.pallas_call(
        functools.partial(_rw_readout_body, gn_eps=gn_eps),
        grid=(rows // tm, d // LANES),
        in_specs=[blk] * 7 + [vec1] * 3 + [pl.BlockSpec((LANES, LANES), lambda i, c: (0, 0))],
        out_specs=blk,
        out_shape=jax.ShapeDtypeStruct((rows, d), BF16),
        compiler_params=_cparams(("parallel", "arbitrary")),
        name="rwkv_readout",
    )(y0, y1, r, v, kd0, kd1, g, r_k.reshape(1, d), ln_w.reshape(1, d), ln_b.reshape(1, d), e128)


def rwkv_mixer(hmix, p, *, batch, seq, ctx_len, d_model):
    xs = rwkv_mix(hmix, p["mix"], batch=batch, seq=seq, ctx_len=ctx_len)
    m = hmix.shape[0]
    xs2 = xs.reshape(xs.shape[0] * m, d_model)
    tm = 1088 if m % 1088 == 0 else 512

    def proj(n, w, **kw):
        return mm(xs2, w, tm=tm, rows=m, x_row_block0=n * (m // tm), **kw)

    r = proj(0, p["w_r"], name="rw_r")
    k = proj(2, p["w_k"], name="rw_k")
    v = proj(3, p["w_v"], name="rw_v")
    g = mm(proj(5, p["g1"], act="sigmoid", out_dtype=BF16, name="rw_g1"), p["g2"], tm=tm, name="rw_g2")
    lws, als = [], []
    for d in range(2):
        tw = proj(1, p["w1"][d], act="tanh", out_dtype=BF16, name="rw_w1")
        lws.append(mm(tw, p["w2"][d], tm=tm, name="rw_w2"))
        ah = proj(4, p["a1"][d], out_dtype=BF16, name="rw_a1")
        als.append(mm(ah, p["a2"][d], tm=tm, name="rw_a2"))
    e128 = jnp.asarray(np.kron(np.eye(LANES // RW_HEAD), np.ones((RW_HEAD, RW_HEAD))), BF16)
    lw0, lw1, kd0, kd1, bb0, bb1, aa = rwkv_prep(k, lws[0], lws[1], als[0], als[1], p["w0"], p["a0"],
                                                 p["k_k"], p["k_a"], e128, tm=tm)
    kw = dict(batch=batch, seq=seq, ctx_len=ctx_len)
    y0 = wkv7(r, lw0, kd0, v, aa, bb0, reverse=False, **kw)
    y1 = wkv7(r, lw1, kd1, v, aa, bb1, reverse=True, **kw)
    return rwkv_readout(y0, y1, r, v, kd0, kd1, g, p["r_k"], p["ln_w"], p["ln_b"], e128,
                        1e-5 * RW_HEAD, rows=m)


def mla_mixer(hmix, p, *, batch, seq, ctx_len, want_ctx):
    m = hmix.shape[0]
    tm = 1088 if m % 1088 == 0 else 512
    scale = (MLA_NOPE + MLA_ROPE) ** -0.5
    down = mm(hmix, p["w_down_ext"], tm=tm, name="mla_down")
    kq = MLA_Q_RANK
    q_raw = mm(down, p["w_uq_ext"], norm_g=p["q_norm_g"], x_col_block=0, tm=tm, name="mla_uq")
    kv = mm(down, p["w_ukv"], norm_g=p["kv_norm_g"], x_col_block=1, tm=tm, out_dtype=BF16, name="mla_ukv")
    cos, sin = _rope_tables(seq, ctx_len, batch, MLA_ROPE)
    zeros = jnp.zeros_like(cos)
    cos_p = jnp.concatenate([cos, zeros], -1)
    sin_p = jnp.concatenate([sin, zeros], -1)
    q = mla_rope(q_raw, 0, MLA_HEADS, cos_p, sin_p, scale, True, tm=tm)
    kcat = mla_kcat(kv, down, (kq + MLA_KV_RANK) // LANES, MLA_HEADS, cos_p, sin_p, tm=tm)
    kw = dict(dqk=2 * LANES, kv_group=1, n_heads=MLA_HEADS, batch=batch, seq=seq, ctx_len=ctx_len)
    o_lat = attention(q, kcat, kv, lambda g: 2 * g + 1, lat_queries=True, **kw)
    if not want_ctx:
        return o_lat
    o_ctx = attention(q, kcat, kv, lambda g: 2 * g + 1, lat_queries=False, **kw)
    return jnp.concatenate([o_lat, o_ctx], 0)


def gqa_mixer(hmix, p, *, batch, seq, ctx_len, want_ctx):
    m = hmix.shape[0]
    tm = 1088 if m % 1088 == 0 else 512
    scale = GQA_HEAD ** -0.5
    raw = mm(hmix, p["w_qkv_ext"], tm=tm, name="gqa_qkv")
    cos, sin = _rope_tables(seq, ctx_len, batch, GQA_HEAD)
    nq, nk = GQA_HEADS, GQA_KV_HEADS
    q = gqa_prep(raw, 0, nq, nq, cos, sin, p["q_norm_g"], p["q_norm_g_rot"], scale, tm=tm)
    k = gqa_prep(raw, 2 * nq, 2 * nq + nk, nk, cos, sin, p["k_norm_g"], p["k_norm_g_rot"], 1.0, tm=tm)
    vcb0 = 2 * nq + 2 * nk
    vb = mm_cast(raw, vcb0, nk, tm=tm)
    grp = nq // nk
    kw = dict(dqk=LANES, kv_group=grp, n_heads=nq, batch=batch, seq=seq, ctx_len=ctx_len)
    o_lat = attention(q, k, vb, lambda g: g, lat_queries=True, **kw)
    if not want_ctx:
        return o_lat
    o_ctx = attention(q, k, vb, lambda g: g, lat_queries=False, **kw)
    return jnp.concatenate([o_lat, o_ctx], 0)


def _cast_body(x_ref, o_ref):
    o_ref[...] = x_ref[...].astype(o_ref.dtype)


def mm_cast(raw, cb0, nblocks, tm=512):
    m = raw.shape[0]
    return pl.pallas_call(
        _cast_body,
        grid=(m // tm, nblocks),
        in_specs=[pl.BlockSpec((tm, LANES), lambda i, c: (i, cb0 + c))],
        out_specs=pl.BlockSpec((tm, LANES), lambda i, c: (i, c)),
        out_shape=jax.ShapeDtypeStruct((m, nblocks * LANES), BF16),
        compiler_params=_cparams(("parallel", "arbitrary")),
        name="cast_cols",
    )(raw)


def _rot_gain(g):
    q = g.shape[-1] // 4
    g4 = g.reshape(2, 2, q)
    return jnp.concatenate([g4[:, 1:2], g4[:, 0:1]], axis=1).reshape(g.shape)


def kernel(x, c, ctx, c_ctx, ada_w, ada_b, norm_g, ffn_w_gu, ffn_w_down, hy_w_in, hy_b_in, hy_conv_w, hy_conv_b, hy_f_w1, hy_f_b1, hy_f_w2, hy_f_b2, hy_f_w3, hy_sin_freq, hy_skip, hy_w_out, hy_b_out, rw_mix, rw_w_rkv, rw_w0, rw_w1, rw_w2, rw_a0, rw_a1, rw_a2, rw_g1, rw_g2, rw_k_k, rw_k_a, rw_r_k, rw_ln_w, rw_ln_b, rw_w_o, mla_w_down, mla_q_norm_g, mla_w_uq, mla_kv_norm_g, mla_w_ukv, mla_w_o, gqa_w_qkv, gqa_q_norm_g, gqa_k_norm_g, gqa_w_o, final_g):
    batch, seq, d = x.shape
    ctx_len = ctx.shape[1]
    depth = ada_w.shape[0]
    n_mix = 4
    rows_lat = batch * seq
    rows_all = rows_lat + batch * ctx_len
    bf = lambda a: a.astype(BF16)
    w_gu_all, w_down_all = bf(ffn_w_gu), bf(ffn_w_down)

    cvec = jnp.concatenate([c, c_ctx[None], jnp.zeros((8 - batch - 1, d), F32)], 0)
    mods_all = adaln_all(cvec, ada_w, ada_b)[:, :batch + 1].reshape(depth, batch + 1, N_MOD, d)

    h = jnp.concatenate([x.reshape(rows_lat, d), ctx.reshape(batch * ctx_len, d)], 0)

    for i in range(depth):
        mi, j = i % n_mix, i // n_mix
        ctx_out = any(jj % n_mix != 0 for jj in range(i + 1, depth))
        mods = mods_all[i]
        gate5 = mods[:, 5:6, :]
        mix_dtype = F32 if mi == 1 else BF16
        h, hmix = ffn_half(h, mods, norm_g[i], w_gu_all, w_down_all, layer=i, which=0, s=0, rows=rows_all,
                           seq_rows=seq, next_g=norm_g[i, 1], next_s=1, next_dtype=mix_dtype)
        rows_out = rows_all if ctx_out else rows_lat
        kw = dict(batch=batch, seq=seq, ctx_len=ctx_len)
        if mi == 0:
            p = dict(w_in=bf(hy_w_in[j]), b_in=hy_b_in[j], conv_w=hy_conv_w[j], conv_b=hy_conv_b[j],
                     f_w1=hy_f_w1[j], f_b1=hy_f_b1[j], f_w2=hy_f_w2[j], f_b2=hy_f_b2[j], f_w3=hy_f_w3[j],
                     sin_freq=hy_sin_freq[j], skip=hy_skip[j])
            z = hyena_mixer(hmix, p, d_model=d, want_ctx=ctx_out, **kw)
            w_o, b_o = bf(hy_w_out[j]), hy_b_out[j]
        elif mi == 1:
            pad = lambda w, ax: jnp.pad(w, [(0, RW_LORA_PAD - w.shape[a]) if a == ax else (0, 0) for a in range(w.ndim)])
            p = dict(mix=rw_mix[j], w_r=bf(rw_w_rkv[j, 0]), w_k=bf(rw_w_rkv[j, 1]), w_v=bf(rw_w_rkv[j, 2]),
                     g1=bf(rw_g1[j]), g2=bf(rw_g2[j]),
                     w1=bf(pad(rw_w1[j], 2)), w2=bf(pad(rw_w2[j], 1)), a1=bf(pad(rw_a1[j], 2)), a2=bf(pad(rw_a2[j], 1)),
                     w0=rw_w0[j], a0=rw_a0[j], k_k=rw_k_k[j], k_a=rw_k_a[j], r_k=rw_r_k[j].reshape(-1),
                     ln_w=rw_ln_w[j], ln_b=rw_ln_b[j])
            z = rwkv_mixer(hmix, p, d_model=d, **kw)
            w_o, b_o = bf(rw_w_o[j]), None
        elif mi == 2:
            wd = mla_w_down[j]
            pe = wd[:, MLA_Q_RANK + MLA_KV_RANK:]
            wuq = mla_w_uq[j].reshape(MLA_Q_RANK, MLA_HEADS, MLA_NOPE + MLA_ROPE)
            wq_pe = wuq[..., MLA_NOPE:]
            wuq_ext = jnp.concatenate([wuq[..., :MLA_NOPE], wq_pe,
                                       _rot_cols(wq_pe.reshape(MLA_Q_RANK, -1), MLA_ROPE).reshape(wq_pe.shape)], -1)
            p = dict(w_down_ext=bf(jnp.concatenate([wd, _rot_cols(pe, MLA_ROPE)], 1)),
                     w_uq_ext=bf(wuq_ext.reshape(MLA_Q_RANK, -1)), w_ukv=bf(mla_w_ukv[j]),
                     q_norm_g=mla_q_norm_g[j], kv_norm_g=mla_kv_norm_g[j])
            z = mla_mixer(hmix, p, want_ctx=ctx_out, **kw)
            w_o, b_o = bf(mla_w_o[j]), None
        else:
            w = gqa_w_qkv[j]
            nqc = GQA_HEADS * GQA_HEAD
            nkc = GQA_KV_HEADS * GQA_HEAD
            wq, wk, wv = w[:, :nqc], w[:, nqc:nqc + nkc], w[:, nqc + nkc:]
            p = dict(w_qkv_ext=bf(jnp.concatenate([wq, _rot_cols(wq, GQA_HEAD), wk, _rot_cols(wk, GQA_HEAD), wv], 1)),
                     q_norm_g=gqa_q_norm_g[j], q_norm_g_rot=_rot_gain(gqa_q_norm_g[j]),
                     k_norm_g=gqa_k_norm_g[j], k_norm_g_rot=_rot_gain(gqa_k_norm_g[j]))
            z = gqa_mixer(hmix, p, want_ctx=ctx_out, **kw)
            w_o, b_o = bf(gqa_w_o[j]), None
        h = mm(z, w_o, bias=b_o, res=h, gate=gate5, seq_rows=seq, rows=rows_out, tn=d, name="mix_out")
        last = i == depth - 1
        if last:
            h, out = ffn_half(h, mods, norm_g[i], w_gu_all, w_down_all, layer=i, which=1, s=2, rows=rows_out,
                              seq_rows=seq, next_g=final_g, next_s=None, next_dtype=F32)
        else:
            h = ffn_half(h, mods, norm_g[i], w_gu_all, w_down_all, layer=i, which=1, s=2, rows=rows_out,
                         seq_rows=seq)
    return out[:rows_lat].reshape(batch, seq, d)
```

```python
import functools
import math

import numpy as np
import jax
import jax.numpy as jnp
from jax import lax
from jax.experimental import pallas as pl
from jax.experimental.pallas import tpu as pltpu

F32 = jnp.float32
BF16 = jnp.bfloat16

VMEM_CAP_BYTES = 56 * 1024 * 1024
LANES = 128
SUBLANES = 8

NORM_EPS = 1e-6
LOG2E = 1.4426950408889634
N_MOD = 9
GRID_W = 64
ROPE_THETA = 10000.0
HY_DECAY_TARGET = 1e-2
HY_DECAY_PCT_SHORT = 0.3
HY_DECAY_PCT_LONG = 1.5
RW_HEAD = 64
RW_LORA_PAD = 128
GQA_HEAD = 128
GQA_HEADS = 16
GQA_KV_HEADS = 8
MLA_HEADS = 16
MLA_NOPE = 128
MLA_ROPE = 64
MLA_V = 128
MLA_Q_RANK = 512
MLA_KV_RANK = 512


def _cparams(sem):
    return pltpu.CompilerParams(dimension_semantics=sem, vmem_limit_bytes=VMEM_CAP_BYTES)


def _dot(a, b):
    return jnp.dot(a, b, preferred_element_type=F32)


def _dot_nt(a, b):
    return lax.dot_general(a, b, (((1,), (1,)), ((), ())), preferred_element_type=F32)


def _dot_tn(a, b):
    return lax.dot_general(a, b, (((0,), (0,)), ((), ())), preferred_element_type=F32)


def _split2(x):
    hi = x.astype(BF16)
    lo = (x - hi.astype(F32)).astype(BF16)
    return hi, lo


def _split3(x):
    hi = x.astype(BF16)
    r1 = x - hi.astype(F32)
    mid = r1.astype(BF16)
    lo = (r1 - mid.astype(F32)).astype(BF16)
    return hi, mid, lo


def _dot3(a, b, f=_dot):
    ah, al = _split2(a)
    bh, bl = _split2(b)
    return f(ah, bh) + f(ah, bl) + f(al, bh)


def _dotc3(mh, ml, d):
    dh, dl = _split2(d)
    return _dot(mh, dh) + _dot(ml, dh) + _dot(mh, dl)


def _dotc2(mh, ml, d):
    dh = d.astype(BF16)
    return _dot(mh, dh) + _dot(ml, dh)


def _rms(x, g):
    return x * lax.rsqrt(jnp.mean(x * x, -1, keepdims=True) + NORM_EPS) * g


def _pick(n, cands):
    for c in cands:
        if n % c == 0:
            return c
    raise ValueError(f"no tile for {n}")


def _adaln_body(c_ref, w_ref, b_ref, o_ref):
    c = c_ref[...]
    s = c * jax.nn.sigmoid(c)
    o_ref[0] = _dot(s.astype(BF16), w_ref[0].astype(BF16)) + b_ref[0]


def adaln_all(cvec8, ada_w, ada_b):
    depth, d, n = ada_w.shape
    tn = _pick(n, (2048, 1024, 512, 256, 128))
    return pl.pallas_call(
        _adaln_body,
        grid=(depth, n // tn),
        in_specs=[
            pl.BlockSpec((8, d), lambda l, j: (0, 0)),
            pl.BlockSpec((1, d, tn), lambda l, j: (l, 0, j)),
            pl.BlockSpec((1, 1, tn), lambda l, j: (l, 0, j)),
        ],
        out_specs=pl.BlockSpec((1, 8, tn), lambda l, j: (l, 0, j)),
        out_shape=jax.ShapeDtypeStruct((depth, 8, n), F32),
        compiler_params=_cparams(("arbitrary", "arbitrary")),
        name="adaln",
    )(cvec8, ada_w, ada_b.reshape(depth, 1, n))


def _mm_body(*refs, has_bias, act, has_norm, has_res):
    it = iter(refs)
    x_ref = next(it)
    w_ref = next(it)
    b_ref = next(it) if has_bias else None
    g_ref = next(it) if has_norm else None
    r_ref = next(it) if has_res else None
    gate_ref = next(it) if has_res else None
    o_ref = next(it)
    x = x_ref[...]
    if has_norm:
        x = _rms(x.astype(F32), g_ref[...])
    acc = _dot(x.astype(BF16), w_ref[...])
    if has_bias:
        acc = acc + b_ref[...]
    if act == "sigmoid":
        acc = jax.nn.sigmoid(acc)
    elif act == "tanh":
        acc = jnp.tanh(acc)
    if has_res:
        acc = r_ref[...] + gate_ref[0] * acc
    o_ref[...] = acc.astype(o_ref.dtype)


def mm(x, w, *, bias=None, act=None, norm_g=None, res=None, gate=None, out_dtype=F32,
       tm=512, tn=None, rows=None, x_col_block=0, x_row_block0=0, seq_rows=None, name="mm"):
    k, n = w.shape
    m = rows if rows is not None else x.shape[0]
    assert m % tm == 0, (m, tm)
    tn = tn or _pick(n, (512, 384, 256, 128))
    has_res = res is not None
    ins = [x, w]
    specs = [pl.BlockSpec((tm, k), lambda i, j: (i + x_row_block0, x_col_block)),
             pl.BlockSpec((k, tn), lambda i, j: (0, j))]
    if bias is not None:
        ins.append(bias.reshape(1, n).astype(F32))
        specs.append(pl.BlockSpec((1, tn), lambda i, j: (0, j)))
    if norm_g is not None:
        ins.append(norm_g.reshape(1, k).astype(F32))
        specs.append(pl.BlockSpec((1, k), lambda i, j: (0, 0)))
    if has_res:
        ngroups = gate.shape[0]
        per = seq_rows // tm
        ins += [res, gate]
        specs += [pl.BlockSpec((tm, tn), lambda i, j: (i, j)),
                  pl.BlockSpec((1, 1, tn), lambda i, j: (jnp.minimum(i // per, ngroups - 1), 0, j))]
    body = functools.partial(_mm_body, has_bias=bias is not None, act=act,
                             has_norm=norm_g is not None, has_res=has_res)
    return pl.pallas_call(
        body,
        grid=(m // tm, n // tn),
        in_specs=specs,
        out_specs=pl.BlockSpec((tm, tn), lambda i, j: (i, j)),
        out_shape=jax.ShapeDtypeStruct((m, n), out_dtype),
        compiler_params=_cparams(("parallel", "arbitrary")),
        name=name,
    )(*ins)


def _ffn_body(*refs, s, n_f, has_next, next_s):
    it = iter(refs)
    x_ref, mod_ref, g_ref, wa_ref, wb_ref, wd_ref = (next(it) for _ in range(6))
    gn_ref = next(it) if has_next else None
    o_ref = next(it)
    hn_ref = next(it) if has_next else None
    xn_sc = next(it)
    acc_sc = next(it)
    j = pl.program_id(1)

    @pl.when(j == 0)
    def _():
        n = _rms(x_ref[...], g_ref[s:s + 1, :])
        n = n * (1.0 + mod_ref[0, 3 * s + 1:3 * s + 2, :]) + mod_ref[0, 3 * s:3 * s + 1, :]
        xn_sc[...] = n.astype(BF16)
        acc_sc[...] = jnp.zeros_like(acc_sc)

    xn = xn_sc[...]
    a = _dot(xn, wa_ref[...])
    b = _dot(xn, wb_ref[...])
    h = (a * jax.nn.sigmoid(a)) * b
    acc_sc[...] += _dot(h.astype(BF16), wd_ref[...])

    @pl.when(j == n_f - 1)
    def _():
        xnew = x_ref[...] + 0.5 * mod_ref[0, 3 * s + 2:3 * s + 3, :] * acc_sc[...]
        o_ref[...] = xnew
        if has_next:
            hn = _rms(xnew, gn_ref[...])
            if next_s is not None:
                hn = hn * (1.0 + mod_ref[0, 3 * next_s + 1:3 * next_s + 2, :]) + mod_ref[0, 3 * next_s:3 * next_s + 1, :]
            hn_ref[...] = hn.astype(hn_ref.dtype)


def ffn_half(x, mods, norm_g3, w_gu, w_down, *, layer, which, s, rows, seq_rows, next_g=None, next_s=None,
             next_dtype=F32, tm=512, fc=512):
    d = x.shape[1]
    d_ff = w_down.shape[2]
    n_f = d_ff // fc
    assert d_ff % fc == 0 and rows % tm == 0
    per = seq_rows // tm
    ngroups = mods.shape[0]
    has_next = next_g is not None
    ins = [x, mods, norm_g3, w_gu, w_gu, w_down]
    specs = [
        pl.BlockSpec((tm, d), lambda i, j: (i, 0)),
        pl.BlockSpec((1, N_MOD, d), lambda i, j: (jnp.minimum(i // per, ngroups - 1), 0, 0)),
        pl.BlockSpec((3, d), lambda i, j: (0, 0)),
        pl.BlockSpec((None, None, d, fc), lambda i, j: (layer, which, 0, j)),
        pl.BlockSpec((None, None, d, fc), lambda i, j: (layer, which, 0, n_f + j)),
        pl.BlockSpec((None, None, fc, d), lambda i, j: (layer, which, j, 0)),
    ]
    out_shape = [jax.ShapeDtypeStruct((rows, d), F32)]
    out_specs = [pl.BlockSpec((tm, d), lambda i, j: (i, 0))]
    if has_next:
        ins.append(next_g.reshape(1, d))
        specs.append(pl.BlockSpec((1, d), lambda i, j: (0, 0)))
        out_shape.append(jax.ShapeDtypeStruct((rows, d), next_dtype))
        out_specs.append(pl.BlockSpec((tm, d), lambda i, j: (i, 0)))
    body = functools.partial(_ffn_body, s=s, n_f=n_f, has_next=has_next, next_s=next_s)
    outs = pl.pallas_call(
        body,
        grid=(rows // tm, n_f),
        in_specs=specs,
        out_specs=out_specs,
        out_shape=out_shape,
        scratch_shapes=[pltpu.VMEM((tm, d), BF16), pltpu.VMEM((tm, d), F32)],
        compiler_params=_cparams(("parallel", "arbitrary")),
        name="ffn_half",
    )(*ins)
    return outs if has_next else outs[0]


def _attn_body(*refs, use_lat, hp, n_kv, dqk):
    it = iter(refs)
    q_ref = next(it)
    kl_ref = next(it) if use_lat else None
    kc_ref = next(it)
    vl_refs = [next(it) for _ in range(n_kv)] if use_lat else None
    vc_refs = [next(it) for _ in range(n_kv)]
    o_ref = next(it)
    hs = range(hp)
    slot = [h * n_kv // hp for h in hs]
    q = [q_ref[:, h * dqk:(h + 1) * dqk] for h in hs]
    s_c = [_dot_nt(kc_ref[:, slot[h] * dqk:(slot[h] + 1) * dqk], q[h]) for h in hs]
    m = [jnp.max(s_c[h], 0, keepdims=True) for h in hs]
    if use_lat:
        s_l = [_dot_nt(kl_ref[:, slot[h] * dqk:(slot[h] + 1) * dqk], q[h]) for h in hs]
        m = [jnp.maximum(m[h], jnp.max(s_l[h], 0, keepdims=True)) for h in hs]
    p_c = [jnp.exp2(s_c[h] - m[h]) for h in hs]
    den = [jnp.sum(p_c[h], 0, keepdims=True) for h in hs]
    o = [_dot_tn(vc_refs[slot[h]][...], p_c[h].astype(BF16)) for h in hs]
    if use_lat:
        p_l = [jnp.exp2(s_l[h] - m[h]) for h in hs]
        den = [den[h] + jnp.sum(p_l[h], 0, keepdims=True) for h in hs]
        o = [o[h] + _dot_tn(vl_refs[slot[h]][...], p_l[h].astype(BF16)) for h in hs]
    for h in hs:
        o_ref[:, h * LANES:(h + 1) * LANES] = (o[h] / den[h]).T.astype(o_ref.dtype)


def attention(q_arr, k_arr, v_arr, v_cb, *, dqk, kv_group, n_heads, batch, seq, ctx_len, lat_queries, tq=512, hp=1):
    assert n_heads % hp == 0 and (kv_group % hp == 0 or hp % kv_group == 0)
    n_kv = max(hp // kv_group, 1)
    kv0 = lambda p: (p * hp) // kv_group
    ctx0 = (batch * seq) // ctx_len
    if lat_queries:
        tq = min(tq, seq)
        nq = seq // tq
        q_row = lambda b, t: b * nq + t
        o_row = lambda b, t: b * nq + t
        rows_q = batch * seq
    else:
        tq = ctx_len
        nq = 1
        q_row = lambda b, t: ctx0 + b
        o_row = lambda b, t: b
        rows_q = batch * ctx_len
    ins = [q_arr]
    specs = [pl.BlockSpec((tq, hp * dqk), lambda b, p, t: (q_row(b, t), p))]
    if lat_queries:
        ins.append(k_arr)
        specs.append(pl.BlockSpec((seq, n_kv * dqk), lambda b, p, t: (b, kv0(p) // n_kv)))
    ins.append(k_arr)
    specs.append(pl.BlockSpec((ctx_len, n_kv * dqk), lambda b, p, t: (ctx0 + b, kv0(p) // n_kv)))
    if lat_queries:
        for s in range(n_kv):
            ins.append(v_arr)
            specs.append(pl.BlockSpec((seq, LANES), lambda b, p, t, s=s: (b, v_cb(kv0(p) + s))))
    for s in range(n_kv):
        ins.append(v_arr)
        specs.append(pl.BlockSpec((ctx_len, LANES), lambda b, p, t, s=s: (ctx0 + b, v_cb(kv0(p) + s))))
    return pl.pallas_call(
        functools.partial(_attn_body, use_lat=lat_queries, hp=hp, n_kv=n_kv, dqk=dqk),
        grid=(batch, n_heads // hp, nq),
        in_specs=specs,
        out_specs=pl.BlockSpec((tq, hp * LANES), lambda b, p, t: (o_row(b, t), p)),
        out_shape=jax.ShapeDtypeStruct((rows_q, n_heads * LANES), BF16),
        compiler_params=_cparams(("parallel", "parallel", "arbitrary")),
        name="attention_lat" if lat_queries else "attention_ctx",
    )(*ins)


def _rope_tables(seq, ctx_len, batch, rot_dim):
    half = rot_dim // 4
    inv_freq = ROPE_THETA ** (-jnp.arange(half, dtype=F32) / half)
    t = jnp.arange(seq, dtype=jnp.int32)
    row = (t // GRID_W).astype(F32)
    col = (t % GRID_W).astype(F32)
    ang_r = row[:, None] * inv_freq[None]
    ang_c = col[:, None] * inv_freq[None]
    cos = jnp.concatenate([jnp.cos(ang_r), jnp.cos(ang_r), jnp.cos(ang_c), jnp.cos(ang_c)], -1)
    sin = jnp.concatenate([jnp.sin(ang_r), jnp.sin(ang_r), jnp.sin(ang_c), jnp.sin(ang_c)], -1)
    cos = jnp.concatenate([jnp.tile(cos, (batch, 1)), jnp.ones((batch * ctx_len, rot_dim), F32)], 0)
    sin = jnp.concatenate([jnp.tile(sin, (batch, 1)), jnp.zeros((batch * ctx_len, rot_dim), F32)], 0)
    return cos, sin


def _rot_cols(w, rot_dim):
    lead = w.shape[:-1]
    q = rot_dim // 4
    w5 = w.reshape(lead + (-1, 2, 2, q))
    rot = jnp.concatenate([-w5[..., 1:2, :], w5[..., 0:1, :]], axis=-2)
    return rot.reshape(w.shape)


def _gqa_prep_body(a_ref, ar_ref, cos_ref, sin_ref, g_ref, gr_ref, o_ref, *, scale):
    a = a_ref[...]
    r = lax.rsqrt(jnp.mean(a * a, -1, keepdims=True) + NORM_EPS) * scale
    o = (a * g_ref[...]) * cos_ref[...] + (ar_ref[...] * gr_ref[...]) * sin_ref[...]
    o_ref[...] = (o * r).astype(o_ref.dtype)


def gqa_prep(raw, cb0, cbr0, n_heads, cos, sin, g, g_rot, scale, tm=1088):
    m = raw.shape[0]
    return pl.pallas_call(
        functools.partial(_gqa_prep_body, scale=scale),
        grid=(m // tm, n_heads),
        in_specs=[
            pl.BlockSpec((tm, LANES), lambda i, h: (i, cb0 + h)),
            pl.BlockSpec((tm, LANES), lambda i, h: (i, cbr0 + h)),
            pl.BlockSpec((tm, LANES), lambda i, h: (i, 0)),
            pl.BlockSpec((tm, LANES), lambda i, h: (i, 0)),
            pl.BlockSpec((1, LANES), lambda i, h: (0, 0)),
            pl.BlockSpec((1, LANES), lambda i, h: (0, 0)),
        ],
        out_specs=pl.BlockSpec((tm, LANES), lambda i, h: (i, h)),
        out_shape=jax.ShapeDtypeStruct((m, n_heads * LANES), BF16),
        compiler_params=_cparams(("parallel", "arbitrary")),
        name="gqa_prep",
    )(raw, raw, cos, sin, g.reshape(1, LANES), g_rot.reshape(1, LANES))


def _mla_rope_body(x_ref, c_ref, s_ref, o_ref, *, scale, with_nope):
    if with_nope:
        a = x_ref[:, :LANES]
        b = x_ref[:, LANES:]
    else:
        b = x_ref[...]
    roped = (b * c_ref[...] + pltpu.roll(b, LANES // 2, 1) * s_ref[...]) * scale
    if with_nope:
        o_ref[:, :LANES] = (a * scale).astype(o_ref.dtype)
        o_ref[:, LANES:] = roped.astype(o_ref.dtype)
    else:
        o_ref[...] = roped.astype(o_ref.dtype)


def mla_rope(raw, cb0, n_heads, cos, sin, scale, with_nope, tm=1088):
    m = raw.shape[0]
    w = 2 * LANES if with_nope else LANES
    return pl.pallas_call(
        functools.partial(_mla_rope_body, scale=scale, with_nope=with_nope),
        grid=(m // tm, n_heads),
        in_specs=[
            pl.BlockSpec((tm, w), lambda i, h: (i, cb0 + h)),
            pl.BlockSpec((tm, LANES), lambda i, h: (i, 0)),
            pl.BlockSpec((tm, LANES), lambda i, h: (i, 0)),
        ],
        out_specs=pl.BlockSpec((tm, w), lambda i, h: (i, h)),
        out_shape=jax.ShapeDtypeStruct((m, n_heads * w), BF16),
        compiler_params=_cparams(("parallel", "arbitrary")),
        name="mla_rope",
    )(raw, cos, sin)


def _mla_kcat_body(kn_ref, pe_ref, c_ref, s_ref, o_ref):
    b = pe_ref[...]
    o_ref[:, :LANES] = kn_ref[...]
    o_ref[:, LANES:] = (b * c_ref[...] + pltpu.roll(b, LANES // 2, 1) * s_ref[...]).astype(o_ref.dtype)


def mla_kcat(kv, down, pe_cb, n_heads, cos, sin, tm=1088):
    m = kv.shape[0]
    return pl.pallas_call(
        _mla_kcat_body,
        grid=(m // tm, n_heads),
        in_specs=[
            pl.BlockSpec((tm, LANES), lambda i, h: (i, 2 * h)),
            pl.BlockSpec((tm, LANES), lambda i, h: (i, pe_cb)),
            pl.BlockSpec((tm, LANES), lambda i, h: (i, 0)),
            pl.BlockSpec((tm, LANES), lambda i, h: (i, 0)),
        ],
        out_specs=pl.BlockSpec((tm, 2 * LANES), lambda i, h: (i, h)),
        out_shape=jax.ShapeDtypeStruct((m, n_heads * 2 * LANES), BF16),
        compiler_params=_cparams(("parallel", "arbitrary")),
        name="mla_kcat",
    )(kv, down, cos, sin)


class _FftPlan:
    def __init__(self, L):
        self.L = L
        self.N = 2 * L
        self.N2 = 128 if L >= 1024 else 64
        self.N1 = self.N // self.N2
        self.n1_in = self.N1 // 2
        self.K1 = self.N1 // 2 + 1
        self.nq = self.N2 // SUBLANES
        self.spec_rows = self.K1 * 2 * self.N2
        self.kf_cols = max(self.n1_in * SUBLANES, LANES)
        self.kf_rows = 2 * self.K1 * SUBLANES
        self.kb_cols = -(-self.kf_rows // LANES) * LANES
        self.kb_rows = self.n1_in * SUBLANES
        self.k1_unroll = 3 if self.K1 % 3 == 0 else 1

    def constants(self):
        N, N1, N2, K1, n1_in = self.N, self.N1, self.N2, self.K1, self.n1_in
        eye = np.eye(SUBLANES)
        k1 = np.arange(K1)[:, None]
        n1 = np.arange(n1_in)[None, :]
        th = 2 * np.pi * k1 * n1 / N1
        kf = np.zeros((self.kf_rows, self.kf_cols))
        kf[:K1 * SUBLANES, :n1_in * SUBLANES] = np.kron(np.cos(th), eye)
        kf[K1 * SUBLANES:, :n1_in * SUBLANES] = np.kron(-np.sin(th), eye)
        c = np.full((K1,), 2.0)
        c[0] = 1.0
        c[-1] = 1.0
        thb = 2 * np.pi * np.arange(n1_in)[:, None] * np.arange(K1)[None, :] / N1
        kb = np.zeros((self.kb_rows, self.kb_cols))
        kb[:, :K1 * SUBLANES] = np.kron(np.cos(thb) * c[None, :], eye)
        kb[:, K1 * SUBLANES:2 * K1 * SUBLANES] = np.kron(-np.sin(thb) * c[None, :], eye)
        a = 2 * np.pi * np.outer(np.arange(N2), np.arange(N2)) / N2
        C, S = np.cos(a), np.sin(a)
        w2f = np.block([[C, S], [-S, C]])
        w2i = np.block([[C, -S], [S, C]])
        tw_ang = 2 * np.pi * np.outer(np.arange(N2), np.arange(K1)) / N
        twr = np.repeat(np.cos(tw_ang), LANES, axis=1)
        twi = np.repeat(np.sin(tw_ang), LANES, axis=1)

        def hl(m):
            m32 = jnp.asarray(m, F32)
            hi = m32.astype(BF16)
            lo = (m32 - hi.astype(F32)).astype(BF16)
            return hi, lo

        out = {}
        for name, mat in (("kf", kf), ("kb", kb), ("w2f", w2f), ("w2i", w2i)):
            out[name + "_h"], out[name + "_l"] = hl(mat)
        out["twr"] = jnp.asarray(twr, F32)
        out["twi"] = jnp.asarray(twi, F32)
        return out


def _cmul(ar, ai, br, bi):
    return ar * br - ai * bi, ar * bi + ai * br


def _time_to_strided(plan, x_ref, xt_ref):
    N2, nq = plan.N2, plan.nq
    for n in range(plan.n1_in):
        xt_ref[n * SUBLANES:(n + 1) * SUBLANES, :] = jnp.concatenate(
            [x_ref[N2 * n + SUBLANES * q:N2 * n + SUBLANES * (q + 1), :] for q in range(nq)], axis=1)
    pad = plan.kf_cols - plan.n1_in * SUBLANES
    if pad:
        xt_ref[plan.n1_in * SUBLANES:, :] = jnp.zeros((pad, xt_ref.shape[1]), F32)


def _strided_to_spectrum(plan, a_ref, s_ref, c):
    N2, K1, nq = plan.N2, plan.K1, plan.nq
    for part in range(2):
        for k in range(K1):
            r0 = (part * K1 + k) * SUBLANES
            s_ref[part * N2:(part + 1) * N2, k * c:(k + 1) * c] = jnp.concatenate(
                [a_ref[r0:r0 + SUBLANES, q * c:(q + 1) * c] for q in range(nq)], axis=0)


def _spectrum_to_strided(plan, s_ref, b_ref, c):
    N2, K1, nq = plan.N2, plan.K1, plan.nq
    for part in range(2):
        for k in range(K1):
            r0 = (part * K1 + k) * SUBLANES
            b_ref[r0:r0 + SUBLANES, :] = jnp.concatenate(
                [s_ref[part * N2 + SUBLANES * q:part * N2 + SUBLANES * (q + 1), k * c:(k + 1) * c] for q in range(nq)],
                axis=1)
    pad = plan.kb_cols - plan.kf_rows
    if pad:
        b_ref[plan.kf_rows:, :] = jnp.zeros((pad, b_ref.shape[1]), F32)


def _strided_to_time(plan, yt, c):
    nq = plan.nq
    slabs = []
    for n in range(plan.n1_in):
        blk = yt[n * SUBLANES:(n + 1) * SUBLANES, :]
        slabs.extend(blk[:, q * c:(q + 1) * c] for q in range(nq))
    return jnp.concatenate(slabs, axis=0)


def _fft_fwd(plan, x_ref, consts, xt_ref, a_ref, s_ref, dotc):
    c = x_ref.shape[1]
    _time_to_strided(plan, x_ref, xt_ref)
    a_ref[:plan.kf_rows, :] = dotc(consts["kf_h"][...], consts["kf_l"][...], xt_ref[...])
    _strided_to_spectrum(plan, a_ref, s_ref, c)
    n2 = plan.N2
    tr, ti = consts["twr"][...], consts["twi"][...]
    xr, xi = _cmul(s_ref[:n2, :], s_ref[n2:, :], tr, -ti)
    s_ref[...] = dotc(consts["w2f_h"][...], consts["w2f_l"][...], jnp.concatenate([xr, xi], axis=0))


def _fft_conv_apply(plan, x_ref, g_ref, consts, xt_ref, a_ref, s_ref):
    c = x_ref.shape[1]
    n2 = plan.N2
    _fft_fwd(plan, x_ref, consts, xt_ref, a_ref, s_ref, _dotc2)
    pr, pi = _cmul(s_ref[:n2, :], s_ref[n2:, :], g_ref[:n2, :], g_ref[n2:, :])
    z = _dotc2(consts["w2i_h"][...], consts["w2i_l"][...], jnp.concatenate([pr, pi], axis=0))
    zr, zi = _cmul(z[:n2], z[n2:], consts["twr"][...], consts["twi"][...])
    s_ref[:n2, :] = zr
    s_ref[n2:, :] = zi
    _spectrum_to_strided(plan, s_ref, a_ref, c)
    yt = _dotc2(consts["kb_h"][...], consts["kb_l"][...], a_ref[...])
    return _strided_to_time(plan, yt, c)


def _fft_scratch(plan):
    wl = plan.nq * LANES
    return [pltpu.VMEM((plan.kf_cols, wl), F32),
            pltpu.VMEM((max(plan.kf_rows, plan.kb_cols), wl), F32),
            pltpu.VMEM((2 * plan.N2, plan.K1 * LANES), F32)]


_CONST_NAMES = ("kf_h", "kf_l", "kb_h", "kb_l", "w2f_h", "w2f_l", "w2i_h", "w2i_l", "twr", "twi")


def _const_specs(consts, nargs):
    ins, specs = [], []
    for nme in _CONST_NAMES:
        a = consts[nme]
        ins.append(a)
        if nargs == 1:
            specs.append(pl.BlockSpec(a.shape, lambda c: (0, 0), pipeline_mode=pl.Buffered(1)))
        else:
            specs.append(pl.BlockSpec(a.shape, lambda c, b: (0, 0), pipeline_mode=pl.Buffered(1)))
    return ins, specs


def _hy_hidden_body(z_ref, w1_ref, b1_ref, w2_ref, b2_ref, sf_ref, o_ref):
    h = jnp.sin(sf_ref[0:1, :] * (_dot3(z_ref[...], w1_ref[...]) + b1_ref[...]))
    h = jnp.sin(sf_ref[1:2, :] * (_dot3(h, w2_ref[...]) + b2_ref[...]))
    o_ref[...] = h


def hyena_hidden(L, f_w1, f_b1, f_w2, f_b2, sin_freq):
    emb, hid = f_w1.shape
    bands = (emb - 1) // 2
    t = jnp.linspace(0.0, 1.0, L, dtype=F32)[:, None]
    w = (2.0 * math.pi / L) * jnp.arange(L, dtype=F32)[:, None]
    f = jnp.linspace(1e-4, bands - 1, bands, dtype=F32)[None]
    z = jnp.concatenate([t, jnp.cos(f * w), -jnp.sin(f * w)], -1)
    embp = -(-emb // LANES) * LANES
    z = jnp.pad(z, ((0, 0), (0, embp - emb)))
    w1 = jnp.pad(f_w1, ((0, embp - emb), (0, 0)))
    ins = [z, w1, f_b1.reshape(1, hid), f_w2, f_b2.reshape(1, hid), sin_freq]
    return pl.pallas_call(
        _hy_hidden_body,
        out_shape=jax.ShapeDtypeStruct((L, hid), F32),
        name="hyena_hidden",
    )(*ins)


def _hy_filter_body(h_ref, wf_ref, wb_ref, dl_ref, *rest, plan):
    consts = {nme: r for nme, r in zip(_CONST_NAMES, rest)}
    g_ref, x_sc, xt_sc, a_sc, s_sc, sb_sc = rest[len(_CONST_NAMES):]
    L, n2 = plan.L, plan.N2
    t = lax.broadcasted_iota(jnp.int32, (L, LANES), 0).astype(F32) * (1.0 / (L - 1))
    win = jnp.exp(-t * dl_ref[...])
    row0 = lax.broadcasted_iota(jnp.int32, (L, LANES), 0) == 0
    h = h_ref[...]
    fwd = _dot3(h, wf_ref[...]) * win
    bwd = jnp.where(row0, 0.0, _dot3(h, wb_ref[...]) * win)
    ss = jnp.sum(fwd * fwd, 0, keepdims=True) + jnp.sum(bwd * bwd, 0, keepdims=True)
    nu = lax.rsqrt(ss + 1e-12) * (1.0 / plan.N)
    x_sc[...] = bwd * nu
    _fft_fwd(plan, x_sc, consts, xt_sc, a_sc, sb_sc, _dotc2)
    x_sc[...] = fwd * nu
    _fft_fwd(plan, x_sc, consts, xt_sc, a_sc, s_sc, _dotc2)
    g_ref[:n2, :] = s_sc[:n2, :] + sb_sc[:n2, :]
    g_ref[n2:, :] = s_sc[n2:, :] - sb_sc[n2:, :]


def hyena_filter_spectra(plan, consts, hidden, f_w3, d_model, n_order):
    L = plan.L
    hid = hidden.shape[1]
    ncb = d_model // LANES
    deltas = jnp.abs(jnp.linspace(math.log(HY_DECAY_TARGET) / HY_DECAY_PCT_SHORT,
                                  math.log(HY_DECAY_TARGET) / HY_DECAY_PCT_LONG, d_model, dtype=F32)).reshape(1, d_model)
    ins = [hidden, f_w3, f_w3, deltas]
    specs = [pl.BlockSpec((L, hid), lambda c, n: (0, 0)),
             pl.BlockSpec((hid, LANES), lambda c, n: (0, (2 * n) * ncb + c)),
             pl.BlockSpec((hid, LANES), lambda c, n: (0, (2 * n + 1) * ncb + c)),
             pl.BlockSpec((1, LANES), lambda c, n: (0, c))]
    ci, cs = _const_specs(consts, 2)
    ins += ci
    specs += cs
    srows, scols = 2 * plan.N2, plan.K1 * LANES
    return pl.pallas_call(
        functools.partial(_hy_filter_body, plan=plan),
        grid=(ncb, n_order),
        in_specs=specs,
        out_specs=pl.BlockSpec((None, None, srows, scols), lambda c, n: (n, c, 0, 0)),
        out_shape=jax.ShapeDtypeStruct((n_order, ncb, srows, scols), F32),
        scratch_shapes=[pltpu.VMEM((L, LANES), F32)] + _fft_scratch(plan) + [pltpu.VMEM((srows, scols), F32)],
        compiler_params=_cparams(("arbitrary", "arbitrary")),
        name="hyena_filter",
    )(*ins)


def _conv3(p, w_ref, b_ref, L):
    rows = lax.broadcasted_iota(jnp.int32, p.shape, 0)
    prev = jnp.where(rows == 0, 0.0, pltpu.roll(p, 1, 0))
    nxt = jnp.where(rows == L - 1, 0.0, pltpu.roll(p, L - 1, 0))
    return prev * w_ref[0:1, :] + p * w_ref[1:2, :] + nxt * w_ref[2:3, :] + b_ref[...]


def _hy_conv_body(*refs, plan, order, conv_u):
    it = iter(refs)
    u_ref, gate_ref = next(it), next(it)
    if conv_u:
        cwu_ref, cbu_ref = next(it), next(it)
    cwg_ref, cbg_ref = next(it), next(it)
    skip_ref = next(it)
    g_ref = next(it)
    consts = {nme: next(it) for nme in _CONST_NAMES}
    o_ref = next(it)
    x_sc, xt_sc, a_sc, s_sc = (next(it) for _ in range(4))
    L = plan.L
    u = u_ref[...]
    if conv_u:
        u = _conv3(u, cwu_ref, cbu_ref, L)
    x_sc[...] = u
    y = _fft_conv_apply(plan, x_sc, g_ref, consts, xt_sc, a_sc, s_sc)
    gate = _conv3(gate_ref[...], cwg_ref, cbg_ref, L)
    o_ref[...] = (gate * (y + x_sc[...] * skip_ref[order:order + 1, :])).astype(o_ref.dtype)


def hyena_conv(plan, consts, u_arr, u_cb0, gate_arr, gate_cb0, conv_w, conv_b, skip, spectra, *,
               order, conv_u, u_row_block0, gate_row_block0, batch, d_model, out_dtype):
    L = plan.L
    ncb = d_model // LANES
    ins = [u_arr, gate_arr]
    specs = [pl.BlockSpec((L, LANES), lambda c, b: (u_row_block0 + b, u_cb0 + c)),
             pl.BlockSpec((L, LANES), lambda c, b: (gate_row_block0 + b, gate_cb0 + c))]
    if conv_u:
        ins += [conv_w, conv_b]
        specs += [pl.BlockSpec((3, LANES), lambda c, b: (0, u_cb0 + c)),
                  pl.BlockSpec((1, LANES), lambda c, b: (0, u_cb0 + c))]
    srows, scols = 2 * plan.N2, plan.K1 * LANES
    ins += [conv_w, conv_b, skip, spectra]
    specs += [pl.BlockSpec((3, LANES), lambda c, b: (0, gate_cb0 + c)),
              pl.BlockSpec((1, LANES), lambda c, b: (0, gate_cb0 + c)),
              pl.BlockSpec((skip.shape[0], LANES), lambda c, b: (0, c)),
              pl.BlockSpec((None, None, srows, scols), lambda c, b: (order, c, 0, 0))]
    ci, cs = _const_specs(consts, 2)
    ins += ci
    specs += cs
    return pl.pallas_call(
        functools.partial(_hy_conv_body, plan=plan, order=order, conv_u=conv_u),
        grid=(ncb, batch),
        in_specs=specs,
        out_specs=pl.BlockSpec((L, LANES), lambda c, b: (b, c)),
        out_shape=jax.ShapeDtypeStruct((batch * L, d_model), out_dtype),
        scratch_shapes=[pltpu.VMEM((L, LANES), F32)] + _fft_scratch(plan),
        compiler_params=_cparams(("parallel", "arbitrary")),
        name=f"hyena_conv{order}_L{L}",
    )(*ins)


def hyena_mixer(hmix, p, *, batch, seq, ctx_len, d_model, want_ctx):
    n_order = p["skip"].shape[0]
    rows = batch * seq + (batch * ctx_len if want_ctx else 0)
    pr = mm(hmix, p["w_in"], bias=p["b_in"], name="hy_in", rows=rows, tm=1088 if rows % 1088 == 0 else 512)
    conv_b = p["conv_b"].reshape(1, -1)
    ncb = d_model // LANES
    outs = []
    streams = [(seq, 0)]
    if want_ctx:
        streams.append((ctx_len, (batch * seq) // ctx_len))
    for L, rb0 in streams:
        plan = _FftPlan(L)
        consts = plan.constants()
        hidden = hyena_hidden(L, p["f_w1"], p["f_b1"], p["f_w2"], p["f_b2"], p["sin_freq"])
        spectra = hyena_filter_spectra(plan, consts, hidden, p["f_w3"], d_model, n_order)
        z1 = hyena_conv(plan, consts, pr, 0, pr, ncb, p["conv_w"], conv_b, p["skip"], spectra,
                        order=0, conv_u=True, u_row_block0=rb0, gate_row_block0=rb0, batch=batch, d_model=d_model,
                        out_dtype=F32)
        z2 = hyena_conv(plan, consts, z1, 0, pr, 2 * ncb, p["conv_w"], conv_b, p["skip"], spectra,
                        order=1, conv_u=False, u_row_block0=0, gate_row_block0=rb0, batch=batch, d_model=d_model,
                        out_dtype=BF16)
        outs.append(z2)
    return outs[0] if len(outs) == 1 else jnp.concatenate(outs, 0)


def _headsum(x, e_ref):
    hi, lo = _split2(x)
    return _dot(hi, e_ref[...]) + _dot(lo, e_ref[...])


def _rw_mix_body(h_ref, hp_ref, hn_ref, mix_ref, o_ref, *, tm, lat_tiles, tiles_per_seq):
    i = pl.program_id(0)
    is_ctx = i >= lat_tiles
    first = jnp.logical_or(is_ctx, i % tiles_per_seq == 0)
    last = jnp.logical_or(is_ctx, i % tiles_per_seq == tiles_per_seq - 1)
    h = h_ref[...]
    rows = lax.broadcasted_iota(jnp.int32, h.shape, 0)
    pedge = jnp.where(first, 0.0, hp_ref[SUBLANES - 1:SUBLANES, :])
    nedge = jnp.where(last, 0.0, hn_ref[0:1, :])
    prev = jnp.where(rows == 0, pedge, pltpu.roll(h, 1, 0))
    nxt = jnp.where(rows == tm - 1, nedge, pltpu.roll(h, tm - 1, 0))
    dp = prev - h
    dn = nxt - h
    for n in range(o_ref.shape[0]):
        o_ref[n] = (h + dp * mix_ref[0, n:n + 1, :] + dn * mix_ref[1, n:n + 1, :]).astype(o_ref.dtype)


def rwkv_mix(h, mix, *, batch, seq, ctx_len):
    m, d = h.shape
    tm = ctx_len
    n_shift = mix.shape[1]
    nblk = m // SUBLANES
    per = tm // SUBLANES
    body = functools.partial(_rw_mix_body, tm=tm, lat_tiles=(batch * seq) // tm, tiles_per_seq=seq // tm)
    return pl.pallas_call(
        body,
        grid=(m // tm,),
        in_specs=[
            pl.BlockSpec((tm, d), lambda i: (i, 0)),
            pl.BlockSpec((SUBLANES, d), lambda i: (jnp.maximum(i * per - 1, 0), 0)),
            pl.BlockSpec((SUBLANES, d), lambda i: (jnp.minimum((i + 1) * per, nblk - 1), 0)),
            pl.BlockSpec((2, n_shift, d), lambda i: (0, 0, 0)),
        ],
        out_specs=pl.BlockSpec((n_shift, tm, d), lambda i: (0, i, 0)),
        out_shape=jax.ShapeDtypeStruct((n_shift, m, d), BF16),
        compiler_params=_cparams(("parallel",)),
        name="rwkv_mix",
    )(h, h, h, mix)


def _rw_prep_body(k_ref, lw0_ref, lw1_ref, al0_ref, al1_ref, w0_ref, a0_ref, kk_ref, ka_ref, e_ref,
                  olw0, olw1, okd0, okd1, obb0, obb1, oaa):
    k = k_ref[...]
    kk0 = k * kk_ref[...]
    kk = kk0 * lax.rsqrt(_headsum(kk0 * kk0, e_ref) + 1e-12)
    oaa[...] = -kk
    for d, (lw_ref, al_ref, olw, okd, obb) in enumerate(((lw0_ref, al0_ref, olw0, okd0, obb0),
                                                         (lw1_ref, al1_ref, olw1, okd1, obb1))):
        z = -(w0_ref[d:d + 1, :] + lw_ref[...])
        softplus = jnp.maximum(z, 0.0) + jnp.log(1.0 + jnp.exp(-jnp.abs(z)))
        log_w = -softplus - 0.5
        olw[...] = -jnp.exp(log_w)
        a = jax.nn.sigmoid(a0_ref[d:d + 1, :] + al_ref[...])
        okd[...] = k * (1.0 + (a - 1.0) * ka_ref[...])
        obb[...] = kk * a


def rwkv_prep(k, lw0, lw1, al0, al1, w0, a0, k_k, k_a, e128, tm=1088):
    m, d = k.shape
    blk = pl.BlockSpec((tm, LANES), lambda i, c: (i, c))
    vec2 = pl.BlockSpec((2, LANES), lambda i, c: (0, c))
    vec1 = pl.BlockSpec((1, LANES), lambda i, c: (0, c))
    sh = jax.ShapeDtypeStruct((m, d), F32)
    return pl.pallas_call(
        _rw_prep_body,
        grid=(m // tm, d // LANES),
        in_specs=[blk] * 5 + [vec2, vec2, vec1, vec1, pl.BlockSpec((LANES, LANES), lambda i, c: (0, 0))],
        out_specs=[blk] * 7,
        out_shape=[sh] * 7,
        compiler_params=_cparams(("parallel", "arbitrary")),
        name="rwkv_prep",
    )(k, lw0, lw1, al0, al1, w0, a0, k_k.reshape(1, d), k_a.reshape(1, d), e128)


def _b16(x):
    return x.astype(BF16)


def _wkv_body(r_ref, lw_ref, k_ref, v_ref, a_ref, b_ref, o_ref, s_sc, *, chunk, n_heads, reverse):
    c = pl.program_id(2)

    @pl.when(c == 0)
    def _():
        s_sc[...] = jnp.zeros_like(s_sc)

    C = chunk
    hs = range(n_heads)
    row = lax.broadcasted_iota(jnp.int32, (C, C), 0)
    col = lax.broadcasted_iota(jnp.int32, (C, C), 1)
    if reverse:
        tri_incl = col >= row
        tri_strict = col > row
        last = 0
    else:
        tri_incl = col <= row
        tri_strict = col < row
        last = C - 1
    same = lambda s: (row >> int(math.log2(s))) == (col >> int(math.log2(s)))
    eye = (row == col).astype(F32)
    cum_m = tri_incl.astype(BF16)
    sls = [slice(h * RW_HEAD, (h + 1) * RW_HEAD) for h in hs]

    lw = [lw_ref[:, sl] for sl in sls]
    cum = []
    for h in hs:
        l1, l2, l3 = _split3(lw[h])
        cum.append(_dot(cum_m, l1) + _dot(cum_m, l2) + _dot(cum_m, l3))
    p_inc = [jnp.exp(cum[h]) for h in hs]
    x1 = [jnp.concatenate([a_ref[:, sls[h]] * jnp.exp(cum[h] - lw[h]), r_ref[:, sls[h]] * p_inc[h]], axis=0) for h in hs]
    x2 = []
    for h in hs:
        p_inv = jnp.exp(-cum[h])
        x2.append(jnp.concatenate([b_ref[:, sls[h]] * p_inv, k_ref[:, sls[h]] * p_inv], axis=0))
    v = [v_ref[:, sl] for sl in sls]
    s0 = [s_sc[h] for h in hs]
    mfull = [_dot3(x1[h], x2[h], _dot_nt) for h in hs]
    w0 = [_dot3(x1[h], s0[h], _dot_nt) for h in hs]
    n_ab = [jnp.where(tri_strict, mfull[h][:C, :C], 0.0) for h in hs]
    a_ak = [_b16(jnp.where(tri_strict, mfull[h][:C, C:], 0.0)) for h in hs]
    a_r = [_b16(jnp.concatenate([jnp.where(tri_incl, mfull[h][C:, :C], 0.0),
                                 jnp.where(tri_incl, mfull[h][C:, C:], 0.0)], axis=1)) for h in hs]
    rhs = [w0[h][:C] + _dot(a_ak[h], _b16(v[h])) for h in hs]

    n4 = [jnp.where(same(4), n_ab[h], 0.0) for h in hs]
    n4b = [_b16(n4[h]) for h in hs]
    sq = [_dot(n4b[h], n4b[h]) for h in hs]
    t = [(eye + n4[h]) + _dot(_b16(eye + n4[h]), _b16(sq[h])) for h in hs]
    s = 4
    while s < C:
        off = jnp.logical_and(jnp.logical_and(same(2 * s), jnp.logical_not(same(s))), tri_strict)
        tb = [_b16(t[h]) for h in hs]
        tmp = [_dot(tb[h], _b16(jnp.where(off, n_ab[h], 0.0))) for h in hs]
        t = [t[h] + _dot(_b16(tmp[h]), tb[h]) for h in hs]
        s *= 2
    tb = [_b16(t[h]) for h in hs]
    u = [_dot(tb[h], _b16(rhs[h])) for h in hs]
    res = [rhs[h] - u[h] + _dot3(n_ab[h], u[h]) for h in hs]
    u = [u[h] + _dot(tb[h], _b16(res[h])) for h in hs]
    uv = [jnp.concatenate([u[h], v[h]], axis=0) for h in hs]
    y = [w0[h][C:] + _dot(a_r[h], _b16(uv[h])) for h in hs]
    for h in hs:
        s_sc[h] = (s0[h] + _dot3(uv[h], x2[h], _dot_tn)) * p_inc[h][last:last + 1, :]
    o_ref[...] = jnp.concatenate(y, axis=1)


def wkv7(r, lw, kd, v, aa, bb, *, reverse, batch, seq, ctx_len, chunk=128, heads_per_step=16):
    m, d = r.shape
    heads_per_step = min(heads_per_step, d // RW_HEAD)
    wl = heads_per_step * RW_HEAD
    n_ctx = ctx_len // chunk
    n_lat = seq // chunk
    n_chunks = n_ctx + n_lat
    ctx0 = (batch * seq) // chunk

    def row_block(b, c):
        if reverse:
            return jnp.where(c < n_ctx, ctx0 + b * n_ctx + (n_ctx - 1 - c), b * n_lat + (n_lat - 1 - (c - n_ctx)))
        return jnp.where(c < n_ctx, ctx0 + b * n_ctx + c, b * n_lat + (c - n_ctx))

    blk = pl.BlockSpec((chunk, wl), lambda b, g, c: (row_block(b, c), g))
    body = functools.partial(_wkv_body, chunk=chunk, n_heads=heads_per_step, reverse=reverse)
    return pl.pallas_call(
        body,
        grid=(batch, d // wl, n_chunks),
        in_specs=[blk] * 6,
        out_specs=blk,
        out_shape=jax.ShapeDtypeStruct((m, d), F32),
        scratch_shapes=[pltpu.VMEM((heads_per_step, RW_HEAD, RW_HEAD), F32)],
        compiler_params=_cparams(("parallel", "parallel", "arbitrary")),
        name="wkv7_rev" if reverse else "wkv7_fwd",
    )(r, lw, kd, v, aa, bb)


def _rw_readout_body(y0_ref, y1_ref, r_ref, v_ref, kd0_ref, kd1_ref, g_ref, rk_ref, lnw_ref, lnb_ref, e_ref, o_ref,
                     *, gn_eps):
    y = y0_ref[...] + y1_ref[...]
    inv = 1.0 / RW_HEAD
    mu = _headsum(y, e_ref) * inv
    dlt = y - mu
    var = _headsum(dlt * dlt, e_ref) * inv
    yn = dlt * lax.rsqrt(var + gn_eps) * lnw_ref[...] + lnb_ref[...]
    rk = r_ref[...] * rk_ref[...]
    bonus = (_headsum(rk * kd0_ref[...], e_ref) + _headsum(rk * kd1_ref[...], e_ref)) * v_ref[...]
    o_ref[...] = ((yn + bonus) * g_ref[...]).astype(o_ref.dtype)


def rwkv_readout(y0, y1, r, v, kd0, kd1, g, r_k, ln_w, ln_b, e128, gn_eps, rows, tm=512):
    d = r.shape[1]
    blk = pl.BlockSpec((tm, LANES), lambda i, c: (i, c))
    vec1 = pl.BlockSpec((1, LANES), lambda i, c: (0, c))
    return pl.pallas_call(
        functools.partial(_rw_readout_body, gn_eps=gn_eps),
        grid=(rows // tm, d // LANES),
        in_specs=[blk] * 7 + [vec1] * 3 + [pl.BlockSpec((LANES, LANES), lambda i, c: (0, 0))],
        out_specs=blk,
        out_shape=jax.ShapeDtypeStruct((rows, d), BF16),
        compiler_params=_cparams(("parallel", "arbitrary")),
        name="rwkv_readout",
    )(y0, y1, r, v, kd0, kd1, g, r_k.reshape(1, d), ln_w.reshape(1, d), ln_b.reshape(1, d), e128)


def rwkv_mixer(hmix, p, *, batch, seq, ctx_len, d_model):
    xs = rwkv_mix(hmix, p["mix"], batch=batch, seq=seq, ctx_len=ctx_len)
    m = hmix.shape[0]
    xs2 = xs.reshape(xs.shape[0] * m, d_model)
    tm = 1088 if m % 1088 == 0 else 512

    def proj(n, w, **kw):
        return mm(xs2, w, tm=tm, rows=m, x_row_block0=n * (m // tm), **kw)

    r = proj(0, p["w_r"], name="rw_r")
    k = proj(2, p["w_k"], name="rw_k")
    v = proj(3, p["w_v"], name="rw_v")
    g = mm(proj(5, p["g1"], act="sigmoid", out_dtype=BF16, name="rw_g1"), p["g2"], tm=tm, name="rw_g2")
    lws, als = [], []
    for d in range(2):
        tw = proj(1, p["w1"][d], act="tanh", out_dtype=BF16, name="rw_w1")
        lws.append(mm(tw, p["w2"][d], tm=tm, name="rw_w2"))
        ah = proj(4, p["a1"][d], out_dtype=BF16, name="rw_a1")
        als.append(mm(ah, p["a2"][d], tm=tm, name="rw_a2"))
    e128 = jnp.asarray(np.kron(np.eye(LANES // RW_HEAD), np.ones((RW_HEAD, RW_HEAD))), BF16)
    lw0, lw1, kd0, kd1, bb0, bb1, aa = rwkv_prep(k, lws[0], lws[1], als[0], als[1], p["w0"], p["a0"],
                                                 p["k_k"], p["k_a"], e128, tm=tm)
    kw = dict(batch=batch, seq=seq, ctx_len=ctx_len)
    y0 = wkv7(r, lw0, kd0, v, aa, bb0, reverse=False, **kw)
    y1 = wkv7(r, lw1, kd1, v, aa, bb1, reverse=True, **kw)
    return rwkv_readout(y0, y1, r, v, kd0, kd1, g, p["r_k"], p["ln_w"], p["ln_b"], e128,
                        1e-5 * RW_HEAD, rows=m)


def mla_mixer(hmix, p, *, batch, seq, ctx_len, want_ctx):
    m = hmix.shape[0]
    tm = 1088 if m % 1088 == 0 else 512
    scale = (MLA_NOPE + MLA_ROPE) ** -0.5 * LOG2E
    down = mm(hmix, p["w_down_ext"], tm=tm, name="mla_down")
    kq = MLA_Q_RANK
    q_raw = mm(down, p["w_uq_ext"], norm_g=p["q_norm_g"], x_col_block=0, tm=tm, name="mla_uq")
    kv = mm(down, p["w_ukv"], norm_g=p["kv_norm_g"], x_col_block=1, tm=tm, out_dtype=BF16, name="mla_ukv")
    cos, sin = _rope_tables(seq, ctx_len, batch, MLA_ROPE)
    zeros = jnp.zeros_like(cos)
    cos_p = jnp.concatenate([cos, zeros], -1)
    sin_p = jnp.concatenate([sin, zeros], -1)
    q = mla_rope(q_raw, 0, MLA_HEADS, cos_p, sin_p, scale, True, tm=tm)
    kcat = mla_kcat(kv, down, (kq + MLA_KV_RANK) // LANES, MLA_HEADS, cos_p, sin_p, tm=tm)
    kw = dict(dqk=2 * LANES, kv_group=1, n_heads=MLA_HEADS, batch=batch, seq=seq, ctx_len=ctx_len)
    o_lat = attention(q, kcat, kv, lambda g: 2 * g + 1, lat_queries=True, **kw)
    if not want_ctx:
        return o_lat
    o_ctx = attention(q, kcat, kv, lambda g: 2 * g + 1, lat_queries=False, **kw)
    return jnp.concatenate([o_lat, o_ctx], 0)


def gqa_mixer(hmix, p, *, batch, seq, ctx_len, want_ctx):
    m = hmix.shape[0]
    tm = 1088 if m % 1088 == 0 else 512
    scale = GQA_HEAD ** -0.5 * LOG2E
    raw = mm(hmix, p["w_qkv_ext"], tm=tm, name="gqa_qkv")
    cos, sin = _rope_tables(seq, ctx_len, batch, GQA_HEAD)
    nq, nk = GQA_HEADS, GQA_KV_HEADS
    q = gqa_prep(raw, 0, nq, nq, cos, sin, p["q_norm_g"], p["q_norm_g_rot"], scale, tm=tm)
    k = gqa_prep(raw, 2 * nq, 2 * nq + nk, nk, cos, sin, p["k_norm_g"], p["k_norm_g_rot"], 1.0, tm=tm)
    vcb0 = 2 * nq + 2 * nk
    vb = mm_cast(raw, vcb0, nk, tm=tm)
    grp = nq // nk
    kw = dict(dqk=LANES, kv_group=grp, n_heads=nq, batch=batch, seq=seq, ctx_len=ctx_len)
    o_lat = attention(q, k, vb, lambda g: g, lat_queries=True, **kw)
    if not want_ctx:
        return o_lat
    o_ctx = attention(q, k, vb, lambda g: g, lat_queries=False, **kw)
    return jnp.concatenate([o_lat, o_ctx], 0)


def _cast_body(x_ref, o_ref):
    o_ref[...] = x_ref[...].astype(o_ref.dtype)


def mm_cast(raw, cb0, nblocks, tm=512):
    m = raw.shape[0]
    return pl.pallas_call(
        _cast_body,
        grid=(m // tm, nblocks),
        in_specs=[pl.BlockSpec((tm, LANES), lambda i, c: (i, cb0 + c))],
        out_specs=pl.BlockSpec((tm, LANES), lambda i, c: (i, c)),
        out_shape=jax.ShapeDtypeStruct((m, nblocks * LANES), BF16),
        compiler_params=_cparams(("parallel", "arbitrary")),
        name="cast_cols",
    )(raw)


def _rot_gain(g):
    q = g.shape[-1] // 4
    g4 = g.reshape(2, 2, q)
    return jnp.concatenate([g4[:, 1:2], g4[:, 0:1]], axis=1).reshape(g.shape)


def kernel(x, c, ctx, c_ctx, ada_w, ada_b, norm_g, ffn_w_gu, ffn_w_down, hy_w_in, hy_b_in, hy_conv_w, hy_conv_b, hy_f_w1, hy_f_b1, hy_f_w2, hy_f_b2, hy_f_w3, hy_sin_freq, hy_skip, hy_w_out, hy_b_out, rw_mix, rw_w_rkv, rw_w0, rw_w1, rw_w2, rw_a0, rw_a1, rw_a2, rw_g1, rw_g2, rw_k_k, rw_k_a, rw_r_k, rw_ln_w, rw_ln_b, rw_w_o, mla_w_down, mla_q_norm_g, mla_w_uq, mla_kv_norm_g, mla_w_ukv, mla_w_o, gqa_w_qkv, gqa_q_norm_g, gqa_k_norm_g, gqa_w_o, final_g):
    batch, seq, d = x.shape
    ctx_len = ctx.shape[1]
    depth = ada_w.shape[0]
    n_mix = 4
    rows_lat = batch * seq
    rows_all = rows_lat + batch * ctx_len
    bf = lambda a: a.astype(BF16)
    w_gu_all, w_down_all = bf(ffn_w_gu), bf(ffn_w_down)

    cvec = jnp.concatenate([c, c_ctx[None], jnp.zeros((8 - batch - 1, d), F32)], 0)
    mods_all = adaln_all(cvec, ada_w, ada_b)[:, :batch + 1].reshape(depth, batch + 1, N_MOD, d)

    h = jnp.concatenate([x.reshape(rows_lat, d), ctx.reshape(batch * ctx_len, d)], 0)

    for i in range(depth):
        mi, j = i % n_mix, i // n_mix
        ctx_out = any(jj % n_mix != 0 for jj in range(i + 1, depth))
        mods = mods_all[i]
        gate5 = mods[:, 5:6, :]
        mix_dtype = F32 if mi == 1 else BF16
        h, hmix = ffn_half(h, mods, norm_g[i], w_gu_all, w_down_all, layer=i, which=0, s=0, rows=rows_all,
                           seq_rows=seq, next_g=norm_g[i, 1], next_s=1, next_dtype=mix_dtype)
        rows_out = rows_all if ctx_out else rows_lat
        kw = dict(batch=batch, seq=seq, ctx_len=ctx_len)
        if mi == 0:
            p = dict(w_in=bf(hy_w_in[j]), b_in=hy_b_in[j], conv_w=hy_conv_w[j], conv_b=hy_conv_b[j],
                     f_w1=hy_f_w1[j], f_b1=hy_f_b1[j], f_w2=hy_f_w2[j], f_b2=hy_f_b2[j], f_w3=hy_f_w3[j],
                     sin_freq=hy_sin_freq[j], skip=hy_skip[j])
            z = hyena_mixer(hmix, p, d_model=d, want_ctx=ctx_out, **kw)
            w_o, b_o = bf(hy_w_out[j]), hy_b_out[j]
        elif mi == 1:
            pad = lambda w, ax: jnp.pad(w, [(0, RW_LORA_PAD - w.shape[a]) if a == ax else (0, 0) for a in range(w.ndim)])
            p = dict(mix=rw_mix[j], w_r=bf(rw_w_rkv[j, 0]), w_k=bf(rw_w_rkv[j, 1]), w_v=bf(rw_w_rkv[j, 2]),
                     g1=bf(rw_g1[j]), g2=bf(rw_g2[j]),
                     w1=bf(pad(rw_w1[j], 2)), w2=bf(pad(rw_w2[j], 1)), a1=bf(pad(rw_a1[j], 2)), a2=bf(pad(rw_a2[j], 1)),
                     w0=rw_w0[j], a0=rw_a0[j], k_k=rw_k_k[j], k_a=rw_k_a[j], r_k=rw_r_k[j].reshape(-1),
                     ln_w=rw_ln_w[j], ln_b=rw_ln_b[j])
            z = rwkv_mixer(hmix, p, d_model=d, **kw)
            w_o, b_o = bf(rw_w_o[j]), None
        elif mi == 2:
            wd = mla_w_down[j]
            pe = wd[:, MLA_Q_RANK + MLA_KV_RANK:]
            wuq = mla_w_uq[j].reshape(MLA_Q_RANK, MLA_HEADS, MLA_NOPE + MLA_ROPE)
            wq_pe = wuq[..., MLA_NOPE:]
            wuq_ext = jnp.concatenate([wuq[..., :MLA_NOPE], wq_pe,
                                       _rot_cols(wq_pe.reshape(MLA_Q_RANK, -1), MLA_ROPE).reshape(wq_pe.shape)], -1)
            p = dict(w_down_ext=bf(jnp.concatenate([wd, _rot_cols(pe, MLA_ROPE)], 1)),
                     w_uq_ext=bf(wuq_ext.reshape(MLA_Q_RANK, -1)), w_ukv=bf(mla_w_ukv[j]),
                     q_norm_g=mla_q_norm_g[j], kv_norm_g=mla_kv_norm_g[j])
            z = mla_mixer(hmix, p, want_ctx=ctx_out, **kw)
            w_o, b_o = bf(mla_w_o[j]), None
        else:
            w = gqa_w_qkv[j]
            nqc = GQA_HEADS * GQA_HEAD
            nkc = GQA_KV_HEADS * GQA_HEAD
            wq, wk, wv = w[:, :nqc], w[:, nqc:nqc + nkc], w[:, nqc + nkc:]
            p = dict(w_qkv_ext=bf(jnp.concatenate([wq, _rot_cols(wq, GQA_HEAD), wk, _rot_cols(wk, GQA_HEAD), wv], 1)),
                     q_norm_g=gqa_q_norm_g[j], q_norm_g_rot=_rot_gain(gqa_q_norm_g[j]),
                     k_norm_g=gqa_k_norm_g[j], k_norm_g_rot=_rot_gain(gqa_k_norm_g[j]))
            z = gqa_mixer(hmix, p, want_ctx=ctx_out, **kw)
            w_o, b_o = bf(gqa_w_o[j]), None
        h = mm(z, w_o, bias=b_o, res=h, gate=gate5, seq_rows=seq, rows=rows_out, tn=d, name="mix_out")
        last = i == depth - 1
        if last:
            h, out = ffn_half(h, mods, norm_g[i], w_gu_all, w_down_all, layer=i, which=1, s=2, rows=rows_out,
                              seq_rows=seq, next_g=final_g, next_s=None, next_dtype=F32)
        else:
            h = ffn_half(h, mods, norm_g[i], w_gu_all, w_down_all, layer=i, which=1, s=2, rows=rows_out,
                         seq_rows=seq)
    return out[:rows_lat].reshape(batch, seq, d)
```

```python
import functools
import math

import numpy as np
import jax
import jax.numpy as jnp
from jax import lax
from jax.experimental import pallas as pl
from jax.experimental.pallas import tpu as pltpu

F32 = jnp.float32
BF16 = jnp.bfloat16

VMEM_CAP_BYTES = 56 * 1024 * 1024
LANES = 128
SUBLANES = 8

NORM_EPS = 1e-6
LOG2E = 1.4426950408889634
N_MOD = 9
GRID_W = 64
ROPE_THETA = 10000.0
HY_DECAY_TARGET = 1e-2
HY_DECAY_PCT_SHORT = 0.3
HY_DECAY_PCT_LONG = 1.5
RW_HEAD = 64
RW_LORA_PAD = 128
GQA_HEAD = 128
GQA_HEADS = 16
GQA_KV_HEADS = 8
MLA_HEADS = 16
MLA_NOPE = 128
MLA_ROPE = 64
MLA_V = 128
MLA_Q_RANK = 512
MLA_KV_RANK = 512


def _cparams(sem):
    return pltpu.CompilerParams(dimension_semantics=sem, vmem_limit_bytes=VMEM_CAP_BYTES)


def _dot(a, b):
    return jnp.dot(a, b, preferred_element_type=F32)


def _dot_nt(a, b):
    return lax.dot_general(a, b, (((1,), (1,)), ((), ())), preferred_element_type=F32)


def _dot_tn(a, b):
    return lax.dot_general(a, b, (((0,), (0,)), ((), ())), preferred_element_type=F32)


def _split2(x):
    hi = x.astype(BF16)
    lo = (x - hi.astype(F32)).astype(BF16)
    return hi, lo


def _split3(x):
    hi = x.astype(BF16)
    r1 = x - hi.astype(F32)
    mid = r1.astype(BF16)
    lo = (r1 - mid.astype(F32)).astype(BF16)
    return hi, mid, lo


def _dot3(a, b, f=_dot):
    ah, al = _split2(a)
    bh, bl = _split2(b)
    return f(ah, bh) + f(ah, bl) + f(al, bh)


def _dotc3(mh, ml, d):
    dh, dl = _split2(d)
    return _dot(mh, dh) + _dot(ml, dh) + _dot(mh, dl)


def _dotc2(mh, ml, d):
    dh = d.astype(BF16)
    return _dot(mh, dh) + _dot(ml, dh)


def _rms(x, g):
    return x * lax.rsqrt(jnp.mean(x * x, -1, keepdims=True) + NORM_EPS) * g


def _pick(n, cands):
    for c in cands:
        if n % c == 0:
            return c
    raise ValueError(f"no tile for {n}")


def _adaln_body(c_ref, w_ref, b_ref, o_ref):
    c = c_ref[...]
    s = c * jax.nn.sigmoid(c)
    o_ref[0] = _dot(s.astype(BF16), w_ref[0].astype(BF16)) + b_ref[0]


def adaln_all(cvec8, ada_w, ada_b):
    depth, d, n = ada_w.shape
    tn = _pick(n, (2048, 1024, 512, 256, 128))
    return pl.pallas_call(
        _adaln_body,
        grid=(depth, n // tn),
        in_specs=[
            pl.BlockSpec((8, d), lambda l, j: (0, 0)),
            pl.BlockSpec((1, d, tn), lambda l, j: (l, 0, j)),
            pl.BlockSpec((1, 1, tn), lambda l, j: (l, 0, j)),
        ],
        out_specs=pl.BlockSpec((1, 8, tn), lambda l, j: (l, 0, j)),
        out_shape=jax.ShapeDtypeStruct((depth, 8, n), F32),
        compiler_params=_cparams(("arbitrary", "arbitrary")),
        name="adaln",
    )(cvec8, ada_w, ada_b.reshape(depth, 1, n))


def _mm_body(*refs, has_bias, act, has_norm, has_res):
    it = iter(refs)
    x_ref = next(it)
    w_ref = next(it)
    b_ref = next(it) if has_bias else None
    g_ref = next(it) if has_norm else None
    r_ref = next(it) if has_res else None
    gate_ref = next(it) if has_res else None
    o_ref = next(it)
    x = x_ref[...]
    if has_norm:
        x = _rms(x.astype(F32), g_ref[...])
    acc = _dot(x.astype(BF16), w_ref[...])
    if has_bias:
        acc = acc + b_ref[...]
    if act == "sigmoid":
        acc = jax.nn.sigmoid(acc)
    elif act == "tanh":
        acc = jnp.tanh(acc)
    if has_res:
        acc = r_ref[...] + gate_ref[0] * acc
    o_ref[...] = acc.astype(o_ref.dtype)


def mm(x, w, *, bias=None, act=None, norm_g=None, res=None, gate=None, out_dtype=F32,
       tm=512, tn=None, rows=None, x_col_block=0, x_row_block0=0, seq_rows=None, name="mm"):
    k, n = w.shape
    m = rows if rows is not None else x.shape[0]
    assert m % tm == 0, (m, tm)
    tn = tn or _pick(n, (512, 384, 256, 128))
    has_res = res is not None
    ins = [x, w]
    specs = [pl.BlockSpec((tm, k), lambda i, j: (i + x_row_block0, x_col_block)),
             pl.BlockSpec((k, tn), lambda i, j: (0, j))]
    if bias is not None:
        ins.append(bias.reshape(1, n).astype(F32))
        specs.append(pl.BlockSpec((1, tn), lambda i, j: (0, j)))
    if norm_g is not None:
        ins.append(norm_g.reshape(1, k).astype(F32))
        specs.append(pl.BlockSpec((1, k), lambda i, j: (0, 0)))
    if has_res:
        ngroups = gate.shape[0]
        per = seq_rows // tm
        ins += [res, gate]
        specs += [pl.BlockSpec((tm, tn), lambda i, j: (i, j)),
                  pl.BlockSpec((1, 1, tn), lambda i, j: (jnp.minimum(i // per, ngroups - 1), 0, j))]
    body = functools.partial(_mm_body, has_bias=bias is not None, act=act,
                             has_norm=norm_g is not None, has_res=has_res)
    return pl.pallas_call(
        body,
        grid=(m // tm, n // tn),
        in_specs=specs,
        out_specs=pl.BlockSpec((tm, tn), lambda i, j: (i, j)),
        out_shape=jax.ShapeDtypeStruct((m, n), out_dtype),
        compiler_params=_cparams(("parallel", "arbitrary")),
        name=name,
    )(*ins)


def _ffn_body(*refs, s, n_f, has_next, next_s):
    it = iter(refs)
    x_ref, mod_ref, g_ref, wa_ref, wb_ref, wd_ref = (next(it) for _ in range(6))
    gn_ref = next(it) if has_next else None
    o_ref = next(it)
    hn_ref = next(it) if has_next else None
    xn_sc = next(it)
    acc_sc = next(it)
    j = pl.program_id(1)

    @pl.when(j == 0)
    def _():
        n = _rms(x_ref[...], g_ref[s:s + 1, :])
        n = n * (1.0 + mod_ref[0, 3 * s + 1:3 * s + 2, :]) + mod_ref[0, 3 * s:3 * s + 1, :]
        xn_sc[...] = n.astype(BF16)
        acc_sc[...] = jnp.zeros_like(acc_sc)

    xn = xn_sc[...]
    a = _dot(xn, wa_ref[...])
    b = _dot(xn, wb_ref[...])
    h = (a * jax.nn.sigmoid(a)) * b
    acc_sc[...] += _dot(h.astype(BF16), wd_ref[...])

    @pl.when(j == n_f - 1)
    def _():
        xnew = x_ref[...] + 0.5 * mod_ref[0, 3 * s + 2:3 * s + 3, :] * acc_sc[...]
        o_ref[...] = xnew
        if has_next:
            hn = _rms(xnew, gn_ref[...])
            if next_s is not None:
                hn = hn * (1.0 + mod_ref[0, 3 * next_s + 1:3 * next_s + 2, :]) + mod_ref[0, 3 * next_s:3 * next_s + 1, :]
            hn_ref[...] = hn.astype(hn_ref.dtype)


def ffn_half(x, mods, norm_g3, w_gu, w_down, *, layer, which, s, rows, seq_rows, next_g=None, next_s=None,
             next_dtype=F32, tm=512, fc=512):
    d = x.shape[1]
    d_ff = w_down.shape[2]
    n_f = d_ff // fc
    assert d_ff % fc == 0 and rows % tm == 0
    per = seq_rows // tm
    ngroups = mods.shape[0]
    has_next = next_g is not None
    ins = [x, mods, norm_g3, w_gu, w_gu, w_down]
    specs = [
        pl.BlockSpec((tm, d), lambda i, j: (i, 0)),
        pl.BlockSpec((1, N_MOD, d), lambda i, j: (jnp.minimum(i // per, ngroups - 1), 0, 0)),
        pl.BlockSpec((3, d), lambda i, j: (0, 0)),
        pl.BlockSpec((None, None, d, fc), lambda i, j: (layer, which, 0, j)),
        pl.BlockSpec((None, None, d, fc), lambda i, j: (layer, which, 0, n_f + j)),
        pl.BlockSpec((None, None, fc, d), lambda i, j: (layer, which, j, 0)),
    ]
    out_shape = [jax.ShapeDtypeStruct((rows, d), F32)]
    out_specs = [pl.BlockSpec((tm, d), lambda i, j: (i, 0))]
    if has_next:
        ins.append(next_g.reshape(1, d))
        specs.append(pl.BlockSpec((1, d), lambda i, j: (0, 0)))
        out_shape.append(jax.ShapeDtypeStruct((rows, d), next_dtype))
        out_specs.append(pl.BlockSpec((tm, d), lambda i, j: (i, 0)))
    body = functools.partial(_ffn_body, s=s, n_f=n_f, has_next=has_next, next_s=next_s)
    outs = pl.pallas_call(
        body,
        grid=(rows // tm, n_f),
        in_specs=specs,
        out_specs=out_specs,
        out_shape=out_shape,
        scratch_shapes=[pltpu.VMEM((tm, d), BF16), pltpu.VMEM((tm, d), F32)],
        compiler_params=_cparams(("parallel", "arbitrary")),
        name="ffn_half",
    )(*ins)
    return outs if has_next else outs[0]


def _attn_body(*refs, use_lat, hp, n_kv, dqk):
    it = iter(refs)
    q_ref = next(it)
    kl_ref = next(it) if use_lat else None
    kc_ref = next(it)
    vl_refs = [next(it) for _ in range(n_kv)] if use_lat else None
    vc_refs = [next(it) for _ in range(n_kv)]
    o_ref = next(it)
    hs = range(hp)
    slot = [h * n_kv // hp for h in hs]
    q = [q_ref[:, h * dqk:(h + 1) * dqk] for h in hs]
    s_c = [_dot_nt(kc_ref[:, slot[h] * dqk:(slot[h] + 1) * dqk], q[h]) for h in hs]
    m = [jnp.max(s_c[h], 0, keepdims=True) for h in hs]
    if use_lat:
        s_l = [_dot_nt(kl_ref[:, slot[h] * dqk:(slot[h] + 1) * dqk], q[h]) for h in hs]
        m = [jnp.maximum(m[h], jnp.max(s_l[h], 0, keepdims=True)) for h in hs]
    p_c = [jnp.exp2(s_c[h] - m[h]) for h in hs]
    den = [jnp.sum(p_c[h], 0, keepdims=True) for h in hs]
    o = [_dot_tn(vc_refs[slot[h]][...], p_c[h].astype(BF16)) for h in hs]
    if use_lat:
        p_l = [jnp.exp2(s_l[h] - m[h]) for h in hs]
        den = [den[h] + jnp.sum(p_l[h], 0, keepdims=True) for h in hs]
        o = [o[h] + _dot_tn(vl_refs[slot[h]][...], p_l[h].astype(BF16)) for h in hs]
    for h in hs:
        o_ref[:, h * LANES:(h + 1) * LANES] = (o[h] / den[h]).T.astype(o_ref.dtype)


def attention(q_arr, k_arr, v_arr, v_cb, *, dqk, kv_group, n_heads, batch, seq, ctx_len, lat_queries, tq=512, hp=1):
    assert n_heads % hp == 0 and (kv_group % hp == 0 or hp % kv_group == 0)
    n_kv = max(hp // kv_group, 1)
    kv0 = lambda p: (p * hp) // kv_group
    ctx0 = (batch * seq) // ctx_len
    if lat_queries:
        tq = min(tq, seq)
        nq = seq // tq
        q_row = lambda b, t: b * nq + t
        o_row = lambda b, t: b * nq + t
        rows_q = batch * seq
    else:
        tq = ctx_len
        nq = 1
        q_row = lambda b, t: ctx0 + b
        o_row = lambda b, t: b
        rows_q = batch * ctx_len
    ins = [q_arr]
    specs = [pl.BlockSpec((tq, hp * dqk), lambda b, p, t: (q_row(b, t), p))]
    if lat_queries:
        ins.append(k_arr)
        specs.append(pl.BlockSpec((seq, n_kv * dqk), lambda b, p, t: (b, kv0(p) // n_kv)))
    ins.append(k_arr)
    specs.append(pl.BlockSpec((ctx_len, n_kv * dqk), lambda b, p, t: (ctx0 + b, kv0(p) // n_kv)))
    if lat_queries:
        for s in range(n_kv):
            ins.append(v_arr)
            specs.append(pl.BlockSpec((seq, LANES), lambda b, p, t, s=s: (b, v_cb(kv0(p) + s))))
    for s in range(n_kv):
        ins.append(v_arr)
        specs.append(pl.BlockSpec((ctx_len, LANES), lambda b, p, t, s=s: (ctx0 + b, v_cb(kv0(p) + s))))
    return pl.pallas_call(
        functools.partial(_attn_body, use_lat=lat_queries, hp=hp, n_kv=n_kv, dqk=dqk),
        grid=(batch, n_heads // hp, nq),
        in_specs=specs,
        out_specs=pl.BlockSpec((tq, hp * LANES), lambda b, p, t: (o_row(b, t), p)),
        out_shape=jax.ShapeDtypeStruct((rows_q, n_heads * LANES), BF16),
        compiler_params=_cparams(("parallel", "parallel", "arbitrary")),
        name="attention_lat" if lat_queries else "attention_ctx",
    )(*ins)


def _rope_tables(seq, ctx_len, batch, rot_dim):
    half = rot_dim // 4
    inv_freq = ROPE_THETA ** (-jnp.arange(half, dtype=F32) / half)
    t = jnp.arange(seq, dtype=jnp.int32)
    row = (t // GRID_W).astype(F32)
    col = (t % GRID_W).astype(F32)
    ang_r = row[:, None] * inv_freq[None]
    ang_c = col[:, None] * inv_freq[None]
    cos = jnp.concatenate([jnp.cos(ang_r), jnp.cos(ang_r), jnp.cos(ang_c), jnp.cos(ang_c)], -1)
    sin = jnp.concatenate([jnp.sin(ang_r), jnp.sin(ang_r), jnp.sin(ang_c), jnp.sin(ang_c)], -1)
    cos = jnp.concatenate([jnp.tile(cos, (batch, 1)), jnp.ones((batch * ctx_len, rot_dim), F32)], 0)
    sin = jnp.concatenate([jnp.tile(sin, (batch, 1)), jnp.zeros((batch * ctx_len, rot_dim), F32)], 0)
    return cos, sin


def _rot_cols(w, rot_dim):
    lead = w.shape[:-1]
    q = rot_dim // 4
    w5 = w.reshape(lead + (-1, 2, 2, q))
    rot = jnp.concatenate([-w5[..., 1:2, :], w5[..., 0:1, :]], axis=-2)
    return rot.reshape(w.shape)


def _rope_half(b, cos, sin):
    return b * cos + pltpu.roll(b, LANES // 2, 1) * sin


def _proj_rope_body(*refs, mode, has_norm, scale):
    it = iter(refs)
    x_ref, w_ref = next(it), next(it)
    ng_ref = next(it) if has_norm else None
    cos_ref, sin_ref = next(it), next(it)
    x = x_ref[...]
    if has_norm:
        x = _rms(x.astype(F32), ng_ref[...])
    acc = _dot(x.astype(BF16), w_ref[...])
    a, b = acc[:, :LANES], acc[:, LANES:]
    if mode == "gqa":
        g_ref, gr_ref, o_ref = next(it), next(it), next(it)
        r = lax.rsqrt(jnp.mean(a * a, -1, keepdims=True) + NORM_EPS) * scale
        o_ref[...] = (((a * g_ref[...]) * cos_ref[...] + (b * gr_ref[...]) * sin_ref[...]) * r).astype(o_ref.dtype)
    elif mode == "mla_q":
        o_ref = next(it)
        o_ref[:, :LANES] = (a * scale).astype(o_ref.dtype)
        o_ref[:, LANES:] = (_rope_half(b, cos_ref[...], sin_ref[...]) * scale).astype(o_ref.dtype)
    else:
        pe_ref, k_ref, v_ref = next(it), next(it), next(it)
        k_ref[:, :LANES] = a.astype(k_ref.dtype)
        k_ref[:, LANES:] = _rope_half(pe_ref[...], cos_ref[...], sin_ref[...]).astype(k_ref.dtype)
        v_ref[...] = b.astype(v_ref.dtype)


def proj_rope(x, w, cos, sin, *, mode, n_heads, scale=1.0, norm_g=None, x_col_block=0, gains=None,
              pe=None, pe_cb=0, tm=1088):
    m = x.shape[0]
    k = w.shape[0]
    ins = [x, w]
    specs = [pl.BlockSpec((tm, k), lambda i, h: (i, x_col_block)),
             pl.BlockSpec((k, 2 * LANES), lambda i, h: (0, h))]
    if norm_g is not None:
        ins.append(norm_g.reshape(1, k).astype(F32))
        specs.append(pl.BlockSpec((1, k), lambda i, h: (0, 0)))
    tab = pl.BlockSpec((tm, LANES), lambda i, h: (i, 0))
    ins += [cos, sin]
    specs += [tab, tab]
    if mode == "gqa":
        vec = pl.BlockSpec((1, LANES), lambda i, h: (0, 0))
        ins += [gains[0].reshape(1, LANES), gains[1].reshape(1, LANES)]
        specs += [vec, vec]
        out_shape = jax.ShapeDtypeStruct((m, n_heads * LANES), BF16)
        out_specs = pl.BlockSpec((tm, LANES), lambda i, h: (i, h))
    elif mode == "mla_q":
        out_shape = jax.ShapeDtypeStruct((m, n_heads * 2 * LANES), BF16)
        out_specs = pl.BlockSpec((tm, 2 * LANES), lambda i, h: (i, h))
    else:
        ins.append(pe)
        specs.append(pl.BlockSpec((tm, LANES), lambda i, h: (i, pe_cb)))
        out_shape = [jax.ShapeDtypeStruct((m, n_heads * 2 * LANES), BF16),
                     jax.ShapeDtypeStruct((m, n_heads * LANES), BF16)]
        out_specs = [pl.BlockSpec((tm, 2 * LANES), lambda i, h: (i, h)),
                     pl.BlockSpec((tm, LANES), lambda i, h: (i, h))]
    return pl.pallas_call(
        functools.partial(_proj_rope_body, mode=mode, has_norm=norm_g is not None, scale=scale),
        grid=(m // tm, n_heads),
        in_specs=specs,
        out_specs=out_specs,
        out_shape=out_shape,
        compiler_params=_cparams(("parallel", "arbitrary")),
        name="proj_" + mode,
    )(*ins)


class _FftPlan:
    def __init__(self, L):
        self.L = L
        self.N = 2 * L
        self.N2 = 128 if L >= 1024 else 64
        self.N1 = self.N // self.N2
        self.n1_in = self.N1 // 2
        self.K1 = self.N1 // 2 + 1
        self.nq = self.N2 // SUBLANES
        self.spec_rows = self.K1 * 2 * self.N2
        self.kf_cols = max(self.n1_in * SUBLANES, LANES)
        self.kf_rows = 2 * self.K1 * SUBLANES
        self.kb_cols = -(-self.kf_rows // LANES) * LANES
        self.kb_rows = self.n1_in * SUBLANES
        self.k1_unroll = 3 if self.K1 % 3 == 0 else 1

    def constants(self):
        N, N1, N2, K1, n1_in = self.N, self.N1, self.N2, self.K1, self.n1_in
        eye = np.eye(SUBLANES)
        k1 = np.arange(K1)[:, None]
        n1 = np.arange(n1_in)[None, :]
        th = 2 * np.pi * k1 * n1 / N1
        kf = np.zeros((self.kf_rows, self.kf_cols))
        kf[:K1 * SUBLANES, :n1_in * SUBLANES] = np.kron(np.cos(th), eye)
        kf[K1 * SUBLANES:, :n1_in * SUBLANES] = np.kron(-np.sin(th), eye)
        c = np.full((K1,), 2.0)
        c[0] = 1.0
        c[-1] = 1.0
        thb = 2 * np.pi * np.arange(n1_in)[:, None] * np.arange(K1)[None, :] / N1
        kb = np.zeros((self.kb_rows, self.kb_cols))
        kb[:, :K1 * SUBLANES] = np.kron(np.cos(thb) * c[None, :], eye)
        kb[:, K1 * SUBLANES:2 * K1 * SUBLANES] = np.kron(-np.sin(thb) * c[None, :], eye)
        a = 2 * np.pi * np.outer(np.arange(N2), np.arange(N2)) / N2
        C, S = np.cos(a), np.sin(a)
        w2f = np.block([[C, S], [-S, C]])
        w2i = np.block([[C, -S], [S, C]])
        tw_ang = 2 * np.pi * np.outer(np.arange(N2), np.arange(K1)) / N
        twr = np.repeat(np.cos(tw_ang), LANES, axis=1)
        twi = np.repeat(np.sin(tw_ang), LANES, axis=1)

        def hl(m):
            m32 = jnp.asarray(m, F32)
            hi = m32.astype(BF16)
            lo = (m32 - hi.astype(F32)).astype(BF16)
            return hi, lo

        out = {}
        for name, mat in (("kf", kf), ("kb", kb), ("w2f", w2f), ("w2i", w2i)):
            out[name + "_h"], out[name + "_l"] = hl(mat)
        out["twr"] = jnp.asarray(twr, F32)
        out["twi"] = jnp.asarray(twi, F32)
        return out


def _cmul(ar, ai, br, bi):
    return ar * br - ai * bi, ar * bi + ai * br


def _time_to_strided(plan, x_ref, xt_ref):
    N2, nq = plan.N2, plan.nq
    for n in range(plan.n1_in):
        xt_ref[n * SUBLANES:(n + 1) * SUBLANES, :] = jnp.concatenate(
            [x_ref[N2 * n + SUBLANES * q:N2 * n + SUBLANES * (q + 1), :] for q in range(nq)], axis=1)
    pad = plan.kf_cols - plan.n1_in * SUBLANES
    if pad:
        xt_ref[plan.n1_in * SUBLANES:, :] = jnp.zeros((pad, xt_ref.shape[1]), F32)


def _strided_to_spectrum(plan, a_ref, s_ref, c):
    N2, K1, nq = plan.N2, plan.K1, plan.nq
    for part in range(2):
        for k in range(K1):
            r0 = (part * K1 + k) * SUBLANES
            s_ref[part * N2:(part + 1) * N2, k * c:(k + 1) * c] = jnp.concatenate(
                [a_ref[r0:r0 + SUBLANES, q * c:(q + 1) * c] for q in range(nq)], axis=0)


def _spectrum_to_strided(plan, s_ref, b_ref, c):
    N2, K1, nq = plan.N2, plan.K1, plan.nq
    for part in range(2):
        for k in range(K1):
            r0 = (part * K1 + k) * SUBLANES
            b_ref[r0:r0 + SUBLANES, :] = jnp.concatenate(
                [s_ref[part * N2 + SUBLANES * q:part * N2 + SUBLANES * (q + 1), k * c:(k + 1) * c] for q in range(nq)],
                axis=1)
    pad = plan.kb_cols - plan.kf_rows
    if pad:
        b_ref[plan.kf_rows:, :] = jnp.zeros((pad, b_ref.shape[1]), F32)


def _strided_to_time(plan, yt, c):
    nq = plan.nq
    slabs = []
    for n in range(plan.n1_in):
        blk = yt[n * SUBLANES:(n + 1) * SUBLANES, :]
        slabs.extend(blk[:, q * c:(q + 1) * c] for q in range(nq))
    return jnp.concatenate(slabs, axis=0)


def _fft_fwd(plan, x_ref, consts, xt_ref, a_ref, s_ref, dotc):
    c = x_ref.shape[1]
    _time_to_strided(plan, x_ref, xt_ref)
    a_ref[:plan.kf_rows, :] = dotc(consts["kf_h"][...], consts["kf_l"][...], xt_ref[...])
    _strided_to_spectrum(plan, a_ref, s_ref, c)
    n2 = plan.N2
    tr, ti = consts["twr"][...], consts["twi"][...]
    xr, xi = _cmul(s_ref[:n2, :], s_ref[n2:, :], tr, -ti)
    s_ref[...] = dotc(consts["w2f_h"][...], consts["w2f_l"][...], jnp.concatenate([xr, xi], axis=0))


def _fft_conv_apply(plan, x_ref, g_ref, consts, xt_ref, a_ref, s_ref):
    c = x_ref.shape[1]
    n2 = plan.N2
    _fft_fwd(plan, x_ref, consts, xt_ref, a_ref, s_ref, _dotc2)
    pr, pi = _cmul(s_ref[:n2, :], s_ref[n2:, :], g_ref[:n2, :], g_ref[n2:, :])
    z = _dotc2(consts["w2i_h"][...], consts["w2i_l"][...], jnp.concatenate([pr, pi], axis=0))
    zr, zi = _cmul(z[:n2], z[n2:], consts["twr"][...], consts["twi"][...])
    s_ref[:n2, :] = zr
    s_ref[n2:, :] = zi
    _spectrum_to_strided(plan, s_ref, a_ref, c)
    yt = _dotc2(consts["kb_h"][...], consts["kb_l"][...], a_ref[...])
    return _strided_to_time(plan, yt, c)


def _fft_scratch(plan):
    wl = plan.nq * LANES
    return [pltpu.VMEM((plan.kf_cols, wl), F32),
            pltpu.VMEM((max(plan.kf_rows, plan.kb_cols), wl), F32),
            pltpu.VMEM((2 * plan.N2, plan.K1 * LANES), F32)]


_CONST_NAMES = ("kf_h", "kf_l", "kb_h", "kb_l", "w2f_h", "w2f_l", "w2i_h", "w2i_l", "twr", "twi")


def _const_specs(consts, nargs):
    ins, specs = [], []
    for nme in _CONST_NAMES:
        a = consts[nme]
        ins.append(a)
        if nargs == 1:
            specs.append(pl.BlockSpec(a.shape, lambda c: (0, 0), pipeline_mode=pl.Buffered(1)))
        else:
            specs.append(pl.BlockSpec(a.shape, lambda c, b: (0, 0), pipeline_mode=pl.Buffered(1)))
    return ins, specs


def _hy_hidden_body(z_ref, w1_ref, b1_ref, w2_ref, b2_ref, sf_ref, o_ref):
    h = jnp.sin(sf_ref[0:1, :] * (_dot3(z_ref[...], w1_ref[...]) + b1_ref[...]))
    h = jnp.sin(sf_ref[1:2, :] * (_dot3(h, w2_ref[...]) + b2_ref[...]))
    o_ref[...] = h


def hyena_hidden(L, f_w1, f_b1, f_w2, f_b2, sin_freq):
    emb, hid = f_w1.shape
    bands = (emb - 1) // 2
    t = jnp.linspace(0.0, 1.0, L, dtype=F32)[:, None]
    w = (2.0 * math.pi / L) * jnp.arange(L, dtype=F32)[:, None]
    f = jnp.linspace(1e-4, bands - 1, bands, dtype=F32)[None]
    z = jnp.concatenate([t, jnp.cos(f * w), -jnp.sin(f * w)], -1)
    embp = -(-emb // LANES) * LANES
    z = jnp.pad(z, ((0, 0), (0, embp - emb)))
    w1 = jnp.pad(f_w1, ((0, embp - emb), (0, 0)))
    ins = [z, w1, f_b1.reshape(1, hid), f_w2, f_b2.reshape(1, hid), sin_freq]
    return pl.pallas_call(
        _hy_hidden_body,
        out_shape=jax.ShapeDtypeStruct((L, hid), F32),
        name="hyena_hidden",
    )(*ins)


def _hy_filter_body(h_ref, wf_ref, wb_ref, dl_ref, *rest, plan):
    consts = {nme: r for nme, r in zip(_CONST_NAMES, rest)}
    g_ref, x_sc, xt_sc, a_sc, s_sc, sb_sc = rest[len(_CONST_NAMES):]
    L, n2 = plan.L, plan.N2
    t = lax.broadcasted_iota(jnp.int32, (L, LANES), 0).astype(F32) * (1.0 / (L - 1))
    win = jnp.exp(-t * dl_ref[...])
    row0 = lax.broadcasted_iota(jnp.int32, (L, LANES), 0) == 0
    h = h_ref[...]
    fwd = _dot3(h, wf_ref[...]) * win
    bwd = jnp.where(row0, 0.0, _dot3(h, wb_ref[...]) * win)
    ss = jnp.sum(fwd * fwd, 0, keepdims=True) + jnp.sum(bwd * bwd, 0, keepdims=True)
    nu = lax.rsqrt(ss + 1e-12) * (1.0 / plan.N)
    x_sc[...] = bwd * nu
    _fft_fwd(plan, x_sc, consts, xt_sc, a_sc, sb_sc, _dotc2)
    x_sc[...] = fwd * nu
    _fft_fwd(plan, x_sc, consts, xt_sc, a_sc, s_sc, _dotc2)
    g_ref[:n2, :] = s_sc[:n2, :] + sb_sc[:n2, :]
    g_ref[n2:, :] = s_sc[n2:, :] - sb_sc[n2:, :]


def hyena_filter_spectra(plan, consts, hidden, f_w3, d_model, n_order):
    L = plan.L
    hid = hidden.shape[1]
    ncb = d_model // LANES
    deltas = jnp.abs(jnp.linspace(math.log(HY_DECAY_TARGET) / HY_DECAY_PCT_SHORT,
                                  math.log(HY_DECAY_TARGET) / HY_DECAY_PCT_LONG, d_model, dtype=F32)).reshape(1, d_model)
    ins = [hidden, f_w3, f_w3, deltas]
    specs = [pl.BlockSpec((L, hid), lambda c, n: (0, 0)),
             pl.BlockSpec((hid, LANES), lambda c, n: (0, (2 * n) * ncb + c)),
             pl.BlockSpec((hid, LANES), lambda c, n: (0, (2 * n + 1) * ncb + c)),
             pl.BlockSpec((1, LANES), lambda c, n: (0, c))]
    ci, cs = _const_specs(consts, 2)
    ins += ci
    specs += cs
    srows, scols = 2 * plan.N2, plan.K1 * LANES
    return pl.pallas_call(
        functools.partial(_hy_filter_body, plan=plan),
        grid=(ncb, n_order),
        in_specs=specs,
        out_specs=pl.BlockSpec((None, None, srows, scols), lambda c, n: (n, c, 0, 0)),
        out_shape=jax.ShapeDtypeStruct((n_order, ncb, srows, scols), F32),
        scratch_shapes=[pltpu.VMEM((L, LANES), F32)] + _fft_scratch(plan) + [pltpu.VMEM((srows, scols), F32)],
        compiler_params=_cparams(("arbitrary", "arbitrary")),
        name="hyena_filter",
    )(*ins)


def _conv3(p, w_ref, b_ref, L):
    rows = lax.broadcasted_iota(jnp.int32, p.shape, 0)
    prev = jnp.where(rows == 0, 0.0, pltpu.roll(p, 1, 0))
    nxt = jnp.where(rows == L - 1, 0.0, pltpu.roll(p, L - 1, 0))
    return prev * w_ref[0:1, :] + p * w_ref[1:2, :] + nxt * w_ref[2:3, :] + b_ref[...]


def _hy_conv_body(*refs, plan, order, conv_u):
    it = iter(refs)
    u_ref, gate_ref = next(it), next(it)
    if conv_u:
        cwu_ref, cbu_ref = next(it), next(it)
    cwg_ref, cbg_ref = next(it), next(it)
    skip_ref = next(it)
    g_ref = next(it)
    consts = {nme: next(it) for nme in _CONST_NAMES}
    o_ref = next(it)
    x_sc, xt_sc, a_sc, s_sc = (next(it) for _ in range(4))
    L = plan.L
    u = u_ref[...]
    if conv_u:
        u = _conv3(u, cwu_ref, cbu_ref, L)
    x_sc[...] = u
    y = _fft_conv_apply(plan, x_sc, g_ref, consts, xt_sc, a_sc, s_sc)
    gate = _conv3(gate_ref[...], cwg_ref, cbg_ref, L)
    o_ref[...] = (gate * (y + x_sc[...] * skip_ref[order:order + 1, :])).astype(o_ref.dtype)


def hyena_conv(plan, consts, u_arr, u_cb0, gate_arr, gate_cb0, conv_w, conv_b, skip, spectra, *,
               order, conv_u, u_row_block0, gate_row_block0, batch, d_model, out_dtype):
    L = plan.L
    ncb = d_model // LANES
    ins = [u_arr, gate_arr]
    specs = [pl.BlockSpec((L, LANES), lambda c, b: (u_row_block0 + b, u_cb0 + c)),
             pl.BlockSpec((L, LANES), lambda c, b: (gate_row_block0 + b, gate_cb0 + c))]
    if conv_u:
        ins += [conv_w, conv_b]
        specs += [pl.BlockSpec((3, LANES), lambda c, b: (0, u_cb0 + c)),
                  pl.BlockSpec((1, LANES), lambda c, b: (0, u_cb0 + c))]
    srows, scols = 2 * plan.N2, plan.K1 * LANES
    ins += [conv_w, conv_b, skip, spectra]
    specs += [pl.BlockSpec((3, LANES), lambda c, b: (0, gate_cb0 + c)),
              pl.BlockSpec((1, LANES), lambda c, b: (0, gate_cb0 + c)),
              pl.BlockSpec((skip.shape[0], LANES), lambda c, b: (0, c)),
              pl.BlockSpec((None, None, srows, scols), lambda c, b: (order, c, 0, 0))]
    ci, cs = _const_specs(consts, 2)
    ins += ci
    specs += cs
    return pl.pallas_call(
        functools.partial(_hy_conv_body, plan=plan, order=order, conv_u=conv_u),
        grid=(ncb, batch),
        in_specs=specs,
        out_specs=pl.BlockSpec((L, LANES), lambda c, b: (b, c)),
        out_shape=jax.ShapeDtypeStruct((batch * L, d_model), out_dtype),
        scratch_shapes=[pltpu.VMEM((L, LANES), F32)] + _fft_scratch(plan),
        compiler_params=_cparams(("parallel", "arbitrary")),
        name=f"hyena_conv{order}_L{L}",
    )(*ins)


def hyena_mixer(hmix, p, *, batch, seq, ctx_len, d_model, want_ctx):
    n_order = p["skip"].shape[0]
    rows = batch * seq + (batch * ctx_len if want_ctx else 0)
    pr = mm(hmix, p["w_in"], bias=p["b_in"], name="hy_in", rows=rows, tm=1088 if rows % 1088 == 0 else 512)
    conv_b = p["conv_b"].reshape(1, -1)
    ncb = d_model // LANES
    outs = []
    streams = [(seq, 0)]
    if want_ctx:
        streams.append((ctx_len, (batch * seq) // ctx_len))
    for L, rb0 in streams:
        plan = _FftPlan(L)
        consts = plan.constants()
        hidden = hyena_hidden(L, p["f_w1"], p["f_b1"], p["f_w2"], p["f_b2"], p["sin_freq"])
        spectra = hyena_filter_spectra(plan, consts, hidden, p["f_w3"], d_model, n_order)
        z1 = hyena_conv(plan, consts, pr, 0, pr, ncb, p["conv_w"], conv_b, p["skip"], spectra,
                        order=0, conv_u=True, u_row_block0=rb0, gate_row_block0=rb0, batch=batch, d_model=d_model,
                        out_dtype=F32)
        z2 = hyena_conv(plan, consts, z1, 0, pr, 2 * ncb, p["conv_w"], conv_b, p["skip"], spectra,
                        order=1, conv_u=False, u_row_block0=0, gate_row_block0=rb0, batch=batch, d_model=d_model,
                        out_dtype=BF16)
        outs.append(z2)
    return outs[0] if len(outs) == 1 else jnp.concatenate(outs, 0)


def _headsum(x, e_ref):
    hi, lo = _split2(x)
    return _dot(hi, e_ref[...]) + _dot(lo, e_ref[...])


def _rw_mix_body(h_ref, hp_ref, hn_ref, mix_ref, o_ref, *, tm, lat_tiles, tiles_per_seq):
    i = pl.program_id(0)
    is_ctx = i >= lat_tiles
    first = jnp.logical_or(is_ctx, i % tiles_per_seq == 0)
    last = jnp.logical_or(is_ctx, i % tiles_per_seq == tiles_per_seq - 1)
    h = h_ref[...]
    rows = lax.broadcasted_iota(jnp.int32, h.shape, 0)
    pedge = jnp.where(first, 0.0, hp_ref[SUBLANES - 1:SUBLANES, :])
    nedge = jnp.where(last, 0.0, hn_ref[0:1, :])
    prev = jnp.where(rows == 0, pedge, pltpu.roll(h, 1, 0))
    nxt = jnp.where(rows == tm - 1, nedge, pltpu.roll(h, tm - 1, 0))
    dp = prev - h
    dn = nxt - h
    for n in range(o_ref.shape[0]):
        o_ref[n] = (h + dp * mix_ref[0, n:n + 1, :] + dn * mix_ref[1, n:n + 1, :]).astype(o_ref.dtype)


def rwkv_mix(h, mix, *, batch, seq, ctx_len):
    m, d = h.shape
    tm = ctx_len
    n_shift = mix.shape[1]
    nblk = m // SUBLANES
    per = tm // SUBLANES
    body = functools.partial(_rw_mix_body, tm=tm, lat_tiles=(batch * seq) // tm, tiles_per_seq=seq // tm)
    return pl.pallas_call(
        body,
        grid=(m // tm,),
        in_specs=[
            pl.BlockSpec((tm, d), lambda i: (i, 0)),
            pl.BlockSpec((SUBLANES, d), lambda i: (jnp.maximum(i * per - 1, 0), 0)),
            pl.BlockSpec((SUBLANES, d), lambda i: (jnp.minimum((i + 1) * per, nblk - 1), 0)),
            pl.BlockSpec((2, n_shift, d), lambda i: (0, 0, 0)),
        ],
        out_specs=pl.BlockSpec((n_shift, tm, d), lambda i: (0, i, 0)),
        out_shape=jax.ShapeDtypeStruct((n_shift, m, d), BF16),
        compiler_params=_cparams(("parallel",)),
        name="rwkv_mix",
    )(h, h, h, mix)


def _rw_prep_body(k_ref, lw0_ref, lw1_ref, al0_ref, al1_ref, w0_ref, a0_ref, kk_ref, ka_ref, e_ref,
                  olw0, olw1, okd0, okd1, obb0, obb1, oaa):
    k = k_ref[...]
    kk0 = k * kk_ref[...]
    kk = kk0 * lax.rsqrt(_headsum(kk0 * kk0, e_ref) + 1e-12)
    oaa[...] = -kk
    for d, (lw_ref, al_ref, olw, okd, obb) in enumerate(((lw0_ref, al0_ref, olw0, okd0, obb0),
                                                         (lw1_ref, al1_ref, olw1, okd1, obb1))):
        z = -(w0_ref[d:d + 1, :] + lw_ref[...])
        softplus = jnp.maximum(z, 0.0) + jnp.log(1.0 + jnp.exp(-jnp.abs(z)))
        log_w = -softplus - 0.5
        olw[...] = -jnp.exp(log_w)
        a = jax.nn.sigmoid(a0_ref[d:d + 1, :] + al_ref[...])
        okd[...] = k * (1.0 + (a - 1.0) * ka_ref[...])
        obb[...] = kk * a


def rwkv_prep(k, lw0, lw1, al0, al1, w0, a0, k_k, k_a, e128, tm=1088):
    m, d = k.shape
    blk = pl.BlockSpec((tm, LANES), lambda i, c: (i, c))
    vec2 = pl.BlockSpec((2, LANES), lambda i, c: (0, c))
    vec1 = pl.BlockSpec((1, LANES), lambda i, c: (0, c))
    sh = jax.ShapeDtypeStruct((m, d), F32)
    return pl.pallas_call(
        _rw_prep_body,
        grid=(m // tm, d // LANES),
        in_specs=[blk] * 5 + [vec2, vec2, vec1, vec1, pl.BlockSpec((LANES, LANES), lambda i, c: (0, 0))],
        out_specs=[blk] * 7,
        out_shape=[sh] * 7,
        compiler_params=_cparams(("parallel", "arbitrary")),
        name="rwkv_prep",
    )(k, lw0, lw1, al0, al1, w0, a0, k_k.reshape(1, d), k_a.reshape(1, d), e128)


def _b16(x):
    return x.astype(BF16)


def _wkv_body(r_ref, lw_ref, k_ref, v_ref, a_ref, b_ref, o_ref, s_sc, *, chunk, n_heads, reverse):
    c = pl.program_id(2)

    @pl.when(c == 0)
    def _():
        s_sc[...] = jnp.zeros_like(s_sc)

    C = chunk
    hs = range(n_heads)
    row = lax.broadcasted_iota(jnp.int32, (C, C), 0)
    col = lax.broadcasted_iota(jnp.int32, (C, C), 1)
    if reverse:
        tri_incl = col >= row
        tri_strict = col > row
        last = 0
    else:
        tri_incl = col <= row
        tri_strict = col < row
        last = C - 1
    same = lambda s: (row >> int(math.log2(s))) == (col >> int(math.log2(s)))
    eye = (row == col).astype(F32)
    cum_m = tri_incl.astype(BF16)
    sls = [slice(h * RW_HEAD, (h + 1) * RW_HEAD) for h in hs]

    lw = [lw_ref[:, sl] for sl in sls]
    cum = []
    for h in hs:
        l1, l2, l3 = _split3(lw[h])
        cum.append(_dot(cum_m, l1) + _dot(cum_m, l2) + _dot(cum_m, l3))
    p_inc = [jnp.exp(cum[h]) for h in hs]
    x1 = [jnp.concatenate([a_ref[:, sls[h]] * jnp.exp(cum[h] - lw[h]), r_ref[:, sls[h]] * p_inc[h]], axis=0) for h in hs]
    x2 = []
    for h in hs:
        p_inv = jnp.exp(-cum[h])
        x2.append(jnp.concatenate([b_ref[:, sls[h]] * p_inv, k_ref[:, sls[h]] * p_inv], axis=0))
    v = [v_ref[:, sl] for sl in sls]
    s0 = [s_sc[h] for h in hs]
    mfull = [_dot3(x1[h], x2[h], _dot_nt) for h in hs]
    w0 = [_dot3(x1[h], s0[h], _dot_nt) for h in hs]
    n_ab = [jnp.where(tri_strict, mfull[h][:C, :C], 0.0) for h in hs]
    a_ak = [_b16(jnp.where(tri_strict, mfull[h][:C, C:], 0.0)) for h in hs]
    a_r = [_b16(jnp.concatenate([jnp.where(tri_incl, mfull[h][C:, :C], 0.0),
                                 jnp.where(tri_incl, mfull[h][C:, C:], 0.0)], axis=1)) for h in hs]
    rhs = [w0[h][:C] + _dot(a_ak[h], _b16(v[h])) for h in hs]

    n4 = [jnp.where(same(4), n_ab[h], 0.0) for h in hs]
    n4b = [_b16(n4[h]) for h in hs]
    sq = [_dot(n4b[h], n4b[h]) for h in hs]
    t = [(eye + n4[h]) + _dot(_b16(eye + n4[h]), _b16(sq[h])) for h in hs]
    s = 4
    while s < C:
        off = jnp.logical_and(jnp.logical_and(same(2 * s), jnp.logical_not(same(s))), tri_strict)
        tb = [_b16(t[h]) for h in hs]
        tmp = [_dot(tb[h], _b16(jnp.where(off, n_ab[h], 0.0))) for h in hs]
        t = [t[h] + _dot(_b16(tmp[h]), tb[h]) for h in hs]
        s *= 2
    tb = [_b16(t[h]) for h in hs]
    u = [_dot(tb[h], _b16(rhs[h])) for h in hs]
    res = [rhs[h] - u[h] + _dot3(n_ab[h], u[h]) for h in hs]
    u = [u[h] + _dot(tb[h], _b16(res[h])) for h in hs]
    uv = [jnp.concatenate([u[h], v[h]], axis=0) for h in hs]
    y = [w0[h][C:] + _dot(a_r[h], _b16(uv[h])) for h in hs]
    for h in hs:
        s_sc[h] = (s0[h] + _dot3(uv[h], x2[h], _dot_tn)) * p_inc[h][last:last + 1, :]
    o_ref[...] = jnp.concatenate(y, axis=1)


def wkv7(r, lw, kd, v, aa, bb, *, reverse, batch, seq, ctx_len, chunk=128, heads_per_step=16):
    m, d = r.shape
    heads_per_step = min(heads_per_step, d // RW_HEAD)
    wl = heads_per_step * RW_HEAD
    n_ctx = ctx_len // chunk
    n_lat = seq // chunk
    n_chunks = n_ctx + n_lat
    ctx0 = (batch * seq) // chunk

    def row_block(b, c):
        if reverse:
            return jnp.where(c < n_ctx, ctx0 + b * n_ctx + (n_ctx - 1 - c), b * n_lat + (n_lat - 1 - (c - n_ctx)))
        return jnp.where(c < n_ctx, ctx0 + b * n_ctx + c, b * n_lat + (c - n_ctx))

    blk = pl.BlockSpec((chunk, wl), lambda b, g, c: (row_block(b, c), g))
    body = functools.partial(_wkv_body, chunk=chunk, n_heads=heads_per_step, reverse=reverse)
    return pl.pallas_call(
        body,
        grid=(batch, d // wl, n_chunks),
        in_specs=[blk] * 6,
        out_specs=blk,
        out_shape=jax.ShapeDtypeStruct((m, d), F32),
        scratch_shapes=[pltpu.VMEM((heads_per_step, RW_HEAD, RW_HEAD), F32)],
        compiler_params=_cparams(("parallel", "parallel", "arbitrary")),
        name="wkv7_rev" if reverse else "wkv7_fwd",
    )(r, lw, kd, v, aa, bb)


def _rw_readout_body(y0_ref, y1_ref, r_ref, v_ref, kd0_ref, kd1_ref, g_ref, rk_ref, lnw_ref, lnb_ref, e_ref, o_ref,
                     *, gn_eps):
    y = y0_ref[...] + y1_ref[...]
    inv = 1.0 / RW_HEAD
    mu = _headsum(y, e_ref) * inv
    dlt = y - mu
    var = _headsum(dlt * dlt, e_ref) * inv
    yn = dlt * lax.rsqrt(var + gn_eps) * lnw_ref[...] + lnb_ref[...]
    rk = r_ref[...] * rk_ref[...]
    bonus = (_headsum(rk * kd0_ref[...], e_ref) + _headsum(rk * kd1_ref[...], e_ref)) * v_ref[...]
    o_ref[...] = ((yn + bonus) * g_ref[...]).astype(o_ref.dtype)


def rwkv_readout(y0, y1, r, v, kd0, kd1, g, r_k, ln_w, ln_b, e128, gn_eps, rows, tm=512):
    d = r.shape[1]
    blk = pl.BlockSpec((tm, LANES), lambda i, c: (i, c))
    vec1 = pl.BlockSpec((1, LANES), lambda i, c: (0, c))
    return pl.pallas_call(
        functools.partial(_rw_readout_body, gn_eps=gn_eps),
        grid=(rows // tm, d // LANES),
        in_specs=[blk] * 7 + [vec1] * 3 + [pl.BlockSpec((LANES, LANES), lambda i, c: (0, 0))],
        out_specs=blk,
        out_shape=jax.ShapeDtypeStruct((rows, d), BF16),
        compiler_params=_cparams(("parallel", "arbitrary")),
        name="rwkv_readout",
    )(y0, y1, r, v, kd0, kd1, g, r_k.reshape(1, d), ln_w.reshape(1, d), ln_b.reshape(1, d), e128)


def rwkv_mixer(hmix, p, *, batch, seq, ctx_len, d_model):
    xs = rwkv_mix(hmix, p["mix"], batch=batch, seq=seq, ctx_len=ctx_len)
    m = hmix.shape[0]
    xs2 = xs.reshape(xs.shape[0] * m, d_model)
    tm = 1088 if m % 1088 == 0 else 512

    def proj(n, w, **kw):
        return mm(xs2, w, tm=tm, rows=m, x_row_block0=n * (m // tm), **kw)

    r = proj(0, p["w_r"], name="rw_r")
    k = proj(2, p["w_k"], name="rw_k")
    v = proj(3, p["w_v"], name="rw_v")
    g = mm(proj(5, p["g1"], act="sigmoid", out_dtype=BF16, name="rw_g1"), p["g2"], tm=tm, name="rw_g2")
    lws, als = [], []
    for d in range(2):
        tw = proj(1, p["w1"][d], act="tanh", out_dtype=BF16, name="rw_w1")
        lws.append(mm(tw, p["w2"][d], tm=tm, name="rw_w2"))
        ah = proj(4, p["a1"][d], out_dtype=BF16, name="rw_a1")
        als.append(mm(ah, p["a2"][d], tm=tm, name="rw_a2"))
    e128 = jnp.asarray(np.kron(np.eye(LANES // RW_HEAD), np.ones((RW_HEAD, RW_HEAD))), BF16)
    lw0, lw1, kd0, kd1, bb0, bb1, aa = rwkv_prep(k, lws[0], lws[1], als[0], als[1], p["w0"], p["a0"],
                                                 p["k_k"], p["k_a"], e128, tm=tm)
    kw = dict(batch=batch, seq=seq, ctx_len=ctx_len)
    y0 = wkv7(r, lw0, kd0, v, aa, bb0, reverse=False, **kw)
    y1 = wkv7(r, lw1, kd1, v, aa, bb1, reverse=True, **kw)
    return rwkv_readout(y0, y1, r, v, kd0, kd1, g, p["r_k"], p["ln_w"], p["ln_b"], e128,
                        1e-5 * RW_HEAD, rows=m)


def mla_mixer(hmix, p, *, batch, seq, ctx_len, want_ctx):
    m = hmix.shape[0]
    tm = 1088 if m % 1088 == 0 else 512
    scale = (MLA_NOPE + MLA_ROPE) ** -0.5 * LOG2E
    down = mm(hmix, p["w_down_ext"], tm=tm, name="mla_down")
    cos, sin = _rope_tables(seq, ctx_len, batch, MLA_ROPE)
    zeros = jnp.zeros_like(cos)
    cos_p = jnp.concatenate([cos, zeros], -1)
    sin_p = jnp.concatenate([sin, zeros], -1)
    q = proj_rope(down, p["w_uq_ext"], cos_p, sin_p, mode="mla_q", n_heads=MLA_HEADS, scale=scale,
                  norm_g=p["q_norm_g"], x_col_block=0, tm=tm)
    kcat, v = proj_rope(down, p["w_ukv"], cos_p, sin_p, mode="mla_kv", n_heads=MLA_HEADS,
                        norm_g=p["kv_norm_g"], x_col_block=1, pe=down,
                        pe_cb=(MLA_Q_RANK + MLA_KV_RANK) // LANES, tm=tm)
    kw = dict(dqk=2 * LANES, kv_group=1, n_heads=MLA_HEADS, batch=batch, seq=seq, ctx_len=ctx_len)
    o_lat = attention(q, kcat, v, lambda g: g, lat_queries=True, **kw)
    if not want_ctx:
        return o_lat
    o_ctx = attention(q, kcat, v, lambda g: g, lat_queries=False, **kw)
    return jnp.concatenate([o_lat, o_ctx], 0)


def gqa_mixer(hmix, p, *, batch, seq, ctx_len, want_ctx):
    m = hmix.shape[0]
    tm = 1088 if m % 1088 == 0 else 512
    scale = GQA_HEAD ** -0.5 * LOG2E
    cos, sin = _rope_tables(seq, ctx_len, batch, GQA_HEAD)
    nq, nk = GQA_HEADS, GQA_KV_HEADS
    q = proj_rope(hmix, p["w_q_ext"], cos, sin, mode="gqa", n_heads=nq, scale=scale,
                  gains=(p["q_norm_g"], p["q_norm_g_rot"]), tm=tm)
    k = proj_rope(hmix, p["w_k_ext"], cos, sin, mode="gqa", n_heads=nk,
                  gains=(p["k_norm_g"], p["k_norm_g_rot"]), tm=tm)
    vb = mm(hmix, p["w_v"], tm=tm, out_dtype=BF16, name="gqa_v")
    grp = nq // nk
    kw = dict(dqk=LANES, kv_group=grp, n_heads=nq, batch=batch, seq=seq, ctx_len=ctx_len)
    o_lat = attention(q, k, vb, lambda g: g, lat_queries=True, **kw)
    if not want_ctx:
        return o_lat
    o_ctx = attention(q, k, vb, lambda g: g, lat_queries=False, **kw)
    return jnp.concatenate([o_lat, o_ctx], 0)


def _rot_gain(g):
    q = g.shape[-1] // 4
    g4 = g.reshape(2, 2, q)
    return jnp.concatenate([g4[:, 1:2], g4[:, 0:1]], axis=1).reshape(g.shape)


def kernel(x, c, ctx, c_ctx, ada_w, ada_b, norm_g, ffn_w_gu, ffn_w_down, hy_w_in, hy_b_in, hy_conv_w, hy_conv_b, hy_f_w1, hy_f_b1, hy_f_w2, hy_f_b2, hy_f_w3, hy_sin_freq, hy_skip, hy_w_out, hy_b_out, rw_mix, rw_w_rkv, rw_w0, rw_w1, rw_w2, rw_a0, rw_a1, rw_a2, rw_g1, rw_g2, rw_k_k, rw_k_a, rw_r_k, rw_ln_w, rw_ln_b, rw_w_o, mla_w_down, mla_q_norm_g, mla_w_uq, mla_kv_norm_g, mla_w_ukv, mla_w_o, gqa_w_qkv, gqa_q_norm_g, gqa_k_norm_g, gqa_w_o, final_g):
    batch, seq, d = x.shape
    ctx_len = ctx.shape[1]
    depth = ada_w.shape[0]
    n_mix = 4
    rows_lat = batch * seq
    rows_all = rows_lat + batch * ctx_len
    bf = lambda a: a.astype(BF16)
    w_gu_all, w_down_all = bf(ffn_w_gu), bf(ffn_w_down)

    cvec = jnp.concatenate([c, c_ctx[None], jnp.zeros((8 - batch - 1, d), F32)], 0)
    mods_all = adaln_all(cvec, ada_w, ada_b)[:, :batch + 1].reshape(depth, batch + 1, N_MOD, d)

    h = jnp.concatenate([x.reshape(rows_lat, d), ctx.reshape(batch * ctx_len, d)], 0)

    for i in range(depth):
        mi, j = i % n_mix, i // n_mix
        ctx_out = any(jj % n_mix != 0 for jj in range(i + 1, depth))
        mods = mods_all[i]
        gate5 = mods[:, 5:6, :]
        mix_dtype = F32 if mi == 1 else BF16
        h, hmix = ffn_half(h, mods, norm_g[i], w_gu_all, w_down_all, layer=i, which=0, s=0, rows=rows_all,
                           seq_rows=seq, next_g=norm_g[i, 1], next_s=1, next_dtype=mix_dtype)
        rows_out = rows_all if ctx_out else rows_lat
        kw = dict(batch=batch, seq=seq, ctx_len=ctx_len)
        if mi == 0:
            p = dict(w_in=bf(hy_w_in[j]), b_in=hy_b_in[j], conv_w=hy_conv_w[j], conv_b=hy_conv_b[j],
                     f_w1=hy_f_w1[j], f_b1=hy_f_b1[j], f_w2=hy_f_w2[j], f_b2=hy_f_b2[j], f_w3=hy_f_w3[j],
                     sin_freq=hy_sin_freq[j], skip=hy_skip[j])
            z = hyena_mixer(hmix, p, d_model=d, want_ctx=ctx_out, **kw)
            w_o, b_o = bf(hy_w_out[j]), hy_b_out[j]
        elif mi == 1:
            pad = lambda w, ax: jnp.pad(w, [(0, RW_LORA_PAD - w.shape[a]) if a == ax else (0, 0) for a in range(w.ndim)])
            p = dict(mix=rw_mix[j], w_r=bf(rw_w_rkv[j, 0]), w_k=bf(rw_w_rkv[j, 1]), w_v=bf(rw_w_rkv[j, 2]),
                     g1=bf(rw_g1[j]), g2=bf(rw_g2[j]),
                     w1=bf(pad(rw_w1[j], 2)), w2=bf(pad(rw_w2[j], 1)), a1=bf(pad(rw_a1[j], 2)), a2=bf(pad(rw_a2[j], 1)),
                     w0=rw_w0[j], a0=rw_a0[j], k_k=rw_k_k[j], k_a=rw_k_a[j], r_k=rw_r_k[j].reshape(-1),
                     ln_w=rw_ln_w[j], ln_b=rw_ln_b[j])
            z = rwkv_mixer(hmix, p, d_model=d, **kw)
            w_o, b_o = bf(rw_w_o[j]), None
        elif mi == 2:
            wd = mla_w_down[j]
            pe = wd[:, MLA_Q_RANK + MLA_KV_RANK:]
            wuq = mla_w_uq[j].reshape(MLA_Q_RANK, MLA_HEADS, MLA_NOPE + MLA_ROPE)
            wq_pe = wuq[..., MLA_NOPE:]
            wuq_ext = jnp.concatenate([wuq[..., :MLA_NOPE], wq_pe,
                                       _rot_cols(wq_pe.reshape(MLA_Q_RANK, -1), MLA_ROPE).reshape(wq_pe.shape)], -1)
            p = dict(w_down_ext=bf(jnp.concatenate([wd, _rot_cols(pe, MLA_ROPE)], 1)),
                     w_uq_ext=bf(wuq_ext.reshape(MLA_Q_RANK, -1)), w_ukv=bf(mla_w_ukv[j]),
                     q_norm_g=mla_q_norm_g[j], kv_norm_g=mla_kv_norm_g[j])
            z = mla_mixer(hmix, p, want_ctx=ctx_out, **kw)
            w_o, b_o = bf(mla_w_o[j]), None
        else:
            w = gqa_w_qkv[j]
            nqc = GQA_HEADS * GQA_HEAD
            nkc = GQA_KV_HEADS * GQA_HEAD
            wq, wk, wv = w[:, :nqc], w[:, nqc:nqc + nkc], w[:, nqc + nkc:]

            def with_rot(wx, nh):
                both = jnp.stack([wx.reshape(d, nh, GQA_HEAD), _rot_cols(wx, GQA_HEAD).reshape(d, nh, GQA_HEAD)], 2)
                return bf(both.reshape(d, 2 * nh * GQA_HEAD))

            p = dict(w_q_ext=with_rot(wq, GQA_HEADS), w_k_ext=with_rot(wk, GQA_KV_HEADS), w_v=bf(wv),
                     q_norm_g=gqa_q_norm_g[j], q_norm_g_rot=_rot_gain(gqa_q_norm_g[j]),
                     k_norm_g=gqa_k_norm_g[j], k_norm_g_rot=_rot_gain(gqa_k_norm_g[j]))
            z = gqa_mixer(hmix, p, want_ctx=ctx_out, **kw)
            w_o, b_o = bf(gqa_w_o[j]), None
        h = mm(z, w_o, bias=b_o, res=h, gate=gate5, seq_rows=seq, rows=rows_out, tn=d, name="mix_out")
        last = i == depth - 1
        if last:
            h, out = ffn_half(h, mods, norm_g[i], w_gu_all, w_down_all, layer=i, which=1, s=2, rows=rows_out,
                              seq_rows=seq, next_g=final_g, next_s=None, next_dtype=F32)
        else:
            h = ffn_half(h, mods, norm_g[i], w_gu_all, w_down_all, layer=i, which=1, s=2, rows=rows_out,
                         seq_rows=seq)
    return out[:rows_lat].reshape(batch, seq, d)
```

```python
import functools
import math

import numpy as np
import jax
import jax.numpy as jnp
from jax import lax
from jax.experimental import pallas as pl
from jax.experimental.pallas import tpu as pltpu

F32 = jnp.float32
BF16 = jnp.bfloat16

VMEM_CAP_BYTES = 56 * 1024 * 1024
LANES = 128
SUBLANES = 8

NORM_EPS = 1e-6
LOG2E = 1.4426950408889634
N_MOD = 9
GRID_W = 64
ROPE_THETA = 10000.0
HY_DECAY_TARGET = 1e-2
HY_DECAY_PCT_SHORT = 0.3
HY_DECAY_PCT_LONG = 1.5
RW_HEAD = 64
RW_LORA_PAD = 128
GQA_HEAD = 128
GQA_HEADS = 16
GQA_KV_HEADS = 8
MLA_HEADS = 16
MLA_NOPE = 128
MLA_ROPE = 64
MLA_V = 128
MLA_Q_RANK = 512
MLA_KV_RANK = 512


def _cparams(sem):
    return pltpu.CompilerParams(dimension_semantics=sem, vmem_limit_bytes=VMEM_CAP_BYTES)


def _dot(a, b):
    return jnp.dot(a, b, preferred_element_type=F32)


def _dot_nt(a, b):
    return lax.dot_general(a, b, (((1,), (1,)), ((), ())), preferred_element_type=F32)


def _dot_tn(a, b):
    return lax.dot_general(a, b, (((0,), (0,)), ((), ())), preferred_element_type=F32)


def _split2(x):
    hi = x.astype(BF16)
    lo = (x - hi.astype(F32)).astype(BF16)
    return hi, lo


def _split3(x):
    hi = x.astype(BF16)
    r1 = x - hi.astype(F32)
    mid = r1.astype(BF16)
    lo = (r1 - mid.astype(F32)).astype(BF16)
    return hi, mid, lo


def _dot3(a, b, f=_dot):
    ah, al = _split2(a)
    bh, bl = _split2(b)
    return f(ah, bh) + f(ah, bl) + f(al, bh)


def _dotc3(mh, ml, d):
    dh, dl = _split2(d)
    return _dot(mh, dh) + _dot(ml, dh) + _dot(mh, dl)


def _dotc2(mh, ml, d):
    dh = d.astype(BF16)
    return _dot(mh, dh) + _dot(ml, dh)


def _rms(x, g):
    return x * lax.rsqrt(jnp.mean(x * x, -1, keepdims=True) + NORM_EPS) * g


def _pick(n, cands):
    for c in cands:
        if n % c == 0:
            return c
    raise ValueError(f"no tile for {n}")


def _adaln_body(c_ref, w_ref, b_ref, o_ref):
    c = c_ref[...]
    s = c * jax.nn.sigmoid(c)
    o_ref[0] = _dot(s.astype(BF16), w_ref[0].astype(BF16)) + b_ref[0]


def adaln_all(cvec8, ada_w, ada_b):
    depth, d, n = ada_w.shape
    tn = _pick(n, (2048, 1024, 512, 256, 128))
    return pl.pallas_call(
        _adaln_body,
        grid=(depth, n // tn),
        in_specs=[
            pl.BlockSpec((8, d), lambda l, j: (0, 0)),
            pl.BlockSpec((1, d, tn), lambda l, j: (l, 0, j)),
            pl.BlockSpec((1, 1, tn), lambda l, j: (l, 0, j)),
        ],
        out_specs=pl.BlockSpec((1, 8, tn), lambda l, j: (l, 0, j)),
        out_shape=jax.ShapeDtypeStruct((depth, 8, n), F32),
        compiler_params=_cparams(("arbitrary", "arbitrary")),
        name="adaln",
    )(cvec8, ada_w, ada_b.reshape(depth, 1, n))


def _mm_body(*refs, has_bias, act, has_norm, has_res, next_s):
    it = iter(refs)
    x_ref = next(it)
    w_ref = next(it)
    b_ref = next(it) if has_bias else None
    g_ref = next(it) if has_norm else None
    r_ref = next(it) if has_res else None
    gate_ref = next(it) if has_res else None
    gn_ref = next(it) if next_s is not None else None
    mod_ref = next(it) if next_s is not None else None
    o_ref = next(it)
    xn_ref = next(it) if next_s is not None else None
    x = x_ref[...]
    if has_norm:
        x = _rms(x.astype(F32), g_ref[...])
    acc = _dot(x.astype(BF16), w_ref[...])
    if has_bias:
        acc = acc + b_ref[...]
    if act == "sigmoid":
        acc = jax.nn.sigmoid(acc)
    elif act == "tanh":
        acc = jnp.tanh(acc)
    if has_res:
        acc = r_ref[...] + gate_ref[0] * acc
    o_ref[...] = acc.astype(o_ref.dtype)
    if next_s is not None:
        xn = _rms(acc, gn_ref[...])
        xn = xn * (1.0 + mod_ref[0, 3 * next_s + 1:3 * next_s + 2, :]) + mod_ref[0, 3 * next_s:3 * next_s + 1, :]
        xn_ref[...] = xn.astype(xn_ref.dtype)


def mm(x, w, *, bias=None, act=None, norm_g=None, res=None, gate=None, out_dtype=F32,
       tm=512, tn=None, rows=None, x_col_block=0, x_row_block0=0, seq_rows=None, next_norm=None, name="mm"):
    k, n = w.shape
    m = rows if rows is not None else x.shape[0]
    assert m % tm == 0, (m, tm)
    tn = tn or _pick(n, (512, 384, 256, 128))
    has_res = res is not None
    ins = [x, w]
    specs = [pl.BlockSpec((tm, k), lambda i, j: (i + x_row_block0, x_col_block)),
             pl.BlockSpec((k, tn), lambda i, j: (0, j))]
    if bias is not None:
        ins.append(bias.reshape(1, n).astype(F32))
        specs.append(pl.BlockSpec((1, tn), lambda i, j: (0, j)))
    if norm_g is not None:
        ins.append(norm_g.reshape(1, k).astype(F32))
        specs.append(pl.BlockSpec((1, k), lambda i, j: (0, 0)))
    if has_res:
        ngroups = gate.shape[0]
        per = seq_rows // tm
        ins += [res, gate]
        specs += [pl.BlockSpec((tm, tn), lambda i, j: (i, j)),
                  pl.BlockSpec((1, 1, tn), lambda i, j: (jnp.minimum(i // per, ngroups - 1), 0, j))]
    out_specs = pl.BlockSpec((tm, tn), lambda i, j: (i, j))
    out_shape = jax.ShapeDtypeStruct((m, n), out_dtype)
    next_s = None
    if next_norm is not None:
        assert has_res and tn == n
        gain, mods, next_s = next_norm
        ins += [gain.reshape(1, n), mods]
        specs += [pl.BlockSpec((1, n), lambda i, j: (0, 0)),
                  pl.BlockSpec((1, N_MOD, n), lambda i, j: (jnp.minimum(i // per, ngroups - 1), 0, 0))]
        out_specs = [out_specs, pl.BlockSpec((tm, n), lambda i, j: (i, 0))]
        out_shape = [out_shape, jax.ShapeDtypeStruct((m, n), BF16)]
    body = functools.partial(_mm_body, has_bias=bias is not None, act=act,
                             has_norm=norm_g is not None, has_res=has_res, next_s=next_s)
    return pl.pallas_call(
        body,
        grid=(m // tm, n // tn),
        in_specs=specs,
        out_specs=out_specs,
        out_shape=out_shape,
        compiler_params=_cparams(("parallel", "arbitrary")),
        name=name,
    )(*ins)


def _ffn_body(*refs, s, n_f, has_next, next_s, has_xn):
    it = iter(refs)
    x_ref, mod_ref, g_ref, wa_ref, wb_ref, wd_ref = (next(it) for _ in range(6))
    gn_ref = next(it) if has_next else None
    xin_ref = next(it) if has_xn else None
    o_ref = next(it)
    hn_ref = next(it) if has_next else None
    xn_sc = next(it)
    acc_sc = next(it)
    j = pl.program_id(1)

    @pl.when(j == 0)
    def _():
        if not has_xn:
            n = _rms(x_ref[...], g_ref[s:s + 1, :])
            n = n * (1.0 + mod_ref[0, 3 * s + 1:3 * s + 2, :]) + mod_ref[0, 3 * s:3 * s + 1, :]
            xn_sc[...] = n.astype(BF16)
        acc_sc[...] = jnp.zeros_like(acc_sc)

    xn = xin_ref[...] if has_xn else xn_sc[...]
    a = _dot(xn, wa_ref[...])
    b = _dot(xn, wb_ref[...])
    h = (a * jax.nn.sigmoid(a)) * b
    acc_sc[...] += _dot(h.astype(BF16), wd_ref[...])

    @pl.when(j == n_f - 1)
    def _():
        xnew = x_ref[...] + 0.5 * mod_ref[0, 3 * s + 2:3 * s + 3, :] * acc_sc[...]
        o_ref[...] = xnew
        if has_next:
            hn = _rms(xnew, gn_ref[...])
            if next_s is not None:
                hn = hn * (1.0 + mod_ref[0, 3 * next_s + 1:3 * next_s + 2, :]) + mod_ref[0, 3 * next_s:3 * next_s + 1, :]
            hn_ref[...] = hn.astype(hn_ref.dtype)


def ffn_half(x, mods, norm_g3, w_gu, w_down, *, layer, which, s, rows, seq_rows, next_g=None, next_s=None,
             next_dtype=F32, xn=None, tm=512, fc=512):
    d = x.shape[1]
    d_ff = w_down.shape[2]
    n_f = d_ff // fc
    assert d_ff % fc == 0 and rows % tm == 0
    per = seq_rows // tm
    ngroups = mods.shape[0]
    has_next = next_g is not None
    ins = [x, mods, norm_g3, w_gu, w_gu, w_down]
    specs = [
        pl.BlockSpec((tm, d), lambda i, j: (i, 0)),
        pl.BlockSpec((1, N_MOD, d), lambda i, j: (jnp.minimum(i // per, ngroups - 1), 0, 0)),
        pl.BlockSpec((3, d), lambda i, j: (0, 0)),
        pl.BlockSpec((None, None, d, fc), lambda i, j: (layer, which, 0, j)),
        pl.BlockSpec((None, None, d, fc), lambda i, j: (layer, which, 0, n_f + j)),
        pl.BlockSpec((None, None, fc, d), lambda i, j: (layer, which, j, 0)),
    ]
    out_shape = [jax.ShapeDtypeStruct((rows, d), F32)]
    out_specs = [pl.BlockSpec((tm, d), lambda i, j: (i, 0))]
    if has_next:
        ins.append(next_g.reshape(1, d))
        specs.append(pl.BlockSpec((1, d), lambda i, j: (0, 0)))
        out_shape.append(jax.ShapeDtypeStruct((rows, d), next_dtype))
        out_specs.append(pl.BlockSpec((tm, d), lambda i, j: (i, 0)))
    if xn is not None:
        ins.append(xn)
        specs.append(pl.BlockSpec((tm, d), lambda i, j: (i, 0)))
    body = functools.partial(_ffn_body, s=s, n_f=n_f, has_next=has_next, next_s=next_s, has_xn=xn is not None)
    outs = pl.pallas_call(
        body,
        grid=(rows // tm, n_f),
        in_specs=specs,
        out_specs=out_specs,
        out_shape=out_shape,
        scratch_shapes=[pltpu.VMEM((tm, d), BF16), pltpu.VMEM((tm, d), F32)],
        compiler_params=_cparams(("parallel", "arbitrary")),
        name="ffn_half",
    )(*ins)
    return outs if has_next else outs[0]


def _attn_body(*refs, use_lat, hp, n_kv, dqk):
    it = iter(refs)
    q_ref = next(it)
    kl_ref = next(it) if use_lat else None
    kc_ref = next(it)
    vl_refs = [next(it) for _ in range(n_kv)] if use_lat else None
    vc_refs = [next(it) for _ in range(n_kv)]
    o_ref = next(it)
    hs = range(hp)
    slot = [h * n_kv // hp for h in hs]
    q = [q_ref[:, h * dqk:(h + 1) * dqk] for h in hs]
    s_c = [_dot_nt(kc_ref[:, slot[h] * dqk:(slot[h] + 1) * dqk], q[h]) for h in hs]
    m = [jnp.max(s_c[h], 0, keepdims=True) for h in hs]
    if use_lat:
        s_l = [_dot_nt(kl_ref[:, slot[h] * dqk:(slot[h] + 1) * dqk], q[h]) for h in hs]
        m = [jnp.maximum(m[h], jnp.max(s_l[h], 0, keepdims=True)) for h in hs]
    p_c = [jnp.exp2(s_c[h] - m[h]) for h in hs]
    den = [jnp.sum(p_c[h], 0, keepdims=True) for h in hs]
    o = [_dot_tn(vc_refs[slot[h]][...], p_c[h].astype(BF16)) for h in hs]
    if use_lat:
        p_l = [jnp.exp2(s_l[h] - m[h]) for h in hs]
        den = [den[h] + jnp.sum(p_l[h], 0, keepdims=True) for h in hs]
        o = [o[h] + _dot_tn(vl_refs[slot[h]][...], p_l[h].astype(BF16)) for h in hs]
    for h in hs:
        o_ref[:, h * LANES:(h + 1) * LANES] = (o[h] / den[h]).T.astype(o_ref.dtype)


def attention(q_arr, k_arr, v_arr, v_cb, *, dqk, kv_group, n_heads, batch, seq, ctx_len, lat_queries, tq=512, hp=1):
    assert n_heads % hp == 0 and (kv_group % hp == 0 or hp % kv_group == 0)
    n_kv = max(hp // kv_group, 1)
    kv0 = lambda p: (p * hp) // kv_group
    ctx0 = (batch * seq) // ctx_len
    if lat_queries:
        tq = min(tq, seq)
        nq = seq // tq
        q_row = lambda b, t: b * nq + t
        o_row = lambda b, t: b * nq + t
        rows_q = batch * seq
    else:
        tq = ctx_len
        nq = 1
        q_row = lambda b, t: ctx0 + b
        o_row = lambda b, t: b
        rows_q = batch * ctx_len
    ins = [q_arr]
    specs = [pl.BlockSpec((tq, hp * dqk), lambda b, p, t: (q_row(b, t), p))]
    if lat_queries:
        ins.append(k_arr)
        specs.append(pl.BlockSpec((seq, n_kv * dqk), lambda b, p, t: (b, kv0(p) // n_kv)))
    ins.append(k_arr)
    specs.append(pl.BlockSpec((ctx_len, n_kv * dqk), lambda b, p, t: (ctx0 + b, kv0(p) // n_kv)))
    if lat_queries:
        for s in range(n_kv):
            ins.append(v_arr)
            specs.append(pl.BlockSpec((seq, LANES), lambda b, p, t, s=s: (b, v_cb(kv0(p) + s))))
    for s in range(n_kv):
        ins.append(v_arr)
        specs.append(pl.BlockSpec((ctx_len, LANES), lambda b, p, t, s=s: (ctx0 + b, v_cb(kv0(p) + s))))
    return pl.pallas_call(
        functools.partial(_attn_body, use_lat=lat_queries, hp=hp, n_kv=n_kv, dqk=dqk),
        grid=(batch, n_heads // hp, nq),
        in_specs=specs,
        out_specs=pl.BlockSpec((tq, hp * LANES), lambda b, p, t: (o_row(b, t), p)),
        out_shape=jax.ShapeDtypeStruct((rows_q, n_heads * LANES), BF16),
        compiler_params=_cparams(("parallel", "parallel", "arbitrary")),
        name="attention_lat" if lat_queries else "attention_ctx",
    )(*ins)


def _rope_tables(seq, ctx_len, batch, rot_dim):
    half = rot_dim // 4
    inv_freq = ROPE_THETA ** (-jnp.arange(half, dtype=F32) / half)
    t = jnp.arange(seq, dtype=jnp.int32)
    row = (t // GRID_W).astype(F32)
    col = (t % GRID_W).astype(F32)
    ang_r = row[:, None] * inv_freq[None]
    ang_c = col[:, None] * inv_freq[None]
    cos = jnp.concatenate([jnp.cos(ang_r), jnp.cos(ang_r), jnp.cos(ang_c), jnp.cos(ang_c)], -1)
    sin = jnp.concatenate([jnp.sin(ang_r), jnp.sin(ang_r), jnp.sin(ang_c), jnp.sin(ang_c)], -1)
    cos = jnp.concatenate([jnp.tile(cos, (batch, 1)), jnp.ones((batch * ctx_len, rot_dim), F32)], 0)
    sin = jnp.concatenate([jnp.tile(sin, (batch, 1)), jnp.zeros((batch * ctx_len, rot_dim), F32)], 0)
    return cos, sin


def _rot_cols(w, rot_dim):
    lead = w.shape[:-1]
    q = rot_dim // 4
    w5 = w.reshape(lead + (-1, 2, 2, q))
    rot = jnp.concatenate([-w5[..., 1:2, :], w5[..., 0:1, :]], axis=-2)
    return rot.reshape(w.shape)


def _rope_half(b, cos, sin):
    return b * cos + pltpu.roll(b, LANES // 2, 1) * sin


def _proj_rope_body(*refs, mode, has_norm, scale):
    it = iter(refs)
    x_ref, w_ref = next(it), next(it)
    ng_ref = next(it) if has_norm else None
    cos_ref, sin_ref = next(it), next(it)
    x = x_ref[...]
    if has_norm:
        x = _rms(x.astype(F32), ng_ref[...])
    acc = _dot(x.astype(BF16), w_ref[...])
    a, b = acc[:, :LANES], acc[:, LANES:]
    if mode == "gqa":
        g_ref, gr_ref, o_ref = next(it), next(it), next(it)
        r = lax.rsqrt(jnp.mean(a * a, -1, keepdims=True) + NORM_EPS) * scale
        o_ref[...] = (((a * g_ref[...]) * cos_ref[...] + (b * gr_ref[...]) * sin_ref[...]) * r).astype(o_ref.dtype)
    elif mode == "mla_q":
        o_ref = next(it)
        o_ref[:, :LANES] = (a * scale).astype(o_ref.dtype)
        o_ref[:, LANES:] = (_rope_half(b, cos_ref[...], sin_ref[...]) * scale).astype(o_ref.dtype)
    else:
        pe_ref, k_ref, v_ref = next(it), next(it), next(it)
        k_ref[:, :LANES] = a.astype(k_ref.dtype)
        k_ref[:, LANES:] = _rope_half(pe_ref[...], cos_ref[...], sin_ref[...]).astype(k_ref.dtype)
        v_ref[...] = b.astype(v_ref.dtype)


def proj_rope(x, w, cos, sin, *, mode, n_heads, scale=1.0, norm_g=None, x_col_block=0, gains=None,
              pe=None, pe_cb=0, tm=1088):
    m = x.shape[0]
    k = w.shape[0]
    ins = [x, w]
    specs = [pl.BlockSpec((tm, k), lambda i, h: (i, x_col_block)),
             pl.BlockSpec((k, 2 * LANES), lambda i, h: (0, h))]
    if norm_g is not None:
        ins.append(norm_g.reshape(1, k).astype(F32))
        specs.append(pl.BlockSpec((1, k), lambda i, h: (0, 0)))
    tab = pl.BlockSpec((tm, LANES), lambda i, h: (i, 0))
    ins += [cos, sin]
    specs += [tab, tab]
    if mode == "gqa":
        vec = pl.BlockSpec((1, LANES), lambda i, h: (0, 0))
        ins += [gains[0].reshape(1, LANES), gains[1].reshape(1, LANES)]
        specs += [vec, vec]
        out_shape = jax.ShapeDtypeStruct((m, n_heads * LANES), BF16)
        out_specs = pl.BlockSpec((tm, LANES), lambda i, h: (i, h))
    elif mode == "mla_q":
        out_shape = jax.ShapeDtypeStruct((m, n_heads * 2 * LANES), BF16)
        out_specs = pl.BlockSpec((tm, 2 * LANES), lambda i, h: (i, h))
    else:
        ins.append(pe)
        specs.append(pl.BlockSpec((tm, LANES), lambda i, h: (i, pe_cb)))
        out_shape = [jax.ShapeDtypeStruct((m, n_heads * 2 * LANES), BF16),
                     jax.ShapeDtypeStruct((m, n_heads * LANES), BF16)]
        out_specs = [pl.BlockSpec((tm, 2 * LANES), lambda i, h: (i, h)),
                     pl.BlockSpec((tm, LANES), lambda i, h: (i, h))]
    return pl.pallas_call(
        functools.partial(_proj_rope_body, mode=mode, has_norm=norm_g is not None, scale=scale),
        grid=(m // tm, n_heads),
        in_specs=specs,
        out_specs=out_specs,
        out_shape=out_shape,
        compiler_params=_cparams(("parallel", "arbitrary")),
        name="proj_" + mode,
    )(*ins)


class _FftPlan:
    def __init__(self, L):
        self.L = L
        self.N = 2 * L
        self.N2 = 128 if L >= 1024 else 64
        self.N1 = self.N // self.N2
        self.n1_in = self.N1 // 2
        self.K1 = self.N1 // 2 + 1
        self.nq = self.N2 // SUBLANES
        self.spec_rows = self.K1 * 2 * self.N2
        self.kf_cols = max(self.n1_in * SUBLANES, LANES)
        self.kf_rows = 2 * self.K1 * SUBLANES
        self.kb_cols = -(-self.kf_rows // LANES) * LANES
        self.kb_rows = self.n1_in * SUBLANES
        self.k1_unroll = 3 if self.K1 % 3 == 0 else 1

    def constants(self):
        N, N1, N2, K1, n1_in = self.N, self.N1, self.N2, self.K1, self.n1_in
        eye = np.eye(SUBLANES)
        k1 = np.arange(K1)[:, None]
        n1 = np.arange(n1_in)[None, :]
        th = 2 * np.pi * k1 * n1 / N1
        kf = np.zeros((self.kf_rows, self.kf_cols))
        kf[:K1 * SUBLANES, :n1_in * SUBLANES] = np.kron(np.cos(th), eye)
        kf[K1 * SUBLANES:, :n1_in * SUBLANES] = np.kron(-np.sin(th), eye)
        c = np.full((K1,), 2.0)
        c[0] = 1.0
        c[-1] = 1.0
        thb = 2 * np.pi * np.arange(n1_in)[:, None] * np.arange(K1)[None, :] / N1
        kb = np.zeros((self.kb_rows, self.kb_cols))
        kb[:, :K1 * SUBLANES] = np.kron(np.cos(thb) * c[None, :], eye)
        kb[:, K1 * SUBLANES:2 * K1 * SUBLANES] = np.kron(-np.sin(thb) * c[None, :], eye)
        a = 2 * np.pi * np.outer(np.arange(N2), np.arange(N2)) / N2
        C, S = np.cos(a), np.sin(a)
        w2f = np.block([[C, S], [-S, C]])
        w2i = np.block([[C, -S], [S, C]])
        tw_ang = 2 * np.pi * np.outer(np.arange(N2), np.arange(K1)) / N
        twr = np.repeat(np.cos(tw_ang), LANES, axis=1)
        twi = np.repeat(np.sin(tw_ang), LANES, axis=1)

        def hl(m):
            m32 = jnp.asarray(m, F32)
            hi = m32.astype(BF16)
            lo = (m32 - hi.astype(F32)).astype(BF16)
            return hi, lo

        out = {}
        for name, mat in (("kf", kf), ("kb", kb), ("w2f", w2f), ("w2i", w2i)):
            out[name + "_h"], out[name + "_l"] = hl(mat)
        out["twr"] = jnp.asarray(twr, F32)
        out["twi"] = jnp.asarray(twi, F32)
        return out


def _cmul(ar, ai, br, bi):
    return ar * br - ai * bi, ar * bi + ai * br


def _time_to_strided(plan, x_ref, xt_ref):
    N2, nq = plan.N2, plan.nq
    for n in range(plan.n1_in):
        xt_ref[n * SUBLANES:(n + 1) * SUBLANES, :] = jnp.concatenate(
            [x_ref[N2 * n + SUBLANES * q:N2 * n + SUBLANES * (q + 1), :] for q in range(nq)], axis=1)
    pad = plan.kf_cols - plan.n1_in * SUBLANES
    if pad:
        xt_ref[plan.n1_in * SUBLANES:, :] = jnp.zeros((pad, xt_ref.shape[1]), F32)


def _strided_to_spectrum(plan, a_ref, s_ref, c):
    N2, K1, nq = plan.N2, plan.K1, plan.nq
    for part in range(2):
        for k in range(K1):
            r0 = (part * K1 + k) * SUBLANES
            s_ref[part * N2:(part + 1) * N2, k * c:(k + 1) * c] = jnp.concatenate(
                [a_ref[r0:r0 + SUBLANES, q * c:(q + 1) * c] for q in range(nq)], axis=0)


def _spectrum_to_strided(plan, s_ref, b_ref, c):
    N2, K1, nq = plan.N2, plan.K1, plan.nq
    for part in range(2):
        for k in range(K1):
            r0 = (part * K1 + k) * SUBLANES
            b_ref[r0:r0 + SUBLANES, :] = jnp.concatenate(
                [s_ref[part * N2 + SUBLANES * q:part * N2 + SUBLANES * (q + 1), k * c:(k + 1) * c] for q in range(nq)],
                axis=1)
    pad = plan.kb_cols - plan.kf_rows
    if pad:
        b_ref[plan.kf_rows:, :] = jnp.zeros((pad, b_ref.shape[1]), F32)


def _strided_to_time(plan, yt, c):
    nq = plan.nq
    slabs = []
    for n in range(plan.n1_in):
        blk = yt[n * SUBLANES:(n + 1) * SUBLANES, :]
        slabs.extend(blk[:, q * c:(q + 1) * c] for q in range(nq))
    return jnp.concatenate(slabs, axis=0)


def _fft_fwd(plan, x_ref, consts, xt_ref, a_ref, s_ref, dotc):
    c = x_ref.shape[1]
    _time_to_strided(plan, x_ref, xt_ref)
    a_ref[:plan.kf_rows, :] = dotc(consts["kf_h"][...], consts["kf_l"][...], xt_ref[...])
    _strided_to_spectrum(plan, a_ref, s_ref, c)
    n2 = plan.N2
    tr, ti = consts["twr"][...], consts["twi"][...]
    xr, xi = _cmul(s_ref[:n2, :], s_ref[n2:, :], tr, -ti)
    s_ref[...] = dotc(consts["w2f_h"][...], consts["w2f_l"][...], jnp.concatenate([xr, xi], axis=0))


def _fft_conv_apply(plan, x_ref, g_ref, consts, xt_ref, a_ref, s_ref):
    c = x_ref.shape[1]
    n2 = plan.N2
    _fft_fwd(plan, x_ref, consts, xt_ref, a_ref, s_ref, _dotc2)
    pr, pi = _cmul(s_ref[:n2, :], s_ref[n2:, :], g_ref[:n2, :], g_ref[n2:, :])
    z = _dotc2(consts["w2i_h"][...], consts["w2i_l"][...], jnp.concatenate([pr, pi], axis=0))
    zr, zi = _cmul(z[:n2], z[n2:], consts["twr"][...], consts["twi"][...])
    s_ref[:n2, :] = zr
    s_ref[n2:, :] = zi
    _spectrum_to_strided(plan, s_ref, a_ref, c)
    yt = _dotc2(consts["kb_h"][...], consts["kb_l"][...], a_ref[...])
    return _strided_to_time(plan, yt, c)


def _fft_scratch(plan):
    wl = plan.nq * LANES
    return [pltpu.VMEM((plan.kf_cols, wl), F32),
            pltpu.VMEM((max(plan.kf_rows, plan.kb_cols), wl), F32),
            pltpu.VMEM((2 * plan.N2, plan.K1 * LANES), F32)]


_CONST_NAMES = ("kf_h", "kf_l", "kb_h", "kb_l", "w2f_h", "w2f_l", "w2i_h", "w2i_l", "twr", "twi")


def _const_specs(consts, nargs):
    ins, specs = [], []
    for nme in _CONST_NAMES:
        a = consts[nme]
        ins.append(a)
        if nargs == 1:
            specs.append(pl.BlockSpec(a.shape, lambda c: (0, 0), pipeline_mode=pl.Buffered(1)))
        else:
            specs.append(pl.BlockSpec(a.shape, lambda c, b: (0, 0), pipeline_mode=pl.Buffered(1)))
    return ins, specs


def _hy_hidden_body(z_ref, w1_ref, b1_ref, w2_ref, b2_ref, sf_ref, o_ref):
    h = jnp.sin(sf_ref[0:1, :] * (_dot3(z_ref[...], w1_ref[...]) + b1_ref[...]))
    h = jnp.sin(sf_ref[1:2, :] * (_dot3(h, w2_ref[...]) + b2_ref[...]))
    o_ref[...] = h


def hyena_hidden(L, f_w1, f_b1, f_w2, f_b2, sin_freq):
    emb, hid = f_w1.shape
    bands = (emb - 1) // 2
    t = jnp.linspace(0.0, 1.0, L, dtype=F32)[:, None]
    w = (2.0 * math.pi / L) * jnp.arange(L, dtype=F32)[:, None]
    f = jnp.linspace(1e-4, bands - 1, bands, dtype=F32)[None]
    z = jnp.concatenate([t, jnp.cos(f * w), -jnp.sin(f * w)], -1)
    embp = -(-emb // LANES) * LANES
    z = jnp.pad(z, ((0, 0), (0, embp - emb)))
    w1 = jnp.pad(f_w1, ((0, embp - emb), (0, 0)))
    ins = [z, w1, f_b1.reshape(1, hid), f_w2, f_b2.reshape(1, hid), sin_freq]
    return pl.pallas_call(
        _hy_hidden_body,
        out_shape=jax.ShapeDtypeStruct((L, hid), F32),
        name="hyena_hidden",
    )(*ins)


def _hy_filter_body(h_ref, wf_ref, wb_ref, dl_ref, *rest, plan):
    consts = {nme: r for nme, r in zip(_CONST_NAMES, rest)}
    g_ref, x_sc, xt_sc, a_sc, s_sc, sb_sc = rest[len(_CONST_NAMES):]
    L, n2 = plan.L, plan.N2
    t = lax.broadcasted_iota(jnp.int32, (L, LANES), 0).astype(F32) * (1.0 / (L - 1))
    win = jnp.exp(-t * dl_ref[...])
    row0 = lax.broadcasted_iota(jnp.int32, (L, LANES), 0) == 0
    h = h_ref[...]
    fwd = _dot3(h, wf_ref[...]) * win
    bwd = jnp.where(row0, 0.0, _dot3(h, wb_ref[...]) * win)
    ss = jnp.sum(fwd * fwd, 0, keepdims=True) + jnp.sum(bwd * bwd, 0, keepdims=True)
    nu = lax.rsqrt(ss + 1e-12) * (1.0 / plan.N)
    x_sc[...] = bwd * nu
    _fft_fwd(plan, x_sc, consts, xt_sc, a_sc, sb_sc, _dotc2)
    x_sc[...] = fwd * nu
    _fft_fwd(plan, x_sc, consts, xt_sc, a_sc, s_sc, _dotc2)
    g_ref[:n2, :] = s_sc[:n2, :] + sb_sc[:n2, :]
    g_ref[n2:, :] = s_sc[n2:, :] - sb_sc[n2:, :]


def hyena_filter_spectra(plan, consts, hidden, f_w3, d_model, n_order):
    L = plan.L
    hid = hidden.shape[1]
    ncb = d_model // LANES
    deltas = jnp.abs(jnp.linspace(math.log(HY_DECAY_TARGET) / HY_DECAY_PCT_SHORT,
                                  math.log(HY_DECAY_TARGET) / HY_DECAY_PCT_LONG, d_model, dtype=F32)).reshape(1, d_model)
    ins = [hidden, f_w3, f_w3, deltas]
    specs = [pl.BlockSpec((L, hid), lambda c, n: (0, 0)),
             pl.BlockSpec((hid, LANES), lambda c, n: (0, (2 * n) * ncb + c)),
             pl.BlockSpec((hid, LANES), lambda c, n: (0, (2 * n + 1) * ncb + c)),
             pl.BlockSpec((1, LANES), lambda c, n: (0, c))]
    ci, cs = _const_specs(consts, 2)
    ins += ci
    specs += cs
    srows, scols = 2 * plan.N2, plan.K1 * LANES
    return pl.pallas_call(
        functools.partial(_hy_filter_body, plan=plan),
        grid=(ncb, n_order),
        in_specs=specs,
        out_specs=pl.BlockSpec((None, None, srows, scols), lambda c, n: (n, c, 0, 0)),
        out_shape=jax.ShapeDtypeStruct((n_order, ncb, srows, scols), F32),
        scratch_shapes=[pltpu.VMEM((L, LANES), F32)] + _fft_scratch(plan) + [pltpu.VMEM((srows, scols), F32)],
        compiler_params=_cparams(("arbitrary", "arbitrary")),
        name="hyena_filter",
    )(*ins)


def _conv3(p, w_ref, b_ref, L):
    rows = lax.broadcasted_iota(jnp.int32, p.shape, 0)
    prev = jnp.where(rows == 0, 0.0, pltpu.roll(p, 1, 0))
    nxt = jnp.where(rows == L - 1, 0.0, pltpu.roll(p, L - 1, 0))
    return prev * w_ref[0:1, :] + p * w_ref[1:2, :] + nxt * w_ref[2:3, :] + b_ref[...]


def _hy_conv_body(*refs, plan, order, conv_u):
    it = iter(refs)
    u_ref, gate_ref = next(it), next(it)
    if conv_u:
        cwu_ref, cbu_ref = next(it), next(it)
    cwg_ref, cbg_ref = next(it), next(it)
    skip_ref = next(it)
    g_ref = next(it)
    consts = {nme: next(it) for nme in _CONST_NAMES}
    o_ref = next(it)
    x_sc, xt_sc, a_sc, s_sc = (next(it) for _ in range(4))
    L = plan.L
    u = u_ref[...]
    if conv_u:
        u = _conv3(u, cwu_ref, cbu_ref, L)
    x_sc[...] = u
    y = _fft_conv_apply(plan, x_sc, g_ref, consts, xt_sc, a_sc, s_sc)
    gate = _conv3(gate_ref[...], cwg_ref, cbg_ref, L)
    o_ref[...] = (gate * (y + x_sc[...] * skip_ref[order:order + 1, :])).astype(o_ref.dtype)


def hyena_conv(plan, consts, u_arr, u_cb0, gate_arr, gate_cb0, conv_w, conv_b, skip, spectra, *,
               order, conv_u, u_row_block0, gate_row_block0, batch, d_model, out_dtype):
    L = plan.L
    ncb = d_model // LANES
    ins = [u_arr, gate_arr]
    specs = [pl.BlockSpec((L, LANES), lambda c, b: (u_row_block0 + b, u_cb0 + c)),
             pl.BlockSpec((L, LANES), lambda c, b: (gate_row_block0 + b, gate_cb0 + c))]
    if conv_u:
        ins += [conv_w, conv_b]
        specs += [pl.BlockSpec((3, LANES), lambda c, b: (0, u_cb0 + c)),
                  pl.BlockSpec((1, LANES), lambda c, b: (0, u_cb0 + c))]
    srows, scols = 2 * plan.N2, plan.K1 * LANES
    ins += [conv_w, conv_b, skip, spectra]
    specs += [pl.BlockSpec((3, LANES), lambda c, b: (0, gate_cb0 + c)),
              pl.BlockSpec((1, LANES), lambda c, b: (0, gate_cb0 + c)),
              pl.BlockSpec((skip.shape[0], LANES), lambda c, b: (0, c)),
              pl.BlockSpec((None, None, srows, scols), lambda c, b: (order, c, 0, 0))]
    ci, cs = _const_specs(consts, 2)
    ins += ci
    specs += cs
    return pl.pallas_call(
        functools.partial(_hy_conv_body, plan=plan, order=order, conv_u=conv_u),
        grid=(ncb, batch),
        in_specs=specs,
        out_specs=pl.BlockSpec((L, LANES), lambda c, b: (b, c)),
        out_shape=jax.ShapeDtypeStruct((batch * L, d_model), out_dtype),
        scratch_shapes=[pltpu.VMEM((L, LANES), F32)] + _fft_scratch(plan),
        compiler_params=_cparams(("parallel", "arbitrary")),
        name=f"hyena_conv{order}_L{L}",
    )(*ins)


def hyena_mixer(hmix, p, *, batch, seq, ctx_len, d_model, want_ctx):
    n_order = p["skip"].shape[0]
    rows = batch * seq + (batch * ctx_len if want_ctx else 0)
    pr = mm(hmix, p["w_in"], bias=p["b_in"], name="hy_in", rows=rows, tm=1088 if rows % 1088 == 0 else 512)
    conv_b = p["conv_b"].reshape(1, -1)
    ncb = d_model // LANES
    outs = []
    streams = [(seq, 0)]
    if want_ctx:
        streams.append((ctx_len, (batch * seq) // ctx_len))
    for L, rb0 in streams:
        plan = _FftPlan(L)
        consts = plan.constants()
        hidden = hyena_hidden(L, p["f_w1"], p["f_b1"], p["f_w2"], p["f_b2"], p["sin_freq"])
        spectra = hyena_filter_spectra(plan, consts, hidden, p["f_w3"], d_model, n_order)
        z1 = hyena_conv(plan, consts, pr, 0, pr, ncb, p["conv_w"], conv_b, p["skip"], spectra,
                        order=0, conv_u=True, u_row_block0=rb0, gate_row_block0=rb0, batch=batch, d_model=d_model,
                        out_dtype=F32)
        z2 = hyena_conv(plan, consts, z1, 0, pr, 2 * ncb, p["conv_w"], conv_b, p["skip"], spectra,
                        order=1, conv_u=False, u_row_block0=0, gate_row_block0=rb0, batch=batch, d_model=d_model,
                        out_dtype=BF16)
        outs.append(z2)
    return outs[0] if len(outs) == 1 else jnp.concatenate(outs, 0)


def _headsum(x, e_ref):
    hi, lo = _split2(x)
    return _dot(hi, e_ref[...]) + _dot(lo, e_ref[...])


def _rw_mix_body(h_ref, hp_ref, hn_ref, mix_ref, o_ref, *, tm, lat_tiles, tiles_per_seq):
    i = pl.program_id(0)
    is_ctx = i >= lat_tiles
    first = jnp.logical_or(is_ctx, i % tiles_per_seq == 0)
    last = jnp.logical_or(is_ctx, i % tiles_per_seq == tiles_per_seq - 1)
    h = h_ref[...]
    rows = lax.broadcasted_iota(jnp.int32, h.shape, 0)
    pedge = jnp.where(first, 0.0, hp_ref[SUBLANES - 1:SUBLANES, :])
    nedge = jnp.where(last, 0.0, hn_ref[0:1, :])
    prev = jnp.where(rows == 0, pedge, pltpu.roll(h, 1, 0))
    nxt = jnp.where(rows == tm - 1, nedge, pltpu.roll(h, tm - 1, 0))
    dp = prev - h
    dn = nxt - h
    for n in range(o_ref.shape[0]):
        o_ref[n] = (h + dp * mix_ref[0, n:n + 1, :] + dn * mix_ref[1, n:n + 1, :]).astype(o_ref.dtype)


def rwkv_mix(h, mix, *, batch, seq, ctx_len):
    m, d = h.shape
    tm = ctx_len
    n_shift = mix.shape[1]
    nblk = m // SUBLANES
    per = tm // SUBLANES
    body = functools.partial(_rw_mix_body, tm=tm, lat_tiles=(batch * seq) // tm, tiles_per_seq=seq // tm)
    return pl.pallas_call(
        body,
        grid=(m // tm,),
        in_specs=[
            pl.BlockSpec((tm, d), lambda i: (i, 0)),
            pl.BlockSpec((SUBLANES, d), lambda i: (jnp.maximum(i * per - 1, 0), 0)),
            pl.BlockSpec((SUBLANES, d), lambda i: (jnp.minimum((i + 1) * per, nblk - 1), 0)),
            pl.BlockSpec((2, n_shift, d), lambda i: (0, 0, 0)),
        ],
        out_specs=pl.BlockSpec((n_shift, tm, d), lambda i: (0, i, 0)),
        out_shape=jax.ShapeDtypeStruct((n_shift, m, d), BF16),
        compiler_params=_cparams(("parallel",)),
        name="rwkv_mix",
    )(h, h, h, mix)


def _rw_prep_body(k_ref, lw0_ref, lw1_ref, al0_ref, al1_ref, w0_ref, a0_ref, kk_ref, ka_ref, e_ref,
                  olw0, olw1, okd0, okd1, obb0, obb1, oaa):
    k = k_ref[...]
    kk0 = k * kk_ref[...]
    kk = kk0 * lax.rsqrt(_headsum(kk0 * kk0, e_ref) + 1e-12)
    oaa[...] = -kk
    for d, (lw_ref, al_ref, olw, okd, obb) in enumerate(((lw0_ref, al0_ref, olw0, okd0, obb0),
                                                         (lw1_ref, al1_ref, olw1, okd1, obb1))):
        z = -(w0_ref[d:d + 1, :] + lw_ref[...])
        softplus = jnp.maximum(z, 0.0) + jnp.log(1.0 + jnp.exp(-jnp.abs(z)))
        log_w = -softplus - 0.5
        olw[...] = -jnp.exp(log_w)
        a = jax.nn.sigmoid(a0_ref[d:d + 1, :] + al_ref[...])
        okd[...] = k * (1.0 + (a - 1.0) * ka_ref[...])
        obb[...] = kk * a


def rwkv_prep(k, lw0, lw1, al0, al1, w0, a0, k_k, k_a, e128, tm=1088):
    m, d = k.shape
    blk = pl.BlockSpec((tm, LANES), lambda i, c: (i, c))
    vec2 = pl.BlockSpec((2, LANES), lambda i, c: (0, c))
    vec1 = pl.BlockSpec((1, LANES), lambda i, c: (0, c))
    sh = jax.ShapeDtypeStruct((m, d), F32)
    return pl.pallas_call(
        _rw_prep_body,
        grid=(m // tm, d // LANES),
        in_specs=[blk] * 5 + [vec2, vec2, vec1, vec1, pl.BlockSpec((LANES, LANES), lambda i, c: (0, 0))],
        out_specs=[blk] * 7,
        out_shape=[sh] * 7,
        compiler_params=_cparams(("parallel", "arbitrary")),
        name="rwkv_prep",
    )(k, lw0, lw1, al0, al1, w0, a0, k_k.reshape(1, d), k_a.reshape(1, d), e128)


def _b16(x):
    return x.astype(BF16)


def _wkv_body(r_ref, lw_ref, k_ref, v_ref, a_ref, b_ref, o_ref, s_sc, *, chunk, n_heads, reverse):
    c = pl.program_id(2)

    @pl.when(c == 0)
    def _():
        s_sc[...] = jnp.zeros_like(s_sc)

    C = chunk
    hs = range(n_heads)
    row = lax.broadcasted_iota(jnp.int32, (C, C), 0)
    col = lax.broadcasted_iota(jnp.int32, (C, C), 1)
    if reverse:
        tri_incl = col >= row
        tri_strict = col > row
        last = 0
    else:
        tri_incl = col <= row
        tri_strict = col < row
        last = C - 1
    same = lambda s: (row >> int(math.log2(s))) == (col >> int(math.log2(s)))
    eye = (row == col).astype(F32)
    cum_m = tri_incl.astype(BF16)
    sls = [slice(h * RW_HEAD, (h + 1) * RW_HEAD) for h in hs]

    lw = [lw_ref[:, sl] for sl in sls]
    cum = []
    for h in hs:
        l1, l2, l3 = _split3(lw[h])
        cum.append(_dot(cum_m, l1) + _dot(cum_m, l2) + _dot(cum_m, l3))
    p_inc = [jnp.exp(cum[h]) for h in hs]
    x1 = [jnp.concatenate([a_ref[:, sls[h]] * jnp.exp(cum[h] - lw[h]), r_ref[:, sls[h]] * p_inc[h]], axis=0) for h in hs]
    x2 = []
    for h in hs:
        p_inv = jnp.exp(-cum[h])
        x2.append(jnp.concatenate([b_ref[:, sls[h]] * p_inv, k_ref[:, sls[h]] * p_inv], axis=0))
    v = [v_ref[:, sl] for sl in sls]
    s0 = [s_sc[h] for h in hs]
    mfull = [_dot3(x1[h], x2[h], _dot_nt) for h in hs]
    w0 = [_dot3(x1[h], s0[h], _dot_nt) for h in hs]
    n_ab = [jnp.where(tri_strict, mfull[h][:C, :C], 0.0) for h in hs]
    a_ak = [_b16(jnp.where(tri_strict, mfull[h][:C, C:], 0.0)) for h in hs]
    a_r = [_b16(jnp.concatenate([jnp.where(tri_incl, mfull[h][C:, :C], 0.0),
                                 jnp.where(tri_incl, mfull[h][C:, C:], 0.0)], axis=1)) for h in hs]
    rhs = [w0[h][:C] + _dot(a_ak[h], _b16(v[h])) for h in hs]

    n4 = [jnp.where(same(4), n_ab[h], 0.0) for h in hs]
    n4b = [_b16(n4[h]) for h in hs]
    sq = [_dot(n4b[h], n4b[h]) for h in hs]
    t = [(eye + n4[h]) + _dot(_b16(eye + n4[h]), _b16(sq[h])) for h in hs]
    s = 4
    while s < C:
        off = jnp.logical_and(jnp.logical_and(same(2 * s), jnp.logical_not(same(s))), tri_strict)
        tb = [_b16(t[h]) for h in hs]
        tmp = [_dot(tb[h], _b16(jnp.where(off, n_ab[h], 0.0))) for h in hs]
        t = [t[h] + _dot(_b16(tmp[h]), tb[h]) for h in hs]
        s *= 2
    tb = [_b16(t[h]) for h in hs]
    u = [_dot(tb[h], _b16(rhs[h])) for h in hs]
    res = [rhs[h] - u[h] + _dot3(n_ab[h], u[h]) for h in hs]
    u = [u[h] + _dot(tb[h], _b16(res[h])) for h in hs]
    uv = [jnp.concatenate([u[h], v[h]], axis=0) for h in hs]
    y = [w0[h][C:] + _dot(a_r[h], _b16(uv[h])) for h in hs]
    for h in hs:
        s_sc[h] = (s0[h] + _dot3(uv[h], x2[h], _dot_tn)) * p_inc[h][last:last + 1, :]
    o_ref[...] = jnp.concatenate(y, axis=1)


def wkv7(r, lw, kd, v, aa, bb, *, reverse, batch, seq, ctx_len, chunk=128, heads_per_step=16):
    m, d = r.shape
    heads_per_step = min(heads_per_step, d // RW_HEAD)
    wl = heads_per_step * RW_HEAD
    n_ctx = ctx_len // chunk
    n_lat = seq // chunk
    n_chunks = n_ctx + n_lat
    ctx0 = (batch * seq) // chunk

    def row_block(b, c):
        if reverse:
            return jnp.where(c < n_ctx, ctx0 + b * n_ctx + (n_ctx - 1 - c), b * n_lat + (n_lat - 1 - (c - n_ctx)))
        return jnp.where(c < n_ctx, ctx0 + b * n_ctx + c, b * n_lat + (c - n_ctx))

    blk = pl.BlockSpec((chunk, wl), lambda b, g, c: (row_block(b, c), g))
    body = functools.partial(_wkv_body, chunk=chunk, n_heads=heads_per_step, reverse=reverse)
    return pl.pallas_call(
        body,
        grid=(batch, d // wl, n_chunks),
        in_specs=[blk] * 6,
        out_specs=blk,
        out_shape=jax.ShapeDtypeStruct((m, d), F32),
        scratch_shapes=[pltpu.VMEM((heads_per_step, RW_HEAD, RW_HEAD), F32)],
        compiler_params=_cparams(("parallel", "parallel", "arbitrary")),
        name="wkv7_rev" if reverse else "wkv7_fwd",
    )(r, lw, kd, v, aa, bb)


def _rw_readout_body(y0_ref, y1_ref, r_ref, v_ref, kd0_ref, kd1_ref, g_ref, rk_ref, lnw_ref, lnb_ref, e_ref, o_ref,
                     *, gn_eps):
    y = y0_ref[...] + y1_ref[...]
    inv = 1.0 / RW_HEAD
    mu = _headsum(y, e_ref) * inv
    dlt = y - mu
    var = _headsum(dlt * dlt, e_ref) * inv
    yn = dlt * lax.rsqrt(var + gn_eps) * lnw_ref[...] + lnb_ref[...]
    rk = r_ref[...] * rk_ref[...]
    bonus = (_headsum(rk * kd0_ref[...], e_ref) + _headsum(rk * kd1_ref[...], e_ref)) * v_ref[...]
    o_ref[...] = ((yn + bonus) * g_ref[...]).astype(o_ref.dtype)


def rwkv_readout(y0, y1, r, v, kd0, kd1, g, r_k, ln_w, ln_b, e128, gn_eps, rows, tm=512):
    d = r.shape[1]
    blk = pl.BlockSpec((tm, LANES), lambda i, c: (i, c))
    vec1 = pl.BlockSpec((1, LANES), lambda i, c: (0, c))
    return pl.pallas_call(
        functools.partial(_rw_readout_body, gn_eps=gn_eps),
        grid=(rows // tm, d // LANES),
        in_specs=[blk] * 7 + [vec1] * 3 + [pl.BlockSpec((LANES, LANES), lambda i, c: (0, 0))],
        out_specs=blk,
        out_shape=jax.ShapeDtypeStruct((rows, d), BF16),
        compiler_params=_cparams(("parallel", "arbitrary")),
        name="rwkv_readout",
    )(y0, y1, r, v, kd0, kd1, g, r_k.reshape(1, d), ln_w.reshape(1, d), ln_b.reshape(1, d), e128)


def rwkv_mixer(hmix, p, *, batch, seq, ctx_len, d_model):
    xs = rwkv_mix(hmix, p["mix"], batch=batch, seq=seq, ctx_len=ctx_len)
    m = hmix.shape[0]
    xs2 = xs.reshape(xs.shape[0] * m, d_model)
    tm = 1088 if m % 1088 == 0 else 512

    def proj(n, w, **kw):
        return mm(xs2, w, tm=tm, rows=m, x_row_block0=n * (m // tm), **kw)

    r = proj(0, p["w_r"], name="rw_r")
    k = proj(2, p["w_k"], name="rw_k")
    v = proj(3, p["w_v"], name="rw_v")
    g = mm(proj(5, p["g1"], act="sigmoid", out_dtype=BF16, name="rw_g1"), p["g2"], tm=tm, name="rw_g2")
    lws, als = [], []
    for d in range(2):
        tw = proj(1, p["w1"][d], act="tanh", out_dtype=BF16, name="rw_w1")
        lws.append(mm(tw, p["w2"][d], tm=tm, name="rw_w2"))
        ah = proj(4, p["a1"][d], out_dtype=BF16, name="rw_a1")
        als.append(mm(ah, p["a2"][d], tm=tm, name="rw_a2"))
    e128 = jnp.asarray(np.kron(np.eye(LANES // RW_HEAD), np.ones((RW_HEAD, RW_HEAD))), BF16)
    lw0, lw1, kd0, kd1, bb0, bb1, aa = rwkv_prep(k, lws[0], lws[1], als[0], als[1], p["w0"], p["a0"],
                                                 p["k_k"], p["k_a"], e128, tm=tm)
    kw = dict(batch=batch, seq=seq, ctx_len=ctx_len)
    y0 = wkv7(r, lw0, kd0, v, aa, bb0, reverse=False, **kw)
    y1 = wkv7(r, lw1, kd1, v, aa, bb1, reverse=True, **kw)
    return rwkv_readout(y0, y1, r, v, kd0, kd1, g, p["r_k"], p["ln_w"], p["ln_b"], e128,
                        1e-5 * RW_HEAD, rows=m)


def mla_mixer(hmix, p, *, batch, seq, ctx_len, want_ctx):
    m = hmix.shape[0]
    tm = 1088 if m % 1088 == 0 else 512
    scale = (MLA_NOPE + MLA_ROPE) ** -0.5 * LOG2E
    down = mm(hmix, p["w_down_ext"], tm=tm, name="mla_down")
    cos, sin = _rope_tables(seq, ctx_len, batch, MLA_ROPE)
    zeros = jnp.zeros_like(cos)
    cos_p = jnp.concatenate([cos, zeros], -1)
    sin_p = jnp.concatenate([sin, zeros], -1)
    q = proj_rope(down, p["w_uq_ext"], cos_p, sin_p, mode="mla_q", n_heads=MLA_HEADS, scale=scale,
                  norm_g=p["q_norm_g"], x_col_block=0, tm=tm)
    kcat, v = proj_rope(down, p["w_ukv"], cos_p, sin_p, mode="mla_kv", n_heads=MLA_HEADS,
                        norm_g=p["kv_norm_g"], x_col_block=1, pe=down,
                        pe_cb=(MLA_Q_RANK + MLA_KV_RANK) // LANES, tm=tm)
    kw = dict(dqk=2 * LANES, kv_group=1, n_heads=MLA_HEADS, batch=batch, seq=seq, ctx_len=ctx_len)
    o_lat = attention(q, kcat, v, lambda g: g, lat_queries=True, **kw)
    if not want_ctx:
        return o_lat
    o_ctx = attention(q, kcat, v, lambda g: g, lat_queries=False, **kw)
    return jnp.concatenate([o_lat, o_ctx], 0)


def gqa_mixer(hmix, p, *, batch, seq, ctx_len, want_ctx):
    m = hmix.shape[0]
    tm = 1088 if m % 1088 == 0 else 512
    scale = GQA_HEAD ** -0.5 * LOG2E
    cos, sin = _rope_tables(seq, ctx_len, batch, GQA_HEAD)
    nq, nk = GQA_HEADS, GQA_KV_HEADS
    q = proj_rope(hmix, p["w_q_ext"], cos, sin, mode="gqa", n_heads=nq, scale=scale,
                  gains=(p["q_norm_g"], p["q_norm_g_rot"]), tm=tm)
    k = proj_rope(hmix, p["w_k_ext"], cos, sin, mode="gqa", n_heads=nk,
                  gains=(p["k_norm_g"], p["k_norm_g_rot"]), tm=tm)
    vb = mm(hmix, p["w_v"], tm=tm, out_dtype=BF16, name="gqa_v")
    grp = nq // nk
    kw = dict(dqk=LANES, kv_group=grp, n_heads=nq, batch=batch, seq=seq, ctx_len=ctx_len)
    o_lat = attention(q, k, vb, lambda g: g, lat_queries=True, **kw)
    if not want_ctx:
        return o_lat
    o_ctx = attention(q, k, vb, lambda g: g, lat_queries=False, **kw)
    return jnp.concatenate([o_lat, o_ctx], 0)


def _rot_gain(g):
    q = g.shape[-1] // 4
    g4 = g.reshape(2, 2, q)
    return jnp.concatenate([g4[:, 1:2], g4[:, 0:1]], axis=1).reshape(g.shape)


def kernel(x, c, ctx, c_ctx, ada_w, ada_b, norm_g, ffn_w_gu, ffn_w_down, hy_w_in, hy_b_in, hy_conv_w, hy_conv_b, hy_f_w1, hy_f_b1, hy_f_w2, hy_f_b2, hy_f_w3, hy_sin_freq, hy_skip, hy_w_out, hy_b_out, rw_mix, rw_w_rkv, rw_w0, rw_w1, rw_w2, rw_a0, rw_a1, rw_a2, rw_g1, rw_g2, rw_k_k, rw_k_a, rw_r_k, rw_ln_w, rw_ln_b, rw_w_o, mla_w_down, mla_q_norm_g, mla_w_uq, mla_kv_norm_g, mla_w_ukv, mla_w_o, gqa_w_qkv, gqa_q_norm_g, gqa_k_norm_g, gqa_w_o, final_g):
    batch, seq, d = x.shape
    ctx_len = ctx.shape[1]
    depth = ada_w.shape[0]
    n_mix = 4
    rows_lat = batch * seq
    rows_all = rows_lat + batch * ctx_len
    bf = lambda a: a.astype(BF16)
    w_gu_all, w_down_all = bf(ffn_w_gu), bf(ffn_w_down)

    cvec = jnp.concatenate([c, c_ctx[None], jnp.zeros((8 - batch - 1, d), F32)], 0)
    mods_all = adaln_all(cvec, ada_w, ada_b)[:, :batch + 1].reshape(depth, batch + 1, N_MOD, d)

    h = jnp.concatenate([x.reshape(rows_lat, d), ctx.reshape(batch * ctx_len, d)], 0)

    for i in range(depth):
        mi, j = i % n_mix, i // n_mix
        ctx_out = any(jj % n_mix != 0 for jj in range(i + 1, depth))
        mods = mods_all[i]
        gate5 = mods[:, 5:6, :]
        mix_dtype = F32 if mi == 1 else BF16
        h, hmix = ffn_half(h, mods, norm_g[i], w_gu_all, w_down_all, layer=i, which=0, s=0, rows=rows_all,
                           seq_rows=seq, next_g=norm_g[i, 1], next_s=1, next_dtype=mix_dtype)
        rows_out = rows_all if ctx_out else rows_lat
        kw = dict(batch=batch, seq=seq, ctx_len=ctx_len)
        if mi == 0:
            p = dict(w_in=bf(hy_w_in[j]), b_in=hy_b_in[j], conv_w=hy_conv_w[j], conv_b=hy_conv_b[j],
                     f_w1=hy_f_w1[j], f_b1=hy_f_b1[j], f_w2=hy_f_w2[j], f_b2=hy_f_b2[j], f_w3=hy_f_w3[j],
                     sin_freq=hy_sin_freq[j], skip=hy_skip[j])
            z = hyena_mixer(hmix, p, d_model=d, want_ctx=ctx_out, **kw)
            w_o, b_o = bf(hy_w_out[j]), hy_b_out[j]
        elif mi == 1:
            pad = lambda w, ax: jnp.pad(w, [(0, RW_LORA_PAD - w.shape[a]) if a == ax else (0, 0) for a in range(w.ndim)])
            p = dict(mix=rw_mix[j], w_r=bf(rw_w_rkv[j, 0]), w_k=bf(rw_w_rkv[j, 1]), w_v=bf(rw_w_rkv[j, 2]),
                     g1=bf(rw_g1[j]), g2=bf(rw_g2[j]),
                     w1=bf(pad(rw_w1[j], 2)), w2=bf(pad(rw_w2[j], 1)), a1=bf(pad(rw_a1[j], 2)), a2=bf(pad(rw_a2[j], 1)),
                     w0=rw_w0[j], a0=rw_a0[j], k_k=rw_k_k[j], k_a=rw_k_a[j], r_k=rw_r_k[j].reshape(-1),
                     ln_w=rw_ln_w[j], ln_b=rw_ln_b[j])
            z = rwkv_mixer(hmix, p, d_model=d, **kw)
            w_o, b_o = bf(rw_w_o[j]), None
        elif mi == 2:
            wd = mla_w_down[j]
            pe = wd[:, MLA_Q_RANK + MLA_KV_RANK:]
            wuq = mla_w_uq[j].reshape(MLA_Q_RANK, MLA_HEADS, MLA_NOPE + MLA_ROPE)
            wq_pe = wuq[..., MLA_NOPE:]
            wuq_ext = jnp.concatenate([wuq[..., :MLA_NOPE], wq_pe,
                                       _rot_cols(wq_pe.reshape(MLA_Q_RANK, -1), MLA_ROPE).reshape(wq_pe.shape)], -1)
            p = dict(w_down_ext=bf(jnp.concatenate([wd, _rot_cols(pe, MLA_ROPE)], 1)),
                     w_uq_ext=bf(wuq_ext.reshape(MLA_Q_RANK, -1)), w_ukv=bf(mla_w_ukv[j]),
                     q_norm_g=mla_q_norm_g[j], kv_norm_g=mla_kv_norm_g[j])
            z = mla_mixer(hmix, p, want_ctx=ctx_out, **kw)
            w_o, b_o = bf(mla_w_o[j]), None
        else:
            w = gqa_w_qkv[j]
            nqc = GQA_HEADS * GQA_HEAD
            nkc = GQA_KV_HEADS * GQA_HEAD
            wq, wk, wv = w[:, :nqc], w[:, nqc:nqc + nkc], w[:, nqc + nkc:]

            def with_rot(wx, nh):
                both = jnp.stack([wx.reshape(d, nh, GQA_HEAD), _rot_cols(wx, GQA_HEAD).reshape(d, nh, GQA_HEAD)], 2)
                return bf(both.reshape(d, 2 * nh * GQA_HEAD))

            p = dict(w_q_ext=with_rot(wq, GQA_HEADS), w_k_ext=with_rot(wk, GQA_KV_HEADS), w_v=bf(wv),
                     q_norm_g=gqa_q_norm_g[j], q_norm_g_rot=_rot_gain(gqa_q_norm_g[j]),
                     k_norm_g=gqa_k_norm_g[j], k_norm_g_rot=_rot_gain(gqa_k_norm_g[j]))
            z = gqa_mixer(hmix, p, want_ctx=ctx_out, **kw)
            w_o, b_o = bf(gqa_w_o[j]), None
        h, xn2 = mm(z, w_o, bias=b_o, res=h, gate=gate5, seq_rows=seq, rows=rows_out, tn=d,
                    next_norm=(norm_g[i, 2], mods, 2), name="mix_out")
        last = i == depth - 1
        if last:
            h, out = ffn_half(h, mods, norm_g[i], w_gu_all, w_down_all, layer=i, which=1, s=2, rows=rows_out,
                              seq_rows=seq, next_g=final_g, next_s=None, next_dtype=F32, xn=xn2)
        else:
            h = ffn_half(h, mods, norm_g[i], w_gu_all, w_down_all, layer=i, which=1, s=2, rows=rows_out,
                         seq_rows=seq, xn=xn2)
    return out[:rows_lat].reshape(batch, seq, d)
```

```python
import functools
import math

import numpy as np
import jax
import jax.numpy as jnp
from jax import lax
from jax.experimental import pallas as pl
from jax.experimental.pallas import tpu as pltpu

F32 = jnp.float32
BF16 = jnp.bfloat16

VMEM_CAP_BYTES = 56 * 1024 * 1024
LANES = 128
SUBLANES = 8

NORM_EPS = 1e-6
LOG2E = 1.4426950408889634
N_MOD = 9
GRID_W = 64
ROPE_THETA = 10000.0
HY_DECAY_TARGET = 1e-2
HY_DECAY_PCT_SHORT = 0.3
HY_DECAY_PCT_LONG = 1.5
RW_HEAD = 64
RW_LORA_PAD = 128
GQA_HEAD = 128
GQA_HEADS = 16
GQA_KV_HEADS = 8
MLA_HEADS = 16
MLA_NOPE = 128
MLA_ROPE = 64
MLA_V = 128
MLA_Q_RANK = 512
MLA_KV_RANK = 512


def _cparams(sem):
    return pltpu.CompilerParams(dimension_semantics=sem, vmem_limit_bytes=VMEM_CAP_BYTES)


def _dot(a, b):
    return jnp.dot(a, b, preferred_element_type=F32)


def _dot_nt(a, b):
    return lax.dot_general(a, b, (((1,), (1,)), ((), ())), preferred_element_type=F32)


def _dot_tn(a, b):
    return lax.dot_general(a, b, (((0,), (0,)), ((), ())), preferred_element_type=F32)


def _split2(x):
    hi = x.astype(BF16)
    lo = (x - hi.astype(F32)).astype(BF16)
    return hi, lo


def _split3(x):
    hi = x.astype(BF16)
    r1 = x - hi.astype(F32)
    mid = r1.astype(BF16)
    lo = (r1 - mid.astype(F32)).astype(BF16)
    return hi, mid, lo


def _dot3(a, b, f=_dot):
    ah, al = _split2(a)
    bh, bl = _split2(b)
    return f(ah, bh) + f(ah, bl) + f(al, bh)


def _dotc3(mh, ml, d):
    dh, dl = _split2(d)
    return _dot(mh, dh) + _dot(ml, dh) + _dot(mh, dl)


def _dotc2(mh, ml, d):
    dh = d.astype(BF16)
    return _dot(mh, dh) + _dot(ml, dh)


def _rms(x, g):
    return x * lax.rsqrt(jnp.mean(x * x, -1, keepdims=True) + NORM_EPS) * g


def _pick(n, cands):
    for c in cands:
        if n % c == 0:
            return c
    raise ValueError(f"no tile for {n}")


def _adaln_body(c_ref, w_ref, b_ref, o_ref):
    c = c_ref[...]
    s = c * jax.nn.sigmoid(c)
    o_ref[0] = _dot(s.astype(BF16), w_ref[0].astype(BF16)) + b_ref[0]


def adaln_all(cvec8, ada_w, ada_b):
    depth, d, n = ada_w.shape
    tn = _pick(n, (2048, 1024, 512, 256, 128))
    return pl.pallas_call(
        _adaln_body,
        grid=(depth, n // tn),
        in_specs=[
            pl.BlockSpec((8, d), lambda l, j: (0, 0)),
            pl.BlockSpec((1, d, tn), lambda l, j: (l, 0, j)),
            pl.BlockSpec((1, 1, tn), lambda l, j: (l, 0, j)),
        ],
        out_specs=pl.BlockSpec((1, 8, tn), lambda l, j: (l, 0, j)),
        out_shape=jax.ShapeDtypeStruct((depth, 8, n), F32),
        compiler_params=_cparams(("arbitrary", "arbitrary")),
        name="adaln",
    )(cvec8, ada_w, ada_b.reshape(depth, 1, n))


def _mm_body(*refs, has_bias, act, has_norm, has_res):
    it = iter(refs)
    x_ref = next(it)
    w_ref = next(it)
    b_ref = next(it) if has_bias else None
    g_ref = next(it) if has_norm else None
    r_ref = next(it) if has_res else None
    gate_ref = next(it) if has_res else None
    o_ref = next(it)
    x = x_ref[...]
    if has_norm:
        x = _rms(x.astype(F32), g_ref[...])
    acc = _dot(x.astype(BF16), w_ref[...])
    if has_bias:
        acc = acc + b_ref[...]
    if act == "sigmoid":
        acc = jax.nn.sigmoid(acc)
    elif act == "tanh":
        acc = jnp.tanh(acc)
    if has_res:
        acc = r_ref[...] + gate_ref[0] * acc
    o_ref[...] = acc.astype(o_ref.dtype)


def mm(x, w, *, bias=None, act=None, norm_g=None, res=None, gate=None, out_dtype=F32,
       tm=512, tn=None, rows=None, x_col_block=0, x_row_block0=0, seq_rows=None, name="mm"):
    k, n = w.shape
    m = rows if rows is not None else x.shape[0]
    assert m % tm == 0, (m, tm)
    tn = tn or _pick(n, (512, 384, 256, 128))
    has_res = res is not None
    ins = [x, w]
    specs = [pl.BlockSpec((tm, k), lambda i, j: (i + x_row_block0, x_col_block)),
             pl.BlockSpec((k, tn), lambda i, j: (0, j))]
    if bias is not None:
        ins.append(bias.reshape(1, n).astype(F32))
        specs.append(pl.BlockSpec((1, tn), lambda i, j: (0, j)))
    if norm_g is not None:
        ins.append(norm_g.reshape(1, k).astype(F32))
        specs.append(pl.BlockSpec((1, k), lambda i, j: (0, 0)))
    if has_res:
        ngroups = gate.shape[0]
        per = seq_rows // tm
        ins += [res, gate]
        specs += [pl.BlockSpec((tm, tn), lambda i, j: (i, j)),
                  pl.BlockSpec((1, 1, tn), lambda i, j: (jnp.minimum(i // per, ngroups - 1), 0, j))]
    body = functools.partial(_mm_body, has_bias=bias is not None, act=act,
                             has_norm=norm_g is not None, has_res=has_res)
    return pl.pallas_call(
        body,
        grid=(m // tm, n // tn),
        in_specs=specs,
        out_specs=pl.BlockSpec((tm, tn), lambda i, j: (i, j)),
        out_shape=jax.ShapeDtypeStruct((m, n), out_dtype),
        compiler_params=_cparams(("parallel", "arbitrary")),
        name=name,
    )(*ins)


def _ffn_body(*refs, s, n_f, has_next, next_s):
    it = iter(refs)
    x_ref, mod_ref, g_ref, wa_ref, wb_ref, wd_ref = (next(it) for _ in range(6))
    gn_ref = next(it) if has_next else None
    o_ref = next(it)
    hn_ref = next(it) if has_next else None
    xn_sc = next(it)
    acc_sc = next(it)
    j = pl.program_id(1)

    @pl.when(j == 0)
    def _():
        n = _rms(x_ref[...], g_ref[s:s + 1, :])
        n = n * (1.0 + mod_ref[0, 3 * s + 1:3 * s + 2, :]) + mod_ref[0, 3 * s:3 * s + 1, :]
        xn_sc[...] = n.astype(BF16)
        acc_sc[...] = jnp.zeros_like(acc_sc)

    xn = xn_sc[...]
    a = _dot(xn, wa_ref[...])
    b = _dot(xn, wb_ref[...])
    h = (a * jax.nn.sigmoid(a)) * b
    acc_sc[...] += _dot(h.astype(BF16), wd_ref[...])

    @pl.when(j == n_f - 1)
    def _():
        xnew = x_ref[...] + 0.5 * mod_ref[0, 3 * s + 2:3 * s + 3, :] * acc_sc[...]
        o_ref[...] = xnew
        if has_next:
            hn = _rms(xnew, gn_ref[...])
            if next_s is not None:
                hn = hn * (1.0 + mod_ref[0, 3 * next_s + 1:3 * next_s + 2, :]) + mod_ref[0, 3 * next_s:3 * next_s + 1, :]
            hn_ref[...] = hn.astype(hn_ref.dtype)


def ffn_half(x, mods, norm_g3, w_gu, w_down, *, layer, which, s, rows, seq_rows, next_g=None, next_s=None,
             next_dtype=F32, tm=512, fc=512):
    d = x.shape[1]
    d_ff = w_down.shape[2]
    n_f = d_ff // fc
    assert d_ff % fc == 0 and rows % tm == 0
    per = seq_rows // tm
    ngroups = mods.shape[0]
    has_next = next_g is not None
    ins = [x, mods, norm_g3, w_gu, w_gu, w_down]
    specs = [
        pl.BlockSpec((tm, d), lambda i, j: (i, 0)),
        pl.BlockSpec((1, N_MOD, d), lambda i, j: (jnp.minimum(i // per, ngroups - 1), 0, 0)),
        pl.BlockSpec((3, d), lambda i, j: (0, 0)),
        pl.BlockSpec((None, None, d, fc), lambda i, j: (layer, which, 0, j)),
        pl.BlockSpec((None, None, d, fc), lambda i, j: (layer, which, 0, n_f + j)),
        pl.BlockSpec((None, None, fc, d), lambda i, j: (layer, which, j, 0)),
    ]
    out_shape = [jax.ShapeDtypeStruct((rows, d), F32)]
    out_specs = [pl.BlockSpec((tm, d), lambda i, j: (i, 0))]
    if has_next:
        ins.append(next_g.reshape(1, d))
        specs.append(pl.BlockSpec((1, d), lambda i, j: (0, 0)))
        out_shape.append(jax.ShapeDtypeStruct((rows, d), next_dtype))
        out_specs.append(pl.BlockSpec((tm, d), lambda i, j: (i, 0)))
    body = functools.partial(_ffn_body, s=s, n_f=n_f, has_next=has_next, next_s=next_s)
    outs = pl.pallas_call(
        body,
        grid=(rows // tm, n_f),
        in_specs=specs,
        out_specs=out_specs,
        out_shape=out_shape,
        scratch_shapes=[pltpu.VMEM((tm, d), BF16), pltpu.VMEM((tm, d), F32)],
        compiler_params=_cparams(("parallel", "arbitrary")),
        name="ffn_half",
    )(*ins)
    return outs if has_next else outs[0]


def _attn_body(*refs, use_lat, hp, n_kv, dqk):
    it = iter(refs)
    q_ref = next(it)
    kl_ref = next(it) if use_lat else None
    kc_ref = next(it)
    vl_refs = [next(it) for _ in range(n_kv)] if use_lat else None
    vc_refs = [next(it) for _ in range(n_kv)]
    o_ref = next(it)
    hs = range(hp)
    slot = [h * n_kv // hp for h in hs]
    q = [q_ref[:, h * dqk:(h + 1) * dqk] for h in hs]
    s_c = [_dot_nt(kc_ref[:, slot[h] * dqk:(slot[h] + 1) * dqk], q[h]) for h in hs]
    m = [jnp.max(s_c[h], 0, keepdims=True) for h in hs]
    if use_lat:
        s_l = [_dot_nt(kl_ref[:, slot[h] * dqk:(slot[h] + 1) * dqk], q[h]) for h in hs]
        m = [jnp.maximum(m[h], jnp.max(s_l[h], 0, keepdims=True)) for h in hs]
    p_c = [jnp.exp2(s_c[h] - m[h]) for h in hs]
    den = [jnp.sum(p_c[h], 0, keepdims=True) for h in hs]
    o = [_dot_tn(vc_refs[slot[h]][...], p_c[h].astype(BF16)) for h in hs]
    if use_lat:
        p_l = [jnp.exp2(s_l[h] - m[h]) for h in hs]
        den = [den[h] + jnp.sum(p_l[h], 0, keepdims=True) for h in hs]
        o = [o[h] + _dot_tn(vl_refs[slot[h]][...], p_l[h].astype(BF16)) for h in hs]
    for h in hs:
        o_ref[:, h * LANES:(h + 1) * LANES] = (o[h] / den[h]).T.astype(o_ref.dtype)


def attention(q_arr, k_arr, v_arr, v_cb, *, dqk, kv_group, n_heads, batch, seq, ctx_len, lat_queries, tq=512, hp=1):
    assert n_heads % hp == 0 and (kv_group % hp == 0 or hp % kv_group == 0)
    n_kv = max(hp // kv_group, 1)
    kv0 = lambda p: (p * hp) // kv_group
    ctx0 = (batch * seq) // ctx_len
    if lat_queries:
        tq = min(tq, seq)
        nq = seq // tq
        q_row = lambda b, t: b * nq + t
        o_row = lambda b, t: b * nq + t
        rows_q = batch * seq
    else:
        tq = ctx_len
        nq = 1
        q_row = lambda b, t: ctx0 + b
        o_row = lambda b, t: b
        rows_q = batch * ctx_len
    ins = [q_arr]
    specs = [pl.BlockSpec((tq, hp * dqk), lambda b, p, t: (q_row(b, t), p))]
    if lat_queries:
        ins.append(k_arr)
        specs.append(pl.BlockSpec((seq, n_kv * dqk), lambda b, p, t: (b, kv0(p) // n_kv)))
    ins.append(k_arr)
    specs.append(pl.BlockSpec((ctx_len, n_kv * dqk), lambda b, p, t: (ctx0 + b, kv0(p) // n_kv)))
    if lat_queries:
        for s in range(n_kv):
            ins.append(v_arr)
            specs.append(pl.BlockSpec((seq, LANES), lambda b, p, t, s=s: (b, v_cb(kv0(p) + s))))
    for s in range(n_kv):
        ins.append(v_arr)
        specs.append(pl.BlockSpec((ctx_len, LANES), lambda b, p, t, s=s: (ctx0 + b, v_cb(kv0(p) + s))))
    return pl.pallas_call(
        functools.partial(_attn_body, use_lat=lat_queries, hp=hp, n_kv=n_kv, dqk=dqk),
        grid=(batch, n_heads // hp, nq),
        in_specs=specs,
        out_specs=pl.BlockSpec((tq, hp * LANES), lambda b, p, t: (o_row(b, t), p)),
        out_shape=jax.ShapeDtypeStruct((rows_q, n_heads * LANES), BF16),
        compiler_params=_cparams(("parallel", "parallel", "arbitrary")),
        name="attention_lat" if lat_queries else "attention_ctx",
    )(*ins)


def _rope_tables(seq, ctx_len, batch, rot_dim):
    half = rot_dim // 4
    inv_freq = ROPE_THETA ** (-jnp.arange(half, dtype=F32) / half)
    t = jnp.arange(seq, dtype=jnp.int32)
    row = (t // GRID_W).astype(F32)
    col = (t % GRID_W).astype(F32)
    ang_r = row[:, None] * inv_freq[None]
    ang_c = col[:, None] * inv_freq[None]
    cos = jnp.concatenate([jnp.cos(ang_r), jnp.cos(ang_r), jnp.cos(ang_c), jnp.cos(ang_c)], -1)
    sin = jnp.concatenate([jnp.sin(ang_r), jnp.sin(ang_r), jnp.sin(ang_c), jnp.sin(ang_c)], -1)
    cos = jnp.concatenate([jnp.tile(cos, (batch, 1)), jnp.ones((batch * ctx_len, rot_dim), F32)], 0)
    sin = jnp.concatenate([jnp.tile(sin, (batch, 1)), jnp.zeros((batch * ctx_len, rot_dim), F32)], 0)
    return cos, sin


def _rot_cols(w, rot_dim):
    lead = w.shape[:-1]
    q = rot_dim // 4
    w5 = w.reshape(lead + (-1, 2, 2, q))
    rot = jnp.concatenate([-w5[..., 1:2, :], w5[..., 0:1, :]], axis=-2)
    return rot.reshape(w.shape)


def _rope_half(b, cos, sin):
    return b * cos + pltpu.roll(b, LANES // 2, 1) * sin


def _proj_rope_body(*refs, mode, has_norm, scale):
    it = iter(refs)
    x_ref, w_ref = next(it), next(it)
    ng_ref = next(it) if has_norm else None
    cos_ref, sin_ref = next(it), next(it)
    x = x_ref[...]
    if has_norm:
        x = _rms(x.astype(F32), ng_ref[...])
    acc = _dot(x.astype(BF16), w_ref[...])
    a, b = acc[:, :LANES], acc[:, LANES:]
    if mode == "gqa":
        g_ref, gr_ref, o_ref = next(it), next(it), next(it)
        r = lax.rsqrt(jnp.mean(a * a, -1, keepdims=True) + NORM_EPS) * scale
        o_ref[...] = (((a * g_ref[...]) * cos_ref[...] + (b * gr_ref[...]) * sin_ref[...]) * r).astype(o_ref.dtype)
    elif mode == "mla_q":
        o_ref = next(it)
        o_ref[:, :LANES] = (a * scale).astype(o_ref.dtype)
        o_ref[:, LANES:] = (_rope_half(b, cos_ref[...], sin_ref[...]) * scale).astype(o_ref.dtype)
    else:
        pe_ref, k_ref, v_ref = next(it), next(it), next(it)
        k_ref[:, :LANES] = a.astype(k_ref.dtype)
        k_ref[:, LANES:] = _rope_half(pe_ref[...], cos_ref[...], sin_ref[...]).astype(k_ref.dtype)
        v_ref[...] = b.astype(v_ref.dtype)


def proj_rope(x, w, cos, sin, *, mode, n_heads, scale=1.0, norm_g=None, x_col_block=0, gains=None,
              pe=None, pe_cb=0, tm=1088):
    m = x.shape[0]
    k = w.shape[0]
    ins = [x, w]
    specs = [pl.BlockSpec((tm, k), lambda i, h: (i, x_col_block)),
             pl.BlockSpec((k, 2 * LANES), lambda i, h: (0, h))]
    if norm_g is not None:
        ins.append(norm_g.reshape(1, k).astype(F32))
        specs.append(pl.BlockSpec((1, k), lambda i, h: (0, 0)))
    tab = pl.BlockSpec((tm, LANES), lambda i, h: (i, 0))
    ins += [cos, sin]
    specs += [tab, tab]
    if mode == "gqa":
        vec = pl.BlockSpec((1, LANES), lambda i, h: (0, 0))
        ins += [gains[0].reshape(1, LANES), gains[1].reshape(1, LANES)]
        specs += [vec, vec]
        out_shape = jax.ShapeDtypeStruct((m, n_heads * LANES), BF16)
        out_specs = pl.BlockSpec((tm, LANES), lambda i, h: (i, h))
    elif mode == "mla_q":
        out_shape = jax.ShapeDtypeStruct((m, n_heads * 2 * LANES), BF16)
        out_specs = pl.BlockSpec((tm, 2 * LANES), lambda i, h: (i, h))
    else:
        ins.append(pe)
        specs.append(pl.BlockSpec((tm, LANES), lambda i, h: (i, pe_cb)))
        out_shape = [jax.ShapeDtypeStruct((m, n_heads * 2 * LANES), BF16),
                     jax.ShapeDtypeStruct((m, n_heads * LANES), BF16)]
        out_specs = [pl.BlockSpec((tm, 2 * LANES), lambda i, h: (i, h)),
                     pl.BlockSpec((tm, LANES), lambda i, h: (i, h))]
    return pl.pallas_call(
        functools.partial(_proj_rope_body, mode=mode, has_norm=norm_g is not None, scale=scale),
        grid=(m // tm, n_heads),
        in_specs=specs,
        out_specs=out_specs,
        out_shape=out_shape,
        compiler_params=_cparams(("parallel", "arbitrary")),
        name="proj_" + mode,
    )(*ins)


class _FftPlan:
    def __init__(self, L):
        self.L = L
        self.N = 2 * L
        self.N2 = 128 if L >= 1024 else 64
        self.N1 = self.N // self.N2
        self.n1_in = self.N1 // 2
        self.K1 = self.N1 // 2 + 1
        self.nq = self.N2 // SUBLANES
        self.spec_rows = self.K1 * 2 * self.N2
        self.kf_cols = max(self.n1_in * SUBLANES, LANES)
        self.kf_rows = 2 * self.K1 * SUBLANES
        self.kb_cols = -(-self.kf_rows // LANES) * LANES
        self.kb_rows = self.n1_in * SUBLANES
        self.k1_unroll = 3 if self.K1 % 3 == 0 else 1

    def constants(self):
        N, N1, N2, K1, n1_in = self.N, self.N1, self.N2, self.K1, self.n1_in
        eye = np.eye(SUBLANES)
        k1 = np.arange(K1)[:, None]
        n1 = np.arange(n1_in)[None, :]
        th = 2 * np.pi * k1 * n1 / N1
        kf = np.zeros((self.kf_rows, self.kf_cols))
        kf[:K1 * SUBLANES, :n1_in * SUBLANES] = np.kron(np.cos(th), eye)
        kf[K1 * SUBLANES:, :n1_in * SUBLANES] = np.kron(-np.sin(th), eye)
        c = np.full((K1,), 2.0)
        c[0] = 1.0
        c[-1] = 1.0
        thb = 2 * np.pi * np.arange(n1_in)[:, None] * np.arange(K1)[None, :] / N1
        kb = np.zeros((self.kb_rows, self.kb_cols))
        kb[:, :K1 * SUBLANES] = np.kron(np.cos(thb) * c[None, :], eye)
        kb[:, K1 * SUBLANES:2 * K1 * SUBLANES] = np.kron(-np.sin(thb) * c[None, :], eye)
        a = 2 * np.pi * np.outer(np.arange(N2), np.arange(N2)) / N2
        C, S = np.cos(a), np.sin(a)
        w2f = np.block([[C, S], [-S, C]])
        w2i = np.block([[C, -S], [S, C]])
        tw_ang = 2 * np.pi * np.outer(np.arange(N2), np.arange(K1)) / N
        twr = np.repeat(np.cos(tw_ang), LANES, axis=1)
        twi = np.repeat(np.sin(tw_ang), LANES, axis=1)

        def hl(m):
            m32 = jnp.asarray(m, F32)
            hi = m32.astype(BF16)
            lo = (m32 - hi.astype(F32)).astype(BF16)
            return hi, lo

        out = {}
        for name, mat in (("kf", kf), ("kb", kb), ("w2f", w2f), ("w2i", w2i)):
            out[name + "_h"], out[name + "_l"] = hl(mat)
        out["twr"] = jnp.asarray(twr, F32)
        out["twi"] = jnp.asarray(twi, F32)
        return out


def _cmul(ar, ai, br, bi):
    return ar * br - ai * bi, ar * bi + ai * br


def _time_to_strided(plan, x_ref, xt_ref):
    N2, nq = plan.N2, plan.nq
    for n in range(plan.n1_in):
        xt_ref[n * SUBLANES:(n + 1) * SUBLANES, :] = jnp.concatenate(
            [x_ref[N2 * n + SUBLANES * q:N2 * n + SUBLANES * (q + 1), :] for q in range(nq)], axis=1)
    pad = plan.kf_cols - plan.n1_in * SUBLANES
    if pad:
        xt_ref[plan.n1_in * SUBLANES:, :] = jnp.zeros((pad, xt_ref.shape[1]), F32)


def _strided_to_spectrum(plan, a_ref, s_ref, c):
    N2, K1, nq = plan.N2, plan.K1, plan.nq
    for part in range(2):
        for k in range(K1):
            r0 = (part * K1 + k) * SUBLANES
            s_ref[part * N2:(part + 1) * N2, k * c:(k + 1) * c] = jnp.concatenate(
                [a_ref[r0:r0 + SUBLANES, q * c:(q + 1) * c] for q in range(nq)], axis=0)


def _spectrum_to_strided(plan, s_ref, b_ref, c):
    N2, K1, nq = plan.N2, plan.K1, plan.nq
    for part in range(2):
        for k in range(K1):
            r0 = (part * K1 + k) * SUBLANES
            b_ref[r0:r0 + SUBLANES, :] = jnp.concatenate(
                [s_ref[part * N2 + SUBLANES * q:part * N2 + SUBLANES * (q + 1), k * c:(k + 1) * c] for q in range(nq)],
                axis=1)
    pad = plan.kb_cols - plan.kf_rows
    if pad:
        b_ref[plan.kf_rows:, :] = jnp.zeros((pad, b_ref.shape[1]), F32)


def _strided_to_time(plan, yt, c):
    nq = plan.nq
    slabs = []
    for n in range(plan.n1_in):
        blk = yt[n * SUBLANES:(n + 1) * SUBLANES, :]
        slabs.extend(blk[:, q * c:(q + 1) * c] for q in range(nq))
    return jnp.concatenate(slabs, axis=0)


def _fft_fwd(plan, x_ref, consts, xt_ref, a_ref, s_ref, dotc):
    c = x_ref.shape[1]
    _time_to_strided(plan, x_ref, xt_ref)
    a_ref[:plan.kf_rows, :] = dotc(consts["kf_h"][...], consts["kf_l"][...], xt_ref[...])
    _strided_to_spectrum(plan, a_ref, s_ref, c)
    n2 = plan.N2
    tr, ti = consts["twr"][...], consts["twi"][...]
    xr, xi = _cmul(s_ref[:n2, :], s_ref[n2:, :], tr, -ti)
    s_ref[...] = dotc(consts["w2f_h"][...], consts["w2f_l"][...], jnp.concatenate([xr, xi], axis=0))


def _fft_conv_apply(plan, x_ref, g_ref, consts, xt_ref, a_ref, s_ref):
    c = x_ref.shape[1]
    n2 = plan.N2
    _fft_fwd(plan, x_ref, consts, xt_ref, a_ref, s_ref, _dotc2)
    pr, pi = _cmul(s_ref[:n2, :], s_ref[n2:, :], g_ref[:n2, :], g_ref[n2:, :])
    z = _dotc2(consts["w2i_h"][...], consts["w2i_l"][...], jnp.concatenate([pr, pi], axis=0))
    zr, zi = _cmul(z[:n2], z[n2:], consts["twr"][...], consts["twi"][...])
    s_ref[:n2, :] = zr
    s_ref[n2:, :] = zi
    _spectrum_to_strided(plan, s_ref, a_ref, c)
    yt = _dotc2(consts["kb_h"][...], consts["kb_l"][...], a_ref[...])
    return _strided_to_time(plan, yt, c)


def _fft_scratch(plan):
    wl = plan.nq * LANES
    return [pltpu.VMEM((plan.kf_cols, wl), F32),
            pltpu.VMEM((max(plan.kf_rows, plan.kb_cols), wl), F32),
            pltpu.VMEM((2 * plan.N2, plan.K1 * LANES), F32)]


_CONST_NAMES = ("kf_h", "kf_l", "kb_h", "kb_l", "w2f_h", "w2f_l", "w2i_h", "w2i_l", "twr", "twi")


def _const_specs(consts, nargs):
    ins, specs = [], []
    for nme in _CONST_NAMES:
        a = consts[nme]
        ins.append(a)
        if nargs == 1:
            specs.append(pl.BlockSpec(a.shape, lambda c: (0, 0), pipeline_mode=pl.Buffered(1)))
        else:
            specs.append(pl.BlockSpec(a.shape, lambda c, b: (0, 0), pipeline_mode=pl.Buffered(1)))
    return ins, specs


def _hy_hidden_body(z_ref, w1_ref, b1_ref, w2_ref, b2_ref, sf_ref, o_ref):
    h = jnp.sin(sf_ref[0:1, :] * (_dot3(z_ref[...], w1_ref[...]) + b1_ref[...]))
    h = jnp.sin(sf_ref[1:2, :] * (_dot3(h, w2_ref[...]) + b2_ref[...]))
    o_ref[...] = h


def hyena_hidden(L, f_w1, f_b1, f_w2, f_b2, sin_freq):
    emb, hid = f_w1.shape
    bands = (emb - 1) // 2
    t = jnp.linspace(0.0, 1.0, L, dtype=F32)[:, None]
    w = (2.0 * math.pi / L) * jnp.arange(L, dtype=F32)[:, None]
    f = jnp.linspace(1e-4, bands - 1, bands, dtype=F32)[None]
    z = jnp.concatenate([t, jnp.cos(f * w), -jnp.sin(f * w)], -1)
    embp = -(-emb // LANES) * LANES
    z = jnp.pad(z, ((0, 0), (0, embp - emb)))
    w1 = jnp.pad(f_w1, ((0, embp - emb), (0, 0)))
    ins = [z, w1, f_b1.reshape(1, hid), f_w2, f_b2.reshape(1, hid), sin_freq]
    return pl.pallas_call(
        _hy_hidden_body,
        out_shape=jax.ShapeDtypeStruct((L, hid), F32),
        name="hyena_hidden",
    )(*ins)


def _hy_filter_body(h_ref, wf_ref, wb_ref, dl_ref, *rest, plan):
    consts = {nme: r for nme, r in zip(_CONST_NAMES, rest)}
    g_ref, x_sc, xt_sc, a_sc, s_sc, sb_sc = rest[len(_CONST_NAMES):]
    L, n2 = plan.L, plan.N2
    t = lax.broadcasted_iota(jnp.int32, (L, LANES), 0).astype(F32) * (1.0 / (L - 1))
    win = jnp.exp(-t * dl_ref[...])
    row0 = lax.broadcasted_iota(jnp.int32, (L, LANES), 0) == 0
    h = h_ref[...]
    fwd = _dot3(h, wf_ref[...]) * win
    bwd = jnp.where(row0, 0.0, _dot3(h, wb_ref[...]) * win)
    ss = jnp.sum(fwd * fwd, 0, keepdims=True) + jnp.sum(bwd * bwd, 0, keepdims=True)
    nu = lax.rsqrt(ss + 1e-12) * (1.0 / plan.N)
    x_sc[...] = bwd * nu
    _fft_fwd(plan, x_sc, consts, xt_sc, a_sc, sb_sc, _dotc2)
    x_sc[...] = fwd * nu
    _fft_fwd(plan, x_sc, consts, xt_sc, a_sc, s_sc, _dotc2)
    g_ref[:n2, :] = s_sc[:n2, :] + sb_sc[:n2, :]
    g_ref[n2:, :] = s_sc[n2:, :] - sb_sc[n2:, :]


def hyena_filter_spectra(plan, consts, hidden, f_w3, d_model, n_order):
    L = plan.L
    hid = hidden.shape[1]
    ncb = d_model // LANES
    deltas = jnp.abs(jnp.linspace(math.log(HY_DECAY_TARGET) / HY_DECAY_PCT_SHORT,
                                  math.log(HY_DECAY_TARGET) / HY_DECAY_PCT_LONG, d_model, dtype=F32)).reshape(1, d_model)
    ins = [hidden, f_w3, f_w3, deltas]
    specs = [pl.BlockSpec((L, hid), lambda c, n: (0, 0)),
             pl.BlockSpec((hid, LANES), lambda c, n: (0, (2 * n) * ncb + c)),
             pl.BlockSpec((hid, LANES), lambda c, n: (0, (2 * n + 1) * ncb + c)),
             pl.BlockSpec((1, LANES), lambda c, n: (0, c))]
    ci, cs = _const_specs(consts, 2)
    ins += ci
    specs += cs
    srows, scols = 2 * plan.N2, plan.K1 * LANES
    return pl.pallas_call(
        functools.partial(_hy_filter_body, plan=plan),
        grid=(ncb, n_order),
        in_specs=specs,
        out_specs=pl.BlockSpec((None, None, srows, scols), lambda c, n: (n, c, 0, 0)),
        out_shape=jax.ShapeDtypeStruct((n_order, ncb, srows, scols), F32),
        scratch_shapes=[pltpu.VMEM((L, LANES), F32)] + _fft_scratch(plan) + [pltpu.VMEM((srows, scols), F32)],
        compiler_params=_cparams(("arbitrary", "arbitrary")),
        name="hyena_filter",
    )(*ins)


def _conv3(p, w_ref, b_ref, L):
    rows = lax.broadcasted_iota(jnp.int32, p.shape, 0)
    prev = jnp.where(rows == 0, 0.0, pltpu.roll(p, 1, 0))
    nxt = jnp.where(rows == L - 1, 0.0, pltpu.roll(p, L - 1, 0))
    return prev * w_ref[0:1, :] + p * w_ref[1:2, :] + nxt * w_ref[2:3, :] + b_ref[...]


def _hy_conv_body(*refs, plan, order, conv_u):
    it = iter(refs)
    u_ref, gate_ref = next(it), next(it)
    if conv_u:
        cwu_ref, cbu_ref = next(it), next(it)
    cwg_ref, cbg_ref = next(it), next(it)
    skip_ref = next(it)
    g_ref = next(it)
    consts = {nme: next(it) for nme in _CONST_NAMES}
    o_ref = next(it)
    x_sc, xt_sc, a_sc, s_sc = (next(it) for _ in range(4))
    L = plan.L
    u = u_ref[...]
    if conv_u:
        u = _conv3(u, cwu_ref, cbu_ref, L)
    x_sc[...] = u
    y = _fft_conv_apply(plan, x_sc, g_ref, consts, xt_sc, a_sc, s_sc)
    gate = _conv3(gate_ref[...], cwg_ref, cbg_ref, L)
    o_ref[...] = (gate * (y + x_sc[...] * skip_ref[order:order + 1, :])).astype(o_ref.dtype)


def hyena_conv(plan, consts, u_arr, u_cb0, gate_arr, gate_cb0, conv_w, conv_b, skip, spectra, *,
               order, conv_u, u_row_block0, gate_row_block0, batch, d_model, out_dtype):
    L = plan.L
    ncb = d_model // LANES
    ins = [u_arr, gate_arr]
    specs = [pl.BlockSpec((L, LANES), lambda c, b: (u_row_block0 + b, u_cb0 + c)),
             pl.BlockSpec((L, LANES), lambda c, b: (gate_row_block0 + b, gate_cb0 + c))]
    if conv_u:
        ins += [conv_w, conv_b]
        specs += [pl.BlockSpec((3, LANES), lambda c, b: (0, u_cb0 + c)),
                  pl.BlockSpec((1, LANES), lambda c, b: (0, u_cb0 + c))]
    srows, scols = 2 * plan.N2, plan.K1 * LANES
    ins += [conv_w, conv_b, skip, spectra]
    specs += [pl.BlockSpec((3, LANES), lambda c, b: (0, gate_cb0 + c)),
              pl.BlockSpec((1, LANES), lambda c, b: (0, gate_cb0 + c)),
              pl.BlockSpec((skip.shape[0], LANES), lambda c, b: (0, c)),
              pl.BlockSpec((None, None, srows, scols), lambda c, b: (order, c, 0, 0))]
    ci, cs = _const_specs(consts, 2)
    ins += ci
    specs += cs
    return pl.pallas_call(
        functools.partial(_hy_conv_body, plan=plan, order=order, conv_u=conv_u),
        grid=(ncb, batch),
        in_specs=specs,
        out_specs=pl.BlockSpec((L, LANES), lambda c, b: (b, c)),
        out_shape=jax.ShapeDtypeStruct((batch * L, d_model), out_dtype),
        scratch_shapes=[pltpu.VMEM((L, LANES), F32)] + _fft_scratch(plan),
        compiler_params=_cparams(("parallel", "arbitrary")),
        name=f"hyena_conv{order}_L{L}",
    )(*ins)


def hyena_mixer(hmix, p, *, batch, seq, ctx_len, d_model, want_ctx):
    n_order = p["skip"].shape[0]
    rows = batch * seq + (batch * ctx_len if want_ctx else 0)
    pr = mm(hmix, p["w_in"], bias=p["b_in"], name="hy_in", rows=rows, tm=1088 if rows % 1088 == 0 else 512)
    conv_b = p["conv_b"].reshape(1, -1)
    ncb = d_model // LANES
    outs = []
    streams = [(seq, 0)]
    if want_ctx:
        streams.append((ctx_len, (batch * seq) // ctx_len))
    for L, rb0 in streams:
        plan = _FftPlan(L)
        consts = plan.constants()
        hidden = hyena_hidden(L, p["f_w1"], p["f_b1"], p["f_w2"], p["f_b2"], p["sin_freq"])
        spectra = hyena_filter_spectra(plan, consts, hidden, p["f_w3"], d_model, n_order)
        z1 = hyena_conv(plan, consts, pr, 0, pr, ncb, p["conv_w"], conv_b, p["skip"], spectra,
                        order=0, conv_u=True, u_row_block0=rb0, gate_row_block0=rb0, batch=batch, d_model=d_model,
                        out_dtype=F32)
        z2 = hyena_conv(plan, consts, z1, 0, pr, 2 * ncb, p["conv_w"], conv_b, p["skip"], spectra,
                        order=1, conv_u=False, u_row_block0=0, gate_row_block0=rb0, batch=batch, d_model=d_model,
                        out_dtype=BF16)
        outs.append(z2)
    return outs[0] if len(outs) == 1 else jnp.concatenate(outs, 0)


def _headsum(x, e_ref):
    hi, lo = _split2(x)
    return _dot(hi, e_ref[...]) + _dot(lo, e_ref[...])


def _rw_mix_body(h_ref, hp_ref, hn_ref, mix_ref, o_ref, *, tm, lat_tiles, tiles_per_seq):
    i = pl.program_id(0)
    is_ctx = i >= lat_tiles
    first = jnp.logical_or(is_ctx, i % tiles_per_seq == 0)
    last = jnp.logical_or(is_ctx, i % tiles_per_seq == tiles_per_seq - 1)
    h = h_ref[...]
    rows = lax.broadcasted_iota(jnp.int32, h.shape, 0)
    pedge = jnp.where(first, 0.0, hp_ref[SUBLANES - 1:SUBLANES, :])
    nedge = jnp.where(last, 0.0, hn_ref[0:1, :])
    prev = jnp.where(rows == 0, pedge, pltpu.roll(h, 1, 0))
    nxt = jnp.where(rows == tm - 1, nedge, pltpu.roll(h, tm - 1, 0))
    dp = prev - h
    dn = nxt - h
    for n in range(o_ref.shape[0]):
        o_ref[n] = (h + dp * mix_ref[0, n:n + 1, :] + dn * mix_ref[1, n:n + 1, :]).astype(o_ref.dtype)


def rwkv_mix(h, mix, *, batch, seq, ctx_len):
    m, d = h.shape
    tm = ctx_len
    n_shift = mix.shape[1]
    nblk = m // SUBLANES
    per = tm // SUBLANES
    body = functools.partial(_rw_mix_body, tm=tm, lat_tiles=(batch * seq) // tm, tiles_per_seq=seq // tm)
    return pl.pallas_call(
        body,
        grid=(m // tm,),
        in_specs=[
            pl.BlockSpec((tm, d), lambda i: (i, 0)),
            pl.BlockSpec((SUBLANES, d), lambda i: (jnp.maximum(i * per - 1, 0), 0)),
            pl.BlockSpec((SUBLANES, d), lambda i: (jnp.minimum((i + 1) * per, nblk - 1), 0)),
            pl.BlockSpec((2, n_shift, d), lambda i: (0, 0, 0)),
        ],
        out_specs=pl.BlockSpec((n_shift, tm, d), lambda i: (0, i, 0)),
        out_shape=jax.ShapeDtypeStruct((n_shift, m, d), BF16),
        compiler_params=_cparams(("parallel",)),
        name="rwkv_mix",
    )(h, h, h, mix)


def _rw_prep_body(k_ref, lw0_ref, lw1_ref, al0_ref, al1_ref, w0_ref, a0_ref, kk_ref, ka_ref, e_ref,
                  olw0, olw1, okd0, okd1, obb0, obb1, oaa):
    k = k_ref[...]
    kk0 = k * kk_ref[...]
    kk = kk0 * lax.rsqrt(_headsum(kk0 * kk0, e_ref) + 1e-12)
    oaa[...] = -kk
    for d, (lw_ref, al_ref, olw, okd, obb) in enumerate(((lw0_ref, al0_ref, olw0, okd0, obb0),
                                                         (lw1_ref, al1_ref, olw1, okd1, obb1))):
        z = -(w0_ref[d:d + 1, :] + lw_ref[...])
        softplus = jnp.maximum(z, 0.0) + jnp.log(1.0 + jnp.exp(-jnp.abs(z)))
        log_w = -softplus - 0.5
        olw[...] = -jnp.exp(log_w)
        a = jax.nn.sigmoid(a0_ref[d:d + 1, :] + al_ref[...])
        okd[...] = k * (1.0 + (a - 1.0) * ka_ref[...])
        obb[...] = kk * a


def rwkv_prep(k, lw0, lw1, al0, al1, w0, a0, k_k, k_a, e128, tm=1088):
    m, d = k.shape
    blk = pl.BlockSpec((tm, LANES), lambda i, c: (i, c))
    vec2 = pl.BlockSpec((2, LANES), lambda i, c: (0, c))
    vec1 = pl.BlockSpec((1, LANES), lambda i, c: (0, c))
    sh = jax.ShapeDtypeStruct((m, d), F32)
    return pl.pallas_call(
        _rw_prep_body,
        grid=(m // tm, d // LANES),
        in_specs=[blk] * 5 + [vec2, vec2, vec1, vec1, pl.BlockSpec((LANES, LANES), lambda i, c: (0, 0))],
        out_specs=[blk] * 7,
        out_shape=[sh] * 7,
        compiler_params=_cparams(("parallel", "arbitrary")),
        name="rwkv_prep",
    )(k, lw0, lw1, al0, al1, w0, a0, k_k.reshape(1, d), k_a.reshape(1, d), e128)


def _b16(x):
    return x.astype(BF16)


def _wkv_body(r_ref, lw_ref, k_ref, v_ref, a_ref, b_ref, o_ref, s_sc, *, chunk, n_heads, reverse):
    c = pl.program_id(2)

    @pl.when(c == 0)
    def _():
        s_sc[...] = jnp.zeros_like(s_sc)

    C = chunk
    hs = range(n_heads)
    row = lax.broadcasted_iota(jnp.int32, (C, C), 0)
    col = lax.broadcasted_iota(jnp.int32, (C, C), 1)
    if reverse:
        tri_incl = col >= row
        tri_strict = col > row
        last = 0
    else:
        tri_incl = col <= row
        tri_strict = col < row
        last = C - 1
    same = lambda s: (row >> int(math.log2(s))) == (col >> int(math.log2(s)))
    eye = (row == col).astype(F32)
    cum_m = tri_incl.astype(BF16)
    sls = [slice(h * RW_HEAD, (h + 1) * RW_HEAD) for h in hs]

    lw = [lw_ref[:, sl] for sl in sls]
    cum = []
    for h in hs:
        l1, l2, l3 = _split3(lw[h])
        cum.append(_dot(cum_m, l1) + _dot(cum_m, l2) + _dot(cum_m, l3))
    p_inc = [jnp.exp(cum[h]) for h in hs]
    x1 = [jnp.concatenate([a_ref[:, sls[h]] * jnp.exp(cum[h] - lw[h]), r_ref[:, sls[h]] * p_inc[h]], axis=0) for h in hs]
    x2 = []
    for h in hs:
        p_inv = jnp.exp(-cum[h])
        x2.append(jnp.concatenate([b_ref[:, sls[h]] * p_inv, k_ref[:, sls[h]] * p_inv], axis=0))
    v = [v_ref[:, sl] for sl in sls]
    s0 = [s_sc[h] for h in hs]
    def cat_a(x):
        hi, lo = _split2(x)
        return jnp.concatenate([hi, lo, hi], axis=1)

    def cat_b(x):
        hi, lo = _split2(x)
        return jnp.concatenate([hi, hi, lo], axis=1)

    x1c = [cat_a(x1[h]) for h in hs]
    mfull = [_dot_nt(x1c[h], cat_b(x2[h])) for h in hs]
    w0 = [_dot_nt(x1c[h], cat_b(s0[h])) for h in hs]
    n_ab = [jnp.where(tri_strict, mfull[h][:C, :C], 0.0) for h in hs]
    a_ak = [_b16(jnp.where(tri_strict, mfull[h][:C, C:], 0.0)) for h in hs]
    a_r = [_b16(jnp.concatenate([jnp.where(tri_incl, mfull[h][C:, :C], 0.0),
                                 jnp.where(tri_incl, mfull[h][C:, C:], 0.0)], axis=1)) for h in hs]
    rhs = [w0[h][:C] + _dot(a_ak[h], _b16(v[h])) for h in hs]

    n4 = [jnp.where(same(4), n_ab[h], 0.0) for h in hs]
    n4b = [_b16(n4[h]) for h in hs]
    sq = [_dot(n4b[h], n4b[h]) for h in hs]
    t = [(eye + n4[h]) + _dot(_b16(eye + n4[h]), _b16(sq[h])) for h in hs]
    s = 4
    while s < C:
        off = jnp.logical_and(jnp.logical_and(same(2 * s), jnp.logical_not(same(s))), tri_strict)
        tb = [_b16(t[h]) for h in hs]
        tmp = [_dot(tb[h], _b16(jnp.where(off, n_ab[h], 0.0))) for h in hs]
        t = [t[h] + _dot(_b16(tmp[h]), tb[h]) for h in hs]
        s *= 2
    tb = [_b16(t[h]) for h in hs]
    u = [_dot(tb[h], _b16(rhs[h])) for h in hs]
    res = [rhs[h] - u[h] + _dot3(n_ab[h], u[h]) for h in hs]
    u = [u[h] + _dot(tb[h], _b16(res[h])) for h in hs]
    uv = [jnp.concatenate([u[h], v[h]], axis=0) for h in hs]
    y = [w0[h][C:] + _dot(a_r[h], _b16(uv[h])) for h in hs]
    for h in hs:
        s_sc[h] = (s0[h] + _dot3(uv[h], x2[h], _dot_tn)) * p_inc[h][last:last + 1, :]
    o_ref[...] = jnp.concatenate(y, axis=1)


def wkv7(r, lw, kd, v, aa, bb, *, reverse, batch, seq, ctx_len, chunk=128, heads_per_step=16):
    m, d = r.shape
    heads_per_step = min(heads_per_step, d // RW_HEAD)
    wl = heads_per_step * RW_HEAD
    n_ctx = ctx_len // chunk
    n_lat = seq // chunk
    n_chunks = n_ctx + n_lat
    ctx0 = (batch * seq) // chunk

    def row_block(b, c):
        if reverse:
            return jnp.where(c < n_ctx, ctx0 + b * n_ctx + (n_ctx - 1 - c), b * n_lat + (n_lat - 1 - (c - n_ctx)))
        return jnp.where(c < n_ctx, ctx0 + b * n_ctx + c, b * n_lat + (c - n_ctx))

    blk = pl.BlockSpec((chunk, wl), lambda b, g, c: (row_block(b, c), g))
    body = functools.partial(_wkv_body, chunk=chunk, n_heads=heads_per_step, reverse=reverse)
    return pl.pallas_call(
        body,
        grid=(batch, d // wl, n_chunks),
        in_specs=[blk] * 6,
        out_specs=blk,
        out_shape=jax.ShapeDtypeStruct((m, d), F32),
        scratch_shapes=[pltpu.VMEM((heads_per_step, RW_HEAD, RW_HEAD), F32)],
        compiler_params=_cparams(("parallel", "parallel", "arbitrary")),
        name="wkv7_rev" if reverse else "wkv7_fwd",
    )(r, lw, kd, v, aa, bb)


def _rw_readout_body(y0_ref, y1_ref, r_ref, v_ref, kd0_ref, kd1_ref, g_ref, rk_ref, lnw_ref, lnb_ref, e_ref, o_ref,
                     *, gn_eps):
    y = y0_ref[...] + y1_ref[...]
    inv = 1.0 / RW_HEAD
    mu = _headsum(y, e_ref) * inv
    dlt = y - mu
    var = _headsum(dlt * dlt, e_ref) * inv
    yn = dlt * lax.rsqrt(var + gn_eps) * lnw_ref[...] + lnb_ref[...]
    rk = r_ref[...] * rk_ref[...]
    bonus = (_headsum(rk * kd0_ref[...], e_ref) + _headsum(rk * kd1_ref[...], e_ref)) * v_ref[...]
    o_ref[...] = ((yn + bonus) * g_ref[...]).astype(o_ref.dtype)


def rwkv_readout(y0, y1, r, v, kd0, kd1, g, r_k, ln_w, ln_b, e128, gn_eps, rows, tm=512):
    d = r.shape[1]
    blk = pl.BlockSpec((tm, LANES), lambda i, c: (i, c))
    vec1 = pl.BlockSpec((1, LANES), lambda i, c: (0, c))
    return pl.pallas_call(
        functools.partial(_rw_readout_body, gn_eps=gn_eps),
        grid=(rows // tm, d // LANES),
        in_specs=[blk] * 7 + [vec1] * 3 + [pl.BlockSpec((LANES, LANES), lambda i, c: (0, 0))],
        out_specs=blk,
        out_shape=jax.ShapeDtypeStruct((rows, d), BF16),
        compiler_params=_cparams(("parallel", "arbitrary")),
        name="rwkv_readout",
    )(y0, y1, r, v, kd0, kd1, g, r_k.reshape(1, d), ln_w.reshape(1, d), ln_b.reshape(1, d), e128)


def rwkv_mixer(hmix, p, *, batch, seq, ctx_len, d_model):
    xs = rwkv_mix(hmix, p["mix"], batch=batch, seq=seq, ctx_len=ctx_len)
    m = hmix.shape[0]
    xs2 = xs.reshape(xs.shape[0] * m, d_model)
    tm = 1088 if m % 1088 == 0 else 512

    def proj(n, w, **kw):
        return mm(xs2, w, tm=tm, rows=m, x_row_block0=n * (m // tm), **kw)

    r = proj(0, p["w_r"], name="rw_r")
    k = proj(2, p["w_k"], name="rw_k")
    v = proj(3, p["w_v"], name="rw_v")
    g = mm(proj(5, p["g1"], act="sigmoid", out_dtype=BF16, name="rw_g1"), p["g2"], tm=tm, name="rw_g2")
    lws, als = [], []
    for d in range(2):
        tw = proj(1, p["w1"][d], act="tanh", out_dtype=BF16, name="rw_w1")
        lws.append(mm(tw, p["w2"][d], tm=tm, name="rw_w2"))
        ah = proj(4, p["a1"][d], out_dtype=BF16, name="rw_a1")
        als.append(mm(ah, p["a2"][d], tm=tm, name="rw_a2"))
    e128 = jnp.asarray(np.kron(np.eye(LANES // RW_HEAD), np.ones((RW_HEAD, RW_HEAD))), BF16)
    lw0, lw1, kd0, kd1, bb0, bb1, aa = rwkv_prep(k, lws[0], lws[1], als[0], als[1], p["w0"], p["a0"],
                                                 p["k_k"], p["k_a"], e128, tm=tm)
    kw = dict(batch=batch, seq=seq, ctx_len=ctx_len)
    y0 = wkv7(r, lw0, kd0, v, aa, bb0, reverse=False, **kw)
    y1 = wkv7(r, lw1, kd1, v, aa, bb1, reverse=True, **kw)
    return rwkv_readout(y0, y1, r, v, kd0, kd1, g, p["r_k"], p["ln_w"], p["ln_b"], e128,
                        1e-5 * RW_HEAD, rows=m)


def mla_mixer(hmix, p, *, batch, seq, ctx_len, want_ctx):
    m = hmix.shape[0]
    tm = 1088 if m % 1088 == 0 else 512
    scale = (MLA_NOPE + MLA_ROPE) ** -0.5 * LOG2E
    down = mm(hmix, p["w_down_ext"], tm=tm, name="mla_down")
    cos, sin = _rope_tables(seq, ctx_len, batch, MLA_ROPE)
    zeros = jnp.zeros_like(cos)
    cos_p = jnp.concatenate([cos, zeros], -1)
    sin_p = jnp.concatenate([sin, zeros], -1)
    q = proj_rope(down, p["w_uq_ext"], cos_p, sin_p, mode="mla_q", n_heads=MLA_HEADS, scale=scale,
                  norm_g=p["q_norm_g"], x_col_block=0, tm=tm)
    kcat, v = proj_rope(down, p["w_ukv"], cos_p, sin_p, mode="mla_kv", n_heads=MLA_HEADS,
                        norm_g=p["kv_norm_g"], x_col_block=1, pe=down,
                        pe_cb=(MLA_Q_RANK + MLA_KV_RANK) // LANES, tm=tm)
    kw = dict(dqk=2 * LANES, kv_group=1, n_heads=MLA_HEADS, batch=batch, seq=seq, ctx_len=ctx_len)
    o_lat = attention(q, kcat, v, lambda g: g, lat_queries=True, **kw)
    if not want_ctx:
        return o_lat
    o_ctx = attention(q, kcat, v, lambda g: g, lat_queries=False, **kw)
    return jnp.concatenate([o_lat, o_ctx], 0)


def gqa_mixer(hmix, p, *, batch, seq, ctx_len, want_ctx):
    m = hmix.shape[0]
    tm = 1088 if m % 1088 == 0 else 512
    scale = GQA_HEAD ** -0.5 * LOG2E
    cos, sin = _rope_tables(seq, ctx_len, batch, GQA_HEAD)
    nq, nk = GQA_HEADS, GQA_KV_HEADS
    q = proj_rope(hmix, p["w_q_ext"], cos, sin, mode="gqa", n_heads=nq, scale=scale,
                  gains=(p["q_norm_g"], p["q_norm_g_rot"]), tm=tm)
    k = proj_rope(hmix, p["w_k_ext"], cos, sin, mode="gqa", n_heads=nk,
                  gains=(p["k_norm_g"], p["k_norm_g_rot"]), tm=tm)
    vb = mm(hmix, p["w_v"], tm=tm, out_dtype=BF16, name="gqa_v")
    grp = nq // nk
    kw = dict(dqk=LANES, kv_group=grp, n_heads=nq, batch=batch, seq=seq, ctx_len=ctx_len)
    o_lat = attention(q, k, vb, lambda g: g, lat_queries=True, **kw)
    if not want_ctx:
        return o_lat
    o_ctx = attention(q, k, vb, lambda g: g, lat_queries=False, **kw)
    return jnp.concatenate([o_lat, o_ctx], 0)


def _rot_gain(g):
    q = g.shape[-1] // 4
    g4 = g.reshape(2, 2, q)
    return jnp.concatenate([g4[:, 1:2], g4[:, 0:1]], axis=1).reshape(g.shape)


def kernel(x, c, ctx, c_ctx, ada_w, ada_b, norm_g, ffn_w_gu, ffn_w_down, hy_w_in, hy_b_in, hy_conv_w, hy_conv_b, hy_f_w1, hy_f_b1, hy_f_w2, hy_f_b2, hy_f_w3, hy_sin_freq, hy_skip, hy_w_out, hy_b_out, rw_mix, rw_w_rkv, rw_w0, rw_w1, rw_w2, rw_a0, rw_a1, rw_a2, rw_g1, rw_g2, rw_k_k, rw_k_a, rw_r_k, rw_ln_w, rw_ln_b, rw_w_o, mla_w_down, mla_q_norm_g, mla_w_uq, mla_kv_norm_g, mla_w_ukv, mla_w_o, gqa_w_qkv, gqa_q_norm_g, gqa_k_norm_g, gqa_w_o, final_g):
    batch, seq, d = x.shape
    ctx_len = ctx.shape[1]
    depth = ada_w.shape[0]
    n_mix = 4
    rows_lat = batch * seq
    rows_all = rows_lat + batch * ctx_len
    bf = lambda a: a.astype(BF16)
    w_gu_all, w_down_all = bf(ffn_w_gu), bf(ffn_w_down)

    cvec = jnp.concatenate([c, c_ctx[None], jnp.zeros((8 - batch - 1, d), F32)], 0)
    mods_all = adaln_all(cvec, ada_w, ada_b)[:, :batch + 1].reshape(depth, batch + 1, N_MOD, d)

    h = jnp.concatenate([x.reshape(rows_lat, d), ctx.reshape(batch * ctx_len, d)], 0)

    for i in range(depth):
        mi, j = i % n_mix, i // n_mix
        ctx_out = any(jj % n_mix != 0 for jj in range(i + 1, depth))
        mods = mods_all[i]
        gate5 = mods[:, 5:6, :]
        mix_dtype = F32 if mi == 1 else BF16
        h, hmix = ffn_half(h, mods, norm_g[i], w_gu_all, w_down_all, layer=i, which=0, s=0, rows=rows_all,
                           seq_rows=seq, next_g=norm_g[i, 1], next_s=1, next_dtype=mix_dtype)
        rows_out = rows_all if ctx_out else rows_lat
        kw = dict(batch=batch, seq=seq, ctx_len=ctx_len)
        if mi == 0:
            p = dict(w_in=bf(hy_w_in[j]), b_in=hy_b_in[j], conv_w=hy_conv_w[j], conv_b=hy_conv_b[j],
                     f_w1=hy_f_w1[j], f_b1=hy_f_b1[j], f_w2=hy_f_w2[j], f_b2=hy_f_b2[j], f_w3=hy_f_w3[j],
                     sin_freq=hy_sin_freq[j], skip=hy_skip[j])
            z = hyena_mixer(hmix, p, d_model=d, want_ctx=ctx_out, **kw)
            w_o, b_o = bf(hy_w_out[j]), hy_b_out[j]
        elif mi == 1:
            pad = lambda w, ax: jnp.pad(w, [(0, RW_LORA_PAD - w.shape[a]) if a == ax else (0, 0) for a in range(w.ndim)])
            p = dict(mix=rw_mix[j], w_r=bf(rw_w_rkv[j, 0]), w_k=bf(rw_w_rkv[j, 1]), w_v=bf(rw_w_rkv[j, 2]),
                     g1=bf(rw_g1[j]), g2=bf(rw_g2[j]),
                     w1=bf(pad(rw_w1[j], 2)), w2=bf(pad(rw_w2[j], 1)), a1=bf(pad(rw_a1[j], 2)), a2=bf(pad(rw_a2[j], 1)),
                     w0=rw_w0[j], a0=rw_a0[j], k_k=rw_k_k[j], k_a=rw_k_a[j], r_k=rw_r_k[j].reshape(-1),
                     ln_w=rw_ln_w[j], ln_b=rw_ln_b[j])
            z = rwkv_mixer(hmix, p, d_model=d, **kw)
            w_o, b_o = bf(rw_w_o[j]), None
        elif mi == 2:
            wd = mla_w_down[j]
            pe = wd[:, MLA_Q_RANK + MLA_KV_RANK:]
            wuq = mla_w_uq[j].reshape(MLA_Q_RANK, MLA_HEADS, MLA_NOPE + MLA_ROPE)
            wq_pe = wuq[..., MLA_NOPE:]
            wuq_ext = jnp.concatenate([wuq[..., :MLA_NOPE], wq_pe,
                                       _rot_cols(wq_pe.reshape(MLA_Q_RANK, -1), MLA_ROPE).reshape(wq_pe.shape)], -1)
            p = dict(w_down_ext=bf(jnp.concatenate([wd, _rot_cols(pe, MLA_ROPE)], 1)),
                     w_uq_ext=bf(wuq_ext.reshape(MLA_Q_RANK, -1)), w_ukv=bf(mla_w_ukv[j]),
                     q_norm_g=mla_q_norm_g[j], kv_norm_g=mla_kv_norm_g[j])
            z = mla_mixer(hmix, p, want_ctx=ctx_out, **kw)
            w_o, b_o = bf(mla_w_o[j]), None
        else:
            w = gqa_w_qkv[j]
            nqc = GQA_HEADS * GQA_HEAD
            nkc = GQA_KV_HEADS * GQA_HEAD
            wq, wk, wv = w[:, :nqc], w[:, nqc:nqc + nkc], w[:, nqc + nkc:]

            def with_rot(wx, nh):
                both = jnp.stack([wx.reshape(d, nh, GQA_HEAD), _rot_cols(wx, GQA_HEAD).reshape(d, nh, GQA_HEAD)], 2)
                return bf(both.reshape(d, 2 * nh * GQA_HEAD))

            p = dict(w_q_ext=with_rot(wq, GQA_HEADS), w_k_ext=with_rot(wk, GQA_KV_HEADS), w_v=bf(wv),
                     q_norm_g=gqa_q_norm_g[j], q_norm_g_rot=_rot_gain(gqa_q_norm_g[j]),
                     k_norm_g=gqa_k_norm_g[j], k_norm_g_rot=_rot_gain(gqa_k_norm_g[j]))
            z = gqa_mixer(hmix, p, want_ctx=ctx_out, **kw)
            w_o, b_o = bf(gqa_w_o[j]), None
        h = mm(z, w_o, bias=b_o, res=h, gate=gate5, seq_rows=seq, rows=rows_out, tn=d, name="mix_out")
        last = i == depth - 1
        if last:
            h, out = ffn_half(h, mods, norm_g[i], w_gu_all, w_down_all, layer=i, which=1, s=2, rows=rows_out,
                              seq_rows=seq, next_g=final_g, next_s=None, next_dtype=F32)
        else:
            h = ffn_half(h, mods, norm_g[i], w_gu_all, w_down_all, layer=i, which=1, s=2, rows=rows_out,
                         seq_rows=seq)
    return out[:rows_lat].reshape(batch, seq, d)
```

```python
import functools
import math

import numpy as np
import jax
import jax.numpy as jnp
from jax import lax
from jax.experimental import pallas as pl
from jax.experimental.pallas import tpu as pltpu

F32 = jnp.float32
BF16 = jnp.bfloat16

VMEM_CAP_BYTES = 56 * 1024 * 1024
LANES = 128
SUBLANES = 8

NORM_EPS = 1e-6
LOG2E = 1.4426950408889634
N_MOD = 9
GRID_W = 64
ROPE_THETA = 10000.0
HY_DECAY_TARGET = 1e-2
HY_DECAY_PCT_SHORT = 0.3
HY_DECAY_PCT_LONG = 1.5
RW_HEAD = 64
RW_LORA_PAD = 128
GQA_HEAD = 128
GQA_HEADS = 16
GQA_KV_HEADS = 8
MLA_HEADS = 16
MLA_NOPE = 128
MLA_ROPE = 64
MLA_V = 128
MLA_Q_RANK = 512
MLA_KV_RANK = 512


def _cparams(sem):
    return pltpu.CompilerParams(dimension_semantics=sem, vmem_limit_bytes=VMEM_CAP_BYTES)


def _dot(a, b):
    return jnp.dot(a, b, preferred_element_type=F32)


def _dot_nt(a, b):
    return lax.dot_general(a, b, (((1,), (1,)), ((), ())), preferred_element_type=F32)


def _dot_tn(a, b):
    return lax.dot_general(a, b, (((0,), (0,)), ((), ())), preferred_element_type=F32)


def _split2(x):
    hi = x.astype(BF16)
    lo = (x - hi.astype(F32)).astype(BF16)
    return hi, lo


def _split3(x):
    hi = x.astype(BF16)
    r1 = x - hi.astype(F32)
    mid = r1.astype(BF16)
    lo = (r1 - mid.astype(F32)).astype(BF16)
    return hi, mid, lo


def _dot3(a, b, f=_dot):
    ah, al = _split2(a)
    bh, bl = _split2(b)
    return f(ah, bh) + f(ah, bl) + f(al, bh)


def _dotc2(mh, ml, d):
    dh = d.astype(BF16)
    return _dot(mh, dh) + _dot(ml, dh)


def _rms(x, g):
    return x * lax.rsqrt(jnp.mean(x * x, -1, keepdims=True) + NORM_EPS) * g


def _pick(n, cands):
    for c in cands:
        if n % c == 0:
            return c
    raise ValueError(f"no tile for {n}")


ROW_TILE = 512
WIDE_ROW_TILE = 1088
FFN_CHUNK = 512


def _row_tile(m):
    return WIDE_ROW_TILE if m % WIDE_ROW_TILE == 0 else ROW_TILE


def _adaln_body(c_ref, w_ref, b_ref, o_ref):
    c = c_ref[...]
    s = c * jax.nn.sigmoid(c)
    o_ref[0] = _dot(s.astype(BF16), w_ref[0].astype(BF16)) + b_ref[0]


def adaln_all(cvec8, ada_w, ada_b):
    depth, d, n = ada_w.shape
    tn = _pick(n, (2048, 1024, 512, 256, 128))
    return pl.pallas_call(
        _adaln_body,
        grid=(depth, n // tn),
        in_specs=[
            pl.BlockSpec((8, d), lambda l, j: (0, 0)),
            pl.BlockSpec((1, d, tn), lambda l, j: (l, 0, j)),
            pl.BlockSpec((1, 1, tn), lambda l, j: (l, 0, j)),
        ],
        out_specs=pl.BlockSpec((1, 8, tn), lambda l, j: (l, 0, j)),
        out_shape=jax.ShapeDtypeStruct((depth, 8, n), F32),
        compiler_params=_cparams(("arbitrary", "arbitrary")),
        name="adaln",
    )(cvec8, ada_w, ada_b.reshape(depth, 1, n))


def _mm_body(*refs, has_bias, act, has_norm, has_res):
    it = iter(refs)
    x_ref = next(it)
    w_ref = next(it)
    b_ref = next(it) if has_bias else None
    g_ref = next(it) if has_norm else None
    r_ref = next(it) if has_res else None
    gate_ref = next(it) if has_res else None
    o_ref = next(it)
    x = x_ref[...]
    if has_norm:
        x = _rms(x.astype(F32), g_ref[...])
    acc = _dot(x.astype(BF16), w_ref[...])
    if has_bias:
        acc = acc + b_ref[...]
    if act == "sigmoid":
        acc = jax.nn.sigmoid(acc)
    elif act == "tanh":
        acc = jnp.tanh(acc)
    if has_res:
        acc = r_ref[...] + gate_ref[0] * acc
    o_ref[...] = acc.astype(o_ref.dtype)


def mm(x, w, *, bias=None, act=None, norm_g=None, res=None, gate=None, out_dtype=F32,
       tm=ROW_TILE, tn=None, rows=None, x_col_block=0, x_row_block0=0, seq_rows=None, name="mm"):
    k, n = w.shape
    m = rows if rows is not None else x.shape[0]
    assert m % tm == 0, (m, tm)
    tn = tn or _pick(n, (512, 384, 256, 128))
    has_res = res is not None
    ins = [x, w]
    specs = [pl.BlockSpec((tm, k), lambda i, j: (i + x_row_block0, x_col_block)),
             pl.BlockSpec((k, tn), lambda i, j: (0, j))]
    if bias is not None:
        ins.append(bias.reshape(1, n).astype(F32))
        specs.append(pl.BlockSpec((1, tn), lambda i, j: (0, j)))
    if norm_g is not None:
        ins.append(norm_g.reshape(1, k).astype(F32))
        specs.append(pl.BlockSpec((1, k), lambda i, j: (0, 0)))
    if has_res:
        ngroups = gate.shape[0]
        per = seq_rows // tm
        ins += [res, gate]
        specs += [pl.BlockSpec((tm, tn), lambda i, j: (i, j)),
                  pl.BlockSpec((1, 1, tn), lambda i, j: (jnp.minimum(i // per, ngroups - 1), 0, j))]
    body = functools.partial(_mm_body, has_bias=bias is not None, act=act,
                             has_norm=norm_g is not None, has_res=has_res)
    return pl.pallas_call(
        body,
        grid=(m // tm, n // tn),
        in_specs=specs,
        out_specs=pl.BlockSpec((tm, tn), lambda i, j: (i, j)),
        out_shape=jax.ShapeDtypeStruct((m, n), out_dtype),
        compiler_params=_cparams(("parallel", "arbitrary")),
        name=name,
    )(*ins)


def _ffn_body(*refs, s, n_f, has_next, next_s):
    it = iter(refs)
    x_ref, mod_ref, g_ref, wa_ref, wb_ref, wd_ref = (next(it) for _ in range(6))
    gn_ref = next(it) if has_next else None
    o_ref = next(it)
    hn_ref = next(it) if has_next else None
    xn_sc = next(it)
    acc_sc = next(it)
    j = pl.program_id(1)

    @pl.when(j == 0)
    def _():
        n = _rms(x_ref[...], g_ref[s:s + 1, :])
        n = n * (1.0 + mod_ref[0, 3 * s + 1:3 * s + 2, :]) + mod_ref[0, 3 * s:3 * s + 1, :]
        xn_sc[...] = n.astype(BF16)
        acc_sc[...] = jnp.zeros_like(acc_sc)

    xn = xn_sc[...]
    a = _dot(xn, wa_ref[...])
    b = _dot(xn, wb_ref[...])
    h = (a * jax.nn.sigmoid(a)) * b
    acc_sc[...] += _dot(h.astype(BF16), wd_ref[...])

    @pl.when(j == n_f - 1)
    def _():
        xnew = x_ref[...] + 0.5 * mod_ref[0, 3 * s + 2:3 * s + 3, :] * acc_sc[...]
        o_ref[...] = xnew
        if has_next:
            hn = _rms(xnew, gn_ref[...])
            if next_s is not None:
                hn = hn * (1.0 + mod_ref[0, 3 * next_s + 1:3 * next_s + 2, :]) + mod_ref[0, 3 * next_s:3 * next_s + 1, :]
            hn_ref[...] = hn.astype(hn_ref.dtype)


def ffn_half(x, mods, norm_g3, w_gu, w_down, *, layer, which, s, rows, seq_rows, next_g=None, next_s=None,
             next_dtype=F32, tm=ROW_TILE, fc=FFN_CHUNK):
    d = x.shape[1]
    d_ff = w_down.shape[2]
    n_f = d_ff // fc
    assert d_ff % fc == 0 and rows % tm == 0
    per = seq_rows // tm
    ngroups = mods.shape[0]
    has_next = next_g is not None
    ins = [x, mods, norm_g3, w_gu, w_gu, w_down]
    specs = [
        pl.BlockSpec((tm, d), lambda i, j: (i, 0)),
        pl.BlockSpec((1, N_MOD, d), lambda i, j: (jnp.minimum(i // per, ngroups - 1), 0, 0)),
        pl.BlockSpec((3, d), lambda i, j: (0, 0)),
        pl.BlockSpec((None, None, d, fc), lambda i, j: (layer, which, 0, j)),
        pl.BlockSpec((None, None, d, fc), lambda i, j: (layer, which, 0, n_f + j)),
        pl.BlockSpec((None, None, fc, d), lambda i, j: (layer, which, j, 0)),
    ]
    out_shape = [jax.ShapeDtypeStruct((rows, d), F32)]
    out_specs = [pl.BlockSpec((tm, d), lambda i, j: (i, 0))]
    if has_next:
        ins.append(next_g.reshape(1, d))
        specs.append(pl.BlockSpec((1, d), lambda i, j: (0, 0)))
        out_shape.append(jax.ShapeDtypeStruct((rows, d), next_dtype))
        out_specs.append(pl.BlockSpec((tm, d), lambda i, j: (i, 0)))
    body = functools.partial(_ffn_body, s=s, n_f=n_f, has_next=has_next, next_s=next_s)
    outs = pl.pallas_call(
        body,
        grid=(rows // tm, n_f),
        in_specs=specs,
        out_specs=out_specs,
        out_shape=out_shape,
        scratch_shapes=[pltpu.VMEM((tm, d), BF16), pltpu.VMEM((tm, d), F32)],
        compiler_params=_cparams(("parallel", "arbitrary")),
        name="ffn_half",
    )(*ins)
    return outs if has_next else outs[0]


def _attn_body(*refs, use_lat, hp, n_kv, dqk):
    it = iter(refs)
    q_ref = next(it)
    kl_ref = next(it) if use_lat else None
    kc_ref = next(it)
    vl_refs = [next(it) for _ in range(n_kv)] if use_lat else None
    vc_refs = [next(it) for _ in range(n_kv)]
    o_ref = next(it)
    hs = range(hp)
    slot = [h * n_kv // hp for h in hs]
    q = [q_ref[:, h * dqk:(h + 1) * dqk] for h in hs]
    s_c = [_dot_nt(kc_ref[:, slot[h] * dqk:(slot[h] + 1) * dqk], q[h]) for h in hs]
    m = [jnp.max(s_c[h], 0, keepdims=True) for h in hs]
    if use_lat:
        s_l = [_dot_nt(kl_ref[:, slot[h] * dqk:(slot[h] + 1) * dqk], q[h]) for h in hs]
        m = [jnp.maximum(m[h], jnp.max(s_l[h], 0, keepdims=True)) for h in hs]
    p_c = [jnp.exp2(s_c[h] - m[h]) for h in hs]
    den = [jnp.sum(p_c[h], 0, keepdims=True) for h in hs]
    o = [_dot_tn(vc_refs[slot[h]][...], p_c[h].astype(BF16)) for h in hs]
    if use_lat:
        p_l = [jnp.exp2(s_l[h] - m[h]) for h in hs]
        den = [den[h] + jnp.sum(p_l[h], 0, keepdims=True) for h in hs]
        o = [o[h] + _dot_tn(vl_refs[slot[h]][...], p_l[h].astype(BF16)) for h in hs]
    for h in hs:
        o_ref[:, h * LANES:(h + 1) * LANES] = (o[h] / den[h]).T.astype(o_ref.dtype)


def attention(q_arr, k_arr, v_arr, v_cb, *, dqk, kv_group, n_heads, batch, seq, ctx_len, lat_queries,
              tq=ROW_TILE, hp=1):
    assert n_heads % hp == 0 and (kv_group % hp == 0 or hp % kv_group == 0)
    n_kv = max(hp // kv_group, 1)
    kv0 = lambda p: (p * hp) // kv_group
    ctx0 = (batch * seq) // ctx_len
    if lat_queries:
        tq = min(tq, seq)
        nq = seq // tq
        q_row = lambda b, t: b * nq + t
        o_row = lambda b, t: b * nq + t
        rows_q = batch * seq
    else:
        tq = ctx_len
        nq = 1
        q_row = lambda b, t: ctx0 + b
        o_row = lambda b, t: b
        rows_q = batch * ctx_len
    ins = [q_arr]
    specs = [pl.BlockSpec((tq, hp * dqk), lambda b, p, t: (q_row(b, t), p))]
    if lat_queries:
        ins.append(k_arr)
        specs.append(pl.BlockSpec((seq, n_kv * dqk), lambda b, p, t: (b, kv0(p) // n_kv)))
    ins.append(k_arr)
    specs.append(pl.BlockSpec((ctx_len, n_kv * dqk), lambda b, p, t: (ctx0 + b, kv0(p) // n_kv)))
    if lat_queries:
        for s in range(n_kv):
            ins.append(v_arr)
            specs.append(pl.BlockSpec((seq, LANES), lambda b, p, t, s=s: (b, v_cb(kv0(p) + s))))
    for s in range(n_kv):
        ins.append(v_arr)
        specs.append(pl.BlockSpec((ctx_len, LANES), lambda b, p, t, s=s: (ctx0 + b, v_cb(kv0(p) + s))))
    return pl.pallas_call(
        functools.partial(_attn_body, use_lat=lat_queries, hp=hp, n_kv=n_kv, dqk=dqk),
        grid=(batch, n_heads // hp, nq),
        in_specs=specs,
        out_specs=pl.BlockSpec((tq, hp * LANES), lambda b, p, t: (o_row(b, t), p)),
        out_shape=jax.ShapeDtypeStruct((rows_q, n_heads * LANES), BF16),
        compiler_params=_cparams(("parallel", "parallel", "arbitrary")),
        name="attention_lat" if lat_queries else "attention_ctx",
    )(*ins)


def _rope_tables(seq, ctx_len, batch, rot_dim):
    half = rot_dim // 4
    inv_freq = ROPE_THETA ** (-jnp.arange(half, dtype=F32) / half)
    t = jnp.arange(seq, dtype=jnp.int32)
    row = (t // GRID_W).astype(F32)
    col = (t % GRID_W).astype(F32)
    ang_r = row[:, None] * inv_freq[None]
    ang_c = col[:, None] * inv_freq[None]
    cos = jnp.concatenate([jnp.cos(ang_r), jnp.cos(ang_r), jnp.cos(ang_c), jnp.cos(ang_c)], -1)
    sin = jnp.concatenate([jnp.sin(ang_r), jnp.sin(ang_r), jnp.sin(ang_c), jnp.sin(ang_c)], -1)
    cos = jnp.concatenate([jnp.tile(cos, (batch, 1)), jnp.ones((batch * ctx_len, rot_dim), F32)], 0)
    sin = jnp.concatenate([jnp.tile(sin, (batch, 1)), jnp.zeros((batch * ctx_len, rot_dim), F32)], 0)
    return cos, sin


def _rot_cols(w, rot_dim):
    lead = w.shape[:-1]
    q = rot_dim // 4
    w5 = w.reshape(lead + (-1, 2, 2, q))
    rot = jnp.concatenate([-w5[..., 1:2, :], w5[..., 0:1, :]], axis=-2)
    return rot.reshape(w.shape)


def _rope_half(b, cos, sin):
    return b * cos + pltpu.roll(b, LANES // 2, 1) * sin


def _proj_rope_body(*refs, mode, has_norm, scale):
    it = iter(refs)
    x_ref, w_ref = next(it), next(it)
    ng_ref = next(it) if has_norm else None
    cos_ref, sin_ref = next(it), next(it)
    x = x_ref[...]
    if has_norm:
        x = _rms(x.astype(F32), ng_ref[...])
    acc = _dot(x.astype(BF16), w_ref[...])
    a, b = acc[:, :LANES], acc[:, LANES:]
    if mode == "gqa":
        g_ref, gr_ref, o_ref = next(it), next(it), next(it)
        r = lax.rsqrt(jnp.mean(a * a, -1, keepdims=True) + NORM_EPS) * scale
        o_ref[...] = (((a * g_ref[...]) * cos_ref[...] + (b * gr_ref[...]) * sin_ref[...]) * r).astype(o_ref.dtype)
    elif mode == "mla_q":
        o_ref = next(it)
        o_ref[:, :LANES] = (a * scale).astype(o_ref.dtype)
        o_ref[:, LANES:] = (_rope_half(b, cos_ref[...], sin_ref[...]) * scale).astype(o_ref.dtype)
    else:
        pe_ref, k_ref, v_ref = next(it), next(it), next(it)
        k_ref[:, :LANES] = a.astype(k_ref.dtype)
        k_ref[:, LANES:] = _rope_half(pe_ref[...], cos_ref[...], sin_ref[...]).astype(k_ref.dtype)
        v_ref[...] = b.astype(v_ref.dtype)


def proj_rope(x, w, cos, sin, *, mode, n_heads, scale=1.0, norm_g=None, x_col_block=0, gains=None,
              pe=None, pe_cb=0, tm=WIDE_ROW_TILE):
    m = x.shape[0]
    k = w.shape[0]
    ins = [x, w]
    specs = [pl.BlockSpec((tm, k), lambda i, h: (i, x_col_block)),
             pl.BlockSpec((k, 2 * LANES), lambda i, h: (0, h))]
    if norm_g is not None:
        ins.append(norm_g.reshape(1, k).astype(F32))
        specs.append(pl.BlockSpec((1, k), lambda i, h: (0, 0)))
    tab = pl.BlockSpec((tm, LANES), lambda i, h: (i, 0))
    ins += [cos, sin]
    specs += [tab, tab]
    if mode == "gqa":
        vec = pl.BlockSpec((1, LANES), lambda i, h: (0, 0))
        ins += [gains[0].reshape(1, LANES), gains[1].reshape(1, LANES)]
        specs += [vec, vec]
        out_shape = jax.ShapeDtypeStruct((m, n_heads * LANES), BF16)
        out_specs = pl.BlockSpec((tm, LANES), lambda i, h: (i, h))
    elif mode == "mla_q":
        out_shape = jax.ShapeDtypeStruct((m, n_heads * 2 * LANES), BF16)
        out_specs = pl.BlockSpec((tm, 2 * LANES), lambda i, h: (i, h))
    else:
        ins.append(pe)
        specs.append(pl.BlockSpec((tm, LANES), lambda i, h: (i, pe_cb)))
        out_shape = [jax.ShapeDtypeStruct((m, n_heads * 2 * LANES), BF16),
                     jax.ShapeDtypeStruct((m, n_heads * LANES), BF16)]
        out_specs = [pl.BlockSpec((tm, 2 * LANES), lambda i, h: (i, h)),
                     pl.BlockSpec((tm, LANES), lambda i, h: (i, h))]
    return pl.pallas_call(
        functools.partial(_proj_rope_body, mode=mode, has_norm=norm_g is not None, scale=scale),
        grid=(m // tm, n_heads),
        in_specs=specs,
        out_specs=out_specs,
        out_shape=out_shape,
        compiler_params=_cparams(("parallel", "arbitrary")),
        name="proj_" + mode,
    )(*ins)


class _FftPlan:
    def __init__(self, L):
        self.L = L
        self.N = 2 * L
        self.N2 = 128 if L >= 1024 else 64
        self.N1 = self.N // self.N2
        self.n1_in = self.N1 // 2
        self.K1 = self.N1 // 2 + 1
        self.nq = self.N2 // SUBLANES
        self.kf_cols = max(self.n1_in * SUBLANES, LANES)
        self.kf_rows = 2 * self.K1 * SUBLANES
        self.kb_cols = -(-self.kf_rows // LANES) * LANES
        self.kb_rows = self.n1_in * SUBLANES

    def constants(self):
        N, N1, N2, K1, n1_in = self.N, self.N1, self.N2, self.K1, self.n1_in
        eye = np.eye(SUBLANES)
        k1 = np.arange(K1)[:, None]
        n1 = np.arange(n1_in)[None, :]
        th = 2 * np.pi * k1 * n1 / N1
        kf = np.zeros((self.kf_rows, self.kf_cols))
        kf[:K1 * SUBLANES, :n1_in * SUBLANES] = np.kron(np.cos(th), eye)
        kf[K1 * SUBLANES:, :n1_in * SUBLANES] = np.kron(-np.sin(th), eye)
        c = np.full((K1,), 2.0)
        c[0] = 1.0
        c[-1] = 1.0
        thb = 2 * np.pi * np.arange(n1_in)[:, None] * np.arange(K1)[None, :] / N1
        kb = np.zeros((self.kb_rows, self.kb_cols))
        kb[:, :K1 * SUBLANES] = np.kron(np.cos(thb) * c[None, :], eye)
        kb[:, K1 * SUBLANES:2 * K1 * SUBLANES] = np.kron(-np.sin(thb) * c[None, :], eye)
        a = 2 * np.pi * np.outer(np.arange(N2), np.arange(N2)) / N2
        C, S = np.cos(a), np.sin(a)
        w2f = np.block([[C, S], [-S, C]])
        w2i = np.block([[C, -S], [S, C]])
        tw_ang = 2 * np.pi * np.outer(np.arange(N2), np.arange(K1)) / N
        twr = np.repeat(np.cos(tw_ang), LANES, axis=1)
        twi = np.repeat(np.sin(tw_ang), LANES, axis=1)

        def hl(m):
            m32 = jnp.asarray(m, F32)
            hi = m32.astype(BF16)
            lo = (m32 - hi.astype(F32)).astype(BF16)
            return hi, lo

        out = {}
        for name, mat in (("kf", kf), ("kb", kb), ("w2f", w2f), ("w2i", w2i)):
            out[name + "_h"], out[name + "_l"] = hl(mat)
        out["twr"] = jnp.asarray(twr, F32)
        out["twi"] = jnp.asarray(twi, F32)
        return out


def _cmul(ar, ai, br, bi):
    return ar * br - ai * bi, ar * bi + ai * br


def _time_to_strided(plan, x_ref, xt_ref):
    N2, nq = plan.N2, plan.nq
    for n in range(plan.n1_in):
        xt_ref[n * SUBLANES:(n + 1) * SUBLANES, :] = jnp.concatenate(
            [x_ref[N2 * n + SUBLANES * q:N2 * n + SUBLANES * (q + 1), :] for q in range(nq)], axis=1)
    pad = plan.kf_cols - plan.n1_in * SUBLANES
    if pad:
        xt_ref[plan.n1_in * SUBLANES:, :] = jnp.zeros((pad, xt_ref.shape[1]), F32)


def _strided_to_spectrum(plan, a_ref, s_ref, c):
    N2, K1, nq = plan.N2, plan.K1, plan.nq
    for part in range(2):
        for k in range(K1):
            r0 = (part * K1 + k) * SUBLANES
            s_ref[part * N2:(part + 1) * N2, k * c:(k + 1) * c] = jnp.concatenate(
                [a_ref[r0:r0 + SUBLANES, q * c:(q + 1) * c] for q in range(nq)], axis=0)


def _spectrum_to_strided(plan, s_ref, b_ref, c):
    N2, K1, nq = plan.N2, plan.K1, plan.nq
    for part in range(2):
        for k in range(K1):
            r0 = (part * K1 + k) * SUBLANES
            b_ref[r0:r0 + SUBLANES, :] = jnp.concatenate(
                [s_ref[part * N2 + SUBLANES * q:part * N2 + SUBLANES * (q + 1), k * c:(k + 1) * c] for q in range(nq)],
                axis=1)
    pad = plan.kb_cols - plan.kf_rows
    if pad:
        b_ref[plan.kf_rows:, :] = jnp.zeros((pad, b_ref.shape[1]), F32)


def _strided_to_time(plan, yt, c):
    nq = plan.nq
    slabs = []
    for n in range(plan.n1_in):
        blk = yt[n * SUBLANES:(n + 1) * SUBLANES, :]
        slabs.extend(blk[:, q * c:(q + 1) * c] for q in range(nq))
    return jnp.concatenate(slabs, axis=0)


def _fft_fwd(plan, x_ref, consts, xt_ref, a_ref, s_ref, dotc):
    c = x_ref.shape[1]
    _time_to_strided(plan, x_ref, xt_ref)
    a_ref[:plan.kf_rows, :] = dotc(consts["kf_h"][...], consts["kf_l"][...], xt_ref[...])
    _strided_to_spectrum(plan, a_ref, s_ref, c)
    n2 = plan.N2
    tr, ti = consts["twr"][...], consts["twi"][...]
    xr, xi = _cmul(s_ref[:n2, :], s_ref[n2:, :], tr, -ti)
    s_ref[...] = dotc(consts["w2f_h"][...], consts["w2f_l"][...], jnp.concatenate([xr, xi], axis=0))


def _fft_conv_apply(plan, x_ref, g_ref, consts, xt_ref, a_ref, s_ref):
    c = x_ref.shape[1]
    n2 = plan.N2
    _fft_fwd(plan, x_ref, consts, xt_ref, a_ref, s_ref, _dotc2)
    pr, pi = _cmul(s_ref[:n2, :], s_ref[n2:, :], g_ref[:n2, :], g_ref[n2:, :])
    z = _dotc2(consts["w2i_h"][...], consts["w2i_l"][...], jnp.concatenate([pr, pi], axis=0))
    zr, zi = _cmul(z[:n2], z[n2:], consts["twr"][...], consts["twi"][...])
    s_ref[:n2, :] = zr
    s_ref[n2:, :] = zi
    _spectrum_to_strided(plan, s_ref, a_ref, c)
    yt = _dotc2(consts["kb_h"][...], consts["kb_l"][...], a_ref[...])
    return _strided_to_time(plan, yt, c)


def _fft_scratch(plan):
    wl = plan.nq * LANES
    return [pltpu.VMEM((plan.kf_cols, wl), F32),
            pltpu.VMEM((max(plan.kf_rows, plan.kb_cols), wl), F32),
            pltpu.VMEM((2 * plan.N2, plan.K1 * LANES), F32)]


_CONST_NAMES = ("kf_h", "kf_l", "kb_h", "kb_l", "w2f_h", "w2f_l", "w2i_h", "w2i_l", "twr", "twi")


def _const_specs(consts, nargs):
    ins, specs = [], []
    for nme in _CONST_NAMES:
        a = consts[nme]
        ins.append(a)
        if nargs == 1:
            specs.append(pl.BlockSpec(a.shape, lambda c: (0, 0), pipeline_mode=pl.Buffered(1)))
        else:
            specs.append(pl.BlockSpec(a.shape, lambda c, b: (0, 0), pipeline_mode=pl.Buffered(1)))
    return ins, specs


def _hy_hidden_body(z_ref, w1_ref, b1_ref, w2_ref, b2_ref, sf_ref, o_ref):
    h = jnp.sin(sf_ref[0:1, :] * (_dot3(z_ref[...], w1_ref[...]) + b1_ref[...]))
    h = jnp.sin(sf_ref[1:2, :] * (_dot3(h, w2_ref[...]) + b2_ref[...]))
    o_ref[...] = h


def hyena_hidden(L, f_w1, f_b1, f_w2, f_b2, sin_freq):
    emb, hid = f_w1.shape
    bands = (emb - 1) // 2
    t = jnp.linspace(0.0, 1.0, L, dtype=F32)[:, None]
    w = (2.0 * math.pi / L) * jnp.arange(L, dtype=F32)[:, None]
    f = jnp.linspace(1e-4, bands - 1, bands, dtype=F32)[None]
    z = jnp.concatenate([t, jnp.cos(f * w), -jnp.sin(f * w)], -1)
    embp = -(-emb // LANES) * LANES
    z = jnp.pad(z, ((0, 0), (0, embp - emb)))
    w1 = jnp.pad(f_w1, ((0, embp - emb), (0, 0)))
    ins = [z, w1, f_b1.reshape(1, hid), f_w2, f_b2.reshape(1, hid), sin_freq]
    return pl.pallas_call(
        _hy_hidden_body,
        out_shape=jax.ShapeDtypeStruct((L, hid), F32),
        name="hyena_hidden",
    )(*ins)


def _hy_filter_body(h_ref, wf_ref, wb_ref, dl_ref, *rest, plan):
    consts = {nme: r for nme, r in zip(_CONST_NAMES, rest)}
    g_ref, x_sc, xt_sc, a_sc, s_sc, sb_sc = rest[len(_CONST_NAMES):]
    L, n2 = plan.L, plan.N2
    t = lax.broadcasted_iota(jnp.int32, (L, LANES), 0).astype(F32) * (1.0 / (L - 1))
    win = jnp.exp(-t * dl_ref[...])
    row0 = lax.broadcasted_iota(jnp.int32, (L, LANES), 0) == 0
    h = h_ref[...]
    fwd = _dot3(h, wf_ref[...]) * win
    bwd = jnp.where(row0, 0.0, _dot3(h, wb_ref[...]) * win)
    ss = jnp.sum(fwd * fwd, 0, keepdims=True) + jnp.sum(bwd * bwd, 0, keepdims=True)
    nu = lax.rsqrt(ss + 1e-12) * (1.0 / plan.N)
    x_sc[...] = bwd * nu
    _fft_fwd(plan, x_sc, consts, xt_sc, a_sc, sb_sc, _dotc2)
    x_sc[...] = fwd * nu
    _fft_fwd(plan, x_sc, consts, xt_sc, a_sc, s_sc, _dotc2)
    g_ref[:n2, :] = s_sc[:n2, :] + sb_sc[:n2, :]
    g_ref[n2:, :] = s_sc[n2:, :] - sb_sc[n2:, :]


def hyena_filter_spectra(plan, consts, hidden, f_w3, d_model, n_order):
    L = plan.L
    hid = hidden.shape[1]
    ncb = d_model // LANES
    deltas = jnp.abs(jnp.linspace(math.log(HY_DECAY_TARGET) / HY_DECAY_PCT_SHORT,
                                  math.log(HY_DECAY_TARGET) / HY_DECAY_PCT_LONG, d_model, dtype=F32)).reshape(1, d_model)
    ins = [hidden, f_w3, f_w3, deltas]
    specs = [pl.BlockSpec((L, hid), lambda c, n: (0, 0)),
             pl.BlockSpec((hid, LANES), lambda c, n: (0, (2 * n) * ncb + c)),
             pl.BlockSpec((hid, LANES), lambda c, n: (0, (2 * n + 1) * ncb + c)),
             pl.BlockSpec((1, LANES), lambda c, n: (0, c))]
    ci, cs = _const_specs(consts, 2)
    ins += ci
    specs += cs
    srows, scols = 2 * plan.N2, plan.K1 * LANES
    return pl.pallas_call(
        functools.partial(_hy_filter_body, plan=plan),
        grid=(ncb, n_order),
        in_specs=specs,
        out_specs=pl.BlockSpec((None, None, srows, scols), lambda c, n: (n, c, 0, 0)),
        out_shape=jax.ShapeDtypeStruct((n_order, ncb, srows, scols), F32),
        scratch_shapes=[pltpu.VMEM((L, LANES), F32)] + _fft_scratch(plan) + [pltpu.VMEM((srows, scols), F32)],
        compiler_params=_cparams(("arbitrary", "arbitrary")),
        name="hyena_filter",
    )(*ins)


def _conv3(p, w_ref, b_ref, L):
    rows = lax.broadcasted_iota(jnp.int32, p.shape, 0)
    prev = jnp.where(rows == 0, 0.0, pltpu.roll(p, 1, 0))
    nxt = jnp.where(rows == L - 1, 0.0, pltpu.roll(p, L - 1, 0))
    return prev * w_ref[0:1, :] + p * w_ref[1:2, :] + nxt * w_ref[2:3, :] + b_ref[...]


def _hy_conv_body(*refs, plan, order, conv_u):
    it = iter(refs)
    u_ref, gate_ref = next(it), next(it)
    if conv_u:
        cwu_ref, cbu_ref = next(it), next(it)
    cwg_ref, cbg_ref = next(it), next(it)
    skip_ref = next(it)
    g_ref = next(it)
    consts = {nme: next(it) for nme in _CONST_NAMES}
    o_ref = next(it)
    x_sc, xt_sc, a_sc, s_sc = (next(it) for _ in range(4))
    L = plan.L
    u = u_ref[...]
    if conv_u:
        u = _conv3(u, cwu_ref, cbu_ref, L)
    x_sc[...] = u
    y = _fft_conv_apply(plan, x_sc, g_ref, consts, xt_sc, a_sc, s_sc)
    gate = _conv3(gate_ref[...], cwg_ref, cbg_ref, L)
    o_ref[...] = (gate * (y + x_sc[...] * skip_ref[order:order + 1, :])).astype(o_ref.dtype)


def hyena_conv(plan, consts, u_arr, u_cb0, gate_arr, gate_cb0, conv_w, conv_b, skip, spectra, *,
               order, conv_u, u_row_block0, gate_row_block0, batch, d_model, out_dtype):
    L = plan.L
    ncb = d_model // LANES
    ins = [u_arr, gate_arr]
    specs = [pl.BlockSpec((L, LANES), lambda c, b: (u_row_block0 + b, u_cb0 + c)),
             pl.BlockSpec((L, LANES), lambda c, b: (gate_row_block0 + b, gate_cb0 + c))]
    if conv_u:
        ins += [conv_w, conv_b]
        specs += [pl.BlockSpec((3, LANES), lambda c, b: (0, u_cb0 + c)),
                  pl.BlockSpec((1, LANES), lambda c, b: (0, u_cb0 + c))]
    srows, scols = 2 * plan.N2, plan.K1 * LANES
    ins += [conv_w, conv_b, skip, spectra]
    specs += [pl.BlockSpec((3, LANES), lambda c, b: (0, gate_cb0 + c)),
              pl.BlockSpec((1, LANES), lambda c, b: (0, gate_cb0 + c)),
              pl.BlockSpec((skip.shape[0], LANES), lambda c, b: (0, c)),
              pl.BlockSpec((None, None, srows, scols), lambda c, b: (order, c, 0, 0))]
    ci, cs = _const_specs(consts, 2)
    ins += ci
    specs += cs
    return pl.pallas_call(
        functools.partial(_hy_conv_body, plan=plan, order=order, conv_u=conv_u),
        grid=(ncb, batch),
        in_specs=specs,
        out_specs=pl.BlockSpec((L, LANES), lambda c, b: (b, c)),
        out_shape=jax.ShapeDtypeStruct((batch * L, d_model), out_dtype),
        scratch_shapes=[pltpu.VMEM((L, LANES), F32)] + _fft_scratch(plan),
        compiler_params=_cparams(("parallel", "arbitrary")),
        name=f"hyena_conv{order}_L{L}",
    )(*ins)


def hyena_mixer(hmix, p, *, batch, seq, ctx_len, d_model, want_ctx):
    n_order = p["skip"].shape[0]
    rows = batch * seq + (batch * ctx_len if want_ctx else 0)
    pr = mm(hmix, p["w_in"], bias=p["b_in"], name="hy_in", rows=rows, tm=_row_tile(rows))
    conv_b = p["conv_b"].reshape(1, -1)
    ncb = d_model // LANES
    outs = []
    streams = [(seq, 0)]
    if want_ctx:
        streams.append((ctx_len, (batch * seq) // ctx_len))
    for L, rb0 in streams:
        plan = _FftPlan(L)
        consts = plan.constants()
        hidden = hyena_hidden(L, p["f_w1"], p["f_b1"], p["f_w2"], p["f_b2"], p["sin_freq"])
        spectra = hyena_filter_spectra(plan, consts, hidden, p["f_w3"], d_model, n_order)
        z1 = hyena_conv(plan, consts, pr, 0, pr, ncb, p["conv_w"], conv_b, p["skip"], spectra,
                        order=0, conv_u=True, u_row_block0=rb0, gate_row_block0=rb0, batch=batch, d_model=d_model,
                        out_dtype=F32)
        z2 = hyena_conv(plan, consts, z1, 0, pr, 2 * ncb, p["conv_w"], conv_b, p["skip"], spectra,
                        order=1, conv_u=False, u_row_block0=0, gate_row_block0=rb0, batch=batch, d_model=d_model,
                        out_dtype=BF16)
        outs.append(z2)
    return outs[0] if len(outs) == 1 else jnp.concatenate(outs, 0)


def _headsum(x, e_ref):
    hi, lo = _split2(x)
    return _dot(hi, e_ref[...]) + _dot(lo, e_ref[...])


def _rw_mix_body(h_ref, hp_ref, hn_ref, mix_ref, o_ref, *, tm, lat_tiles, tiles_per_seq):
    i = pl.program_id(0)
    is_ctx = i >= lat_tiles
    first = jnp.logical_or(is_ctx, i % tiles_per_seq == 0)
    last = jnp.logical_or(is_ctx, i % tiles_per_seq == tiles_per_seq - 1)
    h = h_ref[...]
    rows = lax.broadcasted_iota(jnp.int32, h.shape, 0)
    pedge = jnp.where(first, 0.0, hp_ref[SUBLANES - 1:SUBLANES, :])
    nedge = jnp.where(last, 0.0, hn_ref[0:1, :])
    prev = jnp.where(rows == 0, pedge, pltpu.roll(h, 1, 0))
    nxt = jnp.where(rows == tm - 1, nedge, pltpu.roll(h, tm - 1, 0))
    dp = prev - h
    dn = nxt - h
    for n in range(o_ref.shape[0]):
        o_ref[n] = (h + dp * mix_ref[0, n:n + 1, :] + dn * mix_ref[1, n:n + 1, :]).astype(o_ref.dtype)


def rwkv_mix(h, mix, *, batch, seq, ctx_len):
    m, d = h.shape
    tm = ctx_len
    n_shift = mix.shape[1]
    nblk = m // SUBLANES
    per = tm // SUBLANES
    body = functools.partial(_rw_mix_body, tm=tm, lat_tiles=(batch * seq) // tm, tiles_per_seq=seq // tm)
    return pl.pallas_call(
        body,
        grid=(m // tm,),
        in_specs=[
            pl.BlockSpec((tm, d), lambda i: (i, 0)),
            pl.BlockSpec((SUBLANES, d), lambda i: (jnp.maximum(i * per - 1, 0), 0)),
            pl.BlockSpec((SUBLANES, d), lambda i: (jnp.minimum((i + 1) * per, nblk - 1), 0)),
            pl.BlockSpec((2, n_shift, d), lambda i: (0, 0, 0)),
        ],
        out_specs=pl.BlockSpec((n_shift, tm, d), lambda i: (0, i, 0)),
        out_shape=jax.ShapeDtypeStruct((n_shift, m, d), BF16),
        compiler_params=_cparams(("parallel",)),
        name="rwkv_mix",
    )(h, h, h, mix)


def _rw_prep_body(k_ref, lw0_ref, lw1_ref, al0_ref, al1_ref, w0_ref, a0_ref, kk_ref, ka_ref, e_ref,
                  olw0, olw1, okd0, okd1, obb0, obb1, oaa):
    k = k_ref[...]
    kk0 = k * kk_ref[...]
    kk = kk0 * lax.rsqrt(_headsum(kk0 * kk0, e_ref) + 1e-12)
    oaa[...] = -kk
    for d, (lw_ref, al_ref, olw, okd, obb) in enumerate(((lw0_ref, al0_ref, olw0, okd0, obb0),
                                                         (lw1_ref, al1_ref, olw1, okd1, obb1))):
        z = -(w0_ref[d:d + 1, :] + lw_ref[...])
        softplus = jnp.maximum(z, 0.0) + jnp.log(1.0 + jnp.exp(-jnp.abs(z)))
        log_w = -softplus - 0.5
        olw[...] = -jnp.exp(log_w)
        a = jax.nn.sigmoid(a0_ref[d:d + 1, :] + al_ref[...])
        okd[...] = k * (1.0 + (a - 1.0) * ka_ref[...])
        obb[...] = kk * a


def rwkv_prep(k, lw0, lw1, al0, al1, w0, a0, k_k, k_a, e128, tm=WIDE_ROW_TILE):
    m, d = k.shape
    blk = pl.BlockSpec((tm, LANES), lambda i, c: (i, c))
    vec2 = pl.BlockSpec((2, LANES), lambda i, c: (0, c))
    vec1 = pl.BlockSpec((1, LANES), lambda i, c: (0, c))
    sh = jax.ShapeDtypeStruct((m, d), F32)
    return pl.pallas_call(
        _rw_prep_body,
        grid=(m // tm, d // LANES),
        in_specs=[blk] * 5 + [vec2, vec2, vec1, vec1, pl.BlockSpec((LANES, LANES), lambda i, c: (0, 0))],
        out_specs=[blk] * 7,
        out_shape=[sh] * 7,
        compiler_params=_cparams(("parallel", "arbitrary")),
        name="rwkv_prep",
    )(k, lw0, lw1, al0, al1, w0, a0, k_k.reshape(1, d), k_a.reshape(1, d), e128)


def _b16(x):
    return x.astype(BF16)


def _wkv_body(r_ref, lw_ref, k_ref, v_ref, a_ref, b_ref, o_ref, s_sc, *, chunk, n_heads, reverse):
    c = pl.program_id(2)

    @pl.when(c == 0)
    def _():
        s_sc[...] = jnp.zeros_like(s_sc)

    C = chunk
    hs = range(n_heads)
    row = lax.broadcasted_iota(jnp.int32, (C, C), 0)
    col = lax.broadcasted_iota(jnp.int32, (C, C), 1)
    if reverse:
        tri_incl = col >= row
        tri_strict = col > row
        last = 0
    else:
        tri_incl = col <= row
        tri_strict = col < row
        last = C - 1
    same = lambda s: (row >> int(math.log2(s))) == (col >> int(math.log2(s)))
    eye = (row == col).astype(F32)
    cum_m = tri_incl.astype(BF16)
    sls = [slice(h * RW_HEAD, (h + 1) * RW_HEAD) for h in hs]

    lw = [lw_ref[:, sl] for sl in sls]
    cum = []
    for h in hs:
        l1, l2, l3 = _split3(lw[h])
        cum.append(_dot(cum_m, l1) + _dot(cum_m, l2) + _dot(cum_m, l3))
    p_inc = [jnp.exp(cum[h]) for h in hs]
    x1 = [jnp.concatenate([a_ref[:, sls[h]] * jnp.exp(cum[h] - lw[h]), r_ref[:, sls[h]] * p_inc[h]], axis=0) for h in hs]
    x2 = []
    for h in hs:
        p_inv = jnp.exp(-cum[h])
        x2.append(jnp.concatenate([b_ref[:, sls[h]] * p_inv, k_ref[:, sls[h]] * p_inv], axis=0))
    v = [v_ref[:, sl] for sl in sls]
    s0 = [s_sc[h] for h in hs]
    def cat_a(x):
        hi, lo = _split2(x)
        return jnp.concatenate([hi, lo, hi], axis=1)

    def cat_b(x):
        hi, lo = _split2(x)
        return jnp.concatenate([hi, hi, lo], axis=1)

    x1c = [cat_a(x1[h]) for h in hs]
    mfull = [_dot_nt(x1c[h], cat_b(x2[h])) for h in hs]
    w0 = [_dot_nt(x1c[h], cat_b(s0[h])) for h in hs]
    n_ab = [jnp.where(tri_strict, mfull[h][:C, :C], 0.0) for h in hs]
    a_ak = [_b16(jnp.where(tri_strict, mfull[h][:C, C:], 0.0)) for h in hs]
    a_r = [_b16(jnp.concatenate([jnp.where(tri_incl, mfull[h][C:, :C], 0.0),
                                 jnp.where(tri_incl, mfull[h][C:, C:], 0.0)], axis=1)) for h in hs]
    rhs = [w0[h][:C] + _dot(a_ak[h], _b16(v[h])) for h in hs]

    n4 = [jnp.where(same(4), n_ab[h], 0.0) for h in hs]
    n4b = [_b16(n4[h]) for h in hs]
    sq = [_dot(n4b[h], n4b[h]) for h in hs]
    t = [(eye + n4[h]) + _dot(_b16(eye + n4[h]), _b16(sq[h])) for h in hs]
    s = 4
    while s < C:
        off = jnp.logical_and(jnp.logical_and(same(2 * s), jnp.logical_not(same(s))), tri_strict)
        tb = [_b16(t[h]) for h in hs]
        tmp = [_dot(tb[h], _b16(jnp.where(off, n_ab[h], 0.0))) for h in hs]
        t = [t[h] + _dot(_b16(tmp[h]), tb[h]) for h in hs]
        s *= 2
    tb = [_b16(t[h]) for h in hs]
    u = [_dot(tb[h], _b16(rhs[h])) for h in hs]
    res = [rhs[h] - u[h] + _dot3(n_ab[h], u[h]) for h in hs]
    u = [u[h] + _dot(tb[h], _b16(res[h])) for h in hs]
    uv = [jnp.concatenate([u[h], v[h]], axis=0) for h in hs]
    y = [w0[h][C:] + _dot(a_r[h], _b16(uv[h])) for h in hs]
    for h in hs:
        s_sc[h] = (s0[h] + _dot3(uv[h], x2[h], _dot_tn)) * p_inc[h][last:last + 1, :]
    o_ref[...] = jnp.concatenate(y, axis=1)


def wkv7(r, lw, kd, v, aa, bb, *, reverse, batch, seq, ctx_len, chunk=128, heads_per_step=16):
    m, d = r.shape
    heads_per_step = min(heads_per_step, d // RW_HEAD)
    wl = heads_per_step * RW_HEAD
    n_ctx = ctx_len // chunk
    n_lat = seq // chunk
    n_chunks = n_ctx + n_lat
    ctx0 = (batch * seq) // chunk

    def row_block(b, c):
        if reverse:
            return jnp.where(c < n_ctx, ctx0 + b * n_ctx + (n_ctx - 1 - c), b * n_lat + (n_lat - 1 - (c - n_ctx)))
        return jnp.where(c < n_ctx, ctx0 + b * n_ctx + c, b * n_lat + (c - n_ctx))

    blk = pl.BlockSpec((chunk, wl), lambda b, g, c: (row_block(b, c), g))
    body = functools.partial(_wkv_body, chunk=chunk, n_heads=heads_per_step, reverse=reverse)
    return pl.pallas_call(
        body,
        grid=(batch, d // wl, n_chunks),
        in_specs=[blk] * 6,
        out_specs=blk,
        out_shape=jax.ShapeDtypeStruct((m, d), F32),
        scratch_shapes=[pltpu.VMEM((heads_per_step, RW_HEAD, RW_HEAD), F32)],
        compiler_params=_cparams(("parallel", "parallel", "arbitrary")),
        name="wkv7_rev" if reverse else "wkv7_fwd",
    )(r, lw, kd, v, aa, bb)


def _rw_readout_body(y0_ref, y1_ref, r_ref, v_ref, kd0_ref, kd1_ref, g_ref, rk_ref, lnw_ref, lnb_ref, e_ref, o_ref,
                     *, gn_eps):
    y = y0_ref[...] + y1_ref[...]
    inv = 1.0 / RW_HEAD
    mu = _headsum(y, e_ref) * inv
    dlt = y - mu
    var = _headsum(dlt * dlt, e_ref) * inv
    yn = dlt * lax.rsqrt(var + gn_eps) * lnw_ref[...] + lnb_ref[...]
    rk = r_ref[...] * rk_ref[...]
    bonus = (_headsum(rk * kd0_ref[...], e_ref) + _headsum(rk * kd1_ref[...], e_ref)) * v_ref[...]
    o_ref[...] = ((yn + bonus) * g_ref[...]).astype(o_ref.dtype)


def rwkv_readout(y0, y1, r, v, kd0, kd1, g, r_k, ln_w, ln_b, e128, gn_eps, rows, tm=ROW_TILE):
    d = r.shape[1]
    blk = pl.BlockSpec((tm, LANES), lambda i, c: (i, c))
    vec1 = pl.BlockSpec((1, LANES), lambda i, c: (0, c))
    return pl.pallas_call(
        functools.partial(_rw_readout_body, gn_eps=gn_eps),
        grid=(rows // tm, d // LANES),
        in_specs=[blk] * 7 + [vec1] * 3 + [pl.BlockSpec((LANES, LANES), lambda i, c: (0, 0))],
        out_specs=blk,
        out_shape=jax.ShapeDtypeStruct((rows, d), BF16),
        compiler_params=_cparams(("parallel", "arbitrary")),
        name="rwkv_readout",
    )(y0, y1, r, v, kd0, kd1, g, r_k.reshape(1, d), ln_w.reshape(1, d), ln_b.reshape(1, d), e128)


def rwkv_mixer(hmix, p, *, batch, seq, ctx_len, d_model):
    xs = rwkv_mix(hmix, p["mix"], batch=batch, seq=seq, ctx_len=ctx_len)
    m = hmix.shape[0]
    xs2 = xs.reshape(xs.shape[0] * m, d_model)
    tm = _row_tile(m)

    def proj(n, w, **kw):
        return mm(xs2, w, tm=tm, rows=m, x_row_block0=n * (m // tm), **kw)

    r = proj(0, p["w_r"], name="rw_r")
    k = proj(2, p["w_k"], name="rw_k")
    v = proj(3, p["w_v"], name="rw_v")
    g = mm(proj(5, p["g1"], act="sigmoid", out_dtype=BF16, name="rw_g1"), p["g2"], tm=tm, name="rw_g2")
    lws, als = [], []
    for d in range(2):
        tw = proj(1, p["w1"][d], act="tanh", out_dtype=BF16, name="rw_w1")
        lws.append(mm(tw, p["w2"][d], tm=tm, name="rw_w2"))
        ah = proj(4, p["a1"][d], out_dtype=BF16, name="rw_a1")
        als.append(mm(ah, p["a2"][d], tm=tm, name="rw_a2"))
    e128 = jnp.asarray(np.kron(np.eye(LANES // RW_HEAD), np.ones((RW_HEAD, RW_HEAD))), BF16)
    lw0, lw1, kd0, kd1, bb0, bb1, aa = rwkv_prep(k, lws[0], lws[1], als[0], als[1], p["w0"], p["a0"],
                                                 p["k_k"], p["k_a"], e128, tm=tm)
    kw = dict(batch=batch, seq=seq, ctx_len=ctx_len)
    y0 = wkv7(r, lw0, kd0, v, aa, bb0, reverse=False, **kw)
    y1 = wkv7(r, lw1, kd1, v, aa, bb1, reverse=True, **kw)
    return rwkv_readout(y0, y1, r, v, kd0, kd1, g, p["r_k"], p["ln_w"], p["ln_b"], e128,
                        1e-5 * RW_HEAD, rows=m)


def mla_mixer(hmix, p, *, batch, seq, ctx_len, want_ctx):
    m = hmix.shape[0]
    tm = _row_tile(m)
    scale = (MLA_NOPE + MLA_ROPE) ** -0.5 * LOG2E
    down = mm(hmix, p["w_down_ext"], tm=tm, name="mla_down")
    cos, sin = _rope_tables(seq, ctx_len, batch, MLA_ROPE)
    zeros = jnp.zeros_like(cos)
    cos_p = jnp.concatenate([cos, zeros], -1)
    sin_p = jnp.concatenate([sin, zeros], -1)
    q = proj_rope(down, p["w_uq_ext"], cos_p, sin_p, mode="mla_q", n_heads=MLA_HEADS, scale=scale,
                  norm_g=p["q_norm_g"], x_col_block=0, tm=tm)
    kcat, v = proj_rope(down, p["w_ukv"], cos_p, sin_p, mode="mla_kv", n_heads=MLA_HEADS,
                        norm_g=p["kv_norm_g"], x_col_block=1, pe=down,
                        pe_cb=(MLA_Q_RANK + MLA_KV_RANK) // LANES, tm=tm)
    kw = dict(dqk=2 * LANES, kv_group=1, n_heads=MLA_HEADS, batch=batch, seq=seq, ctx_len=ctx_len)
    o_lat = attention(q, kcat, v, lambda g: g, lat_queries=True, **kw)
    if not want_ctx:
        return o_lat
    o_ctx = attention(q, kcat, v, lambda g: g, lat_queries=False, **kw)
    return jnp.concatenate([o_lat, o_ctx], 0)


def gqa_mixer(hmix, p, *, batch, seq, ctx_len, want_ctx):
    m = hmix.shape[0]
    tm = _row_tile(m)
    scale = GQA_HEAD ** -0.5 * LOG2E
    cos, sin = _rope_tables(seq, ctx_len, batch, GQA_HEAD)
    nq, nk = GQA_HEADS, GQA_KV_HEADS
    q = proj_rope(hmix, p["w_q_ext"], cos, sin, mode="gqa", n_heads=nq, scale=scale,
                  gains=(p["q_norm_g"], p["q_norm_g_rot"]), tm=tm)
    k = proj_rope(hmix, p["w_k_ext"], cos, sin, mode="gqa", n_heads=nk,
                  gains=(p["k_norm_g"], p["k_norm_g_rot"]), tm=tm)
    vb = mm(hmix, p["w_v"], tm=tm, out_dtype=BF16, name="gqa_v")
    grp = nq // nk
    kw = dict(dqk=LANES, kv_group=grp, n_heads=nq, batch=batch, seq=seq, ctx_len=ctx_len)
    o_lat = attention(q, k, vb, lambda g: g, lat_queries=True, **kw)
    if not want_ctx:
        return o_lat
    o_ctx = attention(q, k, vb, lambda g: g, lat_queries=False, **kw)
    return jnp.concatenate([o_lat, o_ctx], 0)


def _rot_gain(g):
    q = g.shape[-1] // 4
    g4 = g.reshape(2, 2, q)
    return jnp.concatenate([g4[:, 1:2], g4[:, 0:1]], axis=1).reshape(g.shape)


def kernel(x, c, ctx, c_ctx, ada_w, ada_b, norm_g, ffn_w_gu, ffn_w_down, hy_w_in, hy_b_in, hy_conv_w, hy_conv_b, hy_f_w1, hy_f_b1, hy_f_w2, hy_f_b2, hy_f_w3, hy_sin_freq, hy_skip, hy_w_out, hy_b_out, rw_mix, rw_w_rkv, rw_w0, rw_w1, rw_w2, rw_a0, rw_a1, rw_a2, rw_g1, rw_g2, rw_k_k, rw_k_a, rw_r_k, rw_ln_w, rw_ln_b, rw_w_o, mla_w_down, mla_q_norm_g, mla_w_uq, mla_kv_norm_g, mla_w_ukv, mla_w_o, gqa_w_qkv, gqa_q_norm_g, gqa_k_norm_g, gqa_w_o, final_g):
    batch, seq, d = x.shape
    ctx_len = ctx.shape[1]
    depth = ada_w.shape[0]
    n_mix = 4
    rows_lat = batch * seq
    rows_all = rows_lat + batch * ctx_len
    bf = lambda a: a.astype(BF16)
    w_gu_all, w_down_all = bf(ffn_w_gu), bf(ffn_w_down)

    cvec = jnp.concatenate([c, c_ctx[None], jnp.zeros((8 - batch - 1, d), F32)], 0)
    mods_all = adaln_all(cvec, ada_w, ada_b)[:, :batch + 1].reshape(depth, batch + 1, N_MOD, d)

    h = jnp.concatenate([x.reshape(rows_lat, d), ctx.reshape(batch * ctx_len, d)], 0)

    for i in range(depth):
        mi, j = i % n_mix, i // n_mix
        ctx_out = any(jj % n_mix != 0 for jj in range(i + 1, depth))
        mods = mods_all[i]
        gate5 = mods[:, 5:6, :]
        mix_dtype = F32 if mi == 1 else BF16
        h, hmix = ffn_half(h, mods, norm_g[i], w_gu_all, w_down_all, layer=i, which=0, s=0, rows=rows_all,
                           seq_rows=seq, next_g=norm_g[i, 1], next_s=1, next_dtype=mix_dtype)
        rows_out = rows_all if ctx_out else rows_lat
        kw = dict(batch=batch, seq=seq, ctx_len=ctx_len)
        if mi == 0:
            p = dict(w_in=bf(hy_w_in[j]), b_in=hy_b_in[j], conv_w=hy_conv_w[j], conv_b=hy_conv_b[j],
                     f_w1=hy_f_w1[j], f_b1=hy_f_b1[j], f_w2=hy_f_w2[j], f_b2=hy_f_b2[j], f_w3=hy_f_w3[j],
                     sin_freq=hy_sin_freq[j], skip=hy_skip[j])
            z = hyena_mixer(hmix, p, d_model=d, want_ctx=ctx_out, **kw)
            w_o, b_o = bf(hy_w_out[j]), hy_b_out[j]
        elif mi == 1:
            pad = lambda w, ax: jnp.pad(w, [(0, RW_LORA_PAD - w.shape[a]) if a == ax else (0, 0) for a in range(w.ndim)])
            p = dict(mix=rw_mix[j], w_r=bf(rw_w_rkv[j, 0]), w_k=bf(rw_w_rkv[j, 1]), w_v=bf(rw_w_rkv[j, 2]),
                     g1=bf(rw_g1[j]), g2=bf(rw_g2[j]),
                     w1=bf(pad(rw_w1[j], 2)), w2=bf(pad(rw_w2[j], 1)), a1=bf(pad(rw_a1[j], 2)), a2=bf(pad(rw_a2[j], 1)),
                     w0=rw_w0[j], a0=rw_a0[j], k_k=rw_k_k[j], k_a=rw_k_a[j], r_k=rw_r_k[j].reshape(-1),
                     ln_w=rw_ln_w[j], ln_b=rw_ln_b[j])
            z = rwkv_mixer(hmix, p, d_model=d, **kw)
            w_o, b_o = bf(rw_w_o[j]), None
        elif mi == 2:
            wd = mla_w_down[j]
            pe = wd[:, MLA_Q_RANK + MLA_KV_RANK:]
            wuq = mla_w_uq[j].reshape(MLA_Q_RANK, MLA_HEADS, MLA_NOPE + MLA_ROPE)
            wq_pe = wuq[..., MLA_NOPE:]
            wuq_ext = jnp.concatenate([wuq[..., :MLA_NOPE], wq_pe,
                                       _rot_cols(wq_pe.reshape(MLA_Q_RANK, -1), MLA_ROPE).reshape(wq_pe.shape)], -1)
            p = dict(w_down_ext=bf(jnp.concatenate([wd, _rot_cols(pe, MLA_ROPE)], 1)),
                     w_uq_ext=bf(wuq_ext.reshape(MLA_Q_RANK, -1)), w_ukv=bf(mla_w_ukv[j]),
                     q_norm_g=mla_q_norm_g[j], kv_norm_g=mla_kv_norm_g[j])
            z = mla_mixer(hmix, p, want_ctx=ctx_out, **kw)
            w_o, b_o = bf(mla_w_o[j]), None
        else:
            w = gqa_w_qkv[j]
            nqc = GQA_HEADS * GQA_HEAD
            nkc = GQA_KV_HEADS * GQA_HEAD
            wq, wk, wv = w[:, :nqc], w[:, nqc:nqc + nkc], w[:, nqc + nkc:]

            def with_rot(wx, nh):
                both = jnp.stack([wx.reshape(d, nh, GQA_HEAD), _rot_cols(wx, GQA_HEAD).reshape(d, nh, GQA_HEAD)], 2)
                return bf(both.reshape(d, 2 * nh * GQA_HEAD))

            p = dict(w_q_ext=with_rot(wq, GQA_HEADS), w_k_ext=with_rot(wk, GQA_KV_HEADS), w_v=bf(wv),
                     q_norm_g=gqa_q_norm_g[j], q_norm_g_rot=_rot_gain(gqa_q_norm_g[j]),
                     k_norm_g=gqa_k_norm_g[j], k_norm_g_rot=_rot_gain(gqa_k_norm_g[j]))
            z = gqa_mixer(hmix, p, want_ctx=ctx_out, **kw)
            w_o, b_o = bf(gqa_w_o[j]), None
        h = mm(z, w_o, bias=b_o, res=h, gate=gate5, seq_rows=seq, rows=rows_out, tn=d, name="mix_out")
        last = i == depth - 1
        if last:
            h, out = ffn_half(h, mods, norm_g[i], w_gu_all, w_down_all, layer=i, which=1, s=2, rows=rows_out,
                              seq_rows=seq, next_g=final_g, next_s=None, next_dtype=F32)
        else:
            h = ffn_half(h, mods, norm_g[i], w_gu_all, w_down_all, layer=i, which=1, s=2, rows=rows_out,
                         seq_rows=seq)
    return out[:rows_lat].reshape(batch, seq, d)
```

```python
import functools
import math

import numpy as np
import jax
import jax.numpy as jnp
from jax import lax
from jax.experimental import pallas as pl
from jax.experimental.pallas import tpu as pltpu

F32 = jnp.float32
BF16 = jnp.bfloat16

VMEM_CAP_BYTES = 56 * 1024 * 1024
LANES = 128
SUBLANES = 8

NORM_EPS = 1e-6
LOG2E = 1.4426950408889634
N_MOD = 9
GRID_W = 64
ROPE_THETA = 10000.0
HY_DECAY_TARGET = 1e-2
HY_DECAY_PCT_SHORT = 0.3
HY_DECAY_PCT_LONG = 1.5
RW_HEAD = 64
RW_LORA_PAD = 128
GQA_HEAD = 128
GQA_HEADS = 16
GQA_KV_HEADS = 8
MLA_HEADS = 16
MLA_NOPE = 128
MLA_ROPE = 64
MLA_V = 128
MLA_Q_RANK = 512
MLA_KV_RANK = 512


def _cparams(sem):
    return pltpu.CompilerParams(dimension_semantics=sem, vmem_limit_bytes=VMEM_CAP_BYTES)


def _dot(a, b):
    return jnp.dot(a, b, preferred_element_type=F32)


def _dot_nt(a, b):
    return lax.dot_general(a, b, (((1,), (1,)), ((), ())), preferred_element_type=F32)


def _dot_tn(a, b):
    return lax.dot_general(a, b, (((0,), (0,)), ((), ())), preferred_element_type=F32)


def _split2(x):
    hi = x.astype(BF16)
    lo = (x - hi.astype(F32)).astype(BF16)
    return hi, lo


def _split3(x):
    hi = x.astype(BF16)
    r1 = x - hi.astype(F32)
    mid = r1.astype(BF16)
    lo = (r1 - mid.astype(F32)).astype(BF16)
    return hi, mid, lo


def _dot3(a, b, f=_dot):
    ah, al = _split2(a)
    bh, bl = _split2(b)
    return f(ah, bh) + f(ah, bl) + f(al, bh)


def _dotc2(mh, ml, d):
    dh = d.astype(BF16)
    return _dot(mh, dh) + _dot(ml, dh)


def _rms(x, g):
    return x * lax.rsqrt(jnp.mean(x * x, -1, keepdims=True) + NORM_EPS) * g


def _pick(n, cands):
    for c in cands:
        if n % c == 0:
            return c
    raise ValueError(f"no tile for {n}")


ROW_TILE = 512
WIDE_ROW_TILE = 1088
FFN_CHUNK = 512


def _row_tile(m):
    return WIDE_ROW_TILE if m % WIDE_ROW_TILE == 0 else ROW_TILE


def _adaln_body(c_ref, w_ref, b_ref, o_ref):
    c = c_ref[...]
    s = c * jax.nn.sigmoid(c)
    o_ref[0] = _dot(s.astype(BF16), w_ref[0].astype(BF16)) + b_ref[0]


def adaln_all(cvec8, ada_w, ada_b):
    depth, d, n = ada_w.shape
    tn = _pick(n, (2048, 1024, 512, 256, 128))
    return pl.pallas_call(
        _adaln_body,
        grid=(depth, n // tn),
        in_specs=[
            pl.BlockSpec((8, d), lambda l, j: (0, 0)),
            pl.BlockSpec((1, d, tn), lambda l, j: (l, 0, j)),
            pl.BlockSpec((1, 1, tn), lambda l, j: (l, 0, j)),
        ],
        out_specs=pl.BlockSpec((1, 8, tn), lambda l, j: (l, 0, j)),
        out_shape=jax.ShapeDtypeStruct((depth, 8, n), F32),
        compiler_params=_cparams(("arbitrary", "arbitrary")),
        name="adaln",
    )(cvec8, ada_w, ada_b.reshape(depth, 1, n))


def _mm_body(*refs, has_bias, act, has_norm, has_res):
    it = iter(refs)
    x_ref = next(it)
    w_ref = next(it)
    b_ref = next(it) if has_bias else None
    g_ref = next(it) if has_norm else None
    r_ref = next(it) if has_res else None
    gate_ref = next(it) if has_res else None
    o_ref = next(it)
    x = x_ref[...]
    if has_norm:
        x = _rms(x.astype(F32), g_ref[...])
    acc = _dot(x.astype(BF16), w_ref[...])
    if has_bias:
        acc = acc + b_ref[...]
    if act == "sigmoid":
        acc = jax.nn.sigmoid(acc)
    elif act == "tanh":
        acc = jnp.tanh(acc)
    if has_res:
        acc = r_ref[...] + gate_ref[0] * acc
    o_ref[...] = acc.astype(o_ref.dtype)


def mm(x, w, *, bias=None, act=None, norm_g=None, res=None, gate=None, out_dtype=F32,
       tm=ROW_TILE, tn=None, rows=None, x_col_block=0, x_row_block0=0, seq_rows=None, name="mm"):
    k, n = w.shape
    m = rows if rows is not None else x.shape[0]
    assert m % tm == 0, (m, tm)
    tn = tn or _pick(n, (512, 384, 256, 128))
    has_res = res is not None
    ins = [x, w]
    specs = [pl.BlockSpec((tm, k), lambda i, j: (i + x_row_block0, x_col_block)),
             pl.BlockSpec((k, tn), lambda i, j: (0, j))]
    if bias is not None:
        ins.append(bias.reshape(1, n).astype(F32))
        specs.append(pl.BlockSpec((1, tn), lambda i, j: (0, j)))
    if norm_g is not None:
        ins.append(norm_g.reshape(1, k).astype(F32))
        specs.append(pl.BlockSpec((1, k), lambda i, j: (0, 0)))
    if has_res:
        ngroups = gate.shape[0]
        per = seq_rows // tm
        ins += [res, gate]
        specs += [pl.BlockSpec((tm, tn), lambda i, j: (i, j)),
                  pl.BlockSpec((1, 1, tn), lambda i, j: (jnp.minimum(i // per, ngroups - 1), 0, j))]
    body = functools.partial(_mm_body, has_bias=bias is not None, act=act,
                             has_norm=norm_g is not None, has_res=has_res)
    return pl.pallas_call(
        body,
        grid=(m // tm, n // tn),
        in_specs=specs,
        out_specs=pl.BlockSpec((tm, tn), lambda i, j: (i, j)),
        out_shape=jax.ShapeDtypeStruct((m, n), out_dtype),
        compiler_params=_cparams(("parallel", "arbitrary")),
        name=name,
    )(*ins)


def _ffn_body(*refs, s, n_f, has_next, next_s):
    it = iter(refs)
    x_ref, mod_ref, g_ref, wa_ref, wb_ref, wd_ref = (next(it) for _ in range(6))
    gn_ref = next(it) if has_next else None
    o_ref = next(it)
    hn_ref = next(it) if has_next else None
    xn_sc = next(it)
    acc_sc = next(it)
    j = pl.program_id(1)

    @pl.when(j == 0)
    def _():
        n = _rms(x_ref[...], g_ref[s:s + 1, :])
        n = n * (1.0 + mod_ref[0, 3 * s + 1:3 * s + 2, :]) + mod_ref[0, 3 * s:3 * s + 1, :]
        xn_sc[...] = n.astype(BF16)
        acc_sc[...] = jnp.zeros_like(acc_sc)

    xn = xn_sc[...]
    a = _dot(xn, wa_ref[...])
    b = _dot(xn, wb_ref[...])
    h = (a * jax.nn.sigmoid(a)) * b
    acc_sc[...] += _dot(h.astype(BF16), wd_ref[...].astype(BF16))

    @pl.when(j == n_f - 1)
    def _():
        xnew = x_ref[...] + 0.5 * mod_ref[0, 3 * s + 2:3 * s + 3, :] * acc_sc[...]
        o_ref[...] = xnew
        if has_next:
            hn = _rms(xnew, gn_ref[...])
            if next_s is not None:
                hn = hn * (1.0 + mod_ref[0, 3 * next_s + 1:3 * next_s + 2, :]) + mod_ref[0, 3 * next_s:3 * next_s + 1, :]
            hn_ref[...] = hn.astype(hn_ref.dtype)


def ffn_half(x, mods, norm_g3, w_gu, w_down, *, layer, which, s, rows, seq_rows, next_g=None, next_s=None,
             next_dtype=F32, tm=ROW_TILE, fc=FFN_CHUNK):
    d = x.shape[1]
    d_ff = w_down.shape[2]
    n_f = d_ff // fc
    assert d_ff % fc == 0 and rows % tm == 0
    per = seq_rows // tm
    ngroups = mods.shape[0]
    has_next = next_g is not None
    ins = [x, mods, norm_g3, w_gu, w_gu, w_down]
    specs = [
        pl.BlockSpec((tm, d), lambda i, j: (i, 0)),
        pl.BlockSpec((1, N_MOD, d), lambda i, j: (jnp.minimum(i // per, ngroups - 1), 0, 0)),
        pl.BlockSpec((3, d), lambda i, j: (0, 0)),
        pl.BlockSpec((None, None, d, fc), lambda i, j: (layer, which, 0, j)),
        pl.BlockSpec((None, None, d, fc), lambda i, j: (layer, which, 0, n_f + j)),
        pl.BlockSpec((None, None, fc, d), lambda i, j: (layer, which, j, 0)),
    ]
    out_shape = [jax.ShapeDtypeStruct((rows, d), F32)]
    out_specs = [pl.BlockSpec((tm, d), lambda i, j: (i, 0))]
    if has_next:
        ins.append(next_g.reshape(1, d))
        specs.append(pl.BlockSpec((1, d), lambda i, j: (0, 0)))
        out_shape.append(jax.ShapeDtypeStruct((rows, d), next_dtype))
        out_specs.append(pl.BlockSpec((tm, d), lambda i, j: (i, 0)))
    body = functools.partial(_ffn_body, s=s, n_f=n_f, has_next=has_next, next_s=next_s)
    outs = pl.pallas_call(
        body,
        grid=(rows // tm, n_f),
        in_specs=specs,
        out_specs=out_specs,
        out_shape=out_shape,
        scratch_shapes=[pltpu.VMEM((tm, d), BF16), pltpu.VMEM((tm, d), F32)],
        compiler_params=_cparams(("parallel", "arbitrary")),
        name="ffn_half",
    )(*ins)
    return outs if has_next else outs[0]


def _attn_body(*refs, use_lat, hp, n_kv, dqk):
    it = iter(refs)
    q_ref = next(it)
    kl_ref = next(it) if use_lat else None
    kc_ref = next(it)
    vl_refs = [next(it) for _ in range(n_kv)] if use_lat else None
    vc_refs = [next(it) for _ in range(n_kv)]
    o_ref = next(it)
    hs = range(hp)
    slot = [h * n_kv // hp for h in hs]
    q = [q_ref[:, h * dqk:(h + 1) * dqk] for h in hs]
    s_c = [_dot_nt(kc_ref[:, slot[h] * dqk:(slot[h] + 1) * dqk], q[h]) for h in hs]
    m = [jnp.max(s_c[h], 0, keepdims=True) for h in hs]
    if use_lat:
        s_l = [_dot_nt(kl_ref[:, slot[h] * dqk:(slot[h] + 1) * dqk], q[h]) for h in hs]
        m = [jnp.maximum(m[h], jnp.max(s_l[h], 0, keepdims=True)) for h in hs]
    p_c = [jnp.exp2(s_c[h] - m[h]) for h in hs]
    den = [jnp.sum(p_c[h], 0, keepdims=True) for h in hs]
    o = [_dot_tn(vc_refs[slot[h]][...], p_c[h].astype(BF16)) for h in hs]
    if use_lat:
        p_l = [jnp.exp2(s_l[h] - m[h]) for h in hs]
        den = [den[h] + jnp.sum(p_l[h], 0, keepdims=True) for h in hs]
        o = [o[h] + _dot_tn(vl_refs[slot[h]][...], p_l[h].astype(BF16)) for h in hs]
    for h in hs:
        o_ref[:, h * LANES:(h + 1) * LANES] = (o[h] / den[h]).T.astype(o_ref.dtype)


def attention(q_arr, k_arr, v_arr, v_cb, *, dqk, kv_group, n_heads, batch, seq, ctx_len, lat_queries,
              tq=ROW_TILE, hp=1):
    assert n_heads % hp == 0 and (kv_group % hp == 0 or hp % kv_group == 0)
    n_kv = max(hp // kv_group, 1)
    kv0 = lambda p: (p * hp) // kv_group
    ctx0 = (batch * seq) // ctx_len
    if lat_queries:
        tq = min(tq, seq)
        nq = seq // tq
        q_row = lambda b, t: b * nq + t
        o_row = lambda b, t: b * nq + t
        rows_q = batch * seq
    else:
        tq = ctx_len
        nq = 1
        q_row = lambda b, t: ctx0 + b
        o_row = lambda b, t: b
        rows_q = batch * ctx_len
    ins = [q_arr]
    specs = [pl.BlockSpec((tq, hp * dqk), lambda b, p, t: (q_row(b, t), p))]
    if lat_queries:
        ins.append(k_arr)
        specs.append(pl.BlockSpec((seq, n_kv * dqk), lambda b, p, t: (b, kv0(p) // n_kv)))
    ins.append(k_arr)
    specs.append(pl.BlockSpec((ctx_len, n_kv * dqk), lambda b, p, t: (ctx0 + b, kv0(p) // n_kv)))
    if lat_queries:
        for s in range(n_kv):
            ins.append(v_arr)
            specs.append(pl.BlockSpec((seq, LANES), lambda b, p, t, s=s: (b, v_cb(kv0(p) + s))))
    for s in range(n_kv):
        ins.append(v_arr)
        specs.append(pl.BlockSpec((ctx_len, LANES), lambda b, p, t, s=s: (ctx0 + b, v_cb(kv0(p) + s))))
    return pl.pallas_call(
        functools.partial(_attn_body, use_lat=lat_queries, hp=hp, n_kv=n_kv, dqk=dqk),
        grid=(batch, n_heads // hp, nq),
        in_specs=specs,
        out_specs=pl.BlockSpec((tq, hp * LANES), lambda b, p, t: (o_row(b, t), p)),
        out_shape=jax.ShapeDtypeStruct((rows_q, n_heads * LANES), BF16),
        compiler_params=_cparams(("parallel", "parallel", "arbitrary")),
        name="attention_lat" if lat_queries else "attention_ctx",
    )(*ins)


def _rope_tables(seq, ctx_len, batch, rot_dim):
    half = rot_dim // 4
    inv_freq = ROPE_THETA ** (-jnp.arange(half, dtype=F32) / half)
    t = jnp.arange(seq, dtype=jnp.int32)
    row = (t // GRID_W).astype(F32)
    col = (t % GRID_W).astype(F32)
    ang_r = row[:, None] * inv_freq[None]
    ang_c = col[:, None] * inv_freq[None]
    cos = jnp.concatenate([jnp.cos(ang_r), jnp.cos(ang_r), jnp.cos(ang_c), jnp.cos(ang_c)], -1)
    sin = jnp.concatenate([jnp.sin(ang_r), jnp.sin(ang_r), jnp.sin(ang_c), jnp.sin(ang_c)], -1)
    cos = jnp.concatenate([jnp.tile(cos, (batch, 1)), jnp.ones((batch * ctx_len, rot_dim), F32)], 0)
    sin = jnp.concatenate([jnp.tile(sin, (batch, 1)), jnp.zeros((batch * ctx_len, rot_dim), F32)], 0)
    return cos, sin


def _rot_cols(w, rot_dim):
    lead = w.shape[:-1]
    q = rot_dim // 4
    w5 = w.reshape(lead + (-1, 2, 2, q))
    rot = jnp.concatenate([-w5[..., 1:2, :], w5[..., 0:1, :]], axis=-2)
    return rot.reshape(w.shape)


def _rope_half(b, cos, sin):
    return b * cos + pltpu.roll(b, LANES // 2, 1) * sin


def _proj_rope_body(*refs, mode, has_norm, scale):
    it = iter(refs)
    x_ref, w_ref = next(it), next(it)
    ng_ref = next(it) if has_norm else None
    cos_ref, sin_ref = next(it), next(it)
    x = x_ref[...]
    if has_norm:
        x = _rms(x.astype(F32), ng_ref[...])
    acc = _dot(x.astype(BF16), w_ref[...])
    a, b = acc[:, :LANES], acc[:, LANES:]
    if mode == "gqa":
        g_ref, gr_ref, o_ref = next(it), next(it), next(it)
        r = lax.rsqrt(jnp.mean(a * a, -1, keepdims=True) + NORM_EPS) * scale
        o_ref[...] = (((a * g_ref[...]) * cos_ref[...] + (b * gr_ref[...]) * sin_ref[...]) * r).astype(o_ref.dtype)
    elif mode == "mla_q":
        o_ref = next(it)
        o_ref[:, :LANES] = (a * scale).astype(o_ref.dtype)
        o_ref[:, LANES:] = (_rope_half(b, cos_ref[...], sin_ref[...]) * scale).astype(o_ref.dtype)
    else:
        pe_ref, k_ref, v_ref = next(it), next(it), next(it)
        k_ref[:, :LANES] = a.astype(k_ref.dtype)
        k_ref[:, LANES:] = _rope_half(pe_ref[...], cos_ref[...], sin_ref[...]).astype(k_ref.dtype)
        v_ref[...] = b.astype(v_ref.dtype)


def proj_rope(x, w, cos, sin, *, mode, n_heads, scale=1.0, norm_g=None, x_col_block=0, gains=None,
              pe=None, pe_cb=0, tm=WIDE_ROW_TILE):
    m = x.shape[0]
    k = w.shape[0]
    ins = [x, w]
    specs = [pl.BlockSpec((tm, k), lambda i, h: (i, x_col_block)),
             pl.BlockSpec((k, 2 * LANES), lambda i, h: (0, h))]
    if norm_g is not None:
        ins.append(norm_g.reshape(1, k).astype(F32))
        specs.append(pl.BlockSpec((1, k), lambda i, h: (0, 0)))
    tab = pl.BlockSpec((tm, LANES), lambda i, h: (i, 0))
    ins += [cos, sin]
    specs += [tab, tab]
    if mode == "gqa":
        vec = pl.BlockSpec((1, LANES), lambda i, h: (0, 0))
        ins += [gains[0].reshape(1, LANES), gains[1].reshape(1, LANES)]
        specs += [vec, vec]
        out_shape = jax.ShapeDtypeStruct((m, n_heads * LANES), BF16)
        out_specs = pl.BlockSpec((tm, LANES), lambda i, h: (i, h))
    elif mode == "mla_q":
        out_shape = jax.ShapeDtypeStruct((m, n_heads * 2 * LANES), BF16)
        out_specs = pl.BlockSpec((tm, 2 * LANES), lambda i, h: (i, h))
    else:
        ins.append(pe)
        specs.append(pl.BlockSpec((tm, LANES), lambda i, h: (i, pe_cb)))
        out_shape = [jax.ShapeDtypeStruct((m, n_heads * 2 * LANES), BF16),
                     jax.ShapeDtypeStruct((m, n_heads * LANES), BF16)]
        out_specs = [pl.BlockSpec((tm, 2 * LANES), lambda i, h: (i, h)),
                     pl.BlockSpec((tm, LANES), lambda i, h: (i, h))]
    return pl.pallas_call(
        functools.partial(_proj_rope_body, mode=mode, has_norm=norm_g is not None, scale=scale),
        grid=(m // tm, n_heads),
        in_specs=specs,
        out_specs=out_specs,
        out_shape=out_shape,
        compiler_params=_cparams(("parallel", "arbitrary")),
        name="proj_" + mode,
    )(*ins)


class _FftPlan:
    def __init__(self, L):
        self.L = L
        self.N = 2 * L
        self.N2 = 128 if L >= 1024 else 64
        self.N1 = self.N // self.N2
        self.n1_in = self.N1 // 2
        self.K1 = self.N1 // 2 + 1
        self.nq = self.N2 // SUBLANES
        self.kf_cols = max(self.n1_in * SUBLANES, LANES)
        self.kf_rows = 2 * self.K1 * SUBLANES
        self.kb_cols = -(-self.kf_rows // LANES) * LANES
        self.kb_rows = self.n1_in * SUBLANES

    def constants(self):
        N, N1, N2, K1, n1_in = self.N, self.N1, self.N2, self.K1, self.n1_in
        eye = np.eye(SUBLANES)
        k1 = np.arange(K1)[:, None]
        n1 = np.arange(n1_in)[None, :]
        th = 2 * np.pi * k1 * n1 / N1
        kf = np.zeros((self.kf_rows, self.kf_cols))
        kf[:K1 * SUBLANES, :n1_in * SUBLANES] = np.kron(np.cos(th), eye)
        kf[K1 * SUBLANES:, :n1_in * SUBLANES] = np.kron(-np.sin(th), eye)
        c = np.full((K1,), 2.0)
        c[0] = 1.0
        c[-1] = 1.0
        thb = 2 * np.pi * np.arange(n1_in)[:, None] * np.arange(K1)[None, :] / N1
        kb = np.zeros((self.kb_rows, self.kb_cols))
        kb[:, :K1 * SUBLANES] = np.kron(np.cos(thb) * c[None, :], eye)
        kb[:, K1 * SUBLANES:2 * K1 * SUBLANES] = np.kron(-np.sin(thb) * c[None, :], eye)
        a = 2 * np.pi * np.outer(np.arange(N2), np.arange(N2)) / N2
        C, S = np.cos(a), np.sin(a)
        w2f = np.block([[C, S], [-S, C]])
        w2i = np.block([[C, -S], [S, C]])
        tw_ang = 2 * np.pi * np.outer(np.arange(N2), np.arange(K1)) / N
        twr = np.repeat(np.cos(tw_ang), LANES, axis=1)
        twi = np.repeat(np.sin(tw_ang), LANES, axis=1)

        def hl(m):
            m32 = jnp.asarray(m, F32)
            hi = m32.astype(BF16)
            lo = (m32 - hi.astype(F32)).astype(BF16)
            return hi, lo

        out = {}
        for name, mat in (("kf", kf), ("kb", kb), ("w2f", w2f), ("w2i", w2i)):
            out[name + "_h"], out[name + "_l"] = hl(mat)
        out["twr"] = jnp.asarray(twr, F32)
        out["twi"] = jnp.asarray(twi, F32)
        return out


def _cmul(ar, ai, br, bi):
    return ar * br - ai * bi, ar * bi + ai * br


def _time_to_strided(plan, x_ref, xt_ref):
    N2, nq = plan.N2, plan.nq
    for n in range(plan.n1_in):
        xt_ref[n * SUBLANES:(n + 1) * SUBLANES, :] = jnp.concatenate(
            [x_ref[N2 * n + SUBLANES * q:N2 * n + SUBLANES * (q + 1), :] for q in range(nq)], axis=1)
    pad = plan.kf_cols - plan.n1_in * SUBLANES
    if pad:
        xt_ref[plan.n1_in * SUBLANES:, :] = jnp.zeros((pad, xt_ref.shape[1]), F32)


def _strided_to_spectrum(plan, a_ref, s_ref, c):
    N2, K1, nq = plan.N2, plan.K1, plan.nq
    for part in range(2):
        for k in range(K1):
            r0 = (part * K1 + k) * SUBLANES
            s_ref[part * N2:(part + 1) * N2, k * c:(k + 1) * c] = jnp.concatenate(
                [a_ref[r0:r0 + SUBLANES, q * c:(q + 1) * c] for q in range(nq)], axis=0)


def _spectrum_to_strided(plan, s_ref, b_ref, c):
    N2, K1, nq = plan.N2, plan.K1, plan.nq
    for part in range(2):
        for k in range(K1):
            r0 = (part * K1 + k) * SUBLANES
            b_ref[r0:r0 + SUBLANES, :] = jnp.concatenate(
                [s_ref[part * N2 + SUBLANES * q:part * N2 + SUBLANES * (q + 1), k * c:(k + 1) * c] for q in range(nq)],
                axis=1)
    pad = plan.kb_cols - plan.kf_rows
    if pad:
        b_ref[plan.kf_rows:, :] = jnp.zeros((pad, b_ref.shape[1]), F32)


def _strided_to_time(plan, yt, c):
    nq = plan.nq
    slabs = []
    for n in range(plan.n1_in):
        blk = yt[n * SUBLANES:(n + 1) * SUBLANES, :]
        slabs.extend(blk[:, q * c:(q + 1) * c] for q in range(nq))
    return jnp.concatenate(slabs, axis=0)


def _fft_fwd(plan, x_ref, consts, xt_ref, a_ref, s_ref, dotc):
    c = x_ref.shape[1]
    _time_to_strided(plan, x_ref, xt_ref)
    a_ref[:plan.kf_rows, :] = dotc(consts["kf_h"][...], consts["kf_l"][...], xt_ref[...])
    _strided_to_spectrum(plan, a_ref, s_ref, c)
    n2 = plan.N2
    tr, ti = consts["twr"][...], consts["twi"][...]
    xr, xi = _cmul(s_ref[:n2, :], s_ref[n2:, :], tr, -ti)
    s_ref[...] = dotc(consts["w2f_h"][...], consts["w2f_l"][...], jnp.concatenate([xr, xi], axis=0))


def _fft_conv_apply(plan, x_ref, g_ref, consts, xt_ref, a_ref, s_ref):
    c = x_ref.shape[1]
    n2 = plan.N2
    _fft_fwd(plan, x_ref, consts, xt_ref, a_ref, s_ref, _dotc2)
    pr, pi = _cmul(s_ref[:n2, :], s_ref[n2:, :], g_ref[:n2, :], g_ref[n2:, :])
    z = _dotc2(consts["w2i_h"][...], consts["w2i_l"][...], jnp.concatenate([pr, pi], axis=0))
    zr, zi = _cmul(z[:n2], z[n2:], consts["twr"][...], consts["twi"][...])
    s_ref[:n2, :] = zr
    s_ref[n2:, :] = zi
    _spectrum_to_strided(plan, s_ref, a_ref, c)
    yt = _dotc2(consts["kb_h"][...], consts["kb_l"][...], a_ref[...])
    return _strided_to_time(plan, yt, c)


def _fft_scratch(plan):
    wl = plan.nq * LANES
    return [pltpu.VMEM((plan.kf_cols, wl), F32),
            pltpu.VMEM((max(plan.kf_rows, plan.kb_cols), wl), F32),
            pltpu.VMEM((2 * plan.N2, plan.K1 * LANES), F32)]


_CONST_NAMES = ("kf_h", "kf_l", "kb_h", "kb_l", "w2f_h", "w2f_l", "w2i_h", "w2i_l", "twr", "twi")


def _const_specs(consts, nargs):
    ins, specs = [], []
    for nme in _CONST_NAMES:
        a = consts[nme]
        ins.append(a)
        if nargs == 1:
            specs.append(pl.BlockSpec(a.shape, lambda c: (0, 0), pipeline_mode=pl.Buffered(1)))
        else:
            specs.append(pl.BlockSpec(a.shape, lambda c, b: (0, 0), pipeline_mode=pl.Buffered(1)))
    return ins, specs


def _hy_hidden_body(z_ref, w1_ref, b1_ref, w2_ref, b2_ref, sf_ref, o_ref):
    h = jnp.sin(sf_ref[0:1, :] * (_dot3(z_ref[...], w1_ref[...]) + b1_ref[...]))
    h = jnp.sin(sf_ref[1:2, :] * (_dot3(h, w2_ref[...]) + b2_ref[...]))
    o_ref[...] = h


def hyena_hidden(L, f_w1, f_b1, f_w2, f_b2, sin_freq):
    emb, hid = f_w1.shape
    bands = (emb - 1) // 2
    t = jnp.linspace(0.0, 1.0, L, dtype=F32)[:, None]
    w = (2.0 * math.pi / L) * jnp.arange(L, dtype=F32)[:, None]
    f = jnp.linspace(1e-4, bands - 1, bands, dtype=F32)[None]
    z = jnp.concatenate([t, jnp.cos(f * w), -jnp.sin(f * w)], -1)
    embp = -(-emb // LANES) * LANES
    z = jnp.pad(z, ((0, 0), (0, embp - emb)))
    w1 = jnp.pad(f_w1, ((0, embp - emb), (0, 0)))
    ins = [z, w1, f_b1.reshape(1, hid), f_w2, f_b2.reshape(1, hid), sin_freq]
    return pl.pallas_call(
        _hy_hidden_body,
        out_shape=jax.ShapeDtypeStruct((L, hid), F32),
        name="hyena_hidden",
    )(*ins)


def _hy_filter_body(h_ref, wf_ref, wb_ref, dl_ref, *rest, plan):
    consts = {nme: r for nme, r in zip(_CONST_NAMES, rest)}
    g_ref, x_sc, xt_sc, a_sc, s_sc, sb_sc = rest[len(_CONST_NAMES):]
    L, n2 = plan.L, plan.N2
    t = lax.broadcasted_iota(jnp.int32, (L, LANES), 0).astype(F32) * (1.0 / (L - 1))
    win = jnp.exp(-t * dl_ref[...])
    row0 = lax.broadcasted_iota(jnp.int32, (L, LANES), 0) == 0
    h = h_ref[...]
    fwd = _dot3(h, wf_ref[...]) * win
    bwd = jnp.where(row0, 0.0, _dot3(h, wb_ref[...]) * win)
    ss = jnp.sum(fwd * fwd, 0, keepdims=True) + jnp.sum(bwd * bwd, 0, keepdims=True)
    nu = lax.rsqrt(ss + 1e-12) * (1.0 / plan.N)
    x_sc[...] = bwd * nu
    _fft_fwd(plan, x_sc, consts, xt_sc, a_sc, sb_sc, _dotc2)
    x_sc[...] = fwd * nu
    _fft_fwd(plan, x_sc, consts, xt_sc, a_sc, s_sc, _dotc2)
    g_ref[:n2, :] = s_sc[:n2, :] + sb_sc[:n2, :]
    g_ref[n2:, :] = s_sc[n2:, :] - sb_sc[n2:, :]


def hyena_filter_spectra(plan, consts, hidden, f_w3, d_model, n_order):
    L = plan.L
    hid = hidden.shape[1]
    ncb = d_model // LANES
    deltas = jnp.abs(jnp.linspace(math.log(HY_DECAY_TARGET) / HY_DECAY_PCT_SHORT,
                                  math.log(HY_DECAY_TARGET) / HY_DECAY_PCT_LONG, d_model, dtype=F32)).reshape(1, d_model)
    ins = [hidden, f_w3, f_w3, deltas]
    specs = [pl.BlockSpec((L, hid), lambda c, n: (0, 0)),
             pl.BlockSpec((hid, LANES), lambda c, n: (0, (2 * n) * ncb + c)),
             pl.BlockSpec((hid, LANES), lambda c, n: (0, (2 * n + 1) * ncb + c)),
             pl.BlockSpec((1, LANES), lambda c, n: (0, c))]
    ci, cs = _const_specs(consts, 2)
    ins += ci
    specs += cs
    srows, scols = 2 * plan.N2, plan.K1 * LANES
    return pl.pallas_call(
        functools.partial(_hy_filter_body, plan=plan),
        grid=(ncb, n_order),
        in_specs=specs,
        out_specs=pl.BlockSpec((None, None, srows, scols), lambda c, n: (n, c, 0, 0)),
        out_shape=jax.ShapeDtypeStruct((n_order, ncb, srows, scols), F32),
        scratch_shapes=[pltpu.VMEM((L, LANES), F32)] + _fft_scratch(plan) + [pltpu.VMEM((srows, scols), F32)],
        compiler_params=_cparams(("arbitrary", "arbitrary")),
        name="hyena_filter",
    )(*ins)


def _conv3(p, w_ref, b_ref, L):
    rows = lax.broadcasted_iota(jnp.int32, p.shape, 0)
    prev = jnp.where(rows == 0, 0.0, pltpu.roll(p, 1, 0))
    nxt = jnp.where(rows == L - 1, 0.0, pltpu.roll(p, L - 1, 0))
    return prev * w_ref[0:1, :] + p * w_ref[1:2, :] + nxt * w_ref[2:3, :] + b_ref[...]


def _hy_conv_body(*refs, plan, order, conv_u):
    it = iter(refs)
    u_ref, gate_ref = next(it), next(it)
    if conv_u:
        cwu_ref, cbu_ref = next(it), next(it)
    cwg_ref, cbg_ref = next(it), next(it)
    skip_ref = next(it)
    g_ref = next(it)
    consts = {nme: next(it) for nme in _CONST_NAMES}
    o_ref = next(it)
    x_sc, xt_sc, a_sc, s_sc = (next(it) for _ in range(4))
    L = plan.L
    u = u_ref[...]
    if conv_u:
        u = _conv3(u, cwu_ref, cbu_ref, L)
    x_sc[...] = u
    y = _fft_conv_apply(plan, x_sc, g_ref, consts, xt_sc, a_sc, s_sc)
    gate = _conv3(gate_ref[...], cwg_ref, cbg_ref, L)
    o_ref[...] = (gate * (y + x_sc[...] * skip_ref[order:order + 1, :])).astype(o_ref.dtype)


def hyena_conv(plan, consts, u_arr, u_cb0, gate_arr, gate_cb0, conv_w, conv_b, skip, spectra, *,
               order, conv_u, u_row_block0, gate_row_block0, batch, d_model, out_dtype):
    L = plan.L
    ncb = d_model // LANES
    ins = [u_arr, gate_arr]
    specs = [pl.BlockSpec((L, LANES), lambda c, b: (u_row_block0 + b, u_cb0 + c)),
             pl.BlockSpec((L, LANES), lambda c, b: (gate_row_block0 + b, gate_cb0 + c))]
    if conv_u:
        ins += [conv_w, conv_b]
        specs += [pl.BlockSpec((3, LANES), lambda c, b: (0, u_cb0 + c)),
                  pl.BlockSpec((1, LANES), lambda c, b: (0, u_cb0 + c))]
    srows, scols = 2 * plan.N2, plan.K1 * LANES
    ins += [conv_w, conv_b, skip, spectra]
    specs += [pl.BlockSpec((3, LANES), lambda c, b: (0, gate_cb0 + c)),
              pl.BlockSpec((1, LANES), lambda c, b: (0, gate_cb0 + c)),
              pl.BlockSpec((skip.shape[0], LANES), lambda c, b: (0, c)),
              pl.BlockSpec((None, None, srows, scols), lambda c, b: (order, c, 0, 0))]
    ci, cs = _const_specs(consts, 2)
    ins += ci
    specs += cs
    return pl.pallas_call(
        functools.partial(_hy_conv_body, plan=plan, order=order, conv_u=conv_u),
        grid=(ncb, batch),
        in_specs=specs,
        out_specs=pl.BlockSpec((L, LANES), lambda c, b: (b, c)),
        out_shape=jax.ShapeDtypeStruct((batch * L, d_model), out_dtype),
        scratch_shapes=[pltpu.VMEM((L, LANES), F32)] + _fft_scratch(plan),
        compiler_params=_cparams(("parallel", "arbitrary")),
        name=f"hyena_conv{order}_L{L}",
    )(*ins)


def hyena_mixer(hmix, p, *, batch, seq, ctx_len, d_model, want_ctx):
    n_order = p["skip"].shape[0]
    rows = batch * seq + (batch * ctx_len if want_ctx else 0)
    pr = mm(hmix, p["w_in"], bias=p["b_in"], name="hy_in", rows=rows, tm=_row_tile(rows))
    conv_b = p["conv_b"].reshape(1, -1)
    ncb = d_model // LANES
    outs = []
    streams = [(seq, 0)]
    if want_ctx:
        streams.append((ctx_len, (batch * seq) // ctx_len))
    for L, rb0 in streams:
        plan = _FftPlan(L)
        consts = plan.constants()
        hidden = hyena_hidden(L, p["f_w1"], p["f_b1"], p["f_w2"], p["f_b2"], p["sin_freq"])
        spectra = hyena_filter_spectra(plan, consts, hidden, p["f_w3"], d_model, n_order)
        z1 = hyena_conv(plan, consts, pr, 0, pr, ncb, p["conv_w"], conv_b, p["skip"], spectra,
                        order=0, conv_u=True, u_row_block0=rb0, gate_row_block0=rb0, batch=batch, d_model=d_model,
                        out_dtype=F32)
        z2 = hyena_conv(plan, consts, z1, 0, pr, 2 * ncb, p["conv_w"], conv_b, p["skip"], spectra,
                        order=1, conv_u=False, u_row_block0=0, gate_row_block0=rb0, batch=batch, d_model=d_model,
                        out_dtype=BF16)
        outs.append(z2)
    return outs[0] if len(outs) == 1 else jnp.concatenate(outs, 0)


def _headsum(x, e_ref):
    hi, lo = _split2(x)
    return _dot(hi, e_ref[...]) + _dot(lo, e_ref[...])


def _rw_mix_body(h_ref, hp_ref, hn_ref, mix_ref, o_ref, *, tm, lat_tiles, tiles_per_seq):
    i = pl.program_id(0)
    is_ctx = i >= lat_tiles
    first = jnp.logical_or(is_ctx, i % tiles_per_seq == 0)
    last = jnp.logical_or(is_ctx, i % tiles_per_seq == tiles_per_seq - 1)
    h = h_ref[...]
    rows = lax.broadcasted_iota(jnp.int32, h.shape, 0)
    pedge = jnp.where(first, 0.0, hp_ref[SUBLANES - 1:SUBLANES, :])
    nedge = jnp.where(last, 0.0, hn_ref[0:1, :])
    prev = jnp.where(rows == 0, pedge, pltpu.roll(h, 1, 0))
    nxt = jnp.where(rows == tm - 1, nedge, pltpu.roll(h, tm - 1, 0))
    dp = prev - h
    dn = nxt - h
    for n in range(o_ref.shape[0]):
        o_ref[n] = (h + dp * mix_ref[0, n:n + 1, :] + dn * mix_ref[1, n:n + 1, :]).astype(o_ref.dtype)


def rwkv_mix(h, mix, *, batch, seq, ctx_len):
    m, d = h.shape
    tm = ctx_len
    n_shift = mix.shape[1]
    nblk = m // SUBLANES
    per = tm // SUBLANES
    body = functools.partial(_rw_mix_body, tm=tm, lat_tiles=(batch * seq) // tm, tiles_per_seq=seq // tm)
    return pl.pallas_call(
        body,
        grid=(m // tm,),
        in_specs=[
            pl.BlockSpec((tm, d), lambda i: (i, 0)),
            pl.BlockSpec((SUBLANES, d), lambda i: (jnp.maximum(i * per - 1, 0), 0)),
            pl.BlockSpec((SUBLANES, d), lambda i: (jnp.minimum((i + 1) * per, nblk - 1), 0)),
            pl.BlockSpec((2, n_shift, d), lambda i: (0, 0, 0)),
        ],
        out_specs=pl.BlockSpec((n_shift, tm, d), lambda i: (0, i, 0)),
        out_shape=jax.ShapeDtypeStruct((n_shift, m, d), BF16),
        compiler_params=_cparams(("parallel",)),
        name="rwkv_mix",
    )(h, h, h, mix)


def _rw_prep_body(k_ref, lw0_ref, lw1_ref, al0_ref, al1_ref, w0_ref, a0_ref, kk_ref, ka_ref, e_ref,
                  olw0, olw1, okd0, okd1, obb0, obb1, oaa):
    k = k_ref[...]
    kk0 = k * kk_ref[...]
    kk = kk0 * lax.rsqrt(_headsum(kk0 * kk0, e_ref) + 1e-12)
    oaa[...] = -kk
    for d, (lw_ref, al_ref, olw, okd, obb) in enumerate(((lw0_ref, al0_ref, olw0, okd0, obb0),
                                                         (lw1_ref, al1_ref, olw1, okd1, obb1))):
        z = -(w0_ref[d:d + 1, :] + lw_ref[...])
        softplus = jnp.maximum(z, 0.0) + jnp.log(1.0 + jnp.exp(-jnp.abs(z)))
        log_w = -softplus - 0.5
        olw[...] = -jnp.exp(log_w)
        a = jax.nn.sigmoid(a0_ref[d:d + 1, :] + al_ref[...])
        okd[...] = k * (1.0 + (a - 1.0) * ka_ref[...])
        obb[...] = kk * a


def rwkv_prep(k, lw0, lw1, al0, al1, w0, a0, k_k, k_a, e128, tm=WIDE_ROW_TILE):
    m, d = k.shape
    blk = pl.BlockSpec((tm, LANES), lambda i, c: (i, c))
    vec2 = pl.BlockSpec((2, LANES), lambda i, c: (0, c))
    vec1 = pl.BlockSpec((1, LANES), lambda i, c: (0, c))
    sh = jax.ShapeDtypeStruct((m, d), F32)
    return pl.pallas_call(
        _rw_prep_body,
        grid=(m // tm, d // LANES),
        in_specs=[blk] * 5 + [vec2, vec2, vec1, vec1, pl.BlockSpec((LANES, LANES), lambda i, c: (0, 0))],
        out_specs=[blk] * 7,
        out_shape=[sh] * 7,
        compiler_params=_cparams(("parallel", "arbitrary")),
        name="rwkv_prep",
    )(k, lw0, lw1, al0, al1, w0, a0, k_k.reshape(1, d), k_a.reshape(1, d), e128)


def _b16(x):
    return x.astype(BF16)


def _wkv_body(r_ref, lw_ref, k_ref, v_ref, a_ref, b_ref, o_ref, s_sc, *, chunk, n_heads, reverse):
    c = pl.program_id(2)

    @pl.when(c == 0)
    def _():
        s_sc[...] = jnp.zeros_like(s_sc)

    C = chunk
    hs = range(n_heads)
    row = lax.broadcasted_iota(jnp.int32, (C, C), 0)
    col = lax.broadcasted_iota(jnp.int32, (C, C), 1)
    if reverse:
        tri_incl = col >= row
        tri_strict = col > row
        last = 0
    else:
        tri_incl = col <= row
        tri_strict = col < row
        last = C - 1
    same = lambda s: (row >> int(math.log2(s))) == (col >> int(math.log2(s)))
    eye = (row == col).astype(F32)
    cum_m = tri_incl.astype(BF16)
    sls = [slice(h * RW_HEAD, (h + 1) * RW_HEAD) for h in hs]

    lw = [lw_ref[:, sl] for sl in sls]
    cum = []
    for h in hs:
        l1, l2, l3 = _split3(lw[h])
        cum.append(_dot(cum_m, l1) + _dot(cum_m, l2) + _dot(cum_m, l3))
    p_inc = [jnp.exp(cum[h]) for h in hs]
    x1 = [jnp.concatenate([a_ref[:, sls[h]] * jnp.exp(cum[h] - lw[h]), r_ref[:, sls[h]] * p_inc[h]], axis=0) for h in hs]
    x2 = []
    for h in hs:
        p_inv = jnp.exp(-cum[h])
        x2.append(jnp.concatenate([b_ref[:, sls[h]] * p_inv, k_ref[:, sls[h]] * p_inv], axis=0))
    v = [v_ref[:, sl] for sl in sls]
    s0 = [s_sc[h] for h in hs]
    def cat_a(x):
        hi, lo = _split2(x)
        return jnp.concatenate([hi, lo, hi], axis=1)

    def cat_b(x):
        hi, lo = _split2(x)
        return jnp.concatenate([hi, hi, lo], axis=1)

    x1c = [cat_a(x1[h]) for h in hs]
    mfull = [_dot_nt(x1c[h], cat_b(x2[h])) for h in hs]
    w0 = [_dot_nt(x1c[h], cat_b(s0[h])) for h in hs]
    n_ab = [jnp.where(tri_strict, mfull[h][:C, :C], 0.0) for h in hs]
    a_ak = [_b16(jnp.where(tri_strict, mfull[h][:C, C:], 0.0)) for h in hs]
    a_r = [_b16(jnp.concatenate([jnp.where(tri_incl, mfull[h][C:, :C], 0.0),
                                 jnp.where(tri_incl, mfull[h][C:, C:], 0.0)], axis=1)) for h in hs]
    rhs = [w0[h][:C] + _dot(a_ak[h], _b16(v[h])) for h in hs]

    n4 = [jnp.where(same(4), n_ab[h], 0.0) for h in hs]
    n4b = [_b16(n4[h]) for h in hs]
    sq = [_dot(n4b[h], n4b[h]) for h in hs]
    t = [(eye + n4[h]) + _dot(_b16(eye + n4[h]), _b16(sq[h])) for h in hs]
    s = 4
    while s < C:
        off = jnp.logical_and(jnp.logical_and(same(2 * s), jnp.logical_not(same(s))), tri_strict)
        tb = [_b16(t[h]) for h in hs]
        tmp = [_dot(tb[h], _b16(jnp.where(off, n_ab[h], 0.0))) for h in hs]
        t = [t[h] + _dot(_b16(tmp[h]), tb[h]) for h in hs]
        s *= 2
    tb = [_b16(t[h]) for h in hs]
    u = [_dot(tb[h], _b16(rhs[h])) for h in hs]
    res = [rhs[h] - u[h] + _dot3(n_ab[h], u[h]) for h in hs]
    u = [u[h] + _dot(tb[h], _b16(res[h])) for h in hs]
    uv = [jnp.concatenate([u[h], v[h]], axis=0) for h in hs]
    y = [w0[h][C:] + _dot(a_r[h], _b16(uv[h])) for h in hs]
    for h in hs:
        s_sc[h] = (s0[h] + _dot3(uv[h], x2[h], _dot_tn)) * p_inc[h][last:last + 1, :]
    o_ref[...] = jnp.concatenate(y, axis=1)


def wkv7(r, lw, kd, v, aa, bb, *, reverse, batch, seq, ctx_len, chunk=128, heads_per_step=16):
    m, d = r.shape
    heads_per_step = min(heads_per_step, d // RW_HEAD)
    wl = heads_per_step * RW_HEAD
    n_ctx = ctx_len // chunk
    n_lat = seq // chunk
    n_chunks = n_ctx + n_lat
    ctx0 = (batch * seq) // chunk

    def row_block(b, c):
        if reverse:
            return jnp.where(c < n_ctx, ctx0 + b * n_ctx + (n_ctx - 1 - c), b * n_lat + (n_lat - 1 - (c - n_ctx)))
        return jnp.where(c < n_ctx, ctx0 + b * n_ctx + c, b * n_lat + (c - n_ctx))

    blk = pl.BlockSpec((chunk, wl), lambda b, g, c: (row_block(b, c), g))
    body = functools.partial(_wkv_body, chunk=chunk, n_heads=heads_per_step, reverse=reverse)
    return pl.pallas_call(
        body,
        grid=(batch, d // wl, n_chunks),
        in_specs=[blk] * 6,
        out_specs=blk,
        out_shape=jax.ShapeDtypeStruct((m, d), F32),
        scratch_shapes=[pltpu.VMEM((heads_per_step, RW_HEAD, RW_HEAD), F32)],
        compiler_params=_cparams(("parallel", "parallel", "arbitrary")),
        name="wkv7_rev" if reverse else "wkv7_fwd",
    )(r, lw, kd, v, aa, bb)


def _rw_readout_body(y0_ref, y1_ref, r_ref, v_ref, kd0_ref, kd1_ref, g_ref, rk_ref, lnw_ref, lnb_ref, e_ref, o_ref,
                     *, gn_eps):
    y = y0_ref[...] + y1_ref[...]
    inv = 1.0 / RW_HEAD
    mu = _headsum(y, e_ref) * inv
    dlt = y - mu
    var = _headsum(dlt * dlt, e_ref) * inv
    yn = dlt * lax.rsqrt(var + gn_eps) * lnw_ref[...] + lnb_ref[...]
    rk = r_ref[...] * rk_ref[...]
    bonus = (_headsum(rk * kd0_ref[...], e_ref) + _headsum(rk * kd1_ref[...], e_ref)) * v_ref[...]
    o_ref[...] = ((yn + bonus) * g_ref[...]).astype(o_ref.dtype)


def rwkv_readout(y0, y1, r, v, kd0, kd1, g, r_k, ln_w, ln_b, e128, gn_eps, rows, tm=ROW_TILE):
    d = r.shape[1]
    blk = pl.BlockSpec((tm, LANES), lambda i, c: (i, c))
    vec1 = pl.BlockSpec((1, LANES), lambda i, c: (0, c))
    return pl.pallas_call(
        functools.partial(_rw_readout_body, gn_eps=gn_eps),
        grid=(rows // tm, d // LANES),
        in_specs=[blk] * 7 + [vec1] * 3 + [pl.BlockSpec((LANES, LANES), lambda i, c: (0, 0))],
        out_specs=blk,
        out_shape=jax.ShapeDtypeStruct((rows, d), BF16),
        compiler_params=_cparams(("parallel", "arbitrary")),
        name="rwkv_readout",
    )(y0, y1, r, v, kd0, kd1, g, r_k.reshape(1, d), ln_w.reshape(1, d), ln_b.reshape(1, d), e128)


def rwkv_mixer(hmix, p, *, batch, seq, ctx_len, d_model):
    xs = rwkv_mix(hmix, p["mix"], batch=batch, seq=seq, ctx_len=ctx_len)
    m = hmix.shape[0]
    xs2 = xs.reshape(xs.shape[0] * m, d_model)
    tm = _row_tile(m)

    def proj(n, w, **kw):
        return mm(xs2, w, tm=tm, rows=m, x_row_block0=n * (m // tm), **kw)

    r = proj(0, p["w_r"], name="rw_r")
    k = proj(2, p["w_k"], name="rw_k")
    v = proj(3, p["w_v"], name="rw_v")
    g = mm(proj(5, p["g1"], act="sigmoid", out_dtype=BF16, name="rw_g1"), p["g2"], tm=tm, name="rw_g2")
    lws, als = [], []
    for d in range(2):
        tw = proj(1, p["w1"][d], act="tanh", out_dtype=BF16, name="rw_w1")
        lws.append(mm(tw, p["w2"][d], tm=tm, name="rw_w2"))
        ah = proj(4, p["a1"][d], out_dtype=BF16, name="rw_a1")
        als.append(mm(ah, p["a2"][d], tm=tm, name="rw_a2"))
    e128 = jnp.asarray(np.kron(np.eye(LANES // RW_HEAD), np.ones((RW_HEAD, RW_HEAD))), BF16)
    lw0, lw1, kd0, kd1, bb0, bb1, aa = rwkv_prep(k, lws[0], lws[1], als[0], als[1], p["w0"], p["a0"],
                                                 p["k_k"], p["k_a"], e128, tm=tm)
    kw = dict(batch=batch, seq=seq, ctx_len=ctx_len)
    y0 = wkv7(r, lw0, kd0, v, aa, bb0, reverse=False, **kw)
    y1 = wkv7(r, lw1, kd1, v, aa, bb1, reverse=True, **kw)
    return rwkv_readout(y0, y1, r, v, kd0, kd1, g, p["r_k"], p["ln_w"], p["ln_b"], e128,
                        1e-5 * RW_HEAD, rows=m)


def mla_mixer(hmix, p, *, batch, seq, ctx_len, want_ctx):
    m = hmix.shape[0]
    tm = _row_tile(m)
    scale = (MLA_NOPE + MLA_ROPE) ** -0.5 * LOG2E
    down = mm(hmix, p["w_down_ext"], tm=tm, name="mla_down")
    cos, sin = _rope_tables(seq, ctx_len, batch, MLA_ROPE)
    zeros = jnp.zeros_like(cos)
    cos_p = jnp.concatenate([cos, zeros], -1)
    sin_p = jnp.concatenate([sin, zeros], -1)
    q = proj_rope(down, p["w_uq_ext"], cos_p, sin_p, mode="mla_q", n_heads=MLA_HEADS, scale=scale,
                  norm_g=p["q_norm_g"], x_col_block=0, tm=tm)
    kcat, v = proj_rope(down, p["w_ukv"], cos_p, sin_p, mode="mla_kv", n_heads=MLA_HEADS,
                        norm_g=p["kv_norm_g"], x_col_block=1, pe=down,
                        pe_cb=(MLA_Q_RANK + MLA_KV_RANK) // LANES, tm=tm)
    kw = dict(dqk=2 * LANES, kv_group=1, n_heads=MLA_HEADS, batch=batch, seq=seq, ctx_len=ctx_len)
    o_lat = attention(q, kcat, v, lambda g: g, lat_queries=True, **kw)
    if not want_ctx:
        return o_lat
    o_ctx = attention(q, kcat, v, lambda g: g, lat_queries=False, **kw)
    return jnp.concatenate([o_lat, o_ctx], 0)


def gqa_mixer(hmix, p, *, batch, seq, ctx_len, want_ctx):
    m = hmix.shape[0]
    tm = _row_tile(m)
    scale = GQA_HEAD ** -0.5 * LOG2E
    cos, sin = _rope_tables(seq, ctx_len, batch, GQA_HEAD)
    nq, nk = GQA_HEADS, GQA_KV_HEADS
    q = proj_rope(hmix, p["w_q_ext"], cos, sin, mode="gqa", n_heads=nq, scale=scale,
                  gains=(p["q_norm_g"], p["q_norm_g_rot"]), tm=tm)
    k = proj_rope(hmix, p["w_k_ext"], cos, sin, mode="gqa", n_heads=nk,
                  gains=(p["k_norm_g"], p["k_norm_g_rot"]), tm=tm)
    vb = mm(hmix, p["w_v"], tm=tm, out_dtype=BF16, name="gqa_v")
    grp = nq // nk
    kw = dict(dqk=LANES, kv_group=grp, n_heads=nq, batch=batch, seq=seq, ctx_len=ctx_len)
    o_lat = attention(q, k, vb, lambda g: g, lat_queries=True, **kw)
    if not want_ctx:
        return o_lat
    o_ctx = attention(q, k, vb, lambda g: g, lat_queries=False, **kw)
    return jnp.concatenate([o_lat, o_ctx], 0)


def _rot_gain(g):
    q = g.shape[-1] // 4
    g4 = g.reshape(2, 2, q)
    return jnp.concatenate([g4[:, 1:2], g4[:, 0:1]], axis=1).reshape(g.shape)


def kernel(x, c, ctx, c_ctx, ada_w, ada_b, norm_g, ffn_w_gu, ffn_w_down, hy_w_in, hy_b_in, hy_conv_w, hy_conv_b, hy_f_w1, hy_f_b1, hy_f_w2, hy_f_b2, hy_f_w3, hy_sin_freq, hy_skip, hy_w_out, hy_b_out, rw_mix, rw_w_rkv, rw_w0, rw_w1, rw_w2, rw_a0, rw_a1, rw_a2, rw_g1, rw_g2, rw_k_k, rw_k_a, rw_r_k, rw_ln_w, rw_ln_b, rw_w_o, mla_w_down, mla_q_norm_g, mla_w_uq, mla_kv_norm_g, mla_w_ukv, mla_w_o, gqa_w_qkv, gqa_q_norm_g, gqa_k_norm_g, gqa_w_o, final_g):
    batch, seq, d = x.shape
    ctx_len = ctx.shape[1]
    depth = ada_w.shape[0]
    n_mix = 4
    rows_lat = batch * seq
    rows_all = rows_lat + batch * ctx_len
    bf = lambda a: a.astype(BF16)
    w_gu_all, w_down_all = bf(ffn_w_gu), ffn_w_down

    cvec = jnp.concatenate([c, c_ctx[None], jnp.zeros((8 - batch - 1, d), F32)], 0)
    mods_all = adaln_all(cvec, ada_w, ada_b)[:, :batch + 1].reshape(depth, batch + 1, N_MOD, d)

    h = jnp.concatenate([x.reshape(rows_lat, d), ctx.reshape(batch * ctx_len, d)], 0)

    for i in range(depth):
        mi, j = i % n_mix, i // n_mix
        ctx_out = any(jj % n_mix != 0 for jj in range(i + 1, depth))
        mods = mods_all[i]
        gate5 = mods[:, 5:6, :]
        mix_dtype = F32 if mi == 1 else BF16
        h, hmix = ffn_half(h, mods, norm_g[i], w_gu_all, w_down_all, layer=i, which=0, s=0, rows=rows_all,
                           seq_rows=seq, next_g=norm_g[i, 1], next_s=1, next_dtype=mix_dtype)
        rows_out = rows_all if ctx_out else rows_lat
        kw = dict(batch=batch, seq=seq, ctx_len=ctx_len)
        if mi == 0:
            p = dict(w_in=bf(hy_w_in[j]), b_in=hy_b_in[j], conv_w=hy_conv_w[j], conv_b=hy_conv_b[j],
                     f_w1=hy_f_w1[j], f_b1=hy_f_b1[j], f_w2=hy_f_w2[j], f_b2=hy_f_b2[j], f_w3=hy_f_w3[j],
                     sin_freq=hy_sin_freq[j], skip=hy_skip[j])
            z = hyena_mixer(hmix, p, d_model=d, want_ctx=ctx_out, **kw)
            w_o, b_o = bf(hy_w_out[j]), hy_b_out[j]
        elif mi == 1:
            pad = lambda w, ax: jnp.pad(w, [(0, RW_LORA_PAD - w.shape[a]) if a == ax else (0, 0) for a in range(w.ndim)])
            p = dict(mix=rw_mix[j], w_r=bf(rw_w_rkv[j, 0]), w_k=bf(rw_w_rkv[j, 1]), w_v=bf(rw_w_rkv[j, 2]),
                     g1=bf(rw_g1[j]), g2=bf(rw_g2[j]),
                     w1=bf(pad(rw_w1[j], 2)), w2=bf(pad(rw_w2[j], 1)), a1=bf(pad(rw_a1[j], 2)), a2=bf(pad(rw_a2[j], 1)),
                     w0=rw_w0[j], a0=rw_a0[j], k_k=rw_k_k[j], k_a=rw_k_a[j], r_k=rw_r_k[j].reshape(-1),
                     ln_w=rw_ln_w[j], ln_b=rw_ln_b[j])
            z = rwkv_mixer(hmix, p, d_model=d, **kw)
            w_o, b_o = bf(rw_w_o[j]), None
        elif mi == 2:
            wd = mla_w_down[j]
            pe = wd[:, MLA_Q_RANK + MLA_KV_RANK:]
            wuq = mla_w_uq[j].reshape(MLA_Q_RANK, MLA_HEADS, MLA_NOPE + MLA_ROPE)
            wq_pe = wuq[..., MLA_NOPE:]
            wuq_ext = jnp.concatenate([wuq[..., :MLA_NOPE], wq_pe,
                                       _rot_cols(wq_pe.reshape(MLA_Q_RANK, -1), MLA_ROPE).reshape(wq_pe.shape)], -1)
            p = dict(w_down_ext=bf(jnp.concatenate([wd, _rot_cols(pe, MLA_ROPE)], 1)),
                     w_uq_ext=bf(wuq_ext.reshape(MLA_Q_RANK, -1)), w_ukv=bf(mla_w_ukv[j]),
                     q_norm_g=mla_q_norm_g[j], kv_norm_g=mla_kv_norm_g[j])
            z = mla_mixer(hmix, p, want_ctx=ctx_out, **kw)
            w_o, b_o = bf(mla_w_o[j]), None
        else:
            w = gqa_w_qkv[j]
            nqc = GQA_HEADS * GQA_HEAD
            nkc = GQA_KV_HEADS * GQA_HEAD
            wq, wk, wv = w[:, :nqc], w[:, nqc:nqc + nkc], w[:, nqc + nkc:]

            def with_rot(wx, nh):
                both = jnp.stack([wx.reshape(d, nh, GQA_HEAD), _rot_cols(wx, GQA_HEAD).reshape(d, nh, GQA_HEAD)], 2)
                return bf(both.reshape(d, 2 * nh * GQA_HEAD))

            p = dict(w_q_ext=with_rot(wq, GQA_HEADS), w_k_ext=with_rot(wk, GQA_KV_HEADS), w_v=bf(wv),
                     q_norm_g=gqa_q_norm_g[j], q_norm_g_rot=_rot_gain(gqa_q_norm_g[j]),
                     k_norm_g=gqa_k_norm_g[j], k_norm_g_rot=_rot_gain(gqa_k_norm_g[j]))
            z = gqa_mixer(hmix, p, want_ctx=ctx_out, **kw)
            w_o, b_o = bf(gqa_w_o[j]), None
        h = mm(z, w_o, bias=b_o, res=h, gate=gate5, seq_rows=seq, rows=rows_out, tn=d, name="mix_out")
        last = i == depth - 1
        if last:
            h, out = ffn_half(h, mods, norm_g[i], w_gu_all, w_down_all, layer=i, which=1, s=2, rows=rows_out,
                              seq_rows=seq, next_g=final_g, next_s=None, next_dtype=F32)
        else:
            h = ffn_half(h, mods, norm_g[i], w_gu_all, w_down_all, layer=i, which=1, s=2, rows=rows_out,
                         seq_rows=seq)
    return out[:rows_lat].reshape(batch, seq, d)
```

```python
import functools
import math

import numpy as np
import jax
import jax.numpy as jnp
from jax import lax
from jax.experimental import pallas as pl
from jax.experimental.pallas import tpu as pltpu

F32 = jnp.float32
BF16 = jnp.bfloat16

VMEM_CAP_BYTES = 56 * 1024 * 1024
LANES = 128
SUBLANES = 8

NORM_EPS = 1e-6
LOG2E = 1.4426950408889634
N_MOD = 9
GRID_W = 64
ROPE_THETA = 10000.0
HY_DECAY_TARGET = 1e-2
HY_DECAY_PCT_SHORT = 0.3
HY_DECAY_PCT_LONG = 1.5
RW_HEAD = 64
RW_LORA_PAD = 128
GQA_HEAD = 128
GQA_HEADS = 16
GQA_KV_HEADS = 8
MLA_HEADS = 16
MLA_NOPE = 128
MLA_ROPE = 64
MLA_V = 128
MLA_Q_RANK = 512
MLA_KV_RANK = 512


def _cparams(sem):
    return pltpu.CompilerParams(dimension_semantics=sem, vmem_limit_bytes=VMEM_CAP_BYTES)


def _dot(a, b):
    return jnp.dot(a, b, preferred_element_type=F32)


def _dot_nt(a, b):
    return lax.dot_general(a, b, (((1,), (1,)), ((), ())), preferred_element_type=F32)


def _dot_tn(a, b):
    return lax.dot_general(a, b, (((0,), (0,)), ((), ())), preferred_element_type=F32)


def _split2(x):
    hi = x.astype(BF16)
    lo = (x - hi.astype(F32)).astype(BF16)
    return hi, lo


def _split3(x):
    hi = x.astype(BF16)
    r1 = x - hi.astype(F32)
    mid = r1.astype(BF16)
    lo = (r1 - mid.astype(F32)).astype(BF16)
    return hi, mid, lo


def _dot3(a, b, f=_dot):
    ah, al = _split2(a)
    bh, bl = _split2(b)
    return f(ah, bh) + f(ah, bl) + f(al, bh)


def _dotc2(mh, ml, d):
    dh = d.astype(BF16)
    return _dot(mh, dh) + _dot(ml, dh)


def _rms(x, g):
    return x * lax.rsqrt(jnp.mean(x * x, -1, keepdims=True) + NORM_EPS) * g


def _pick(n, cands):
    for c in cands:
        if n % c == 0:
            return c
    raise ValueError(f"no tile for {n}")


ROW_TILE = 512
WIDE_ROW_TILE = 1088
FFN_CHUNK = 512


def _row_tile(m):
    return WIDE_ROW_TILE if m % WIDE_ROW_TILE == 0 else ROW_TILE


def _adaln_body(c_ref, w_ref, b_ref, o_ref):
    c = c_ref[...]
    s = c * jax.nn.sigmoid(c)
    o_ref[0] = _dot(s.astype(BF16), w_ref[0].astype(BF16)) + b_ref[0]


def adaln_all(cvec8, ada_w, ada_b):
    depth, d, n = ada_w.shape
    tn = _pick(n, (2048, 1024, 512, 256, 128))
    return pl.pallas_call(
        _adaln_body,
        grid=(depth, n // tn),
        in_specs=[
            pl.BlockSpec((8, d), lambda l, j: (0, 0)),
            pl.BlockSpec((1, d, tn), lambda l, j: (l, 0, j)),
            pl.BlockSpec((1, 1, tn), lambda l, j: (l, 0, j)),
        ],
        out_specs=pl.BlockSpec((1, 8, tn), lambda l, j: (l, 0, j)),
        out_shape=jax.ShapeDtypeStruct((depth, 8, n), F32),
        compiler_params=_cparams(("arbitrary", "arbitrary")),
        name="adaln",
    )(cvec8, ada_w, ada_b.reshape(depth, 1, n))


def _mm_body(*refs, has_bias, act, has_norm, has_res):
    it = iter(refs)
    x_ref = next(it)
    w_ref = next(it)
    b_ref = next(it) if has_bias else None
    g_ref = next(it) if has_norm else None
    r_ref = next(it) if has_res else None
    gate_ref = next(it) if has_res else None
    o_ref = next(it)
    x = x_ref[...]
    if has_norm:
        x = _rms(x.astype(F32), g_ref[...])
    acc = _dot(x.astype(BF16), w_ref[...])
    if has_bias:
        acc = acc + b_ref[...]
    if act == "sigmoid":
        acc = jax.nn.sigmoid(acc)
    elif act == "tanh":
        acc = jnp.tanh(acc)
    if has_res:
        acc = r_ref[...] + gate_ref[0] * acc
    o_ref[...] = acc.astype(o_ref.dtype)


def mm(x, w, *, bias=None, act=None, norm_g=None, res=None, gate=None, out_dtype=F32,
       tm=ROW_TILE, tn=None, rows=None, x_col_block=0, x_row_block0=0, seq_rows=None, name="mm"):
    k, n = w.shape
    m = rows if rows is not None else x.shape[0]
    assert m % tm == 0, (m, tm)
    tn = tn or _pick(n, (512, 384, 256, 128))
    has_res = res is not None
    ins = [x, w]
    specs = [pl.BlockSpec((tm, k), lambda i, j: (i + x_row_block0, x_col_block)),
             pl.BlockSpec((k, tn), lambda i, j: (0, j))]
    if bias is not None:
        ins.append(bias.reshape(1, n).astype(F32))
        specs.append(pl.BlockSpec((1, tn), lambda i, j: (0, j)))
    if norm_g is not None:
        ins.append(norm_g.reshape(1, k).astype(F32))
        specs.append(pl.BlockSpec((1, k), lambda i, j: (0, 0)))
    if has_res:
        ngroups = gate.shape[0]
        per = seq_rows // tm
        ins += [res, gate]
        specs += [pl.BlockSpec((tm, tn), lambda i, j: (i, j)),
                  pl.BlockSpec((1, 1, tn), lambda i, j: (jnp.minimum(i // per, ngroups - 1), 0, j))]
    body = functools.partial(_mm_body, has_bias=bias is not None, act=act,
                             has_norm=norm_g is not None, has_res=has_res)
    return pl.pallas_call(
        body,
        grid=(m // tm, n // tn),
        in_specs=specs,
        out_specs=pl.BlockSpec((tm, tn), lambda i, j: (i, j)),
        out_shape=jax.ShapeDtypeStruct((m, n), out_dtype),
        compiler_params=_cparams(("parallel", "arbitrary")),
        name=name,
    )(*ins)


def _ffn_body(*refs, s, n_f, has_next, next_s):
    it = iter(refs)
    x_ref, mod_ref, g_ref, wa_ref, wb_ref, wd_ref = (next(it) for _ in range(6))
    gn_ref = next(it) if has_next else None
    o_ref = next(it)
    hn_ref = next(it) if has_next else None
    xn_sc = next(it)
    acc_sc = next(it)
    j = pl.program_id(1)

    @pl.when(j == 0)
    def _():
        n = _rms(x_ref[...], g_ref[s:s + 1, :])
        n = n * (1.0 + mod_ref[0, 3 * s + 1:3 * s + 2, :]) + mod_ref[0, 3 * s:3 * s + 1, :]
        xn_sc[...] = n.astype(BF16)
        acc_sc[...] = jnp.zeros_like(acc_sc)

    xn = xn_sc[...]
    a = _dot(xn, wa_ref[...])
    b = _dot(xn, wb_ref[...])
    h = (a * jax.nn.sigmoid(a)) * b
    acc_sc[...] += _dot(h.astype(BF16), wd_ref[...])

    @pl.when(j == n_f - 1)
    def _():
        xnew = x_ref[...] + 0.5 * mod_ref[0, 3 * s + 2:3 * s + 3, :] * acc_sc[...]
        o_ref[...] = xnew
        if has_next:
            hn = _rms(xnew, gn_ref[...])
            if next_s is not None:
                hn = hn * (1.0 + mod_ref[0, 3 * next_s + 1:3 * next_s + 2, :]) + mod_ref[0, 3 * next_s:3 * next_s + 1, :]
            hn_ref[...] = hn.astype(hn_ref.dtype)


def ffn_half(x, mods, norm_g3, w_gu, w_down, *, layer, which, s, rows, seq_rows, next_g=None, next_s=None,
             next_dtype=F32, tm=ROW_TILE, fc=FFN_CHUNK):
    d = x.shape[1]
    d_ff = w_down.shape[2]
    n_f = d_ff // fc
    assert d_ff % fc == 0 and rows % tm == 0
    per = seq_rows // tm
    ngroups = mods.shape[0]
    has_next = next_g is not None
    ins = [x, mods, norm_g3, w_gu, w_gu, w_down]
    specs = [
        pl.BlockSpec((tm, d), lambda i, j: (i, 0)),
        pl.BlockSpec((1, N_MOD, d), lambda i, j: (jnp.minimum(i // per, ngroups - 1), 0, 0)),
        pl.BlockSpec((3, d), lambda i, j: (0, 0)),
        pl.BlockSpec((None, None, d, fc), lambda i, j: (layer, which, 0, j)),
        pl.BlockSpec((None, None, d, fc), lambda i, j: (layer, which, 0, n_f + j)),
        pl.BlockSpec((None, None, fc, d), lambda i, j: (layer, which, j, 0)),
    ]
    out_shape = [jax.ShapeDtypeStruct((rows, d), F32)]
    out_specs = [pl.BlockSpec((tm, d), lambda i, j: (i, 0))]
    if has_next:
        ins.append(next_g.reshape(1, d))
        specs.append(pl.BlockSpec((1, d), lambda i, j: (0, 0)))
        out_shape.append(jax.ShapeDtypeStruct((rows, d), next_dtype))
        out_specs.append(pl.BlockSpec((tm, d), lambda i, j: (i, 0)))
    body = functools.partial(_ffn_body, s=s, n_f=n_f, has_next=has_next, next_s=next_s)
    outs = pl.pallas_call(
        body,
        grid=(rows // tm, n_f),
        in_specs=specs,
        out_specs=out_specs,
        out_shape=out_shape,
        scratch_shapes=[pltpu.VMEM((tm, d), BF16), pltpu.VMEM((tm, d), F32)],
        compiler_params=_cparams(("parallel", "arbitrary")),
        name="ffn_half",
    )(*ins)
    return outs if has_next else outs[0]


def _attn_body(*refs, use_lat, hp, n_kv, dqk):
    it = iter(refs)
    q_ref = next(it)
    kl_ref = next(it) if use_lat else None
    kc_ref = next(it)
    vl_refs = [next(it) for _ in range(n_kv)] if use_lat else None
    vc_refs = [next(it) for _ in range(n_kv)]
    o_ref = next(it)
    hs = range(hp)
    slot = [h * n_kv // hp for h in hs]
    q = [q_ref[:, h * dqk:(h + 1) * dqk] for h in hs]
    s_c = [_dot_nt(kc_ref[:, slot[h] * dqk:(slot[h] + 1) * dqk], q[h]) for h in hs]
    m = [jnp.max(s_c[h], 0, keepdims=True) for h in hs]
    if use_lat:
        s_l = [_dot_nt(kl_ref[:, slot[h] * dqk:(slot[h] + 1) * dqk], q[h]) for h in hs]
        m = [jnp.maximum(m[h], jnp.max(s_l[h], 0, keepdims=True)) for h in hs]
    p_c = [jnp.exp2(s_c[h] - m[h]) for h in hs]
    den = [jnp.sum(p_c[h], 0, keepdims=True) for h in hs]
    o = [_dot_tn(vc_refs[slot[h]][...], p_c[h].astype(BF16)) for h in hs]
    if use_lat:
        p_l = [jnp.exp2(s_l[h] - m[h]) for h in hs]
        den = [den[h] + jnp.sum(p_l[h], 0, keepdims=True) for h in hs]
        o = [o[h] + _dot_tn(vl_refs[slot[h]][...], p_l[h].astype(BF16)) for h in hs]
    for h in hs:
        o_ref[:, h * LANES:(h + 1) * LANES] = (o[h] / den[h]).T.astype(o_ref.dtype)


def attention(q_arr, k_arr, v_arr, v_cb, *, dqk, kv_group, n_heads, batch, seq, ctx_len, lat_queries,
              tq=2 * ROW_TILE, hp=1):
    assert n_heads % hp == 0 and (kv_group % hp == 0 or hp % kv_group == 0)
    n_kv = max(hp // kv_group, 1)
    kv0 = lambda p: (p * hp) // kv_group
    ctx0 = (batch * seq) // ctx_len
    if lat_queries:
        tq = min(tq, seq)
        nq = seq // tq
        q_row = lambda b, t: b * nq + t
        o_row = lambda b, t: b * nq + t
        rows_q = batch * seq
    else:
        tq = ctx_len
        nq = 1
        q_row = lambda b, t: ctx0 + b
        o_row = lambda b, t: b
        rows_q = batch * ctx_len
    ins = [q_arr]
    specs = [pl.BlockSpec((tq, hp * dqk), lambda b, p, t: (q_row(b, t), p))]
    if lat_queries:
        ins.append(k_arr)
        specs.append(pl.BlockSpec((seq, n_kv * dqk), lambda b, p, t: (b, kv0(p) // n_kv)))
    ins.append(k_arr)
    specs.append(pl.BlockSpec((ctx_len, n_kv * dqk), lambda b, p, t: (ctx0 + b, kv0(p) // n_kv)))
    if lat_queries:
        for s in range(n_kv):
            ins.append(v_arr)
            specs.append(pl.BlockSpec((seq, LANES), lambda b, p, t, s=s: (b, v_cb(kv0(p) + s))))
    for s in range(n_kv):
        ins.append(v_arr)
        specs.append(pl.BlockSpec((ctx_len, LANES), lambda b, p, t, s=s: (ctx0 + b, v_cb(kv0(p) + s))))
    return pl.pallas_call(
        functools.partial(_attn_body, use_lat=lat_queries, hp=hp, n_kv=n_kv, dqk=dqk),
        grid=(batch, n_heads // hp, nq),
        in_specs=specs,
        out_specs=pl.BlockSpec((tq, hp * LANES), lambda b, p, t: (o_row(b, t), p)),
        out_shape=jax.ShapeDtypeStruct((rows_q, n_heads * LANES), BF16),
        compiler_params=_cparams(("parallel", "parallel", "arbitrary")),
        name="attention_lat" if lat_queries else "attention_ctx",
    )(*ins)


def _rope_tables(seq, ctx_len, batch, rot_dim):
    half = rot_dim // 4
    inv_freq = ROPE_THETA ** (-jnp.arange(half, dtype=F32) / half)
    t = jnp.arange(seq, dtype=jnp.int32)
    row = (t // GRID_W).astype(F32)
    col = (t % GRID_W).astype(F32)
    ang_r = row[:, None] * inv_freq[None]
    ang_c = col[:, None] * inv_freq[None]
    cos = jnp.concatenate([jnp.cos(ang_r), jnp.cos(ang_r), jnp.cos(ang_c), jnp.cos(ang_c)], -1)
    sin = jnp.concatenate([jnp.sin(ang_r), jnp.sin(ang_r), jnp.sin(ang_c), jnp.sin(ang_c)], -1)
    cos = jnp.concatenate([jnp.tile(cos, (batch, 1)), jnp.ones((batch * ctx_len, rot_dim), F32)], 0)
    sin = jnp.concatenate([jnp.tile(sin, (batch, 1)), jnp.zeros((batch * ctx_len, rot_dim), F32)], 0)
    return cos, sin


def _rot_cols(w, rot_dim):
    lead = w.shape[:-1]
    q = rot_dim // 4
    w5 = w.reshape(lead + (-1, 2, 2, q))
    rot = jnp.concatenate([-w5[..., 1:2, :], w5[..., 0:1, :]], axis=-2)
    return rot.reshape(w.shape)


def _rope_half(b, cos, sin):
    return b * cos + pltpu.roll(b, LANES // 2, 1) * sin


def _proj_rope_body(*refs, mode, has_norm, scale):
    it = iter(refs)
    x_ref, w_ref = next(it), next(it)
    ng_ref = next(it) if has_norm else None
    cos_ref, sin_ref = next(it), next(it)
    x = x_ref[...]
    if has_norm:
        x = _rms(x.astype(F32), ng_ref[...])
    acc = _dot(x.astype(BF16), w_ref[...])
    a, b = acc[:, :LANES], acc[:, LANES:]
    if mode == "gqa":
        g_ref, gr_ref, o_ref = next(it), next(it), next(it)
        r = lax.rsqrt(jnp.mean(a * a, -1, keepdims=True) + NORM_EPS) * scale
        o_ref[...] = (((a * g_ref[...]) * cos_ref[...] + (b * gr_ref[...]) * sin_ref[...]) * r).astype(o_ref.dtype)
    elif mode == "mla_q":
        o_ref = next(it)
        o_ref[:, :LANES] = (a * scale).astype(o_ref.dtype)
        o_ref[:, LANES:] = (_rope_half(b, cos_ref[...], sin_ref[...]) * scale).astype(o_ref.dtype)
    else:
        pe_ref, k_ref, v_ref = next(it), next(it), next(it)
        k_ref[:, :LANES] = a.astype(k_ref.dtype)
        k_ref[:, LANES:] = _rope_half(pe_ref[...], cos_ref[...], sin_ref[...]).astype(k_ref.dtype)
        v_ref[...] = b.astype(v_ref.dtype)


def proj_rope(x, w, cos, sin, *, mode, n_heads, scale=1.0, norm_g=None, x_col_block=0, gains=None,
              pe=None, pe_cb=0, tm=WIDE_ROW_TILE):
    m = x.shape[0]
    k = w.shape[0]
    ins = [x, w]
    specs = [pl.BlockSpec((tm, k), lambda i, h: (i, x_col_block)),
             pl.BlockSpec((k, 2 * LANES), lambda i, h: (0, h))]
    if norm_g is not None:
        ins.append(norm_g.reshape(1, k).astype(F32))
        specs.append(pl.BlockSpec((1, k), lambda i, h: (0, 0)))
    tab = pl.BlockSpec((tm, LANES), lambda i, h: (i, 0))
    ins += [cos, sin]
    specs += [tab, tab]
    if mode == "gqa":
        vec = pl.BlockSpec((1, LANES), lambda i, h: (0, 0))
        ins += [gains[0].reshape(1, LANES), gains[1].reshape(1, LANES)]
        specs += [vec, vec]
        out_shape = jax.ShapeDtypeStruct((m, n_heads * LANES), BF16)
        out_specs = pl.BlockSpec((tm, LANES), lambda i, h: (i, h))
    elif mode == "mla_q":
        out_shape = jax.ShapeDtypeStruct((m, n_heads * 2 * LANES), BF16)
        out_specs = pl.BlockSpec((tm, 2 * LANES), lambda i, h: (i, h))
    else:
        ins.append(pe)
        specs.append(pl.BlockSpec((tm, LANES), lambda i, h: (i, pe_cb)))
        out_shape = [jax.ShapeDtypeStruct((m, n_heads * 2 * LANES), BF16),
                     jax.ShapeDtypeStruct((m, n_heads * LANES), BF16)]
        out_specs = [pl.BlockSpec((tm, 2 * LANES), lambda i, h: (i, h)),
                     pl.BlockSpec((tm, LANES), lambda i, h: (i, h))]
    return pl.pallas_call(
        functools.partial(_proj_rope_body, mode=mode, has_norm=norm_g is not None, scale=scale),
        grid=(m // tm, n_heads),
        in_specs=specs,
        out_specs=out_specs,
        out_shape=out_shape,
        compiler_params=_cparams(("parallel", "arbitrary")),
        name="proj_" + mode,
    )(*ins)


class _FftPlan:
    def __init__(self, L):
        self.L = L
        self.N = 2 * L
        self.N2 = 128 if L >= 1024 else 64
        self.N1 = self.N // self.N2
        self.n1_in = self.N1 // 2
        self.K1 = self.N1 // 2 + 1
        self.nq = self.N2 // SUBLANES
        self.kf_cols = max(self.n1_in * SUBLANES, LANES)
        self.kf_rows = 2 * self.K1 * SUBLANES
        self.kb_cols = -(-self.kf_rows // LANES) * LANES
        self.kb_rows = self.n1_in * SUBLANES

    def constants(self):
        N, N1, N2, K1, n1_in = self.N, self.N1, self.N2, self.K1, self.n1_in
        eye = np.eye(SUBLANES)
        k1 = np.arange(K1)[:, None]
        n1 = np.arange(n1_in)[None, :]
        th = 2 * np.pi * k1 * n1 / N1
        kf = np.zeros((self.kf_rows, self.kf_cols))
        kf[:K1 * SUBLANES, :n1_in * SUBLANES] = np.kron(np.cos(th), eye)
        kf[K1 * SUBLANES:, :n1_in * SUBLANES] = np.kron(-np.sin(th), eye)
        c = np.full((K1,), 2.0)
        c[0] = 1.0
        c[-1] = 1.0
        thb = 2 * np.pi * np.arange(n1_in)[:, None] * np.arange(K1)[None, :] / N1
        kb = np.zeros((self.kb_rows, self.kb_cols))
        kb[:, :K1 * SUBLANES] = np.kron(np.cos(thb) * c[None, :], eye)
        kb[:, K1 * SUBLANES:2 * K1 * SUBLANES] = np.kron(-np.sin(thb) * c[None, :], eye)
        a = 2 * np.pi * np.outer(np.arange(N2), np.arange(N2)) / N2
        C, S = np.cos(a), np.sin(a)
        w2f = np.block([[C, S], [-S, C]])
        w2i = np.block([[C, -S], [S, C]])
        tw_ang = 2 * np.pi * np.outer(np.arange(N2), np.arange(K1)) / N
        twr = np.repeat(np.cos(tw_ang), LANES, axis=1)
        twi = np.repeat(np.sin(tw_ang), LANES, axis=1)

        def hl(m):
            m32 = jnp.asarray(m, F32)
            hi = m32.astype(BF16)
            lo = (m32 - hi.astype(F32)).astype(BF16)
            return hi, lo

        out = {}
        for name, mat in (("kf", kf), ("kb", kb), ("w2f", w2f), ("w2i", w2i)):
            out[name + "_h"], out[name + "_l"] = hl(mat)
        out["twr"] = jnp.asarray(twr, F32)
        out["twi"] = jnp.asarray(twi, F32)
        return out


def _cmul(ar, ai, br, bi):
    return ar * br - ai * bi, ar * bi + ai * br


def _time_to_strided(plan, x_ref, xt_ref):
    N2, nq = plan.N2, plan.nq
    for n in range(plan.n1_in):
        xt_ref[n * SUBLANES:(n + 1) * SUBLANES, :] = jnp.concatenate(
            [x_ref[N2 * n + SUBLANES * q:N2 * n + SUBLANES * (q + 1), :] for q in range(nq)], axis=1)
    pad = plan.kf_cols - plan.n1_in * SUBLANES
    if pad:
        xt_ref[plan.n1_in * SUBLANES:, :] = jnp.zeros((pad, xt_ref.shape[1]), F32)


def _strided_to_spectrum(plan, a_ref, s_ref, c):
    N2, K1, nq = plan.N2, plan.K1, plan.nq
    for part in range(2):
        for k in range(K1):
            r0 = (part * K1 + k) * SUBLANES
            s_ref[part * N2:(part + 1) * N2, k * c:(k + 1) * c] = jnp.concatenate(
                [a_ref[r0:r0 + SUBLANES, q * c:(q + 1) * c] for q in range(nq)], axis=0)


def _spectrum_to_strided(plan, s_ref, b_ref, c):
    N2, K1, nq = plan.N2, plan.K1, plan.nq
    for part in range(2):
        for k in range(K1):
            r0 = (part * K1 + k) * SUBLANES
            b_ref[r0:r0 + SUBLANES, :] = jnp.concatenate(
                [s_ref[part * N2 + SUBLANES * q:part * N2 + SUBLANES * (q + 1), k * c:(k + 1) * c] for q in range(nq)],
                axis=1)
    pad = plan.kb_cols - plan.kf_rows
    if pad:
        b_ref[plan.kf_rows:, :] = jnp.zeros((pad, b_ref.shape[1]), F32)


def _strided_to_time(plan, yt, c):
    nq = plan.nq
    slabs = []
    for n in range(plan.n1_in):
        blk = yt[n * SUBLANES:(n + 1) * SUBLANES, :]
        slabs.extend(blk[:, q * c:(q + 1) * c] for q in range(nq))
    return jnp.concatenate(slabs, axis=0)


def _fft_fwd(plan, x_ref, consts, xt_ref, a_ref, s_ref, dotc):
    c = x_ref.shape[1]
    _time_to_strided(plan, x_ref, xt_ref)
    a_ref[:plan.kf_rows, :] = dotc(consts["kf_h"][...], consts["kf_l"][...], xt_ref[...])
    _strided_to_spectrum(plan, a_ref, s_ref, c)
    n2 = plan.N2
    tr, ti = consts["twr"][...], consts["twi"][...]
    xr, xi = _cmul(s_ref[:n2, :], s_ref[n2:, :], tr, -ti)
    s_ref[...] = dotc(consts["w2f_h"][...], consts["w2f_l"][...], jnp.concatenate([xr, xi], axis=0))


def _fft_conv_apply(plan, x_ref, g_ref, consts, xt_ref, a_ref, s_ref):
    c = x_ref.shape[1]
    n2 = plan.N2
    _fft_fwd(plan, x_ref, consts, xt_ref, a_ref, s_ref, _dotc2)
    pr, pi = _cmul(s_ref[:n2, :], s_ref[n2:, :], g_ref[:n2, :], g_ref[n2:, :])
    z = _dotc2(consts["w2i_h"][...], consts["w2i_l"][...], jnp.concatenate([pr, pi], axis=0))
    zr, zi = _cmul(z[:n2], z[n2:], consts["twr"][...], consts["twi"][...])
    s_ref[:n2, :] = zr
    s_ref[n2:, :] = zi
    _spectrum_to_strided(plan, s_ref, a_ref, c)
    yt = _dotc2(consts["kb_h"][...], consts["kb_l"][...], a_ref[...])
    return _strided_to_time(plan, yt, c)


def _fft_scratch(plan):
    wl = plan.nq * LANES
    return [pltpu.VMEM((plan.kf_cols, wl), F32),
            pltpu.VMEM((max(plan.kf_rows, plan.kb_cols), wl), F32),
            pltpu.VMEM((2 * plan.N2, plan.K1 * LANES), F32)]


_CONST_NAMES = ("kf_h", "kf_l", "kb_h", "kb_l", "w2f_h", "w2f_l", "w2i_h", "w2i_l", "twr", "twi")


def _const_specs(consts, nargs):
    ins, specs = [], []
    for nme in _CONST_NAMES:
        a = consts[nme]
        ins.append(a)
        if nargs == 1:
            specs.append(pl.BlockSpec(a.shape, lambda c: (0, 0), pipeline_mode=pl.Buffered(1)))
        else:
            specs.append(pl.BlockSpec(a.shape, lambda c, b: (0, 0), pipeline_mode=pl.Buffered(1)))
    return ins, specs


def _hy_hidden_body(z_ref, w1_ref, b1_ref, w2_ref, b2_ref, sf_ref, o_ref):
    h = jnp.sin(sf_ref[0:1, :] * (_dot3(z_ref[...], w1_ref[...]) + b1_ref[...]))
    h = jnp.sin(sf_ref[1:2, :] * (_dot3(h, w2_ref[...]) + b2_ref[...]))
    o_ref[...] = h


def hyena_hidden(L, f_w1, f_b1, f_w2, f_b2, sin_freq):
    emb, hid = f_w1.shape
    bands = (emb - 1) // 2
    t = jnp.linspace(0.0, 1.0, L, dtype=F32)[:, None]
    w = (2.0 * math.pi / L) * jnp.arange(L, dtype=F32)[:, None]
    f = jnp.linspace(1e-4, bands - 1, bands, dtype=F32)[None]
    z = jnp.concatenate([t, jnp.cos(f * w), -jnp.sin(f * w)], -1)
    embp = -(-emb // LANES) * LANES
    z = jnp.pad(z, ((0, 0), (0, embp - emb)))
    w1 = jnp.pad(f_w1, ((0, embp - emb), (0, 0)))
    ins = [z, w1, f_b1.reshape(1, hid), f_w2, f_b2.reshape(1, hid), sin_freq]
    return pl.pallas_call(
        _hy_hidden_body,
        out_shape=jax.ShapeDtypeStruct((L, hid), F32),
        name="hyena_hidden",
    )(*ins)


def _hy_filter_body(h_ref, wf_ref, wb_ref, dl_ref, *rest, plan):
    consts = {nme: r for nme, r in zip(_CONST_NAMES, rest)}
    g_ref, x_sc, xt_sc, a_sc, s_sc, sb_sc = rest[len(_CONST_NAMES):]
    L, n2 = plan.L, plan.N2
    t = lax.broadcasted_iota(jnp.int32, (L, LANES), 0).astype(F32) * (1.0 / (L - 1))
    win = jnp.exp(-t * dl_ref[...])
    row0 = lax.broadcasted_iota(jnp.int32, (L, LANES), 0) == 0
    h = h_ref[...]
    fwd = _dot3(h, wf_ref[...]) * win
    bwd = jnp.where(row0, 0.0, _dot3(h, wb_ref[...]) * win)
    ss = jnp.sum(fwd * fwd, 0, keepdims=True) + jnp.sum(bwd * bwd, 0, keepdims=True)
    nu = lax.rsqrt(ss + 1e-12) * (1.0 / plan.N)
    x_sc[...] = bwd * nu
    _fft_fwd(plan, x_sc, consts, xt_sc, a_sc, sb_sc, _dotc2)
    x_sc[...] = fwd * nu
    _fft_fwd(plan, x_sc, consts, xt_sc, a_sc, s_sc, _dotc2)
    g_ref[:n2, :] = s_sc[:n2, :] + sb_sc[:n2, :]
    g_ref[n2:, :] = s_sc[n2:, :] - sb_sc[n2:, :]


def hyena_filter_spectra(plan, consts, hidden, f_w3, d_model, n_order):
    L = plan.L
    hid = hidden.shape[1]
    ncb = d_model // LANES
    deltas = jnp.abs(jnp.linspace(math.log(HY_DECAY_TARGET) / HY_DECAY_PCT_SHORT,
                                  math.log(HY_DECAY_TARGET) / HY_DECAY_PCT_LONG, d_model, dtype=F32)).reshape(1, d_model)
    ins = [hidden, f_w3, f_w3, deltas]
    specs = [pl.BlockSpec((L, hid), lambda c, n: (0, 0)),
             pl.BlockSpec((hid, LANES), lambda c, n: (0, (2 * n) * ncb + c)),
             pl.BlockSpec((hid, LANES), lambda c, n: (0, (2 * n + 1) * ncb + c)),
             pl.BlockSpec((1, LANES), lambda c, n: (0, c))]
    ci, cs = _const_specs(consts, 2)
    ins += ci
    specs += cs
    srows, scols = 2 * plan.N2, plan.K1 * LANES
    return pl.pallas_call(
        functools.partial(_hy_filter_body, plan=plan),
        grid=(ncb, n_order),
        in_specs=specs,
        out_specs=pl.BlockSpec((None, None, srows, scols), lambda c, n: (n, c, 0, 0)),
        out_shape=jax.ShapeDtypeStruct((n_order, ncb, srows, scols), F32),
        scratch_shapes=[pltpu.VMEM((L, LANES), F32)] + _fft_scratch(plan) + [pltpu.VMEM((srows, scols), F32)],
        compiler_params=_cparams(("arbitrary", "arbitrary")),
        name="hyena_filter",
    )(*ins)


def _conv3(p, w_ref, b_ref, L):
    rows = lax.broadcasted_iota(jnp.int32, p.shape, 0)
    prev = jnp.where(rows == 0, 0.0, pltpu.roll(p, 1, 0))
    nxt = jnp.where(rows == L - 1, 0.0, pltpu.roll(p, L - 1, 0))
    return prev * w_ref[0:1, :] + p * w_ref[1:2, :] + nxt * w_ref[2:3, :] + b_ref[...]


def _hy_conv_body(*refs, plan, order, conv_u):
    it = iter(refs)
    u_ref, gate_ref = next(it), next(it)
    if conv_u:
        cwu_ref, cbu_ref = next(it), next(it)
    cwg_ref, cbg_ref = next(it), next(it)
    skip_ref = next(it)
    g_ref = next(it)
    consts = {nme: next(it) for nme in _CONST_NAMES}
    o_ref = next(it)
    x_sc, xt_sc, a_sc, s_sc = (next(it) for _ in range(4))
    L = plan.L
    u = u_ref[...]
    if conv_u:
        u = _conv3(u, cwu_ref, cbu_ref, L)
    x_sc[...] = u
    y = _fft_conv_apply(plan, x_sc, g_ref, consts, xt_sc, a_sc, s_sc)
    gate = _conv3(gate_ref[...], cwg_ref, cbg_ref, L)
    o_ref[...] = (gate * (y + x_sc[...] * skip_ref[order:order + 1, :])).astype(o_ref.dtype)


def hyena_conv(plan, consts, u_arr, u_cb0, gate_arr, gate_cb0, conv_w, conv_b, skip, spectra, *,
               order, conv_u, u_row_block0, gate_row_block0, batch, d_model, out_dtype):
    L = plan.L
    ncb = d_model // LANES
    ins = [u_arr, gate_arr]
    specs = [pl.BlockSpec((L, LANES), lambda c, b: (u_row_block0 + b, u_cb0 + c)),
             pl.BlockSpec((L, LANES), lambda c, b: (gate_row_block0 + b, gate_cb0 + c))]
    if conv_u:
        ins += [conv_w, conv_b]
        specs += [pl.BlockSpec((3, LANES), lambda c, b: (0, u_cb0 + c)),
                  pl.BlockSpec((1, LANES), lambda c, b: (0, u_cb0 + c))]
    srows, scols = 2 * plan.N2, plan.K1 * LANES
    ins += [conv_w, conv_b, skip, spectra]
    specs += [pl.BlockSpec((3, LANES), lambda c, b: (0, gate_cb0 + c)),
              pl.BlockSpec((1, LANES), lambda c, b: (0, gate_cb0 + c)),
              pl.BlockSpec((skip.shape[0], LANES), lambda c, b: (0, c)),
              pl.BlockSpec((None, None, srows, scols), lambda c, b: (order, c, 0, 0))]
    ci, cs = _const_specs(consts, 2)
    ins += ci
    specs += cs
    return pl.pallas_call(
        functools.partial(_hy_conv_body, plan=plan, order=order, conv_u=conv_u),
        grid=(ncb, batch),
        in_specs=specs,
        out_specs=pl.BlockSpec((L, LANES), lambda c, b: (b, c)),
        out_shape=jax.ShapeDtypeStruct((batch * L, d_model), out_dtype),
        scratch_shapes=[pltpu.VMEM((L, LANES), F32)] + _fft_scratch(plan),
        compiler_params=_cparams(("parallel", "arbitrary")),
        name=f"hyena_conv{order}_L{L}",
    )(*ins)


def hyena_mixer(hmix, p, *, batch, seq, ctx_len, d_model, want_ctx):
    n_order = p["skip"].shape[0]
    rows = batch * seq + (batch * ctx_len if want_ctx else 0)
    pr = mm(hmix, p["w_in"], bias=p["b_in"], name="hy_in", rows=rows, tm=_row_tile(rows))
    conv_b = p["conv_b"].reshape(1, -1)
    ncb = d_model // LANES
    outs = []
    streams = [(seq, 0)]
    if want_ctx:
        streams.append((ctx_len, (batch * seq) // ctx_len))
    for L, rb0 in streams:
        plan = _FftPlan(L)
        consts = plan.constants()
        hidden = hyena_hidden(L, p["f_w1"], p["f_b1"], p["f_w2"], p["f_b2"], p["sin_freq"])
        spectra = hyena_filter_spectra(plan, consts, hidden, p["f_w3"], d_model, n_order)
        z1 = hyena_conv(plan, consts, pr, 0, pr, ncb, p["conv_w"], conv_b, p["skip"], spectra,
                        order=0, conv_u=True, u_row_block0=rb0, gate_row_block0=rb0, batch=batch, d_model=d_model,
                        out_dtype=F32)
        z2 = hyena_conv(plan, consts, z1, 0, pr, 2 * ncb, p["conv_w"], conv_b, p["skip"], spectra,
                        order=1, conv_u=False, u_row_block0=0, gate_row_block0=rb0, batch=batch, d_model=d_model,
                        out_dtype=BF16)
        outs.append(z2)
    return outs[0] if len(outs) == 1 else jnp.concatenate(outs, 0)


def _headsum(x, e_ref):
    hi, lo = _split2(x)
    return _dot(hi, e_ref[...]) + _dot(lo, e_ref[...])


def _rw_mix_body(h_ref, hp_ref, hn_ref, mix_ref, o_ref, *, tm, lat_tiles, tiles_per_seq):
    i = pl.program_id(0)
    is_ctx = i >= lat_tiles
    first = jnp.logical_or(is_ctx, i % tiles_per_seq == 0)
    last = jnp.logical_or(is_ctx, i % tiles_per_seq == tiles_per_seq - 1)
    h = h_ref[...]
    rows = lax.broadcasted_iota(jnp.int32, h.shape, 0)
    pedge = jnp.where(first, 0.0, hp_ref[SUBLANES - 1:SUBLANES, :])
    nedge = jnp.where(last, 0.0, hn_ref[0:1, :])
    prev = jnp.where(rows == 0, pedge, pltpu.roll(h, 1, 0))
    nxt = jnp.where(rows == tm - 1, nedge, pltpu.roll(h, tm - 1, 0))
    dp = prev - h
    dn = nxt - h
    for n in range(o_ref.shape[0]):
        o_ref[n] = (h + dp * mix_ref[0, n:n + 1, :] + dn * mix_ref[1, n:n + 1, :]).astype(o_ref.dtype)


def rwkv_mix(h, mix, *, batch, seq, ctx_len):
    m, d = h.shape
    tm = ctx_len
    n_shift = mix.shape[1]
    nblk = m // SUBLANES
    per = tm // SUBLANES
    body = functools.partial(_rw_mix_body, tm=tm, lat_tiles=(batch * seq) // tm, tiles_per_seq=seq // tm)
    return pl.pallas_call(
        body,
        grid=(m // tm,),
        in_specs=[
            pl.BlockSpec((tm, d), lambda i: (i, 0)),
            pl.BlockSpec((SUBLANES, d), lambda i: (jnp.maximum(i * per - 1, 0), 0)),
            pl.BlockSpec((SUBLANES, d), lambda i: (jnp.minimum((i + 1) * per, nblk - 1), 0)),
            pl.BlockSpec((2, n_shift, d), lambda i: (0, 0, 0)),
        ],
        out_specs=pl.BlockSpec((n_shift, tm, d), lambda i: (0, i, 0)),
        out_shape=jax.ShapeDtypeStruct((n_shift, m, d), BF16),
        compiler_params=_cparams(("parallel",)),
        name="rwkv_mix",
    )(h, h, h, mix)


def _rw_prep_body(k_ref, lw0_ref, lw1_ref, al0_ref, al1_ref, w0_ref, a0_ref, kk_ref, ka_ref, e_ref,
                  olw0, olw1, okd0, okd1, obb0, obb1, oaa):
    k = k_ref[...]
    kk0 = k * kk_ref[...]
    kk = kk0 * lax.rsqrt(_headsum(kk0 * kk0, e_ref) + 1e-12)
    oaa[...] = -kk
    for d, (lw_ref, al_ref, olw, okd, obb) in enumerate(((lw0_ref, al0_ref, olw0, okd0, obb0),
                                                         (lw1_ref, al1_ref, olw1, okd1, obb1))):
        z = -(w0_ref[d:d + 1, :] + lw_ref[...])
        softplus = jnp.maximum(z, 0.0) + jnp.log(1.0 + jnp.exp(-jnp.abs(z)))
        log_w = -softplus - 0.5
        olw[...] = -jnp.exp(log_w)
        a = jax.nn.sigmoid(a0_ref[d:d + 1, :] + al_ref[...])
        okd[...] = k * (1.0 + (a - 1.0) * ka_ref[...])
        obb[...] = kk * a


def rwkv_prep(k, lw0, lw1, al0, al1, w0, a0, k_k, k_a, e128, tm=WIDE_ROW_TILE):
    m, d = k.shape
    blk = pl.BlockSpec((tm, LANES), lambda i, c: (i, c))
    vec2 = pl.BlockSpec((2, LANES), lambda i, c: (0, c))
    vec1 = pl.BlockSpec((1, LANES), lambda i, c: (0, c))
    sh = jax.ShapeDtypeStruct((m, d), F32)
    return pl.pallas_call(
        _rw_prep_body,
        grid=(m // tm, d // LANES),
        in_specs=[blk] * 5 + [vec2, vec2, vec1, vec1, pl.BlockSpec((LANES, LANES), lambda i, c: (0, 0))],
        out_specs=[blk] * 7,
        out_shape=[sh] * 7,
        compiler_params=_cparams(("parallel", "arbitrary")),
        name="rwkv_prep",
    )(k, lw0, lw1, al0, al1, w0, a0, k_k.reshape(1, d), k_a.reshape(1, d), e128)


def _b16(x):
    return x.astype(BF16)


def _wkv_body(r_ref, lw_ref, k_ref, v_ref, a_ref, b_ref, o_ref, s_sc, *, chunk, n_heads, reverse):
    c = pl.program_id(2)

    @pl.when(c == 0)
    def _():
        s_sc[...] = jnp.zeros_like(s_sc)

    C = chunk
    hs = range(n_heads)
    row = lax.broadcasted_iota(jnp.int32, (C, C), 0)
    col = lax.broadcasted_iota(jnp.int32, (C, C), 1)
    if reverse:
        tri_incl = col >= row
        tri_strict = col > row
        last = 0
    else:
        tri_incl = col <= row
        tri_strict = col < row
        last = C - 1
    same = lambda s: (row >> int(math.log2(s))) == (col >> int(math.log2(s)))
    eye = (row == col).astype(F32)
    cum_m = tri_incl.astype(BF16)
    sls = [slice(h * RW_HEAD, (h + 1) * RW_HEAD) for h in hs]

    lw = [lw_ref[:, sl] for sl in sls]
    cum = []
    for h in hs:
        l1, l2, l3 = _split3(lw[h])
        cum.append(_dot(cum_m, l1) + _dot(cum_m, l2) + _dot(cum_m, l3))
    p_inc = [jnp.exp(cum[h]) for h in hs]
    x1 = [jnp.concatenate([a_ref[:, sls[h]] * jnp.exp(cum[h] - lw[h]), r_ref[:, sls[h]] * p_inc[h]], axis=0) for h in hs]
    x2 = []
    for h in hs:
        p_inv = jnp.exp(-cum[h])
        x2.append(jnp.concatenate([b_ref[:, sls[h]] * p_inv, k_ref[:, sls[h]] * p_inv], axis=0))
    v = [v_ref[:, sl] for sl in sls]
    s0 = [s_sc[h] for h in hs]
    def cat_a(x):
        hi, lo = _split2(x)
        return jnp.concatenate([hi, lo, hi], axis=1)

    def cat_b(x):
        hi, lo = _split2(x)
        return jnp.concatenate([hi, hi, lo], axis=1)

    x1c = [cat_a(x1[h]) for h in hs]
    mfull = [_dot_nt(x1c[h], cat_b(x2[h])) for h in hs]
    w0 = [_dot_nt(x1c[h], cat_b(s0[h])) for h in hs]
    n_ab = [jnp.where(tri_strict, mfull[h][:C, :C], 0.0) for h in hs]
    a_ak = [_b16(jnp.where(tri_strict, mfull[h][:C, C:], 0.0)) for h in hs]
    a_r = [_b16(jnp.concatenate([jnp.where(tri_incl, mfull[h][C:, :C], 0.0),
                                 jnp.where(tri_incl, mfull[h][C:, C:], 0.0)], axis=1)) for h in hs]
    rhs = [w0[h][:C] + _dot(a_ak[h], _b16(v[h])) for h in hs]

    n4 = [jnp.where(same(4), n_ab[h], 0.0) for h in hs]
    n4b = [_b16(n4[h]) for h in hs]
    sq = [_dot(n4b[h], n4b[h]) for h in hs]
    t = [(eye + n4[h]) + _dot(_b16(eye + n4[h]), _b16(sq[h])) for h in hs]
    s = 4
    while s < C:
        off = jnp.logical_and(jnp.logical_and(same(2 * s), jnp.logical_not(same(s))), tri_strict)
        tb = [_b16(t[h]) for h in hs]
        tmp = [_dot(tb[h], _b16(jnp.where(off, n_ab[h], 0.0))) for h in hs]
        t = [t[h] + _dot(_b16(tmp[h]), tb[h]) for h in hs]
        s *= 2
    tb = [_b16(t[h]) for h in hs]
    u = [_dot(tb[h], _b16(rhs[h])) for h in hs]
    res = [rhs[h] - u[h] + _dot3(n_ab[h], u[h]) for h in hs]
    u = [u[h] + _dot(tb[h], _b16(res[h])) for h in hs]
    uv = [jnp.concatenate([u[h], v[h]], axis=0) for h in hs]
    y = [w0[h][C:] + _dot(a_r[h], _b16(uv[h])) for h in hs]
    for h in hs:
        s_sc[h] = (s0[h] + _dot3(uv[h], x2[h], _dot_tn)) * p_inc[h][last:last + 1, :]
    o_ref[...] = jnp.concatenate(y, axis=1)


def wkv7(r, lw, kd, v, aa, bb, *, reverse, batch, seq, ctx_len, chunk=128, heads_per_step=16):
    m, d = r.shape
    heads_per_step = min(heads_per_step, d // RW_HEAD)
    wl = heads_per_step * RW_HEAD
    n_ctx = ctx_len // chunk
    n_lat = seq // chunk
    n_chunks = n_ctx + n_lat
    ctx0 = (batch * seq) // chunk

    def row_block(b, c):
        if reverse:
            return jnp.where(c < n_ctx, ctx0 + b * n_ctx + (n_ctx - 1 - c), b * n_lat + (n_lat - 1 - (c - n_ctx)))
        return jnp.where(c < n_ctx, ctx0 + b * n_ctx + c, b * n_lat + (c - n_ctx))

    blk = pl.BlockSpec((chunk, wl), lambda b, g, c: (row_block(b, c), g))
    body = functools.partial(_wkv_body, chunk=chunk, n_heads=heads_per_step, reverse=reverse)
    return pl.pallas_call(
        body,
        grid=(batch, d // wl, n_chunks),
        in_specs=[blk] * 6,
        out_specs=blk,
        out_shape=jax.ShapeDtypeStruct((m, d), F32),
        scratch_shapes=[pltpu.VMEM((heads_per_step, RW_HEAD, RW_HEAD), F32)],
        compiler_params=_cparams(("parallel", "parallel", "arbitrary")),
        name="wkv7_rev" if reverse else "wkv7_fwd",
    )(r, lw, kd, v, aa, bb)


def _rw_readout_body(y0_ref, y1_ref, r_ref, v_ref, kd0_ref, kd1_ref, g_ref, rk_ref, lnw_ref, lnb_ref, e_ref, o_ref,
                     *, gn_eps):
    y = y0_ref[...] + y1_ref[...]
    inv = 1.0 / RW_HEAD
    mu = _headsum(y, e_ref) * inv
    dlt = y - mu
    var = _headsum(dlt * dlt, e_ref) * inv
    yn = dlt * lax.rsqrt(var + gn_eps) * lnw_ref[...] + lnb_ref[...]
    rk = r_ref[...] * rk_ref[...]
    bonus = (_headsum(rk * kd0_ref[...], e_ref) + _headsum(rk * kd1_ref[...], e_ref)) * v_ref[...]
    o_ref[...] = ((yn + bonus) * g_ref[...]).astype(o_ref.dtype)


def rwkv_readout(y0, y1, r, v, kd0, kd1, g, r_k, ln_w, ln_b, e128, gn_eps, rows, tm=ROW_TILE):
    d = r.shape[1]
    blk = pl.BlockSpec((tm, LANES), lambda i, c: (i, c))
    vec1 = pl.BlockSpec((1, LANES), lambda i, c: (0, c))
    return pl.pallas_call(
        functools.partial(_rw_readout_body, gn_eps=gn_eps),
        grid=(rows // tm, d // LANES),
        in_specs=[blk] * 7 + [vec1] * 3 + [pl.BlockSpec((LANES, LANES), lambda i, c: (0, 0))],
        out_specs=blk,
        out_shape=jax.ShapeDtypeStruct((rows, d), BF16),
        compiler_params=_cparams(("parallel", "arbitrary")),
        name="rwkv_readout",
    )(y0, y1, r, v, kd0, kd1, g, r_k.reshape(1, d), ln_w.reshape(1, d), ln_b.reshape(1, d), e128)


def rwkv_mixer(hmix, p, *, batch, seq, ctx_len, d_model):
    xs = rwkv_mix(hmix, p["mix"], batch=batch, seq=seq, ctx_len=ctx_len)
    m = hmix.shape[0]
    xs2 = xs.reshape(xs.shape[0] * m, d_model)
    tm = _row_tile(m)

    def proj(n, w, **kw):
        return mm(xs2, w, tm=tm, rows=m, x_row_block0=n * (m // tm), **kw)

    r = proj(0, p["w_r"], name="rw_r")
    k = proj(2, p["w_k"], name="rw_k")
    v = proj(3, p["w_v"], name="rw_v")
    g = mm(proj(5, p["g1"], act="sigmoid", out_dtype=BF16, name="rw_g1"), p["g2"], tm=tm, name="rw_g2")
    lws, als = [], []
    for d in range(2):
        tw = proj(1, p["w1"][d], act="tanh", out_dtype=BF16, name="rw_w1")
        lws.append(mm(tw, p["w2"][d], tm=tm, name="rw_w2"))
        ah = proj(4, p["a1"][d], out_dtype=BF16, name="rw_a1")
        als.append(mm(ah, p["a2"][d], tm=tm, name="rw_a2"))
    e128 = jnp.asarray(np.kron(np.eye(LANES // RW_HEAD), np.ones((RW_HEAD, RW_HEAD))), BF16)
    lw0, lw1, kd0, kd1, bb0, bb1, aa = rwkv_prep(k, lws[0], lws[1], als[0], als[1], p["w0"], p["a0"],
                                                 p["k_k"], p["k_a"], e128, tm=tm)
    kw = dict(batch=batch, seq=seq, ctx_len=ctx_len)
    y0 = wkv7(r, lw0, kd0, v, aa, bb0, reverse=False, **kw)
    y1 = wkv7(r, lw1, kd1, v, aa, bb1, reverse=True, **kw)
    return rwkv_readout(y0, y1, r, v, kd0, kd1, g, p["r_k"], p["ln_w"], p["ln_b"], e128,
                        1e-5 * RW_HEAD, rows=m)


def mla_mixer(hmix, p, *, batch, seq, ctx_len, want_ctx):
    m = hmix.shape[0]
    tm = _row_tile(m)
    scale = (MLA_NOPE + MLA_ROPE) ** -0.5 * LOG2E
    down = mm(hmix, p["w_down_ext"], tm=tm, name="mla_down")
    cos, sin = _rope_tables(seq, ctx_len, batch, MLA_ROPE)
    zeros = jnp.zeros_like(cos)
    cos_p = jnp.concatenate([cos, zeros], -1)
    sin_p = jnp.concatenate([sin, zeros], -1)
    q = proj_rope(down, p["w_uq_ext"], cos_p, sin_p, mode="mla_q", n_heads=MLA_HEADS, scale=scale,
                  norm_g=p["q_norm_g"], x_col_block=0, tm=tm)
    kcat, v = proj_rope(down, p["w_ukv"], cos_p, sin_p, mode="mla_kv", n_heads=MLA_HEADS,
                        norm_g=p["kv_norm_g"], x_col_block=1, pe=down,
                        pe_cb=(MLA_Q_RANK + MLA_KV_RANK) // LANES, tm=tm)
    kw = dict(dqk=2 * LANES, kv_group=1, n_heads=MLA_HEADS, batch=batch, seq=seq, ctx_len=ctx_len)
    o_lat = attention(q, kcat, v, lambda g: g, lat_queries=True, **kw)
    if not want_ctx:
        return o_lat
    o_ctx = attention(q, kcat, v, lambda g: g, lat_queries=False, **kw)
    return jnp.concatenate([o_lat, o_ctx], 0)


def gqa_mixer(hmix, p, *, batch, seq, ctx_len, want_ctx):
    m = hmix.shape[0]
    tm = _row_tile(m)
    scale = GQA_HEAD ** -0.5 * LOG2E
    cos, sin = _rope_tables(seq, ctx_len, batch, GQA_HEAD)
    nq, nk = GQA_HEADS, GQA_KV_HEADS
    q = proj_rope(hmix, p["w_q_ext"], cos, sin, mode="gqa", n_heads=nq, scale=scale,
                  gains=(p["q_norm_g"], p["q_norm_g_rot"]), tm=tm)
    k = proj_rope(hmix, p["w_k_ext"], cos, sin, mode="gqa", n_heads=nk,
                  gains=(p["k_norm_g"], p["k_norm_g_rot"]), tm=tm)
    vb = mm(hmix, p["w_v"], tm=tm, out_dtype=BF16, name="gqa_v")
    grp = nq // nk
    kw = dict(dqk=LANES, kv_group=grp, n_heads=nq, batch=batch, seq=seq, ctx_len=ctx_len)
    o_lat = attention(q, k, vb, lambda g: g, lat_queries=True, **kw)
    if not want_ctx:
        return o_lat
    o_ctx = attention(q, k, vb, lambda g: g, lat_queries=False, **kw)
    return jnp.concatenate([o_lat, o_ctx], 0)


def _rot_gain(g):
    q = g.shape[-1] // 4
    g4 = g.reshape(2, 2, q)
    return jnp.concatenate([g4[:, 1:2], g4[:, 0:1]], axis=1).reshape(g.shape)


def kernel(x, c, ctx, c_ctx, ada_w, ada_b, norm_g, ffn_w_gu, ffn_w_down, hy_w_in, hy_b_in, hy_conv_w, hy_conv_b, hy_f_w1, hy_f_b1, hy_f_w2, hy_f_b2, hy_f_w3, hy_sin_freq, hy_skip, hy_w_out, hy_b_out, rw_mix, rw_w_rkv, rw_w0, rw_w1, rw_w2, rw_a0, rw_a1, rw_a2, rw_g1, rw_g2, rw_k_k, rw_k_a, rw_r_k, rw_ln_w, rw_ln_b, rw_w_o, mla_w_down, mla_q_norm_g, mla_w_uq, mla_kv_norm_g, mla_w_ukv, mla_w_o, gqa_w_qkv, gqa_q_norm_g, gqa_k_norm_g, gqa_w_o, final_g):
    batch, seq, d = x.shape
    ctx_len = ctx.shape[1]
    depth = ada_w.shape[0]
    n_mix = 4
    rows_lat = batch * seq
    rows_all = rows_lat + batch * ctx_len
    bf = lambda a: a.astype(BF16)
    w_gu_all, w_down_all = bf(ffn_w_gu), bf(ffn_w_down)

    cvec = jnp.concatenate([c, c_ctx[None], jnp.zeros((8 - batch - 1, d), F32)], 0)
    mods_all = adaln_all(cvec, ada_w, ada_b)[:, :batch + 1].reshape(depth, batch + 1, N_MOD, d)

    h = jnp.concatenate([x.reshape(rows_lat, d), ctx.reshape(batch * ctx_len, d)], 0)

    for i in range(depth):
        mi, j = i % n_mix, i // n_mix
        ctx_out = any(jj % n_mix != 0 for jj in range(i + 1, depth))
        mods = mods_all[i]
        gate5 = mods[:, 5:6, :]
        mix_dtype = F32 if mi == 1 else BF16
        h, hmix = ffn_half(h, mods, norm_g[i], w_gu_all, w_down_all, layer=i, which=0, s=0, rows=rows_all,
                           seq_rows=seq, next_g=norm_g[i, 1], next_s=1, next_dtype=mix_dtype)
        rows_out = rows_all if ctx_out else rows_lat
        kw = dict(batch=batch, seq=seq, ctx_len=ctx_len)
        if mi == 0:
            p = dict(w_in=bf(hy_w_in[j]), b_in=hy_b_in[j], conv_w=hy_conv_w[j], conv_b=hy_conv_b[j],
                     f_w1=hy_f_w1[j], f_b1=hy_f_b1[j], f_w2=hy_f_w2[j], f_b2=hy_f_b2[j], f_w3=hy_f_w3[j],
                     sin_freq=hy_sin_freq[j], skip=hy_skip[j])
            z = hyena_mixer(hmix, p, d_model=d, want_ctx=ctx_out, **kw)
            w_o, b_o = bf(hy_w_out[j]), hy_b_out[j]
        elif mi == 1:
            pad = lambda w, ax: jnp.pad(w, [(0, RW_LORA_PAD - w.shape[a]) if a == ax else (0, 0) for a in range(w.ndim)])
            p = dict(mix=rw_mix[j], w_r=bf(rw_w_rkv[j, 0]), w_k=bf(rw_w_rkv[j, 1]), w_v=bf(rw_w_rkv[j, 2]),
                     g1=bf(rw_g1[j]), g2=bf(rw_g2[j]),
                     w1=bf(pad(rw_w1[j], 2)), w2=bf(pad(rw_w2[j], 1)), a1=bf(pad(rw_a1[j], 2)), a2=bf(pad(rw_a2[j], 1)),
                     w0=rw_w0[j], a0=rw_a0[j], k_k=rw_k_k[j], k_a=rw_k_a[j], r_k=rw_r_k[j].reshape(-1),
                     ln_w=rw_ln_w[j], ln_b=rw_ln_b[j])
            z = rwkv_mixer(hmix, p, d_model=d, **kw)
            w_o, b_o = bf(rw_w_o[j]), None
        elif mi == 2:
            wd = mla_w_down[j]
            pe = wd[:, MLA_Q_RANK + MLA_KV_RANK:]
            wuq = mla_w_uq[j].reshape(MLA_Q_RANK, MLA_HEADS, MLA_NOPE + MLA_ROPE)
            wq_pe = wuq[..., MLA_NOPE:]
            wuq_ext = jnp.concatenate([wuq[..., :MLA_NOPE], wq_pe,
                                       _rot_cols(wq_pe.reshape(MLA_Q_RANK, -1), MLA_ROPE).reshape(wq_pe.shape)], -1)
            p = dict(w_down_ext=bf(jnp.concatenate([wd, _rot_cols(pe, MLA_ROPE)], 1)),
                     w_uq_ext=bf(wuq_ext.reshape(MLA_Q_RANK, -1)), w_ukv=bf(mla_w_ukv[j]),
                     q_norm_g=mla_q_norm_g[j], kv_norm_g=mla_kv_norm_g[j])
            z = mla_mixer(hmix, p, want_ctx=ctx_out, **kw)
            w_o, b_o = bf(mla_w_o[j]), None
        else:
            w = gqa_w_qkv[j]
            nqc = GQA_HEADS * GQA_HEAD
            nkc = GQA_KV_HEADS * GQA_HEAD
            wq, wk, wv = w[:, :nqc], w[:, nqc:nqc + nkc], w[:, nqc + nkc:]

            def with_rot(wx, nh):
                both = jnp.stack([wx.reshape(d, nh, GQA_HEAD), _rot_cols(wx, GQA_HEAD).reshape(d, nh, GQA_HEAD)], 2)
                return bf(both.reshape(d, 2 * nh * GQA_HEAD))

            p = dict(w_q_ext=with_rot(wq, GQA_HEADS), w_k_ext=with_rot(wk, GQA_KV_HEADS), w_v=bf(wv),
                     q_norm_g=gqa_q_norm_g[j], q_norm_g_rot=_rot_gain(gqa_q_norm_g[j]),
                     k_norm_g=gqa_k_norm_g[j], k_norm_g_rot=_rot_gain(gqa_k_norm_g[j]))
            z = gqa_mixer(hmix, p, want_ctx=ctx_out, **kw)
            w_o, b_o = bf(gqa_w_o[j]), None
        h = mm(z, w_o, bias=b_o, res=h, gate=gate5, seq_rows=seq, rows=rows_out, tn=d, name="mix_out")
        last = i == depth - 1
        if last:
            h, out = ffn_half(h, mods, norm_g[i], w_gu_all, w_down_all, layer=i, which=1, s=2, rows=rows_out,
                              seq_rows=seq, next_g=final_g, next_s=None, next_dtype=F32)
        else:
            h = ffn_half(h, mods, norm_g[i], w_gu_all, w_down_all, layer=i, which=1, s=2, rows=rows_out,
                         seq_rows=seq)
    return out[:rows_lat].reshape(batch, seq, d)
```
